```python
import jax
import jax.numpy as jnp
from jax import lax
import numpy as np

D_MODEL = 1024
BATCH = 16
SEQ = 2048
DEPTH = 1

CTX_LEN = 256
GRID_W = 64
MIX_WIDTH = D_MODEL
HGRN_WIDTH = MIX_WIDTH // 2
HGRN_HEAD_DIM = 128
HGRN_HEADS = HGRN_WIDTH // HGRN_HEAD_DIM
CONV_WIDTH = MIX_WIDTH - HGRN_WIDTH
CONV_K = 31
CONV_PAD = CONV_K // 2
CHUNK = 64
IN_COLS = 5 * HGRN_WIDTH + 2 * CONV_WIDTH
N_GROUPS = 4
EXPERTS_PER_GROUP = 8
N_EXPERTS = N_GROUPS * EXPERTS_PER_GROUP
TOP_K = 2
D_EXPERT = D_MODEL // 2
EXPERT_BLOCK = 128
N_MOD = 6
EPS = 1e-6

kernel_name = 'hybrid_hgrn2_conformer_hmoe_dit_layer'


def _rmsnorm(x, g):
    xf = x.astype(jnp.float32)
    y = xf * lax.rsqrt(jnp.mean(xf * xf, axis=-1, keepdims=True) + EPS)
    return (y * g.astype(jnp.float32)).astype(x.dtype)


def _layernorm(x, g, b):
    xf = x.astype(jnp.float32)
    mu = jnp.mean(xf, axis=-1, keepdims=True)
    var = jnp.mean(jnp.square(xf - mu), axis=-1, keepdims=True)
    y = (xf - mu) * lax.rsqrt(var + EPS)
    return (y * g.astype(jnp.float32) + b.astype(jnp.float32)).astype(x.dtype)


def _modulate(h, shift, scale):
    return h * (1 + scale) + shift


def _lower_bound(lb_logits, layer):
    p = jax.nn.softmax(lb_logits.astype(jnp.float32), axis=0)
    return jnp.cumsum(p, axis=0)[layer]


def _heads(t):
    b, l, _ = t.shape
    return t.reshape(b, l, HGRN_HEADS, HGRN_HEAD_DIM).transpose(0, 2, 1, 3)


def _rev(t):
    return jnp.flip(t, axis=2)


def _forget(f_pre, lb):
    lb = lb.reshape(HGRN_HEADS, 1, HGRN_HEAD_DIM)
    f = lb + (1.0 - lb) * jax.nn.sigmoid(f_pre.astype(jnp.float32))
    return jnp.log(f), 1.0 - f


def _gla_chunked(q, k, v, logf, s0):
    b, h, l, dk = q.shape
    dv = v.shape[-1]
    n = l // CHUNK
    q = q.reshape(b, h, n, CHUNK, dk)
    k = k.reshape(b, h, n, CHUNK, dk)
    v = v.reshape(b, h, n, CHUNK, dv)
    cum = jnp.cumsum(logf.reshape(b, h, n, CHUNK, dk), axis=3)
    last = cum[:, :, :, -1:, :]
    q_dec = q * jnp.exp(cum)
    k_intra = k * jnp.exp(-cum)
    k_state = k * jnp.exp(last - cum)
    mask = jnp.tril(jnp.ones((CHUNK, CHUNK), dtype=bool))
    scores = jnp.where(mask, jnp.einsum('bhncd,bhnsd->bhncs', q_dec, k_intra), 0.0)
    o_intra = jnp.einsum('bhncs,bhnsv->bhncv', scores, v)
    kv = jnp.einsum('bhnsd,bhnsv->bhndv', k_state, v)
    decay = jnp.exp(last[:, :, :, 0, :])

    def step(s, inp):
        kv_n, d_n = inp
        return d_n[..., None] * s + kv_n, s

    s_final, s_prev = lax.scan(step, s0, (jnp.moveaxis(kv, 2, 0), jnp.moveaxis(decay, 2, 0)))
    s_prev = jnp.moveaxis(s_prev, 0, 2)
    o_inter = jnp.einsum('bhncd,bhndv->bhncv', q_dec, s_prev)
    return (o_intra + o_inter).reshape(b, h, l, dv), s_final


def _gla_final_state(k, v, logf):
    cum = jnp.cumsum(logf, axis=2)
    w = jnp.exp(cum[:, :, -1:, :] - cum)
    return jnp.einsum('bhld,bhlv->bhdv', k * w, v)


def _hgrn2_mix(z, lb_f, lb_b, s0_f, s0_b, g_norm):
    W = HGRN_WIDTH
    logf_f, k_f = _forget(_heads(z[..., 0:W]), lb_f)
    logf_b, k_b = _forget(_heads(z[..., W:2 * W]), lb_b)
    v = _heads(z[..., 2 * W:3 * W]).astype(jnp.float32)
    q = jax.nn.silu(_heads(z[..., 3 * W:4 * W]).astype(jnp.float32))
    gate = z[..., 4 * W:5 * W].astype(jnp.float32)
    o_f, s_f = _gla_chunked(q, k_f, v, logf_f, s0_f)
    o_b, s_b = _gla_chunked(_rev(q), _rev(k_b), _rev(v), _rev(logf_b), s0_b)
    o = (o_f + _rev(o_b)).transpose(0, 2, 1, 3)
    o = o * lax.rsqrt(jnp.mean(o * o, axis=-1, keepdims=True) + EPS) * g_norm.astype(jnp.float32)
    b, l = gate.shape[:2]
    o = o.reshape(b, l, W) * jax.nn.silu(gate)
    return o.astype(z.dtype), s_f, s_b


def _hgrn2_ctx_states(z, lb_f, lb_b):
    W = HGRN_WIDTH
    logf_f, k_f = _forget(_heads(z[..., 0:W]), lb_f)
    logf_b, k_b = _forget(_heads(z[..., W:2 * W]), lb_b)
    v = _heads(z[..., 2 * W:3 * W]).astype(jnp.float32)
    return _gla_final_state(k_f, v, logf_f), _gla_final_state(_rev(k_b), _rev(v), _rev(logf_b))


def _conv_module(z, dw_k, dw_b, ln_g, ln_b, n_seg, seg_len):
    a, gt = z[..., :CONV_WIDTH], z[..., CONV_WIDTH:]
    u = a * jax.nn.sigmoid(gt)
    b, l, ch = u.shape
    u = u.reshape(b * n_seg, seg_len, ch)
    u = lax.conv_general_dilated(u, dw_k[:, None, :].astype(u.dtype), window_strides=(1,),
                                 padding=[(CONV_PAD, CONV_PAD)],
                                 dimension_numbers=('NWC', 'WIO', 'NWC'), feature_group_count=ch)
    u = (u + dw_b.astype(u.dtype)).reshape(b, l, ch)
    return jax.nn.silu(_layernorm(u, ln_g, ln_b))


def _hier_moe(h, wg, bg, we, be, w_gate, w_up, w_down):
    n_tok, d = h.shape
    hf = h.astype(jnp.float32)
    g_logits = hf @ wg.astype(jnp.float32) + bg.astype(jnp.float32)
    grp = jnp.argmax(g_logits, axis=-1)
    p_grp = jnp.take_along_axis(jax.nn.softmax(g_logits, axis=-1), grp[:, None], axis=-1)
    e_logits = (hf @ we.astype(jnp.float32) + be.astype(jnp.float32)).reshape(n_tok, N_GROUPS, EXPERTS_PER_GROUP)
    e_logits = jnp.take_along_axis(e_logits, grp[:, None, None], axis=1)[:, 0]
    top_v, top_i = lax.top_k(e_logits, TOP_K)
    wts = (jax.nn.softmax(top_v, axis=-1) * p_grp).reshape(-1)
    eid = (grp[:, None] * EXPERTS_PER_GROUP + top_i).reshape(-1).astype(jnp.int32)
    n_asg = n_tok * TOP_K
    tok = jnp.arange(n_asg, dtype=jnp.int32) // TOP_K
    order = jnp.argsort(eid)
    e_s, tok_s, w_s = eid[order], tok[order], wts[order]
    counts = jax.ops.segment_sum(jnp.ones((n_asg,), jnp.int32), eid, num_segments=N_EXPERTS)
    starts = jnp.cumsum(counts) - counts
    pcounts = (counts + EXPERT_BLOCK - 1) // EXPERT_BLOCK * EXPERT_BLOCK
    pends = jnp.cumsum(pcounts)
    pstarts = pends - pcounts
    pos = pstarts[e_s] + jnp.arange(n_asg, dtype=jnp.int32) - starts[e_s]
    n_blk = -(-n_asg // EXPERT_BLOCK) + N_EXPERTS
    xbuf = jnp.zeros((n_blk * EXPERT_BLOCK, d), h.dtype).at[pos].set(h[tok_s])
    blk_e = jnp.minimum(jnp.searchsorted(pends, jnp.arange(n_blk, dtype=jnp.int32) * EXPERT_BLOCK, side='right'),
                        N_EXPERTS - 1)

    def expert_block(args):
        xb, e = args
        hid = jax.nn.silu(xb @ w_gate[e]) * (xb @ w_up[e])
        return hid @ w_down[e]

    ybuf = lax.map(expert_block, (xbuf.reshape(n_blk, EXPERT_BLOCK, d), blk_e)).reshape(-1, d)
    y = ybuf[pos] * w_s[:, None].astype(h.dtype)
    return jax.ops.segment_sum(y, tok_s, num_segments=n_tok)


def setup_inputs(seed: int = 0) -> dict:
    key = jax.random.key(seed)
    ks = jax.random.split(key, 24)
    f32 = jnp.float32
    D, W = D_MODEL, HGRN_WIDTH

    def nrm(k, shape, s):
        return s * jax.random.normal(k, shape, f32)

    return {
        'x': nrm(ks[0], (BATCH, SEQ, D), 1.0),
        'c': nrm(ks[1], (BATCH, D), 1.0),
        'ctx': nrm(ks[2], (BATCH, CTX_LEN, D), 1.0),
        'c_ctx': nrm(ks[3], (D,), 1.0),
        'w_ada': nrm(ks[4], (DEPTH, D, N_MOD * D), 0.5 * D ** -0.5),
        'b_ada': nrm(ks[5], (DEPTH, N_MOD * D), 0.01),
        'norm1_g': 1.0 + nrm(ks[6], (DEPTH, D), 0.05),
        'w_in': nrm(ks[7], (DEPTH, D, IN_COLS), D ** -0.5),
        'lb_logits': nrm(ks[8], (2, DEPTH + 1, W), 0.1),
        'hgrn_norm_g': 1.0 + nrm(ks[9], (DEPTH, HGRN_HEAD_DIM), 0.05),
        'dw_kernel': nrm(ks[10], (DEPTH, CONV_K, CONV_WIDTH), CONV_K ** -0.5),
        'dw_bias': nrm(ks[11], (DEPTH, CONV_WIDTH), 0.01),
        'conv_ln_g': 1.0 + nrm(ks[12], (DEPTH, CONV_WIDTH), 0.05),
        'conv_ln_b': nrm(ks[13], (DEPTH, CONV_WIDTH), 0.01),
        'w_out': nrm(ks[14], (DEPTH, MIX_WIDTH, D), MIX_WIDTH ** -0.5),
        'norm2_g': 1.0 + nrm(ks[15], (DEPTH, D), 0.05),
        'router_group_w': nrm(ks[16], (DEPTH, D, N_GROUPS), D ** -0.5),
        'router_group_b': nrm(ks[17], (DEPTH, N_GROUPS), 0.01),
        'router_expert_w': nrm(ks[18], (DEPTH, D, N_EXPERTS), D ** -0.5),
        'router_expert_b': nrm(ks[19], (DEPTH, N_EXPERTS), 0.01),
        'w_expert_gate': nrm(ks[20], (DEPTH, N_EXPERTS, D, D_EXPERT), D ** -0.5),
        'w_expert_up': nrm(ks[21], (DEPTH, N_EXPERTS, D, D_EXPERT), D ** -0.5),
        'w_expert_down': nrm(ks[22], (DEPTH, N_EXPERTS, D_EXPERT, D), D_EXPERT ** -0.5),
        'final_norm_g': 1.0 + nrm(ks[23], (D,), 0.05),
    }


def reference(x, c, ctx, c_ctx, w_ada, b_ada, norm1_g, w_in, lb_logits, hgrn_norm_g, dw_kernel, dw_bias,
              conv_ln_g, conv_ln_b, w_out, norm2_g, router_group_w, router_group_b, router_expert_w,
              router_expert_b, w_expert_gate, w_expert_up, w_expert_down, final_norm_g):
    W = HGRN_WIDTH
    bsz, n_lat, d = x.shape
    n_ctx = ctx.shape[1]
    rows = n_lat // GRID_W
    for l in range(DEPTH):
        update_ctx = l < DEPTH - 1
        mod = jax.nn.silu(c) @ w_ada[l] + b_ada[l]
        mod_c = jax.nn.silu(c_ctx) @ w_ada[l] + b_ada[l]
        sh1, sc1, g1, sh2, sc2, g2 = jnp.split(mod[:, None, :], N_MOD, axis=-1)
        csh1, csc1, cg1, csh2, csc2, cg2 = jnp.split(mod_c, N_MOD, axis=-1)
        lb_f = _lower_bound(lb_logits[0], l)
        lb_b = _lower_bound(lb_logits[1], l)
        hx = _modulate(_rmsnorm(x, norm1_g[l]), sh1, sc1)
        hc = _modulate(_rmsnorm(ctx, norm1_g[l]), csh1, csc1)
        zx = hx @ w_in[l]
        if update_ctx:
            zc = hc @ w_in[l]
            s0 = jnp.zeros((bsz, HGRN_HEADS, HGRN_HEAD_DIM, HGRN_HEAD_DIM), jnp.float32)
            o_c, s_ctx_f, s_ctx_b = _hgrn2_mix(zc[..., :5 * W], lb_f, lb_b, s0, s0, hgrn_norm_g[l])
            conv_c = _conv_module(zc[..., 5 * W:], dw_kernel[l], dw_bias[l], conv_ln_g[l], conv_ln_b[l], 1, n_ctx)
            ctx_new = ctx + cg1 * (jnp.concatenate([o_c, conv_c], axis=-1) @ w_out[l])
            hc2 = _modulate(_rmsnorm(ctx_new, norm2_g[l]), csh2, csc2)
            ctx_new = ctx_new + cg2 * _hier_moe(hc2.reshape(-1, d), router_group_w[l], router_group_b[l],
                                                router_expert_w[l], router_expert_b[l], w_expert_gate[l],
                                                w_expert_up[l], w_expert_down[l]).reshape(ctx.shape)
        else:
            zc = hc @ w_in[l][:, :3 * W]
            s_ctx_f, s_ctx_b = _hgrn2_ctx_states(zc, lb_f, lb_b)
        o_x, _, _ = _hgrn2_mix(zx[..., :5 * W], lb_f, lb_b, s_ctx_f, s_ctx_b, hgrn_norm_g[l])
        conv_x = _conv_module(zx[..., 5 * W:], dw_kernel[l], dw_bias[l], conv_ln_g[l], conv_ln_b[l], rows, GRID_W)
        x = x + g1 * (jnp.concatenate([o_x, conv_x], axis=-1) @ w_out[l])
        hx2 = _modulate(_rmsnorm(x, norm2_g[l]), sh2, sc2)
        x = x + g2 * _hier_moe(hx2.reshape(-1, d), router_group_w[l], router_group_b[l], router_expert_w[l],
                               router_expert_b[l], w_expert_gate[l], w_expert_up[l],
                               w_expert_down[l]).reshape(x.shape)
        if update_ctx:
            ctx = ctx_new
    return _rmsnorm(x, final_norm_g)
```

```python
import functools

import jax
import jax.numpy as jnp
from jax import lax
from jax.experimental import pallas as pl
from jax.experimental.pallas import tpu as pltpu

F32 = jnp.float32
BF16 = jnp.bfloat16
I32 = jnp.int32

EPS = 1e-6
HEAD_DIM = 128
CHUNK = 64
GRID_W = 64
CONV_K = 31
CONV_PAD = CONV_K // 2
N_GROUPS = 4
EXPERTS_PER_GROUP = 8
N_EXPERTS = N_GROUPS * EXPERTS_PER_GROUP
TOP_K = 2
N_MOD = 6
ROUTER_ROWS = 48

ADA_TN = 1024
INPROJ_TM = 256
HGRN_TT = 256
MIX_TM = 256
MOE_TM = 256
FFN_TB = 256
VMEM_LIMIT = 48 * 1024 * 1024


def _cparams(sem):
    return pltpu.CompilerParams(dimension_semantics=sem, vmem_limit_bytes=VMEM_LIMIT)


def _silu(v):
    return v * jax.nn.sigmoid(v)


def _dot(a, b):
    return jnp.dot(a, b, preferred_element_type=F32)


def _dot_nt(a, b):
    return lax.dot_general(a, b, (((1,), (1,)), ((), ())), preferred_element_type=F32)


def _dot_tn(a, b):
    return lax.dot_general(a, b, (((0,), (0,)), ((), ())), preferred_element_type=F32)


def _split3(a):
    hi = a.astype(BF16)
    r = a - hi.astype(F32)
    mid = r.astype(BF16)
    lo = (r - mid.astype(F32)).astype(BF16)
    return hi, mid, lo


def _ada_kernel(c_ref, w_ref, b_ref, o_ref):
    s = _silu(c_ref[...]).astype(BF16)
    o_ref[...] = _dot(s, w_ref[...].astype(BF16)) + b_ref[...]


def _ada(cc, w, b):
    r, d = cc.shape
    nc = w.shape[1]
    return pl.pallas_call(
        _ada_kernel,
        grid=(nc // ADA_TN,),
        in_specs=[pl.BlockSpec((r, d), lambda j: (0, 0)),
                  pl.BlockSpec((d, ADA_TN), lambda j: (0, j)),
                  pl.BlockSpec((1, ADA_TN), lambda j: (0, j))],
        out_specs=pl.BlockSpec((r, ADA_TN), lambda j: (0, j)),
        out_shape=jax.ShapeDtypeStruct((r, nc), F32),
        compiler_params=_cparams(("arbitrary",)),
        name="ada",
    )(cc, w, b)


def _inproj_kernel(x_ref, g_ref, sc_ref, sh_ref, w_ref, o_ref):
    x = x_ref[0]
    ms = jnp.mean(x * x, axis=-1, keepdims=True)
    gs = g_ref[...] * (1.0 + sc_ref[0])
    h = x * lax.rsqrt(ms + EPS) * gs + sh_ref[0]
    o_ref[0] = _dot(h.astype(BF16), w_ref[...])


def _inproj(x, g, sc, sh, w):
    bsz, l, d = x.shape
    nc = w.shape[1]
    tm = min(INPROJ_TM, l)
    return pl.pallas_call(
        _inproj_kernel,
        grid=(bsz, l // tm),
        in_specs=[pl.BlockSpec((1, tm, d), lambda b, i: (b, i, 0)),
                  pl.BlockSpec((1, d), lambda b, i: (0, 0)),
                  pl.BlockSpec((1, 1, d), lambda b, i: (b, 0, 0)),
                  pl.BlockSpec((1, 1, d), lambda b, i: (b, 0, 0)),
                  pl.BlockSpec((d, nc), lambda b, i: (0, 0))],
        out_specs=pl.BlockSpec((1, tm, nc), lambda b, i: (b, i, 0)),
        out_shape=jax.ShapeDtypeStruct((bsz, l, nc), F32),
        compiler_params=_cparams(("arbitrary", "arbitrary")),
        name="inproj",
    )(x, g, sc, sh, w)


def _hgrn_kernel(lbl_ref, cf_ref, cv_ref, f_ref, v_ref, q_ref, o_ref, st_ref, *, reverse, n_ctx, tt):
    heads = st_ref.shape[0]
    a = lbl_ref[...]
    e = jnp.exp(a - jnp.max(a, axis=0, keepdims=True))
    lb = e[0:1] / jnp.sum(e, axis=0, keepdims=True)
    row = lax.broadcasted_iota(I32, (CHUNK, CHUNK), 0)
    col = lax.broadcasted_iota(I32, (CHUNK, CHUNK), 1)
    tri = (col >= row) if reverse else (col <= row)
    tm = jnp.where(tri, 1.0, 0.0).astype(BF16)

    def gates(fpre):
        f = lb + (1.0 - lb) * jax.nn.sigmoid(fpre)
        logf = jnp.log(f)
        hi, mid, lo = _split3(logf)
        cum = _dot(tm, hi) + _dot(tm, mid) + _dot(tm, lo)
        last = cum[0:1] if reverse else cum[CHUNK - 1:CHUNK]
        return 1.0 - f, cum, last

    def update_state(k, cum, last, v):
        ks = (k * jnp.exp(last - cum)).astype(BF16)
        dec = jnp.exp(last)
        vb = v.astype(BF16)
        for h in range(heads):
            sl = slice(h * HEAD_DIM, (h + 1) * HEAD_DIM)
            st_ref[h] = st_ref[h] * dec[:, sl] + _dot_tn(vb[:, sl], ks[:, sl])

    order = lambda n: range(n - 1, -1, -1) if reverse else range(n)

    @pl.when(pl.program_id(1) == 0)
    def _():
        st_ref[...] = jnp.zeros_like(st_ref)
        for j in order(n_ctx):
            rows = slice(j * CHUNK, (j + 1) * CHUNK)
            k, cum, last = gates(cf_ref[0, rows, :])
            update_state(k, cum, last, cv_ref[0, rows, :])

    for j in order(tt // CHUNK):
        rows = slice(j * CHUNK, (j + 1) * CHUNK)
        k, cum, last = gates(f_ref[0, rows, :])
        v = v_ref[0, rows, :]
        qd = (_silu(q_ref[0, rows, :]) * jnp.exp(cum)).astype(BF16)
        ki = (k * jnp.exp(-cum)).astype(BF16)
        vb = v.astype(BF16)
        for h in range(heads):
            sl = slice(h * HEAD_DIM, (h + 1) * HEAD_DIM)
            sc = jnp.where(tri, _dot_nt(qd[:, sl], ki[:, sl]), 0.0).astype(BF16)
            o_ref[0, rows, sl] = _dot(sc, vb[:, sl]) + _dot_nt(qd[:, sl], st_ref[h].astype(BF16))
        update_state(k, cum, last, v)


def _hgrn(lbl, zc, z, *, reverse):
    bsz, l, _ = z.shape
    n_ctx_tok = zc.shape[1]
    w = lbl.shape[1]
    heads = w // HEAD_DIM
    tt = min(HGRN_TT, l)
    nt = l // tt
    fcol = 1 if reverse else 0
    tile = (lambda i: nt - 1 - i) if reverse else (lambda i: i)
    kern = functools.partial(_hgrn_kernel, reverse=reverse, n_ctx=n_ctx_tok // CHUNK, tt=tt)
    return pl.pallas_call(
        kern,
        grid=(bsz, nt),
        in_specs=[pl.BlockSpec(lbl.shape, lambda b, i: (0, 0)),
                  pl.BlockSpec((1, n_ctx_tok, w), lambda b, i: (b, 0, fcol)),
                  pl.BlockSpec((1, n_ctx_tok, w), lambda b, i: (b, 0, 2)),
                  pl.BlockSpec((1, tt, w), lambda b, i: (b, tile(i), fcol)),
                  pl.BlockSpec((1, tt, w), lambda b, i: (b, tile(i), 2)),
                  pl.BlockSpec((1, tt, w), lambda b, i: (b, tile(i), 3))],
        out_specs=pl.BlockSpec((1, tt, w), lambda b, i: (b, tile(i), 0)),
        out_shape=jax.ShapeDtypeStruct((bsz, l, w), F32),
        scratch_shapes=[pltpu.VMEM((heads, HEAD_DIM, HEAD_DIM), F32)],
        compiler_params=_cparams(("arbitrary", "arbitrary")),
        name="hgrn_bwd" if reverse else "hgrn_fwd",
    )(lbl, zc, zc, z, z, z)


def _mix_kernel(of_ref, ob_ref, zg_ref, za_ref, zt_ref, x_ref, g1_ref, sc2_ref, sh2_ref,
                hg_ref, dwk_ref, dwb_ref, lng_ref, lnb_ref, wo_ref, n2g_ref, wr_ref, br_ref,
                xm_ref, h2_ref, eid_ref, wts_ref, rank_ref, cnt_ref,
                pad_ref, cv_ref, carry_ref, *, tm):
    w = of_ref.shape[2]
    heads = w // HEAD_DIM

    @pl.when((pl.program_id(0) == 0) & (pl.program_id(1) == 0))
    def _():
        carry_ref[...] = jnp.zeros_like(carry_ref)

    o = of_ref[0] + ob_ref[0]
    parts = []
    for h in range(heads):
        oh = o[:, h * HEAD_DIM:(h + 1) * HEAD_DIM]
        parts.append(oh * lax.rsqrt(jnp.mean(oh * oh, axis=-1, keepdims=True) + EPS))
    o_mix = jnp.concatenate(parts, axis=-1) * hg_ref[...] * _silu(zg_ref[0])

    u = za_ref[0] * jax.nn.sigmoid(zt_ref[0])
    lo, hi = 16, 16 + GRID_W
    pad_ref[0:lo, :] = jnp.zeros((lo, u.shape[1]), F32)
    pad_ref[hi:hi + 16, :] = jnp.zeros((16, u.shape[1]), F32)
    for s in range(tm // GRID_W):
        pad_ref[lo:hi, :] = u[s * GRID_W:(s + 1) * GRID_W]
        acc = jnp.zeros((GRID_W, u.shape[1]), F32) + dwb_ref[...]
        for k in range(CONV_K):
            off = lo - CONV_PAD + k
            acc = acc + pad_ref[off:off + GRID_W, :] * dwk_ref[k:k + 1, :]
        cv_ref[s * GRID_W:(s + 1) * GRID_W, :] = acc
    cv = cv_ref[...]
    mu = jnp.mean(cv, axis=-1, keepdims=True)
    cen = cv - mu
    var = jnp.mean(cen * cen, axis=-1, keepdims=True)
    c_mix = _silu(cen * lax.rsqrt(var + EPS) * lng_ref[...] + lnb_ref[...])

    mix = _dot(o_mix.astype(BF16), wo_ref[0:w, :]) + _dot(c_mix.astype(BF16), wo_ref[w:, :])
    xm = x_ref[0] + g1_ref[0] * mix
    xm_ref[0] = xm

    ms = jnp.mean(xm * xm, axis=-1, keepdims=True)
    h2 = xm * lax.rsqrt(ms + EPS) * (n2g_ref[...] * (1.0 + sc2_ref[0])) + sh2_ref[0]
    h2_ref[0] = h2

    hh = h2.astype(BF16)
    hl = (h2 - hh.astype(F32)).astype(BF16)
    wr = wr_ref[...]
    wh = wr.astype(BF16)
    wl = (wr - wh.astype(F32)).astype(BF16)
    lg = _dot_nt(wh, hh) + _dot_nt(wh, hl) + _dot_nt(wl, hh) + br_ref[...]

    neg = -jnp.inf
    r8 = lax.broadcasted_iota(I32, (EXPERTS_PER_GROUP, tm), 0).astype(F32)
    gl = jnp.where(r8 < N_GROUPS, lg[0:8], neg)
    gmax = jnp.max(gl, axis=0, keepdims=True)
    grp = jnp.min(jnp.where(gl == gmax, r8, 8.0), axis=0, keepdims=True)
    p_grp = 1.0 / jnp.sum(jnp.exp(gl - gmax), axis=0, keepdims=True)
    es = lg[8:16]
    for g in range(1, N_GROUPS):
        es = jnp.where(grp == float(g), lg[8 + 8 * g:16 + 8 * g], es)
    m1 = jnp.max(es, axis=0, keepdims=True)
    i1 = jnp.min(jnp.where(es == m1, r8, 8.0), axis=0, keepdims=True)
    es2 = jnp.where(r8 == i1, neg, es)
    m2 = jnp.max(es2, axis=0, keepdims=True)
    i2 = jnp.min(jnp.where(es2 == m2, r8, 8.0), axis=0, keepdims=True)
    e2 = jnp.exp(m2 - m1)
    w1 = p_grp / (1.0 + e2)
    w2 = p_grp * e2 / (1.0 + e2)
    eid0 = grp * float(EXPERTS_PER_GROUP) + i1
    eid1 = grp * float(EXPERTS_PER_GROUP) + i2
    eid_ref[0:1, :] = eid0.astype(I32)
    eid_ref[1:2, :] = eid1.astype(I32)
    wts_ref[0:1, :] = w1
    wts_ref[1:2, :] = w2

    r32 = lax.broadcasted_iota(I32, (N_EXPERTS, tm), 0).astype(F32)
    oh0 = jnp.where(r32 == eid0, 1.0, 0.0)
    oh1 = jnp.where(r32 == eid1, 1.0, 0.0)
    src = lax.broadcasted_iota(I32, (tm, tm), 0)
    dst = lax.broadcasted_iota(I32, (tm, tm), 1)
    before = jnp.where(src < dst, 1.0, 0.0).astype(BF16)
    ex = _dot(jnp.concatenate([oh0, oh1], axis=0).astype(BF16), before)
    base = ex[0:N_EXPERTS] + ex[N_EXPERTS:] + carry_ref[...]
    rank_ref[0:1, :] = jnp.sum(oh0 * base, axis=0, keepdims=True).astype(I32)
    rank_ref[1:2, :] = jnp.sum(oh1 * base, axis=0, keepdims=True).astype(I32)
    carry = carry_ref[...] + jnp.sum(oh0 + oh1, axis=1, keepdims=True)
    carry_ref[...] = carry
    cnt_ref[...] = jnp.broadcast_to(carry, cnt_ref.shape)


def _mix(o_f, o_b, z, x, g1, sc2, sh2, hg, dwk, dwb, lng, lnb, wo, n2g, wr, br):
    bsz, l, d = x.shape
    w = o_f.shape[2]
    cw = dwk.shape[1]
    tm = min(MIX_TM, l)
    nt = l // tm
    n = bsz * l
    tok = lambda b, i: (b, i, 0)
    vec = lambda b, i: (0, 0)
    per_b = lambda b, i: (b, 0, 0)
    flat = lambda b, i: (0, b * nt + i)
    kern = functools.partial(_mix_kernel, tm=tm)
    return pl.pallas_call(
        kern,
        grid=(bsz, nt),
        in_specs=[pl.BlockSpec((1, tm, w), tok),
                  pl.BlockSpec((1, tm, w), tok),
                  pl.BlockSpec((1, tm, w), lambda b, i: (b, i, 4)),
                  pl.BlockSpec((1, tm, cw), lambda b, i: (b, i, 5)),
                  pl.BlockSpec((1, tm, cw), lambda b, i: (b, i, 6)),
                  pl.BlockSpec((1, tm, d), tok),
                  pl.BlockSpec((1, 1, d), per_b),
                  pl.BlockSpec((1, 1, d), per_b),
                  pl.BlockSpec((1, 1, d), per_b),
                  pl.BlockSpec((1, w), vec),
                  pl.BlockSpec((CONV_K, cw), vec),
                  pl.BlockSpec((1, cw), vec),
                  pl.BlockSpec((1, cw), vec),
                  pl.BlockSpec((1, cw), vec),
                  pl.BlockSpec((w + cw, d), vec),
                  pl.BlockSpec((1, d), vec),
                  pl.BlockSpec((ROUTER_ROWS, d), vec),
                  pl.BlockSpec((ROUTER_ROWS, 1), vec)],
        out_specs=[pl.BlockSpec((1, tm, d), tok),
                   pl.BlockSpec((1, tm, d), tok),
                   pl.BlockSpec((TOP_K, tm), flat),
                   pl.BlockSpec((TOP_K, tm), flat),
                   pl.BlockSpec((TOP_K, tm), flat),
                   pl.BlockSpec((N_EXPERTS, 128), vec)],
        out_shape=[jax.ShapeDtypeStruct((bsz, l, d), F32),
                   jax.ShapeDtypeStruct((bsz, l, d), F32),
                   jax.ShapeDtypeStruct((TOP_K, n), I32),
                   jax.ShapeDtypeStruct((TOP_K, n), F32),
                   jax.ShapeDtypeStruct((TOP_K, n), I32),
                   jax.ShapeDtypeStruct((N_EXPERTS, 128), F32)],
        scratch_shapes=[pltpu.VMEM((GRID_W + 32, cw), F32),
                        pltpu.VMEM((tm, cw), F32),
                        pltpu.VMEM((N_EXPERTS, 1), F32)],
        compiler_params=_cparams(("arbitrary", "arbitrary")),
        name="mix",
    )(o_f, o_b, z, z, z, x, g1, sc2, sh2, hg, dwk, dwb, lng, lnb, wo, n2g, wr, br)


def _row_copy(src, s, dst, d, sem):
    return pltpu.make_async_copy(src.at[pl.ds(s, 1)], dst.at[pl.ds(d, 1)], sem)


def _dispatch_kernel(pos_hbm, h_ref, buf_in, buf_hbm, pos_smem, idx_sem, row_sem, *, tm):
    del buf_in
    i = pl.program_id(0)
    idx = pltpu.make_async_copy(pos_hbm.at[i], pos_smem, idx_sem)
    idx.start()
    idx.wait()

    def issue(t, carry):
        for k in range(TOP_K):
            _row_copy(h_ref, t, buf_hbm, pos_smem[0, k * tm + t], row_sem).start()
        return carry

    lax.fori_loop(0, tm, issue, 0)

    def drain(t, carry):
        for k in range(TOP_K):
            _row_copy(h_ref, 0, buf_hbm, 0, row_sem).wait()
        return carry

    lax.fori_loop(0, tm, drain, 0)


def _dispatch(pos3, h2, buf0):
    n, d = h2.shape
    tm = pos3.shape[2] // TOP_K
    kern = functools.partial(_dispatch_kernel, tm=tm)
    return pl.pallas_call(
        kern,
        grid=(n // tm,),
        in_specs=[pl.BlockSpec(memory_space=pl.ANY),
                  pl.BlockSpec((tm, d), lambda i: (i, 0)),
                  pl.BlockSpec(memory_space=pl.ANY)],
        out_specs=pl.BlockSpec(memory_space=pl.ANY),
        out_shape=jax.ShapeDtypeStruct(buf0.shape, buf0.dtype),
        scratch_shapes=[pltpu.SMEM((1, TOP_K * tm), I32),
                        pltpu.SemaphoreType.DMA,
                        pltpu.SemaphoreType.DMA],
        input_output_aliases={2: 0},
        compiler_params=_cparams(("arbitrary",)),
        name="dispatch",
    )(pos3, h2, buf0)


def _ffn_kernel(blk_e_ref, nblk_ref, x_ref, wg_ref, wu_ref, wd_ref, y_ref):
    del blk_e_ref
    used = pl.program_id(0) < nblk_ref[0]

    @pl.when(used)
    def _():
        x = x_ref[...].astype(BF16)
        g = _dot(x, wg_ref[0].astype(BF16))
        u = _dot(x, wu_ref[0].astype(BF16))
        y_ref[...] = _dot((_silu(g) * u).astype(BF16), wd_ref[0].astype(BF16))

    @pl.when(jnp.logical_not(used))
    def _():
        y_ref[...] = jnp.zeros_like(y_ref)


def _ffn(blk_e, nblk, xbuf, wg, wu, wd):
    n_pad, d = xbuf.shape
    de = wg.shape[2]
    tb = FFN_TB
    grid_spec = pltpu.PrefetchScalarGridSpec(
        num_scalar_prefetch=2,
        grid=(n_pad // tb,),
        in_specs=[pl.BlockSpec((tb, d), lambda i, be, nb: (i, 0)),
                  pl.BlockSpec((1, d, de), lambda i, be, nb: (be[i], 0, 0)),
                  pl.BlockSpec((1, d, de), lambda i, be, nb: (be[i], 0, 0)),
                  pl.BlockSpec((1, de, d), lambda i, be, nb: (be[i], 0, 0))],
        out_specs=pl.BlockSpec((tb, d), lambda i, be, nb: (i, 0)),
    )
    return pl.pallas_call(
        _ffn_kernel,
        grid_spec=grid_spec,
        out_shape=jax.ShapeDtypeStruct((n_pad, d), F32),
        compiler_params=_cparams(("arbitrary",)),
        name="ffn",
    )(blk_e, nblk, xbuf, wg, wu, wd)


def _final_kernel(pos_hbm, ybuf_hbm, xm_ref, wt_ref, g2_ref, fg_ref, o_ref,
                  pos_smem, rows_ref, idx_sem, row_sem, *, tm, nt):
    i = pl.program_id(0) * nt + pl.program_id(1)
    idx = pltpu.make_async_copy(pos_hbm.at[i], pos_smem, idx_sem)
    idx.start()
    idx.wait()

    def issue(t, carry):
        for k in range(TOP_K):
            _row_copy(ybuf_hbm, pos_smem[0, k * tm + t], rows_ref.at[k], t, row_sem).start()
        return carry

    lax.fori_loop(0, tm, issue, 0)

    def drain(t, carry):
        for k in range(TOP_K):
            _row_copy(ybuf_hbm, 0, rows_ref.at[k], 0, row_sem).wait()
        return carry

    lax.fori_loop(0, tm, drain, 0)

    wt = wt_ref[...]
    y = rows_ref[0] * wt[:, 0:1] + rows_ref[1] * wt[:, 1:2]
    x = xm_ref[0] + g2_ref[0] * y
    ms = jnp.mean(x * x, axis=-1, keepdims=True)
    o_ref[0] = x * lax.rsqrt(ms + EPS) * fg_ref[...]


def _final(pos3, ybuf, xm, wt, g2, fg):
    bsz, l, d = xm.shape
    tm = pos3.shape[2] // TOP_K
    nt = l // tm
    kern = functools.partial(_final_kernel, tm=tm, nt=nt)
    return pl.pallas_call(
        kern,
        grid=(bsz, nt),
        in_specs=[pl.BlockSpec(memory_space=pl.ANY),
                  pl.BlockSpec(memory_space=pl.ANY),
                  pl.BlockSpec((1, tm, d), lambda b, i: (b, i, 0)),
                  pl.BlockSpec((tm, TOP_K), lambda b, i: (b * nt + i, 0)),
                  pl.BlockSpec((1, 1, d), lambda b, i: (b, 0, 0)),
                  pl.BlockSpec((1, d), lambda b, i: (0, 0))],
        out_specs=pl.BlockSpec((1, tm, d), lambda b, i: (b, i, 0)),
        out_shape=jax.ShapeDtypeStruct((bsz, l, d), F32),
        scratch_shapes=[pltpu.SMEM((1, TOP_K * tm), I32),
                        pltpu.VMEM((TOP_K, tm, d), F32),
                        pltpu.SemaphoreType.DMA,
                        pltpu.SemaphoreType.DMA],
        compiler_params=_cparams(("arbitrary", "arbitrary")),
        name="final",
    )(pos3, ybuf, xm, wt, g2, fg)


def _layer(x, ctx, mod, mod_c, norm1_g, w_in, lb_logits, hgrn_norm_g, dw_kernel, dw_bias, conv_ln_g,
           conv_ln_b, w_out, norm2_g, wr, br, w_gate, w_up, w_down, final_norm_g):
    bsz, l, d = x.shape
    w = lb_logits.shape[2]
    heads = w // HEAD_DIM
    n = bsz * l
    row = lambda v: v.reshape(1, -1)
    per_b = lambda v: v.reshape(bsz, 1, d)
    sh1, sc1, g1, sh2, sc2, g2 = [per_b(mod[:, j]) for j in range(N_MOD)]
    csh1 = jnp.broadcast_to(mod_c[0].reshape(1, 1, d), (bsz, 1, d))
    csc1 = jnp.broadcast_to(mod_c[1].reshape(1, 1, d), (bsz, 1, d))

    w_in_b = w_in.astype(BF16)
    z = _inproj(x, row(norm1_g), sc1, sh1, w_in_b)
    zc = _inproj(ctx, row(norm1_g), csc1, csh1, w_in_b[:, :3 * w])
    o_f = _hgrn(lb_logits[0], zc, z, reverse=False)
    o_b = _hgrn(lb_logits[1], zc, z, reverse=True)

    xm, h2, eid, wts, rank, cnt = _mix(
        o_f, o_b, z, x, g1, sc2, sh2, row(jnp.tile(hgrn_norm_g, heads)), dw_kernel, row(dw_bias),
        row(conv_ln_g), row(conv_ln_b), w_out.astype(BF16), row(norm2_g), wr, br)

    tb = FFN_TB
    n_blk = -(-(n * TOP_K) // tb) + N_EXPERTS
    counts = cnt[:, 0].astype(I32)
    pcounts = (counts + tb - 1) // tb * tb
    pends = jnp.cumsum(pcounts)
    pstarts = pends - pcounts
    pos = pstarts[eid] + rank
    blk_e = jnp.minimum(jnp.searchsorted(pends, jnp.arange(n_blk, dtype=I32) * tb, side='right'),
                        N_EXPERTS - 1).astype(I32)
    nblk_used = (pends[-1:] // tb).astype(I32)

    tm = min(MOE_TM, l)
    pos3 = pos.reshape(TOP_K, n // tm, tm).transpose(1, 0, 2).reshape(n // tm, 1, TOP_K * tm)
    xbuf = _dispatch(pos3, h2.reshape(n, d), jnp.zeros((n_blk * tb, d), F32))
    ybuf = _ffn(blk_e, nblk_used, xbuf, w_gate, w_up, w_down)
    return _final(pos3, ybuf, xm, wts.T, g2, row(final_norm_g))


def kernel(x, c, ctx, c_ctx, w_ada, b_ada, norm1_g, w_in, lb_logits, hgrn_norm_g, dw_kernel, dw_bias,
           conv_ln_g, conv_ln_b, w_out, norm2_g, router_group_w, router_group_b, router_expert_w,
           router_expert_b, w_expert_gate, w_expert_up, w_expert_down, final_norm_g):
    depth = w_ada.shape[0]
    assert depth == 1, "context tokens are only updated between layers; a single layer is implemented"
    bsz, l, d = x.shape
    l0 = 0
    rows = -(-(bsz + 1) // 8) * 8
    cc = jnp.zeros((rows, d), F32).at[:bsz].set(c).at[bsz].set(c_ctx)
    mod_all = _ada(cc, w_ada[l0], b_ada[l0].reshape(1, -1))
    mod = mod_all[:bsz].reshape(bsz, N_MOD, d)
    mod_c = mod_all[bsz].reshape(N_MOD, d)
    pad8 = jnp.zeros((d, 8 - N_GROUPS), F32)
    wr = jnp.concatenate([router_group_w[l0], pad8, router_expert_w[l0], jnp.zeros((d, 8), F32)], axis=1).T
    br = jnp.concatenate([router_group_b[l0], jnp.zeros((8 - N_GROUPS,), F32), router_expert_b[l0],
                          jnp.zeros((8,), F32)]).reshape(ROUTER_ROWS, 1)
    return _layer(x, ctx, mod, mod_c, norm1_g[l0], w_in[l0], lb_logits, hgrn_norm_g[l0], dw_kernel[l0], dw_bias[l0], conv_ln_g[l0], conv_ln_b[l0], w_out[l0],
                  norm2_g[l0], wr, br, w_expert_gate[l0], w_expert_up[l0], w_expert_down[l0], final_norm_g)
```

```python
import functools

import jax
import jax.numpy as jnp
from jax import lax
from jax.experimental import pallas as pl
from jax.experimental.pallas import tpu as pltpu

F32 = jnp.float32
BF16 = jnp.bfloat16
I32 = jnp.int32

EPS = 1e-6
HEAD_DIM = 128
CHUNK = 64
GRID_W = 64
CONV_K = 31
CONV_PAD = CONV_K // 2
N_GROUPS = 4
EXPERTS_PER_GROUP = 8
N_EXPERTS = N_GROUPS * EXPERTS_PER_GROUP
TOP_K = 2
N_MOD = 6
ROUTER_ROWS = 48

ADA_TN = 1024
INPROJ_TM = 256
HGRN_TT = 256
MIX_TM = 256
MOE_TM = 256
FFN_TB = 256
VMEM_LIMIT = 48 * 1024 * 1024


def _cparams(sem):
    return pltpu.CompilerParams(dimension_semantics=sem, vmem_limit_bytes=VMEM_LIMIT)


def _silu(v):
    return v * jax.nn.sigmoid(v)


def _dot(a, b):
    return jnp.dot(a, b, preferred_element_type=F32)


def _dot_nt(a, b):
    return lax.dot_general(a, b, (((1,), (1,)), ((), ())), preferred_element_type=F32)


def _dot_tn(a, b):
    return lax.dot_general(a, b, (((0,), (0,)), ((), ())), preferred_element_type=F32)


def _split3(a):
    hi = a.astype(BF16)
    r = a - hi.astype(F32)
    mid = r.astype(BF16)
    lo = (r - mid.astype(F32)).astype(BF16)
    return hi, mid, lo


def _ada_kernel(c_ref, w_ref, b_ref, o_ref):
    s = _silu(c_ref[...]).astype(BF16)
    o_ref[...] = _dot(s, w_ref[...].astype(BF16)) + b_ref[...]


def _ada(cc, w, b):
    r, d = cc.shape
    nc = w.shape[1]
    return pl.pallas_call(
        _ada_kernel,
        grid=(nc // ADA_TN,),
        in_specs=[pl.BlockSpec((r, d), lambda j: (0, 0)),
                  pl.BlockSpec((d, ADA_TN), lambda j: (0, j)),
                  pl.BlockSpec((1, ADA_TN), lambda j: (0, j))],
        out_specs=pl.BlockSpec((r, ADA_TN), lambda j: (0, j)),
        out_shape=jax.ShapeDtypeStruct((r, nc), F32),
        compiler_params=_cparams(("arbitrary",)),
        name="ada",
    )(cc, w, b)


def _inproj_kernel(x_ref, g_ref, sc_ref, sh_ref, w_ref, o_ref):
    x = x_ref[0]
    ms = jnp.mean(x * x, axis=-1, keepdims=True)
    gs = g_ref[...] * (1.0 + sc_ref[0])
    h = x * lax.rsqrt(ms + EPS) * gs + sh_ref[0]
    o_ref[0] = _dot(h.astype(BF16), w_ref[...])


def _inproj(x, g, sc, sh, w):
    bsz, l, d = x.shape
    nc = w.shape[1]
    tm = min(INPROJ_TM, l)
    return pl.pallas_call(
        _inproj_kernel,
        grid=(bsz, l // tm),
        in_specs=[pl.BlockSpec((1, tm, d), lambda b, i: (b, i, 0)),
                  pl.BlockSpec((1, d), lambda b, i: (0, 0)),
                  pl.BlockSpec((1, 1, d), lambda b, i: (b, 0, 0)),
                  pl.BlockSpec((1, 1, d), lambda b, i: (b, 0, 0)),
                  pl.BlockSpec((d, nc), lambda b, i: (0, 0))],
        out_specs=pl.BlockSpec((1, tm, nc), lambda b, i: (b, i, 0)),
        out_shape=jax.ShapeDtypeStruct((bsz, l, nc), F32),
        compiler_params=_cparams(("arbitrary", "arbitrary")),
        name="inproj",
    )(x, g, sc, sh, w)


def _hgrn_kernel(lbl_ref, cf_ref, cv_ref, f_ref, v_ref, q_ref, o_ref, st_ref, *, reverse, n_ctx, tt):
    heads = st_ref.shape[0]
    a = lbl_ref[...]
    e = jnp.exp(a - jnp.max(a, axis=0, keepdims=True))
    lb = e[0:1] / jnp.sum(e, axis=0, keepdims=True)
    row = lax.broadcasted_iota(I32, (CHUNK, CHUNK), 0)
    col = lax.broadcasted_iota(I32, (CHUNK, CHUNK), 1)
    tri = (col >= row) if reverse else (col <= row)
    tm = jnp.where(tri, 1.0, 0.0).astype(BF16)

    def gates(fpre):
        f = lb + (1.0 - lb) * jax.nn.sigmoid(fpre)
        logf = jnp.log(f)
        hi, mid, lo = _split3(logf)
        cum = _dot(tm, hi) + _dot(tm, mid) + _dot(tm, lo)
        last = cum[0:1] if reverse else cum[CHUNK - 1:CHUNK]
        return 1.0 - f, cum, last

    def update_state(k, cum, last, v):
        ks = (k * jnp.exp(last - cum)).astype(BF16)
        dec = jnp.exp(last)
        vb = v.astype(BF16)
        for h in range(heads):
            sl = slice(h * HEAD_DIM, (h + 1) * HEAD_DIM)
            st_ref[h] = st_ref[h] * dec[:, sl] + _dot_tn(vb[:, sl], ks[:, sl])

    order = lambda n: range(n - 1, -1, -1) if reverse else range(n)

    @pl.when(pl.program_id(1) == 0)
    def _():
        st_ref[...] = jnp.zeros_like(st_ref)
        for j in order(n_ctx):
            rows = slice(j * CHUNK, (j + 1) * CHUNK)
            k, cum, last = gates(cf_ref[0, rows, :])
            update_state(k, cum, last, cv_ref[0, rows, :])

    for j in order(tt // CHUNK):
        rows = slice(j * CHUNK, (j + 1) * CHUNK)
        k, cum, last = gates(f_ref[0, rows, :])
        v = v_ref[0, rows, :]
        qd = (_silu(q_ref[0, rows, :]) * jnp.exp(cum)).astype(BF16)
        ki = (k * jnp.exp(-cum)).astype(BF16)
        vb = v.astype(BF16)
        for h in range(heads):
            sl = slice(h * HEAD_DIM, (h + 1) * HEAD_DIM)
            sc = jnp.where(tri, _dot_nt(qd[:, sl], ki[:, sl]), 0.0).astype(BF16)
            o_ref[0, rows, sl] = _dot(sc, vb[:, sl]) + _dot_nt(qd[:, sl], st_ref[h].astype(BF16))
        update_state(k, cum, last, v)


def _hgrn(lbl, zc, z, *, reverse):
    bsz, l, _ = z.shape
    n_ctx_tok = zc.shape[1]
    w = lbl.shape[1]
    heads = w // HEAD_DIM
    tt = min(HGRN_TT, l)
    nt = l // tt
    fcol = 1 if reverse else 0
    tile = (lambda i: nt - 1 - i) if reverse else (lambda i: i)
    kern = functools.partial(_hgrn_kernel, reverse=reverse, n_ctx=n_ctx_tok // CHUNK, tt=tt)
    return pl.pallas_call(
        kern,
        grid=(bsz, nt),
        in_specs=[pl.BlockSpec(lbl.shape, lambda b, i: (0, 0)),
                  pl.BlockSpec((1, n_ctx_tok, w), lambda b, i: (b, 0, fcol)),
                  pl.BlockSpec((1, n_ctx_tok, w), lambda b, i: (b, 0, 2)),
                  pl.BlockSpec((1, tt, w), lambda b, i: (b, tile(i), fcol)),
                  pl.BlockSpec((1, tt, w), lambda b, i: (b, tile(i), 2)),
                  pl.BlockSpec((1, tt, w), lambda b, i: (b, tile(i), 3))],
        out_specs=pl.BlockSpec((1, tt, w), lambda b, i: (b, tile(i), 0)),
        out_shape=jax.ShapeDtypeStruct((bsz, l, w), F32),
        scratch_shapes=[pltpu.VMEM((heads, HEAD_DIM, HEAD_DIM), F32)],
        compiler_params=_cparams(("arbitrary", "arbitrary")),
        name="hgrn_bwd" if reverse else "hgrn_fwd",
    )(lbl, zc, zc, z, z, z)


def _mix_kernel(of_ref, ob_ref, zg_ref, za_ref, zt_ref, x_ref, g1_ref, sc2_ref, sh2_ref,
                hg_ref, dwk_ref, dwb_ref, lng_ref, lnb_ref, wo_ref, n2g_ref, wr_ref, br_ref,
                xm_ref, h2_ref, eid_ref, wts_ref, rank_ref, cnt_ref,
                pad_ref, cv_ref, carry_ref, *, tm):
    w = of_ref.shape[2]
    heads = w // HEAD_DIM

    @pl.when((pl.program_id(0) == 0) & (pl.program_id(1) == 0))
    def _():
        carry_ref[...] = jnp.zeros_like(carry_ref)

    o = of_ref[0] + ob_ref[0]
    parts = []
    for h in range(heads):
        oh = o[:, h * HEAD_DIM:(h + 1) * HEAD_DIM]
        parts.append(oh * lax.rsqrt(jnp.mean(oh * oh, axis=-1, keepdims=True) + EPS))
    o_mix = jnp.concatenate(parts, axis=-1) * hg_ref[...] * _silu(zg_ref[0])

    u = za_ref[0] * jax.nn.sigmoid(zt_ref[0])
    lo, hi = 16, 16 + GRID_W
    pad_ref[0:lo, :] = jnp.zeros((lo, u.shape[1]), F32)
    pad_ref[hi:hi + 16, :] = jnp.zeros((16, u.shape[1]), F32)
    for s in range(tm // GRID_W):
        pad_ref[lo:hi, :] = u[s * GRID_W:(s + 1) * GRID_W]
        acc = jnp.zeros((GRID_W, u.shape[1]), F32) + dwb_ref[...]
        for k in range(CONV_K):
            off = lo - CONV_PAD + k
            acc = acc + pad_ref[off:off + GRID_W, :] * dwk_ref[k:k + 1, :]
        cv_ref[s * GRID_W:(s + 1) * GRID_W, :] = acc
    cv = cv_ref[...]
    mu = jnp.mean(cv, axis=-1, keepdims=True)
    cen = cv - mu
    var = jnp.mean(cen * cen, axis=-1, keepdims=True)
    c_mix = _silu(cen * lax.rsqrt(var + EPS) * lng_ref[...] + lnb_ref[...])

    mix = _dot(o_mix.astype(BF16), wo_ref[0:w, :]) + _dot(c_mix.astype(BF16), wo_ref[w:, :])
    xm = x_ref[0] + g1_ref[0] * mix
    xm_ref[0] = xm

    ms = jnp.mean(xm * xm, axis=-1, keepdims=True)
    h2 = xm * lax.rsqrt(ms + EPS) * (n2g_ref[...] * (1.0 + sc2_ref[0])) + sh2_ref[0]
    h2_ref[0] = h2

    hh = h2.astype(BF16)
    hl = (h2 - hh.astype(F32)).astype(BF16)
    wr = wr_ref[...]
    wh = wr.astype(BF16)
    wl = (wr - wh.astype(F32)).astype(BF16)
    lg = _dot_nt(wh, hh) + _dot_nt(wh, hl) + _dot_nt(wl, hh) + br_ref[...]

    neg = -jnp.inf
    r8 = lax.broadcasted_iota(I32, (EXPERTS_PER_GROUP, tm), 0).astype(F32)
    gl = jnp.where(r8 < N_GROUPS, lg[0:8], neg)
    gmax = jnp.max(gl, axis=0, keepdims=True)
    grp = jnp.min(jnp.where(gl == gmax, r8, 8.0), axis=0, keepdims=True)
    p_grp = 1.0 / jnp.sum(jnp.exp(gl - gmax), axis=0, keepdims=True)
    es = lg[8:16]
    for g in range(1, N_GROUPS):
        es = jnp.where(grp == float(g), lg[8 + 8 * g:16 + 8 * g], es)
    m1 = jnp.max(es, axis=0, keepdims=True)
    i1 = jnp.min(jnp.where(es == m1, r8, 8.0), axis=0, keepdims=True)
    es2 = jnp.where(r8 == i1, neg, es)
    m2 = jnp.max(es2, axis=0, keepdims=True)
    i2 = jnp.min(jnp.where(es2 == m2, r8, 8.0), axis=0, keepdims=True)
    e2 = jnp.exp(m2 - m1)
    w1 = p_grp / (1.0 + e2)
    w2 = p_grp * e2 / (1.0 + e2)
    eid0 = grp * float(EXPERTS_PER_GROUP) + i1
    eid1 = grp * float(EXPERTS_PER_GROUP) + i2
    eid_ref[0:1, :] = eid0.astype(I32)
    eid_ref[1:2, :] = eid1.astype(I32)
    wts_ref[0:1, :] = w1
    wts_ref[1:2, :] = w2

    r32 = lax.broadcasted_iota(I32, (N_EXPERTS, tm), 0).astype(F32)
    oh0 = jnp.where(r32 == eid0, 1.0, 0.0)
    oh1 = jnp.where(r32 == eid1, 1.0, 0.0)
    src = lax.broadcasted_iota(I32, (tm, tm), 0)
    dst = lax.broadcasted_iota(I32, (tm, tm), 1)
    before = jnp.where(src < dst, 1.0, 0.0).astype(BF16)
    ex = _dot(jnp.concatenate([oh0, oh1], axis=0).astype(BF16), before)
    base = ex[0:N_EXPERTS] + ex[N_EXPERTS:] + carry_ref[...]
    rank_ref[0:1, :] = jnp.sum(oh0 * base, axis=0, keepdims=True).astype(I32)
    rank_ref[1:2, :] = jnp.sum(oh1 * base, axis=0, keepdims=True).astype(I32)
    carry = carry_ref[...] + jnp.sum(oh0 + oh1, axis=1, keepdims=True)
    carry_ref[...] = carry
    cnt_ref[...] = jnp.broadcast_to(carry, cnt_ref.shape)


def _mix(o_f, o_b, z, x, g1, sc2, sh2, hg, dwk, dwb, lng, lnb, wo, n2g, wr, br):
    bsz, l, d = x.shape
    w = o_f.shape[2]
    cw = dwk.shape[1]
    tm = min(MIX_TM, l)
    nt = l // tm
    n = bsz * l
    tok = lambda b, i: (b, i, 0)
    vec = lambda b, i: (0, 0)
    per_b = lambda b, i: (b, 0, 0)
    flat = lambda b, i: (0, b * nt + i)
    kern = functools.partial(_mix_kernel, tm=tm)
    return pl.pallas_call(
        kern,
        grid=(bsz, nt),
        in_specs=[pl.BlockSpec((1, tm, w), tok),
                  pl.BlockSpec((1, tm, w), tok),
                  pl.BlockSpec((1, tm, w), lambda b, i: (b, i, 4)),
                  pl.BlockSpec((1, tm, cw), lambda b, i: (b, i, 5)),
                  pl.BlockSpec((1, tm, cw), lambda b, i: (b, i, 6)),
                  pl.BlockSpec((1, tm, d), tok),
                  pl.BlockSpec((1, 1, d), per_b),
                  pl.BlockSpec((1, 1, d), per_b),
                  pl.BlockSpec((1, 1, d), per_b),
                  pl.BlockSpec((1, w), vec),
                  pl.BlockSpec((CONV_K, cw), vec),
                  pl.BlockSpec((1, cw), vec),
                  pl.BlockSpec((1, cw), vec),
                  pl.BlockSpec((1, cw), vec),
                  pl.BlockSpec((w + cw, d), vec),
                  pl.BlockSpec((1, d), vec),
                  pl.BlockSpec((ROUTER_ROWS, d), vec),
                  pl.BlockSpec((ROUTER_ROWS, 1), vec)],
        out_specs=[pl.BlockSpec((1, tm, d), tok),
                   pl.BlockSpec((1, tm, d), tok),
                   pl.BlockSpec((TOP_K, tm), flat),
                   pl.BlockSpec((TOP_K, tm), flat),
                   pl.BlockSpec((TOP_K, tm), flat),
                   pl.BlockSpec((N_EXPERTS, 128), vec)],
        out_shape=[jax.ShapeDtypeStruct((bsz, l, d), F32),
                   jax.ShapeDtypeStruct((bsz, l, d), F32),
                   jax.ShapeDtypeStruct((TOP_K, n), I32),
                   jax.ShapeDtypeStruct((TOP_K, n), F32),
                   jax.ShapeDtypeStruct((TOP_K, n), I32),
                   jax.ShapeDtypeStruct((N_EXPERTS, 128), F32)],
        scratch_shapes=[pltpu.VMEM((GRID_W + 32, cw), F32),
                        pltpu.VMEM((tm, cw), F32),
                        pltpu.VMEM((N_EXPERTS, 1), F32)],
        compiler_params=_cparams(("arbitrary", "arbitrary")),
        name="mix",
    )(o_f, o_b, z, z, z, x, g1, sc2, sh2, hg, dwk, dwb, lng, lnb, wo, n2g, wr, br)


def _pos_kernel(ps_ref, eid_ref, rank_ref, pos_ref):
    eid = eid_ref[...]
    acc = rank_ref[...]
    for e in range(N_EXPERTS):
        acc = acc + jnp.where(eid == e, ps_ref[e], 0)
    pos_ref[...] = acc


def _positions(pstarts, eid, rank):
    shape = eid.shape
    eid2 = eid.reshape(-1, 128)
    full = pl.BlockSpec(eid2.shape, lambda i: (0, 0))
    pos = pl.pallas_call(
        _pos_kernel,
        grid=(1,),
        in_specs=[pl.BlockSpec(memory_space=pltpu.SMEM), full, full],
        out_specs=full,
        out_shape=jax.ShapeDtypeStruct(eid2.shape, I32),
        name="positions",
    )(pstarts, eid2, rank.reshape(-1, 128))
    return pos.reshape(shape)


def _row_copy(src, s, dst, d, sem):
    return pltpu.make_async_copy(src.at[pl.ds(s, 1)], dst.at[pl.ds(d, 1)], sem)


def _dispatch_kernel(pos_hbm, h_ref, buf_in, buf_hbm, pos_smem, idx_sem, row_sem, *, tm):
    del buf_in
    i = pl.program_id(0)
    idx = pltpu.make_async_copy(pos_hbm.at[i], pos_smem, idx_sem)
    idx.start()
    idx.wait()

    def issue(t, carry):
        for k in range(TOP_K):
            _row_copy(h_ref, t, buf_hbm, pos_smem[0, k * tm + t], row_sem).start(priority=k)
        return carry

    lax.fori_loop(0, tm, issue, 0)

    def drain(t, carry):
        for k in range(TOP_K):
            _row_copy(h_ref, 0, buf_hbm, 0, row_sem).wait()
        return carry

    lax.fori_loop(0, tm, drain, 0)


def _dispatch(pos3, h2, buf0):
    n, d = h2.shape
    tm = pos3.shape[2] // TOP_K
    kern = functools.partial(_dispatch_kernel, tm=tm)
    return pl.pallas_call(
        kern,
        grid=(n // tm,),
        in_specs=[pl.BlockSpec(memory_space=pl.ANY),
                  pl.BlockSpec((tm, d), lambda i: (i, 0)),
                  pl.BlockSpec(memory_space=pl.ANY)],
        out_specs=pl.BlockSpec(memory_space=pl.ANY),
        out_shape=jax.ShapeDtypeStruct(buf0.shape, buf0.dtype),
        scratch_shapes=[pltpu.SMEM((1, TOP_K * tm), I32),
                        pltpu.SemaphoreType.DMA,
                        pltpu.SemaphoreType.DMA],
        input_output_aliases={2: 0},
        compiler_params=_cparams(("arbitrary",)),
        name="dispatch",
    )(pos3, h2, buf0)


def _ffn_kernel(blk_e_ref, nblk_ref, x_ref, wg_ref, wu_ref, wd_ref, y_ref):
    del blk_e_ref
    used = pl.program_id(0) < nblk_ref[0]

    @pl.when(used)
    def _():
        x = x_ref[...].astype(BF16)
        g = _dot(x, wg_ref[0].astype(BF16))
        u = _dot(x, wu_ref[0].astype(BF16))
        y_ref[...] = _dot((_silu(g) * u).astype(BF16), wd_ref[0].astype(BF16))

    @pl.when(jnp.logical_not(used))
    def _():
        y_ref[...] = jnp.zeros_like(y_ref)


def _ffn(blk_e, nblk, xbuf, wg, wu, wd):
    n_pad, d = xbuf.shape
    de = wg.shape[2]
    tb = FFN_TB
    grid_spec = pltpu.PrefetchScalarGridSpec(
        num_scalar_prefetch=2,
        grid=(n_pad // tb,),
        in_specs=[pl.BlockSpec((tb, d), lambda i, be, nb: (i, 0)),
                  pl.BlockSpec((1, d, de), lambda i, be, nb: (be[i], 0, 0)),
                  pl.BlockSpec((1, d, de), lambda i, be, nb: (be[i], 0, 0)),
                  pl.BlockSpec((1, de, d), lambda i, be, nb: (be[i], 0, 0))],
        out_specs=pl.BlockSpec((tb, d), lambda i, be, nb: (i, 0)),
    )
    return pl.pallas_call(
        _ffn_kernel,
        grid_spec=grid_spec,
        out_shape=jax.ShapeDtypeStruct((n_pad, d), F32),
        compiler_params=_cparams(("arbitrary",)),
        name="ffn",
    )(blk_e, nblk, xbuf, wg, wu, wd)


def _final_kernel(pos_hbm, ybuf_hbm, xm_ref, wt_ref, g2_ref, fg_ref, o_ref,
                  pos_smem, rows_ref, idx_sem, row_sem, *, tm, nt):
    i = pl.program_id(0) * nt + pl.program_id(1)
    idx = pltpu.make_async_copy(pos_hbm.at[i], pos_smem, idx_sem)
    idx.start()
    idx.wait()

    def issue(t, carry):
        for k in range(TOP_K):
            _row_copy(ybuf_hbm, pos_smem[0, k * tm + t], rows_ref.at[k], t, row_sem).start(priority=k)
        return carry

    lax.fori_loop(0, tm, issue, 0)

    def drain(t, carry):
        for k in range(TOP_K):
            _row_copy(ybuf_hbm, 0, rows_ref.at[k], 0, row_sem).wait()
        return carry

    lax.fori_loop(0, tm, drain, 0)

    wt = wt_ref[...]
    y = rows_ref[0] * wt[:, 0:1] + rows_ref[1] * wt[:, 1:2]
    x = xm_ref[0] + g2_ref[0] * y
    ms = jnp.mean(x * x, axis=-1, keepdims=True)
    o_ref[0] = x * lax.rsqrt(ms + EPS) * fg_ref[...]


def _final(pos3, ybuf, xm, wt, g2, fg):
    bsz, l, d = xm.shape
    tm = pos3.shape[2] // TOP_K
    nt = l // tm
    kern = functools.partial(_final_kernel, tm=tm, nt=nt)
    return pl.pallas_call(
        kern,
        grid=(bsz, nt),
        in_specs=[pl.BlockSpec(memory_space=pl.ANY),
                  pl.BlockSpec(memory_space=pl.ANY),
                  pl.BlockSpec((1, tm, d), lambda b, i: (b, i, 0)),
                  pl.BlockSpec((tm, TOP_K), lambda b, i: (b * nt + i, 0)),
                  pl.BlockSpec((1, 1, d), lambda b, i: (b, 0, 0)),
                  pl.BlockSpec((1, d), lambda b, i: (0, 0))],
        out_specs=pl.BlockSpec((1, tm, d), lambda b, i: (b, i, 0)),
        out_shape=jax.ShapeDtypeStruct((bsz, l, d), F32),
        scratch_shapes=[pltpu.SMEM((1, TOP_K * tm), I32),
                        pltpu.VMEM((TOP_K, tm, d), F32),
                        pltpu.SemaphoreType.DMA,
                        pltpu.SemaphoreType.DMA],
        compiler_params=_cparams(("arbitrary", "arbitrary")),
        name="final",
    )(pos3, ybuf, xm, wt, g2, fg)


def _layer(x, ctx, mod, mod_c, norm1_g, w_in, lb_logits, hgrn_norm_g, dw_kernel, dw_bias, conv_ln_g,
           conv_ln_b, w_out, norm2_g, wr, br, w_gate, w_up, w_down, final_norm_g):
    bsz, l, d = x.shape
    w = lb_logits.shape[2]
    heads = w // HEAD_DIM
    n = bsz * l
    row = lambda v: v.reshape(1, -1)
    per_b = lambda v: v.reshape(bsz, 1, d)
    sh1, sc1, g1, sh2, sc2, g2 = [per_b(mod[:, j]) for j in range(N_MOD)]
    csh1 = jnp.broadcast_to(mod_c[0].reshape(1, 1, d), (bsz, 1, d))
    csc1 = jnp.broadcast_to(mod_c[1].reshape(1, 1, d), (bsz, 1, d))

    w_in_b = w_in.astype(BF16)
    z = _inproj(x, row(norm1_g), sc1, sh1, w_in_b)
    zc = _inproj(ctx, row(norm1_g), csc1, csh1, w_in_b[:, :3 * w])
    o_f = _hgrn(lb_logits[0], zc, z, reverse=False)
    o_b = _hgrn(lb_logits[1], zc, z, reverse=True)

    xm, h2, eid, wts, rank, cnt = _mix(
        o_f, o_b, z, x, g1, sc2, sh2, row(jnp.tile(hgrn_norm_g, heads)), dw_kernel, row(dw_bias),
        row(conv_ln_g), row(conv_ln_b), w_out.astype(BF16), row(norm2_g), wr, br)

    tb = FFN_TB
    n_blk = -(-(n * TOP_K) // tb) + N_EXPERTS
    counts = cnt[:, 0].astype(I32)
    pcounts = (counts + tb - 1) // tb * tb
    pends = jnp.cumsum(pcounts)
    pstarts = pends - pcounts
    pos = _positions(pstarts.astype(I32), eid, rank)
    blk_start = jnp.arange(n_blk, dtype=I32) * tb
    blk_e = jnp.minimum(jnp.sum((pends[None, :] <= blk_start[:, None]).astype(I32), axis=1), N_EXPERTS - 1)
    nblk_used = (pends[-1:] // tb).astype(I32)

    tm = min(MOE_TM, l)
    pos3 = pos.reshape(TOP_K, n // tm, tm).transpose(1, 0, 2).reshape(n // tm, 1, TOP_K * tm)
    xbuf = _dispatch(pos3, h2.reshape(n, d), jnp.zeros((n_blk * tb, d), F32))
    ybuf = _ffn(blk_e, nblk_used, xbuf, w_gate, w_up, w_down)
    return _final(pos3, ybuf, xm, wts.T, g2, row(final_norm_g))


def kernel(x, c, ctx, c_ctx, w_ada, b_ada, norm1_g, w_in, lb_logits, hgrn_norm_g, dw_kernel, dw_bias,
           conv_ln_g, conv_ln_b, w_out, norm2_g, router_group_w, router_group_b, router_expert_w,
           router_expert_b, w_expert_gate, w_expert_up, w_expert_down, final_norm_g):
    depth = w_ada.shape[0]
    assert depth == 1, "context tokens are only updated between layers; a single layer is implemented"
    bsz, l, d = x.shape
    l0 = 0
    rows = -(-(bsz + 1) // 8) * 8
    cc = jnp.zeros((rows, d), F32).at[:bsz].set(c).at[bsz].set(c_ctx)
    mod_all = _ada(cc, w_ada[l0], b_ada[l0].reshape(1, -1))
    mod = mod_all[:bsz].reshape(bsz, N_MOD, d)
    mod_c = mod_all[bsz].reshape(N_MOD, d)
    pad8 = jnp.zeros((d, 8 - N_GROUPS), F32)
    wr = jnp.concatenate([router_group_w[l0], pad8, router_expert_w[l0], jnp.zeros((d, 8), F32)], axis=1).T
    br = jnp.concatenate([router_group_b[l0], jnp.zeros((8 - N_GROUPS,), F32), router_expert_b[l0],
                          jnp.zeros((8,), F32)]).reshape(ROUTER_ROWS, 1)
    return _layer(x, ctx, mod, mod_c, norm1_g[l0], w_in[l0], lb_logits, hgrn_norm_g[l0], dw_kernel[l0], dw_bias[l0], conv_ln_g[l0], conv_ln_b[l0], w_out[l0],
                  norm2_g[l0], wr, br, w_expert_gate[l0], w_expert_up[l0], w_expert_down[l0], final_norm_g)
```

```python
import functools

import jax
import jax.numpy as jnp
from jax import lax
from jax.experimental import pallas as pl
from jax.experimental.pallas import tpu as pltpu

F32 = jnp.float32
BF16 = jnp.bfloat16
I32 = jnp.int32

EPS = 1e-6
HEAD_DIM = 128
CHUNK = 64
GRID_W = 64
CONV_K = 31
CONV_PAD = CONV_K // 2
N_GROUPS = 4
EXPERTS_PER_GROUP = 8
N_EXPERTS = N_GROUPS * EXPERTS_PER_GROUP
TOP_K = 2
N_MOD = 6
ROUTER_ROWS = 48

ADA_TN = 1024
INPROJ_TM = 256
HGRN_TT = 256
MIX_TM = 256
MOE_TM = 256
FFN_TB = 512
VMEM_LIMIT = 48 * 1024 * 1024


def _cparams(sem):
    return pltpu.CompilerParams(dimension_semantics=sem, vmem_limit_bytes=VMEM_LIMIT)


def _silu(v):
    return v * jax.nn.sigmoid(v)


def _dot(a, b):
    return jnp.dot(a, b, preferred_element_type=F32)


def _dot_nt(a, b):
    return lax.dot_general(a, b, (((1,), (1,)), ((), ())), preferred_element_type=F32)


def _dot_tn(a, b):
    return lax.dot_general(a, b, (((0,), (0,)), ((), ())), preferred_element_type=F32)


def _split3(a):
    hi = a.astype(BF16)
    r = a - hi.astype(F32)
    mid = r.astype(BF16)
    lo = (r - mid.astype(F32)).astype(BF16)
    return hi, mid, lo


def _ada_kernel(c_ref, w_ref, b_ref, o_ref):
    s = _silu(c_ref[...]).astype(BF16)
    o_ref[...] = _dot(s, w_ref[...].astype(BF16)) + b_ref[...]


def _ada(cc, w, b):
    r, d = cc.shape
    nc = w.shape[1]
    return pl.pallas_call(
        _ada_kernel,
        grid=(nc // ADA_TN,),
        in_specs=[pl.BlockSpec((r, d), lambda j: (0, 0)),
                  pl.BlockSpec((d, ADA_TN), lambda j: (0, j)),
                  pl.BlockSpec((1, ADA_TN), lambda j: (0, j))],
        out_specs=pl.BlockSpec((r, ADA_TN), lambda j: (0, j)),
        out_shape=jax.ShapeDtypeStruct((r, nc), F32),
        compiler_params=_cparams(("arbitrary",)),
        name="ada",
    )(cc, w, b)


def _inproj_kernel(x_ref, g_ref, sc_ref, sh_ref, w_ref, o_ref):
    x = x_ref[0]
    ms = jnp.mean(x * x, axis=-1, keepdims=True)
    gs = g_ref[...] * (1.0 + sc_ref[0])
    h = x * lax.rsqrt(ms + EPS) * gs + sh_ref[0]
    o_ref[0] = _dot(h.astype(BF16), w_ref[...])


def _inproj(x, g, sc, sh, w):
    bsz, l, d = x.shape
    nc = w.shape[1]
    tm = min(INPROJ_TM, l)
    return pl.pallas_call(
        _inproj_kernel,
        grid=(bsz, l // tm),
        in_specs=[pl.BlockSpec((1, tm, d), lambda b, i: (b, i, 0)),
                  pl.BlockSpec((1, d), lambda b, i: (0, 0)),
                  pl.BlockSpec((1, 1, d), lambda b, i: (b, 0, 0)),
                  pl.BlockSpec((1, 1, d), lambda b, i: (b, 0, 0)),
                  pl.BlockSpec((d, nc), lambda b, i: (0, 0))],
        out_specs=pl.BlockSpec((1, tm, nc), lambda b, i: (b, i, 0)),
        out_shape=jax.ShapeDtypeStruct((bsz, l, nc), F32),
        compiler_params=_cparams(("arbitrary", "arbitrary")),
        name="inproj",
    )(x, g, sc, sh, w)


def _hgrn_kernel(lbl_ref, cf_ref, cv_ref, f_ref, v_ref, q_ref, o_ref, st_ref, *, reverse, n_ctx, tt):
    heads = st_ref.shape[0]
    a = lbl_ref[...]
    e = jnp.exp(a - jnp.max(a, axis=0, keepdims=True))
    lb = e[0:1] / jnp.sum(e, axis=0, keepdims=True)
    row = lax.broadcasted_iota(I32, (CHUNK, CHUNK), 0)
    col = lax.broadcasted_iota(I32, (CHUNK, CHUNK), 1)
    tri = (col >= row) if reverse else (col <= row)
    tm = jnp.where(tri, 1.0, 0.0).astype(BF16)

    def gates(fpre):
        f = lb + (1.0 - lb) * jax.nn.sigmoid(fpre)
        logf = jnp.log(f)
        hi, mid, lo = _split3(logf)
        cum = _dot(tm, hi) + _dot(tm, mid) + _dot(tm, lo)
        last = cum[0:1] if reverse else cum[CHUNK - 1:CHUNK]
        return 1.0 - f, cum, last

    def update_state(k, cum, last, v):
        ks = (k * jnp.exp(last - cum)).astype(BF16)
        dec = jnp.exp(last)
        vb = v.astype(BF16)
        for h in range(heads):
            sl = slice(h * HEAD_DIM, (h + 1) * HEAD_DIM)
            st_ref[h] = st_ref[h] * dec[:, sl] + _dot_tn(vb[:, sl], ks[:, sl])

    order = lambda n: range(n - 1, -1, -1) if reverse else range(n)

    @pl.when(pl.program_id(1) == 0)
    def _():
        st_ref[...] = jnp.zeros_like(st_ref)
        for j in order(n_ctx):
            rows = slice(j * CHUNK, (j + 1) * CHUNK)
            k, cum, last = gates(cf_ref[0, rows, :])
            update_state(k, cum, last, cv_ref[0, rows, :])

    for j in order(tt // CHUNK):
        rows = slice(j * CHUNK, (j + 1) * CHUNK)
        k, cum, last = gates(f_ref[0, rows, :])
        v = v_ref[0, rows, :]
        qd = (_silu(q_ref[0, rows, :]) * jnp.exp(cum)).astype(BF16)
        ki = (k * jnp.exp(-cum)).astype(BF16)
        vb = v.astype(BF16)
        for h in range(heads):
            sl = slice(h * HEAD_DIM, (h + 1) * HEAD_DIM)
            sc = jnp.where(tri, _dot_nt(qd[:, sl], ki[:, sl]), 0.0).astype(BF16)
            o_ref[0, rows, sl] = _dot(sc, vb[:, sl]) + _dot_nt(qd[:, sl], st_ref[h].astype(BF16))
        update_state(k, cum, last, v)


def _hgrn(lbl, zc, z, *, reverse):
    bsz, l, _ = z.shape
    n_ctx_tok = zc.shape[1]
    w = lbl.shape[1]
    heads = w // HEAD_DIM
    tt = min(HGRN_TT, l)
    nt = l // tt
    fcol = 1 if reverse else 0
    tile = (lambda i: nt - 1 - i) if reverse else (lambda i: i)
    kern = functools.partial(_hgrn_kernel, reverse=reverse, n_ctx=n_ctx_tok // CHUNK, tt=tt)
    return pl.pallas_call(
        kern,
        grid=(bsz, nt),
        in_specs=[pl.BlockSpec(lbl.shape, lambda b, i: (0, 0)),
                  pl.BlockSpec((1, n_ctx_tok, w), lambda b, i: (b, 0, fcol)),
                  pl.BlockSpec((1, n_ctx_tok, w), lambda b, i: (b, 0, 2)),
                  pl.BlockSpec((1, tt, w), lambda b, i: (b, tile(i), fcol)),
                  pl.BlockSpec((1, tt, w), lambda b, i: (b, tile(i), 2)),
                  pl.BlockSpec((1, tt, w), lambda b, i: (b, tile(i), 3))],
        out_specs=pl.BlockSpec((1, tt, w), lambda b, i: (b, tile(i), 0)),
        out_shape=jax.ShapeDtypeStruct((bsz, l, w), F32),
        scratch_shapes=[pltpu.VMEM((heads, HEAD_DIM, HEAD_DIM), F32)],
        compiler_params=_cparams(("arbitrary", "arbitrary")),
        name="hgrn_bwd" if reverse else "hgrn_fwd",
    )(lbl, zc, zc, z, z, z)


def _mix_kernel(of_ref, ob_ref, zg_ref, za_ref, zt_ref, x_ref, g1_ref, sc2_ref, sh2_ref,
                hg_ref, dwk_ref, dwb_ref, lng_ref, lnb_ref, wo_ref, n2g_ref, wr_ref, br_ref,
                xm_ref, h2_ref, eid_ref, wts_ref, rank_ref, cnt_ref,
                pad_ref, shf_ref, cv_ref, carry_ref, *, tm):
    w = of_ref.shape[2]
    heads = w // HEAD_DIM

    @pl.when((pl.program_id(0) == 0) & (pl.program_id(1) == 0))
    def _():
        carry_ref[...] = jnp.zeros_like(carry_ref)

    o = of_ref[0] + ob_ref[0]
    parts = []
    for h in range(heads):
        oh = o[:, h * HEAD_DIM:(h + 1) * HEAD_DIM]
        parts.append(oh * lax.rsqrt(jnp.mean(oh * oh, axis=-1, keepdims=True) + EPS))
    o_mix = jnp.concatenate(parts, axis=-1) * hg_ref[...] * _silu(zg_ref[0])

    u = za_ref[0] * jax.nn.sigmoid(zt_ref[0])
    lo, hi = 16, 16 + GRID_W
    pad_ref[0:lo, :] = jnp.zeros((lo, u.shape[1]), F32)
    pad_ref[hi:hi + 16, :] = jnp.zeros((16, u.shape[1]), F32)
    span = shf_ref.shape[1]
    for s in range(tm // GRID_W):
        pad_ref[lo:hi, :] = u[s * GRID_W:(s + 1) * GRID_W]
        for b in range(1, 8):
            shf_ref[b] = pad_ref[b:b + span, :]
        acc = jnp.zeros((GRID_W, u.shape[1]), F32) + dwb_ref[...]
        for k in range(CONV_K):
            a, b = divmod(lo - CONV_PAD + k, 8)
            src = pad_ref if b == 0 else shf_ref.at[b]
            acc = acc + src[8 * a:8 * a + GRID_W, :] * dwk_ref[k:k + 1, :]
        cv_ref[s * GRID_W:(s + 1) * GRID_W, :] = acc
    cv = cv_ref[...]
    mu = jnp.mean(cv, axis=-1, keepdims=True)
    cen = cv - mu
    var = jnp.mean(cen * cen, axis=-1, keepdims=True)
    c_mix = _silu(cen * lax.rsqrt(var + EPS) * lng_ref[...] + lnb_ref[...])

    mix = _dot(o_mix.astype(BF16), wo_ref[0:w, :]) + _dot(c_mix.astype(BF16), wo_ref[w:, :])
    xm = x_ref[0] + g1_ref[0] * mix
    xm_ref[0] = xm

    ms = jnp.mean(xm * xm, axis=-1, keepdims=True)
    h2 = xm * lax.rsqrt(ms + EPS) * (n2g_ref[...] * (1.0 + sc2_ref[0])) + sh2_ref[0]
    h2_ref[0] = h2

    hh = h2.astype(BF16)
    hl = (h2 - hh.astype(F32)).astype(BF16)
    wr = wr_ref[...]
    wh = wr.astype(BF16)
    wl = (wr - wh.astype(F32)).astype(BF16)
    lg = _dot_nt(wh, hh) + _dot_nt(wh, hl) + _dot_nt(wl, hh) + br_ref[...]

    neg = -jnp.inf
    r8 = lax.broadcasted_iota(I32, (EXPERTS_PER_GROUP, tm), 0).astype(F32)
    gl = jnp.where(r8 < N_GROUPS, lg[0:8], neg)
    gmax = jnp.max(gl, axis=0, keepdims=True)
    grp = jnp.min(jnp.where(gl == gmax, r8, 8.0), axis=0, keepdims=True)
    p_grp = 1.0 / jnp.sum(jnp.exp(gl - gmax), axis=0, keepdims=True)
    es = lg[8:16]
    for g in range(1, N_GROUPS):
        es = jnp.where(grp == float(g), lg[8 + 8 * g:16 + 8 * g], es)
    m1 = jnp.max(es, axis=0, keepdims=True)
    i1 = jnp.min(jnp.where(es == m1, r8, 8.0), axis=0, keepdims=True)
    es2 = jnp.where(r8 == i1, neg, es)
    m2 = jnp.max(es2, axis=0, keepdims=True)
    i2 = jnp.min(jnp.where(es2 == m2, r8, 8.0), axis=0, keepdims=True)
    e2 = jnp.exp(m2 - m1)
    w1 = p_grp / (1.0 + e2)
    w2 = p_grp * e2 / (1.0 + e2)
    eid0 = grp * float(EXPERTS_PER_GROUP) + i1
    eid1 = grp * float(EXPERTS_PER_GROUP) + i2
    eid_ref[0:1, :] = eid0.astype(I32)
    eid_ref[1:2, :] = eid1.astype(I32)
    wts_ref[0:1, :] = w1
    wts_ref[1:2, :] = w2

    r32 = lax.broadcasted_iota(I32, (N_EXPERTS, tm), 0).astype(F32)
    oh0 = jnp.where(r32 == eid0, 1.0, 0.0)
    oh1 = jnp.where(r32 == eid1, 1.0, 0.0)
    src = lax.broadcasted_iota(I32, (tm, tm), 0)
    dst = lax.broadcasted_iota(I32, (tm, tm), 1)
    before = jnp.where(src < dst, 1.0, 0.0).astype(BF16)
    ex = _dot(jnp.concatenate([oh0, oh1], axis=0).astype(BF16), before)
    base = ex[0:N_EXPERTS] + ex[N_EXPERTS:] + carry_ref[...]
    rank_ref[0:1, :] = jnp.sum(oh0 * base, axis=0, keepdims=True).astype(I32)
    rank_ref[1:2, :] = jnp.sum(oh1 * base, axis=0, keepdims=True).astype(I32)
    carry = carry_ref[...] + jnp.sum(oh0 + oh1, axis=1, keepdims=True)
    carry_ref[...] = carry
    cnt_ref[...] = jnp.broadcast_to(carry, cnt_ref.shape)


def _mix(o_f, o_b, z, x, g1, sc2, sh2, hg, dwk, dwb, lng, lnb, wo, n2g, wr, br):
    bsz, l, d = x.shape
    w = o_f.shape[2]
    cw = dwk.shape[1]
    tm = min(MIX_TM, l)
    nt = l // tm
    n = bsz * l
    tok = lambda b, i: (b, i, 0)
    vec = lambda b, i: (0, 0)
    per_b = lambda b, i: (b, 0, 0)
    flat = lambda b, i: (0, b * nt + i)
    kern = functools.partial(_mix_kernel, tm=tm)
    return pl.pallas_call(
        kern,
        grid=(bsz, nt),
        in_specs=[pl.BlockSpec((1, tm, w), tok),
                  pl.BlockSpec((1, tm, w), tok),
                  pl.BlockSpec((1, tm, w), lambda b, i: (b, i, 4)),
                  pl.BlockSpec((1, tm, cw), lambda b, i: (b, i, 5)),
                  pl.BlockSpec((1, tm, cw), lambda b, i: (b, i, 6)),
                  pl.BlockSpec((1, tm, d), tok),
                  pl.BlockSpec((1, 1, d), per_b),
                  pl.BlockSpec((1, 1, d), per_b),
                  pl.BlockSpec((1, 1, d), per_b),
                  pl.BlockSpec((1, w), vec),
                  pl.BlockSpec((CONV_K, cw), vec),
                  pl.BlockSpec((1, cw), vec),
                  pl.BlockSpec((1, cw), vec),
                  pl.BlockSpec((1, cw), vec),
                  pl.BlockSpec((w + cw, d), vec),
                  pl.BlockSpec((1, d), vec),
                  pl.BlockSpec((ROUTER_ROWS, d), vec),
                  pl.BlockSpec((ROUTER_ROWS, 1), vec)],
        out_specs=[pl.BlockSpec((1, tm, d), tok),
                   pl.BlockSpec((1, tm, d), tok),
                   pl.BlockSpec((TOP_K, tm), flat),
                   pl.BlockSpec((TOP_K, tm), flat),
                   pl.BlockSpec((TOP_K, tm), flat),
                   pl.BlockSpec((N_EXPERTS, 128), vec)],
        out_shape=[jax.ShapeDtypeStruct((bsz, l, d), F32),
                   jax.ShapeDtypeStruct((bsz, l, d), F32),
                   jax.ShapeDtypeStruct((TOP_K, n), I32),
                   jax.ShapeDtypeStruct((TOP_K, n), F32),
                   jax.ShapeDtypeStruct((TOP_K, n), I32),
                   jax.ShapeDtypeStruct((N_EXPERTS, 128), F32)],
        scratch_shapes=[pltpu.VMEM((GRID_W + 32, cw), F32),
                        pltpu.VMEM((8, GRID_W + 24, cw), F32),
                        pltpu.VMEM((tm, cw), F32),
                        pltpu.VMEM((N_EXPERTS, 1), F32)],
        compiler_params=_cparams(("arbitrary", "arbitrary")),
        name="mix",
    )(o_f, o_b, z, z, z, x, g1, sc2, sh2, hg, dwk, dwb, lng, lnb, wo, n2g, wr, br)


def _pos_kernel(ps_ref, eid_ref, rank_ref, pos_ref):
    eid = eid_ref[...]
    acc = rank_ref[...]
    for e in range(N_EXPERTS):
        acc = acc + jnp.where(eid == e, ps_ref[e], 0)
    pos_ref[...] = acc


def _positions(pstarts, eid, rank):
    shape = eid.shape
    eid2 = eid.reshape(-1, 128)
    full = pl.BlockSpec(eid2.shape, lambda i: (0, 0))
    pos = pl.pallas_call(
        _pos_kernel,
        grid=(1,),
        in_specs=[pl.BlockSpec(memory_space=pltpu.SMEM), full, full],
        out_specs=full,
        out_shape=jax.ShapeDtypeStruct(eid2.shape, I32),
        name="positions",
    )(pstarts, eid2, rank.reshape(-1, 128))
    return pos.reshape(shape)


def _row_copy(src, s, dst, d, sem):
    return pltpu.make_async_copy(src.at[pl.ds(s, 1)], dst.at[pl.ds(d, 1)], sem)


def _dispatch_kernel(pstart_ref, pcnt_ref, pos_hbm, h_ref, buf_hbm, pos_smem, zero_ref, idx_sem, row_sem,
                     zero_sem, *, tm, tb):
    i = pl.program_id(0)
    n_blk = buf_hbm.shape[0] // tb

    @pl.when(i == 0)
    def _():
        zero_ref[...] = jnp.zeros_like(zero_ref)

        def zero_copy(e):
            start = pl.multiple_of(pstart_ref[e] + pcnt_ref[e] - tb, 8)
            return pltpu.make_async_copy(zero_ref, buf_hbm.at[pl.ds(start, tb)], zero_sem)

        n_used = (pstart_ref[N_EXPERTS - 1] + pcnt_ref[N_EXPERTS - 1]) // tb

        def tail_copy(j):
            start = pl.multiple_of((n_used + j) * tb, 8)
            return pltpu.make_async_copy(zero_ref, buf_hbm.at[pl.ds(start, tb)], zero_sem)

        for e in range(N_EXPERTS):
            pl.when(pcnt_ref[e] > 0)(lambda e=e: zero_copy(e).start())
            pl.when(n_used + e < n_blk)(lambda e=e: tail_copy(e).start())
        for e in range(N_EXPERTS):
            pl.when(pcnt_ref[e] > 0)(lambda e=e: zero_copy(e).wait())
            pl.when(n_used + e < n_blk)(lambda e=e: tail_copy(e).wait())

    idx = pltpu.make_async_copy(pos_hbm.at[i], pos_smem, idx_sem)
    idx.start()
    idx.wait()

    def issue(t, carry):
        for k in range(TOP_K):
            _row_copy(h_ref, t, buf_hbm, pos_smem[0, k * tm + t], row_sem).start()
        return carry

    lax.fori_loop(0, tm, issue, 0, unroll=8)
    for _ in range(TOP_K * tm):
        _row_copy(h_ref, 0, buf_hbm, 0, row_sem).wait()


def _dispatch(pstarts, pcounts, pos3, h2, n_pad, tb):
    n, d = h2.shape
    tm = pos3.shape[2] // TOP_K
    kern = functools.partial(_dispatch_kernel, tm=tm, tb=tb)
    grid_spec = pltpu.PrefetchScalarGridSpec(
        num_scalar_prefetch=2,
        grid=(n // tm,),
        in_specs=[pl.BlockSpec(memory_space=pl.ANY),
                  pl.BlockSpec((tm, d), lambda i, ps, pc: (i, 0))],
        out_specs=pl.BlockSpec(memory_space=pl.ANY),
        scratch_shapes=[pltpu.SMEM((1, TOP_K * tm), I32),
                        pltpu.VMEM((tb, d), F32),
                        pltpu.SemaphoreType.DMA,
                        pltpu.SemaphoreType.DMA,
                        pltpu.SemaphoreType.DMA],
    )
    return pl.pallas_call(
        kern,
        grid_spec=grid_spec,
        out_shape=jax.ShapeDtypeStruct((n_pad, d), F32),
        compiler_params=_cparams(("arbitrary",)),
        name="dispatch",
    )(pstarts, pcounts, pos3, h2)


def _ffn_kernel(blk_e_ref, nblk_ref, x_ref, wg_ref, wu_ref, wd_ref, y_ref, wgb_ref, wub_ref, wdb_ref):
    i = pl.program_id(0)
    used = i < nblk_ref[0]
    new_expert = (i == 0) | (blk_e_ref[i] != blk_e_ref[jnp.maximum(i - 1, 0)])

    @pl.when(used & new_expert)
    def _():
        wgb_ref[...] = wg_ref[0].astype(BF16)
        wub_ref[...] = wu_ref[0].astype(BF16)
        wdb_ref[...] = wd_ref[0].astype(BF16)

    @pl.when(used)
    def _():
        x = x_ref[...].astype(BF16)
        g = _dot(x, wgb_ref[...])
        u = _dot(x, wub_ref[...])
        y_ref[...] = _dot((_silu(g) * u).astype(BF16), wdb_ref[...])

    @pl.when(jnp.logical_not(used))
    def _():
        y_ref[...] = jnp.zeros_like(y_ref)


def _ffn(blk_e, nblk, xbuf, wg, wu, wd, tb):
    n_pad, d = xbuf.shape
    de = wg.shape[2]
    x_blk = lambda i, be, nb: (jnp.minimum(i, nb[0] - 1), 0)
    grid_spec = pltpu.PrefetchScalarGridSpec(
        num_scalar_prefetch=2,
        grid=(n_pad // tb,),
        in_specs=[pl.BlockSpec((tb, d), x_blk),
                  pl.BlockSpec((1, d, de), lambda i, be, nb: (be[i], 0, 0)),
                  pl.BlockSpec((1, d, de), lambda i, be, nb: (be[i], 0, 0)),
                  pl.BlockSpec((1, de, d), lambda i, be, nb: (be[i], 0, 0))],
        out_specs=pl.BlockSpec((tb, d), lambda i, be, nb: (i, 0)),
        scratch_shapes=[pltpu.VMEM((d, de), BF16),
                        pltpu.VMEM((d, de), BF16),
                        pltpu.VMEM((de, d), BF16)],
    )
    return pl.pallas_call(
        _ffn_kernel,
        grid_spec=grid_spec,
        out_shape=jax.ShapeDtypeStruct((n_pad, d), F32),
        compiler_params=_cparams(("arbitrary",)),
        name="ffn",
    )(blk_e, nblk, xbuf, wg, wu, wd)


def _final_kernel(pos_hbm, ybuf_hbm, xm_ref, wt_ref, g2_ref, fg_ref, o_ref,
                  pos_smem, rows_ref, idx_sem, row_sem, *, tm, nt):
    i = pl.program_id(0) * nt + pl.program_id(1)
    idx = pltpu.make_async_copy(pos_hbm.at[i], pos_smem, idx_sem)
    idx.start()
    idx.wait()

    def issue(t, carry):
        for k in range(TOP_K):
            _row_copy(ybuf_hbm, pos_smem[0, k * tm + t], rows_ref.at[k], t, row_sem).start()
        return carry

    lax.fori_loop(0, tm, issue, 0, unroll=8)
    for _ in range(TOP_K * tm):
        _row_copy(ybuf_hbm, 0, rows_ref.at[0], 0, row_sem).wait()

    wt = wt_ref[...]
    y = rows_ref[0] * wt[:, 0:1] + rows_ref[1] * wt[:, 1:2]
    x = xm_ref[0] + g2_ref[0] * y
    ms = jnp.mean(x * x, axis=-1, keepdims=True)
    o_ref[0] = x * lax.rsqrt(ms + EPS) * fg_ref[...]


def _final(pos3, ybuf, xm, wt, g2, fg):
    bsz, l, d = xm.shape
    tm = pos3.shape[2] // TOP_K
    nt = l // tm
    kern = functools.partial(_final_kernel, tm=tm, nt=nt)
    return pl.pallas_call(
        kern,
        grid=(bsz, nt),
        in_specs=[pl.BlockSpec(memory_space=pl.ANY),
                  pl.BlockSpec(memory_space=pl.ANY),
                  pl.BlockSpec((1, tm, d), lambda b, i: (b, i, 0)),
                  pl.BlockSpec((tm, TOP_K), lambda b, i: (b * nt + i, 0)),
                  pl.BlockSpec((1, 1, d), lambda b, i: (b, 0, 0)),
                  pl.BlockSpec((1, d), lambda b, i: (0, 0))],
        out_specs=pl.BlockSpec((1, tm, d), lambda b, i: (b, i, 0)),
        out_shape=jax.ShapeDtypeStruct((bsz, l, d), F32),
        scratch_shapes=[pltpu.SMEM((1, TOP_K * tm), I32),
                        pltpu.VMEM((TOP_K, tm, d), F32),
                        pltpu.SemaphoreType.DMA,
                        pltpu.SemaphoreType.DMA],
        compiler_params=_cparams(("arbitrary", "arbitrary")),
        name="final",
    )(pos3, ybuf, xm, wt, g2, fg)


def _layer(x, ctx, mod, mod_c, norm1_g, w_in, lb_logits, hgrn_norm_g, dw_kernel, dw_bias, conv_ln_g,
           conv_ln_b, w_out, norm2_g, wr, br, w_gate, w_up, w_down, final_norm_g):
    bsz, l, d = x.shape
    w = lb_logits.shape[2]
    heads = w // HEAD_DIM
    n = bsz * l
    row = lambda v: v.reshape(1, -1)
    per_b = lambda v: v.reshape(bsz, 1, d)
    sh1, sc1, g1, sh2, sc2, g2 = [per_b(mod[:, j]) for j in range(N_MOD)]
    csh1 = jnp.broadcast_to(mod_c[0].reshape(1, 1, d), (bsz, 1, d))
    csc1 = jnp.broadcast_to(mod_c[1].reshape(1, 1, d), (bsz, 1, d))

    w_in_b = w_in.astype(BF16)
    z = _inproj(x, row(norm1_g), sc1, sh1, w_in_b)
    zc = _inproj(ctx, row(norm1_g), csc1, csh1, w_in_b[:, :3 * w])
    o_f = _hgrn(lb_logits[0], zc, z, reverse=False)
    o_b = _hgrn(lb_logits[1], zc, z, reverse=True)

    xm, h2, eid, wts, rank, cnt = _mix(
        o_f, o_b, z, x, g1, sc2, sh2, row(jnp.tile(hgrn_norm_g, heads)), dw_kernel, row(dw_bias),
        row(conv_ln_g), row(conv_ln_b), w_out.astype(BF16), row(norm2_g), wr, br)

    tb = FFN_TB
    n_blk = -(-(n * TOP_K) // tb) + N_EXPERTS
    counts = cnt[:, 0].astype(I32)
    pcounts = (counts + tb - 1) // tb * tb
    pends = jnp.cumsum(pcounts)
    pstarts = pends - pcounts
    pos = _positions(pstarts.astype(I32), eid, rank)
    blk_start = jnp.arange(n_blk, dtype=I32) * tb
    blk_e = jnp.minimum(jnp.sum((pends[None, :] <= blk_start[:, None]).astype(I32), axis=1), N_EXPERTS - 1)
    nblk_used = (pends[-1:] // tb).astype(I32)

    tm = min(MOE_TM, l)
    pos3 = pos.reshape(TOP_K, n // tm, tm).transpose(1, 0, 2).reshape(n // tm, 1, TOP_K * tm)
    xbuf = _dispatch(pstarts.astype(I32), pcounts, pos3, h2.reshape(n, d), n_blk * tb, tb)
    ybuf = _ffn(blk_e, nblk_used, xbuf, w_gate, w_up, w_down, tb)
    return _final(pos3, ybuf, xm, wts.T, g2, row(final_norm_g))


def kernel(x, c, ctx, c_ctx, w_ada, b_ada, norm1_g, w_in, lb_logits, hgrn_norm_g, dw_kernel, dw_bias,
           conv_ln_g, conv_ln_b, w_out, norm2_g, router_group_w, router_group_b, router_expert_w,
           router_expert_b, w_expert_gate, w_expert_up, w_expert_down, final_norm_g):
    depth = w_ada.shape[0]
    assert depth == 1, "context tokens are only updated between layers; a single layer is implemented"
    bsz, l, d = x.shape
    l0 = 0
    rows = -(-(bsz + 1) // 8) * 8
    cc = jnp.zeros((rows, d), F32).at[:bsz].set(c).at[bsz].set(c_ctx)
    mod_all = _ada(cc, w_ada[l0], b_ada[l0].reshape(1, -1))
    mod = mod_all[:bsz].reshape(bsz, N_MOD, d)
    mod_c = mod_all[bsz].reshape(N_MOD, d)
    pad8 = jnp.zeros((d, 8 - N_GROUPS), F32)
    wr = jnp.concatenate([router_group_w[l0], pad8, router_expert_w[l0], jnp.zeros((d, 8), F32)], axis=1).T
    br = jnp.concatenate([router_group_b[l0], jnp.zeros((8 - N_GROUPS,), F32), router_expert_b[l0],
                          jnp.zeros((8,), F32)]).reshape(ROUTER_ROWS, 1)
    return _layer(x, ctx, mod, mod_c, norm1_g[l0], w_in[l0], lb_logits, hgrn_norm_g[l0], dw_kernel[l0], dw_bias[l0], conv_ln_g[l0], conv_ln_b[l0], w_out[l0],
                  norm2_g[l0], wr, br, w_expert_gate[l0], w_expert_up[l0], w_expert_down[l0], final_norm_g)
```

```python
import functools

import jax
import jax.numpy as jnp
from jax import lax
from jax.experimental import pallas as pl
from jax.experimental.pallas import tpu as pltpu
from jax.experimental.pallas import tpu_sc as plsc

F32 = jnp.float32
BF16 = jnp.bfloat16
I32 = jnp.int32
U32 = jnp.uint32

EPS = 1e-6
HEAD_DIM = 128
CHUNK = 64
GRID_W = 64
CONV_K = 31
CONV_PAD = CONV_K // 2
N_GROUPS = 4
EXPERTS_PER_GROUP = 8
N_EXPERTS = N_GROUPS * EXPERTS_PER_GROUP
TOP_K = 2
N_MOD = 6
ROUTER_ROWS = 48

ADA_TN = 1024
INPROJ_TM = 256
HGRN_TT = 256
MIX_TM = 256
MOE_TM = 256
FFN_TB = 512
VMEM_LIMIT = 48 * 1024 * 1024
SC_CORES = 2
SC_SUBCORES = 16
SC_W = 128


def _cparams(sem):
    return pltpu.CompilerParams(dimension_semantics=sem, vmem_limit_bytes=VMEM_LIMIT)


def _silu(v):
    return v * jax.nn.sigmoid(v)


def _dot(a, b):
    return jnp.dot(a, b, preferred_element_type=F32)


def _dot_nt(a, b):
    return lax.dot_general(a, b, (((1,), (1,)), ((), ())), preferred_element_type=F32)


def _dot_tn(a, b):
    return lax.dot_general(a, b, (((0,), (0,)), ((), ())), preferred_element_type=F32)


def _pack_halves(v):
    m = v.shape[1] // 2
    lo = lax.bitcast_convert_type(v[:, :m].astype(BF16).astype(F32), U32) >> 16
    hi = lax.bitcast_convert_type(v[:, m:].astype(BF16).astype(F32), U32) & jnp.uint32(0xFFFF0000)
    return lax.bitcast_convert_type(lo | hi, F32)


def _unpack_halves(p):
    u = lax.bitcast_convert_type(p, U32)
    lo = lax.bitcast_convert_type(u << 16, F32)
    hi = lax.bitcast_convert_type(u & jnp.uint32(0xFFFF0000), F32)
    return lo.astype(BF16), hi.astype(BF16)


def _split3(a):
    hi = a.astype(BF16)
    r = a - hi.astype(F32)
    mid = r.astype(BF16)
    lo = (r - mid.astype(F32)).astype(BF16)
    return hi, mid, lo


def _ada_kernel(c_ref, w_ref, b_ref, o_ref):
    s = _silu(c_ref[...]).astype(BF16)
    o_ref[...] = _dot(s, w_ref[...].astype(BF16)) + b_ref[...]


def _ada(cc, w, b):
    r, d = cc.shape
    nc = w.shape[1]
    return pl.pallas_call(
        _ada_kernel,
        grid=(nc // ADA_TN,),
        in_specs=[pl.BlockSpec((r, d), lambda j: (0, 0)),
                  pl.BlockSpec((d, ADA_TN), lambda j: (0, j)),
                  pl.BlockSpec((1, ADA_TN), lambda j: (0, j))],
        out_specs=pl.BlockSpec((r, ADA_TN), lambda j: (0, j)),
        out_shape=jax.ShapeDtypeStruct((r, nc), F32),
        compiler_params=_cparams(("arbitrary",)),
        name="ada",
    )(cc, w, b)


def _inproj_kernel(x_ref, g_ref, sc_ref, sh_ref, w_ref, o_ref):
    x = x_ref[0]
    ms = jnp.mean(x * x, axis=-1, keepdims=True)
    gs = g_ref[...] * (1.0 + sc_ref[0])
    h = x * lax.rsqrt(ms + EPS) * gs + sh_ref[0]
    o_ref[0] = _dot(h.astype(BF16), w_ref[...])


def _inproj(x, g, sc, sh, w):
    bsz, l, d = x.shape
    nc = w.shape[1]
    tm = min(INPROJ_TM, l)
    return pl.pallas_call(
        _inproj_kernel,
        grid=(bsz, l // tm),
        in_specs=[pl.BlockSpec((1, tm, d), lambda b, i: (b, i, 0)),
                  pl.BlockSpec((1, d), lambda b, i: (0, 0)),
                  pl.BlockSpec((1, 1, d), lambda b, i: (b, 0, 0)),
                  pl.BlockSpec((1, 1, d), lambda b, i: (b, 0, 0)),
                  pl.BlockSpec((d, nc), lambda b, i: (0, 0))],
        out_specs=pl.BlockSpec((1, tm, nc), lambda b, i: (b, i, 0)),
        out_shape=jax.ShapeDtypeStruct((bsz, l, nc), F32),
        compiler_params=_cparams(("arbitrary", "arbitrary")),
        name="inproj",
    )(x, g, sc, sh, w)


def _hgrn_kernel(lbl_ref, cf_ref, cv_ref, f_ref, v_ref, q_ref, o_ref, st_ref, *, reverse, n_ctx, tt):
    heads = st_ref.shape[0]
    a = lbl_ref[...]
    e = jnp.exp(a - jnp.max(a, axis=0, keepdims=True))
    lb = e[0:1] / jnp.sum(e, axis=0, keepdims=True)
    row = lax.broadcasted_iota(I32, (CHUNK, CHUNK), 0)
    col = lax.broadcasted_iota(I32, (CHUNK, CHUNK), 1)
    tri = (col >= row) if reverse else (col <= row)
    tm = jnp.where(tri, 1.0, 0.0).astype(BF16)

    def gates(fpre):
        f = lb + (1.0 - lb) * jax.nn.sigmoid(fpre)
        logf = jnp.log(f)
        hi, mid, lo = _split3(logf)
        cum = _dot(tm, hi) + _dot(tm, mid) + _dot(tm, lo)
        last = cum[0:1] if reverse else cum[CHUNK - 1:CHUNK]
        return 1.0 - f, cum, last

    def update_state(k, cum, last, v):
        ks = (k * jnp.exp(last - cum)).astype(BF16)
        dec = jnp.exp(last)
        vb = v.astype(BF16)
        for h in range(heads):
            sl = slice(h * HEAD_DIM, (h + 1) * HEAD_DIM)
            st_ref[h] = st_ref[h] * dec[:, sl] + _dot_tn(vb[:, sl], ks[:, sl])

    order = lambda n: range(n - 1, -1, -1) if reverse else range(n)

    @pl.when(pl.program_id(1) == 0)
    def _():
        st_ref[...] = jnp.zeros_like(st_ref)
        for j in order(n_ctx):
            rows = slice(j * CHUNK, (j + 1) * CHUNK)
            k, cum, last = gates(cf_ref[0, rows, :])
            update_state(k, cum, last, cv_ref[0, rows, :])

    for j in order(tt // CHUNK):
        rows = slice(j * CHUNK, (j + 1) * CHUNK)
        k, cum, last = gates(f_ref[0, rows, :])
        v = v_ref[0, rows, :]
        qd = (_silu(q_ref[0, rows, :]) * jnp.exp(cum)).astype(BF16)
        ki = (k * jnp.exp(-cum)).astype(BF16)
        vb = v.astype(BF16)
        for h in range(heads):
            sl = slice(h * HEAD_DIM, (h + 1) * HEAD_DIM)
            sc = jnp.where(tri, _dot_nt(qd[:, sl], ki[:, sl]), 0.0).astype(BF16)
            o_ref[0, rows, sl] = _dot(sc, vb[:, sl]) + _dot_nt(qd[:, sl], st_ref[h].astype(BF16))
        update_state(k, cum, last, v)


def _hgrn(lbl, zc, z, *, reverse):
    bsz, l, _ = z.shape
    n_ctx_tok = zc.shape[1]
    w = lbl.shape[1]
    heads = w // HEAD_DIM
    tt = min(HGRN_TT, l)
    nt = l // tt
    fcol = 1 if reverse else 0
    tile = (lambda i: nt - 1 - i) if reverse else (lambda i: i)
    kern = functools.partial(_hgrn_kernel, reverse=reverse, n_ctx=n_ctx_tok // CHUNK, tt=tt)
    return pl.pallas_call(
        kern,
        grid=(bsz, nt),
        in_specs=[pl.BlockSpec(lbl.shape, lambda b, i: (0, 0)),
                  pl.BlockSpec((1, n_ctx_tok, w), lambda b, i: (b, 0, fcol)),
                  pl.BlockSpec((1, n_ctx_tok, w), lambda b, i: (b, 0, 2)),
                  pl.BlockSpec((1, tt, w), lambda b, i: (b, tile(i), fcol)),
                  pl.BlockSpec((1, tt, w), lambda b, i: (b, tile(i), 2)),
                  pl.BlockSpec((1, tt, w), lambda b, i: (b, tile(i), 3))],
        out_specs=pl.BlockSpec((1, tt, w), lambda b, i: (b, tile(i), 0)),
        out_shape=jax.ShapeDtypeStruct((bsz, l, w), F32),
        scratch_shapes=[pltpu.VMEM((heads, HEAD_DIM, HEAD_DIM), F32)],
        compiler_params=_cparams(("arbitrary", "arbitrary")),
        name="hgrn_bwd" if reverse else "hgrn_fwd",
    )(lbl, zc, zc, z, z, z)


def _mix_kernel(of_ref, ob_ref, zg_ref, za_ref, zt_ref, x_ref, g1_ref, sc2_ref, sh2_ref,
                hg_ref, dwk_ref, dwb_ref, lng_ref, lnb_ref, wo_ref, n2g_ref, wr_ref, br_ref,
                xm_ref, h2_ref, eid_ref, wts_ref, rank_ref, cnt_ref,
                pad_ref, shf_ref, cv_ref, carry_ref, *, tm):
    w = of_ref.shape[2]
    heads = w // HEAD_DIM

    @pl.when((pl.program_id(0) == 0) & (pl.program_id(1) == 0))
    def _():
        carry_ref[...] = jnp.zeros_like(carry_ref)

    o = of_ref[0] + ob_ref[0]
    parts = []
    for h in range(heads):
        oh = o[:, h * HEAD_DIM:(h + 1) * HEAD_DIM]
        parts.append(oh * lax.rsqrt(jnp.mean(oh * oh, axis=-1, keepdims=True) + EPS))
    o_mix = jnp.concatenate(parts, axis=-1) * hg_ref[...] * _silu(zg_ref[0])

    u = za_ref[0] * jax.nn.sigmoid(zt_ref[0])
    lo, hi = 16, 16 + GRID_W
    pad_ref[0:lo, :] = jnp.zeros((lo, u.shape[1]), F32)
    pad_ref[hi:hi + 16, :] = jnp.zeros((16, u.shape[1]), F32)
    span = shf_ref.shape[1]
    for s in range(tm // GRID_W):
        pad_ref[lo:hi, :] = u[s * GRID_W:(s + 1) * GRID_W]
        for b in range(1, 8):
            shf_ref[b] = pad_ref[b:b + span, :]
        acc = jnp.zeros((GRID_W, u.shape[1]), F32) + dwb_ref[...]
        for k in range(CONV_K):
            a, b = divmod(lo - CONV_PAD + k, 8)
            src = pad_ref if b == 0 else shf_ref.at[b]
            acc = acc + src[8 * a:8 * a + GRID_W, :] * dwk_ref[k:k + 1, :]
        cv_ref[s * GRID_W:(s + 1) * GRID_W, :] = acc
    cv = cv_ref[...]
    mu = jnp.mean(cv, axis=-1, keepdims=True)
    cen = cv - mu
    var = jnp.mean(cen * cen, axis=-1, keepdims=True)
    c_mix = _silu(cen * lax.rsqrt(var + EPS) * lng_ref[...] + lnb_ref[...])

    mix = _dot(o_mix.astype(BF16), wo_ref[0:w, :]) + _dot(c_mix.astype(BF16), wo_ref[w:, :])
    xm = x_ref[0] + g1_ref[0] * mix
    xm_ref[0] = xm

    ms = jnp.mean(xm * xm, axis=-1, keepdims=True)
    h2 = xm * lax.rsqrt(ms + EPS) * (n2g_ref[...] * (1.0 + sc2_ref[0])) + sh2_ref[0]
    h2_ref[0] = _pack_halves(h2)

    hh = h2.astype(BF16)
    hl = (h2 - hh.astype(F32)).astype(BF16)
    wr = wr_ref[...]
    wh = wr.astype(BF16)
    wl = (wr - wh.astype(F32)).astype(BF16)
    lg = _dot_nt(wh, hh) + _dot_nt(wh, hl) + _dot_nt(wl, hh) + br_ref[...]

    neg = -jnp.inf
    r8 = lax.broadcasted_iota(I32, (EXPERTS_PER_GROUP, tm), 0).astype(F32)
    gl = jnp.where(r8 < N_GROUPS, lg[0:8], neg)
    gmax = jnp.max(gl, axis=0, keepdims=True)
    grp = jnp.min(jnp.where(gl == gmax, r8, 8.0), axis=0, keepdims=True)
    p_grp = 1.0 / jnp.sum(jnp.exp(gl - gmax), axis=0, keepdims=True)
    es = lg[8:16]
    for g in range(1, N_GROUPS):
        es = jnp.where(grp == float(g), lg[8 + 8 * g:16 + 8 * g], es)
    m1 = jnp.max(es, axis=0, keepdims=True)
    i1 = jnp.min(jnp.where(es == m1, r8, 8.0), axis=0, keepdims=True)
    es2 = jnp.where(r8 == i1, neg, es)
    m2 = jnp.max(es2, axis=0, keepdims=True)
    i2 = jnp.min(jnp.where(es2 == m2, r8, 8.0), axis=0, keepdims=True)
    e2 = jnp.exp(m2 - m1)
    w1 = p_grp / (1.0 + e2)
    w2 = p_grp * e2 / (1.0 + e2)
    eid0 = grp * float(EXPERTS_PER_GROUP) + i1
    eid1 = grp * float(EXPERTS_PER_GROUP) + i2
    eid_ref[0:1, :] = eid0.astype(I32)
    eid_ref[1:2, :] = eid1.astype(I32)
    wts_ref[0:1, :] = w1
    wts_ref[1:2, :] = w2

    r32 = lax.broadcasted_iota(I32, (N_EXPERTS, tm), 0).astype(F32)
    oh0 = jnp.where(r32 == eid0, 1.0, 0.0)
    oh1 = jnp.where(r32 == eid1, 1.0, 0.0)
    src = lax.broadcasted_iota(I32, (tm, tm), 0)
    dst = lax.broadcasted_iota(I32, (tm, tm), 1)
    before = jnp.where(src < dst, 1.0, 0.0).astype(BF16)
    ex = _dot(jnp.concatenate([oh0, oh1], axis=0).astype(BF16), before)
    base = ex[0:N_EXPERTS] + ex[N_EXPERTS:] + carry_ref[...]
    rank_ref[0:1, :] = jnp.sum(oh0 * base, axis=0, keepdims=True).astype(I32)
    rank_ref[1:2, :] = jnp.sum(oh1 * base, axis=0, keepdims=True).astype(I32)
    carry = carry_ref[...] + jnp.sum(oh0 + oh1, axis=1, keepdims=True)
    carry_ref[...] = carry
    cnt_ref[...] = jnp.broadcast_to(carry, cnt_ref.shape)


def _mix(o_f, o_b, z, x, g1, sc2, sh2, hg, dwk, dwb, lng, lnb, wo, n2g, wr, br):
    bsz, l, d = x.shape
    w = o_f.shape[2]
    cw = dwk.shape[1]
    tm = min(MIX_TM, l)
    nt = l // tm
    n = bsz * l
    tok = lambda b, i: (b, i, 0)
    vec = lambda b, i: (0, 0)
    per_b = lambda b, i: (b, 0, 0)
    flat = lambda b, i: (0, b * nt + i)
    kern = functools.partial(_mix_kernel, tm=tm)
    return pl.pallas_call(
        kern,
        grid=(bsz, nt),
        in_specs=[pl.BlockSpec((1, tm, w), tok),
                  pl.BlockSpec((1, tm, w), tok),
                  pl.BlockSpec((1, tm, w), lambda b, i: (b, i, 4)),
                  pl.BlockSpec((1, tm, cw), lambda b, i: (b, i, 5)),
                  pl.BlockSpec((1, tm, cw), lambda b, i: (b, i, 6)),
                  pl.BlockSpec((1, tm, d), tok),
                  pl.BlockSpec((1, 1, d), per_b),
                  pl.BlockSpec((1, 1, d), per_b),
                  pl.BlockSpec((1, 1, d), per_b),
                  pl.BlockSpec((1, w), vec),
                  pl.BlockSpec((CONV_K, cw), vec),
                  pl.BlockSpec((1, cw), vec),
                  pl.BlockSpec((1, cw), vec),
                  pl.BlockSpec((1, cw), vec),
                  pl.BlockSpec((w + cw, d), vec),
                  pl.BlockSpec((1, d), vec),
                  pl.BlockSpec((ROUTER_ROWS, d), vec),
                  pl.BlockSpec((ROUTER_ROWS, 1), vec)],
        out_specs=[pl.BlockSpec((1, tm, d), tok),
                   pl.BlockSpec((1, tm, d // 2), tok),
                   pl.BlockSpec((TOP_K, tm), flat),
                   pl.BlockSpec((TOP_K, tm), flat),
                   pl.BlockSpec((TOP_K, tm), flat),
                   pl.BlockSpec((N_EXPERTS, 128), vec)],
        out_shape=[jax.ShapeDtypeStruct((bsz, l, d), F32),
                   jax.ShapeDtypeStruct((bsz, l, d // 2), F32),
                   jax.ShapeDtypeStruct((TOP_K, n), I32),
                   jax.ShapeDtypeStruct((TOP_K, n), F32),
                   jax.ShapeDtypeStruct((TOP_K, n), I32),
                   jax.ShapeDtypeStruct((N_EXPERTS, 128), F32)],
        scratch_shapes=[pltpu.VMEM((GRID_W + 32, cw), F32),
                        pltpu.VMEM((8, GRID_W + 24, cw), F32),
                        pltpu.VMEM((tm, cw), F32),
                        pltpu.VMEM((N_EXPERTS, 1), F32)],
        compiler_params=_cparams(("arbitrary", "arbitrary")),
        name="mix",
    )(o_f, o_b, z, z, z, x, g1, sc2, sh2, hg, dwk, dwb, lng, lnb, wo, n2g, wr, br)


def _pos_kernel(ps_ref, eid_ref, rank_ref, pos_ref):
    eid = eid_ref[...]
    acc = rank_ref[...]
    for e in range(N_EXPERTS):
        acc = acc + jnp.where(eid == e, ps_ref[e], 0)
    pos_ref[...] = acc


def _positions(pstarts, eid, rank):
    shape = eid.shape
    eid2 = eid.reshape(-1, 128)
    full = pl.BlockSpec(eid2.shape, lambda i: (0, 0))
    pos = pl.pallas_call(
        _pos_kernel,
        grid=(1,),
        in_specs=[pl.BlockSpec(memory_space=pltpu.SMEM), full, full],
        out_specs=full,
        out_shape=jax.ShapeDtypeStruct(eid2.shape, I32),
        name="positions",
    )(pstarts, eid2, rank.reshape(-1, 128))
    return pos.reshape(shape)


def _sc_mesh():
    return plsc.VectorSubcoreMesh(core_axis_name="c", subcore_axis_name="s",
                                  num_cores=SC_CORES, num_subcores=SC_SUBCORES)


def _sc_worker():
    return lax.axis_index("s") * SC_CORES + lax.axis_index("c")


def _sc_dispatch(h2p, pos4, padidx, zero_rows, n_rows):
    n, dw = h2p.shape
    _, workers, chunks, _ = pos4.shape
    pad_chunks = padidx.shape[1]
    per_worker = n // workers

    def body(h_hbm, pos_hbm, pad_hbm, zero_hbm, out_hbm, idx_v, pad_v, rows_v):
        wid = _sc_worker()
        base = wid * per_worker
        for k in range(TOP_K):
            pltpu.sync_copy(pos_hbm.at[k, wid], idx_v.at[k])
        pltpu.sync_copy(pad_hbm.at[wid], pad_v)
        pltpu.sync_copy(zero_hbm, rows_v)
        for j in range(pad_chunks):
            pltpu.sync_copy(rows_v, out_hbm.at[pad_v.at[j]])
        for j in range(chunks):
            pltpu.sync_copy(h_hbm.at[pl.ds(base + j * SC_W, SC_W)], rows_v)
            for k in range(TOP_K):
                pltpu.sync_copy(rows_v, out_hbm.at[idx_v.at[k, j]])

    return pl.kernel(
        body,
        out_type=jax.ShapeDtypeStruct((n_rows, dw), F32),
        mesh=_sc_mesh(),
        scratch_types=[pltpu.VMEM((TOP_K, chunks, SC_W), I32),
                       pltpu.VMEM((pad_chunks, SC_W), I32),
                       pltpu.VMEM((SC_W, dw), F32)],
        name="sc_dispatch",
    )(h2p, pos4, padidx, zero_rows)


def _sc_combine_gather(ybuf, pos4):
    _, dw = ybuf.shape
    _, workers, chunks, _ = pos4.shape
    per_worker = chunks * SC_W

    def body(y_hbm, pos_hbm, out_hbm, idx_v, rows_v):
        wid = _sc_worker()
        base = wid * per_worker
        for k in range(TOP_K):
            pltpu.sync_copy(pos_hbm.at[k, wid], idx_v.at[k])
        for k in range(TOP_K):
            for j in range(chunks):
                pltpu.sync_copy(y_hbm.at[idx_v.at[k, j]], rows_v)
                pltpu.sync_copy(rows_v, out_hbm.at[k, pl.ds(base + j * SC_W, SC_W)])

    return pl.kernel(
        body,
        out_type=jax.ShapeDtypeStruct((TOP_K, workers * per_worker, dw), F32),
        mesh=_sc_mesh(),
        scratch_types=[pltpu.VMEM((TOP_K, chunks, SC_W), I32),
                       pltpu.VMEM((SC_W, dw), F32)],
        name="sc_combine",
    )(ybuf, pos4)


def _ffn_kernel(blk_e_ref, nblk_ref, x_ref, wg_ref, wu_ref, wd_ref, y_ref, wgb_ref, wub_ref, wdb_ref):
    i = pl.program_id(0)
    used = i < nblk_ref[0]
    new_expert = (i == 0) | (blk_e_ref[i] != blk_e_ref[jnp.maximum(i - 1, 0)])
    half = x_ref.shape[1]

    @pl.when(used & new_expert)
    def _():
        wgb_ref[...] = wg_ref[0].astype(BF16)
        wub_ref[...] = wu_ref[0].astype(BF16)
        wdb_ref[...] = wd_ref[0].astype(BF16)

    @pl.when(used)
    def _():
        x_lo, x_hi = _unpack_halves(x_ref[...])
        g = _dot(x_lo, wgb_ref[0:half, :]) + _dot(x_hi, wgb_ref[half:, :])
        u = _dot(x_lo, wub_ref[0:half, :]) + _dot(x_hi, wub_ref[half:, :])
        y_ref[...] = _pack_halves(_dot((_silu(g) * u).astype(BF16), wdb_ref[...]))

    @pl.when(jnp.logical_not(used))
    def _():
        y_ref[...] = jnp.zeros_like(y_ref)


def _ffn(blk_e, nblk, xbuf, wg, wu, wd, tb):
    _, dw = xbuf.shape
    _, d, de = wg.shape
    n_blk = blk_e.shape[0]
    x_blk = lambda i, be, nb: (jnp.minimum(i, nb[0] - 1), 0)
    grid_spec = pltpu.PrefetchScalarGridSpec(
        num_scalar_prefetch=2,
        grid=(n_blk,),
        in_specs=[pl.BlockSpec((tb, dw), x_blk),
                  pl.BlockSpec((1, d, de), lambda i, be, nb: (be[i], 0, 0)),
                  pl.BlockSpec((1, d, de), lambda i, be, nb: (be[i], 0, 0)),
                  pl.BlockSpec((1, de, d), lambda i, be, nb: (be[i], 0, 0))],
        out_specs=pl.BlockSpec((tb, dw), lambda i, be, nb: (i, 0)),
        scratch_shapes=[pltpu.VMEM((d, de), BF16),
                        pltpu.VMEM((d, de), BF16),
                        pltpu.VMEM((de, d), BF16)],
    )
    return pl.pallas_call(
        _ffn_kernel,
        grid_spec=grid_spec,
        out_shape=jax.ShapeDtypeStruct((n_blk * tb, dw), F32),
        compiler_params=_cparams(("arbitrary",)),
        name="ffn",
    )(blk_e, nblk, xbuf, wg, wu, wd)


def _final_kernel(yg_ref, xm_ref, wt_ref, g2_ref, fg_ref, o_ref):
    wt = wt_ref[...]
    y = None
    for k in range(TOP_K):
        lo, hi = _unpack_halves(yg_ref[k])
        yk = jnp.concatenate([lo, hi], axis=-1).astype(F32) * wt[:, k:k + 1]
        y = yk if y is None else y + yk
    x = xm_ref[0] + g2_ref[0] * y
    ms = jnp.mean(x * x, axis=-1, keepdims=True)
    o_ref[0] = x * lax.rsqrt(ms + EPS) * fg_ref[...]


def _final(yg, xm, wt, g2, fg):
    bsz, l, d = xm.shape
    dw = yg.shape[2]
    tm = min(MOE_TM, l)
    nt = l // tm
    return pl.pallas_call(
        _final_kernel,
        grid=(bsz, nt),
        in_specs=[pl.BlockSpec((TOP_K, tm, dw), lambda b, i: (0, b * nt + i, 0)),
                  pl.BlockSpec((1, tm, d), lambda b, i: (b, i, 0)),
                  pl.BlockSpec((tm, TOP_K), lambda b, i: (b * nt + i, 0)),
                  pl.BlockSpec((1, 1, d), lambda b, i: (b, 0, 0)),
                  pl.BlockSpec((1, d), lambda b, i: (0, 0))],
        out_specs=pl.BlockSpec((1, tm, d), lambda b, i: (b, i, 0)),
        out_shape=jax.ShapeDtypeStruct((bsz, l, d), F32),
        compiler_params=_cparams(("arbitrary", "arbitrary")),
        name="final",
    )(yg, xm, wt, g2, fg)


def _layer(x, ctx, mod, mod_c, norm1_g, w_in, lb_logits, hgrn_norm_g, dw_kernel, dw_bias, conv_ln_g,
           conv_ln_b, w_out, norm2_g, wr, br, w_gate, w_up, w_down, final_norm_g):
    bsz, l, d = x.shape
    w = lb_logits.shape[2]
    heads = w // HEAD_DIM
    n = bsz * l
    row = lambda v: v.reshape(1, -1)
    per_b = lambda v: v.reshape(bsz, 1, d)
    sh1, sc1, g1, sh2, sc2, g2 = [per_b(mod[:, j]) for j in range(N_MOD)]
    csh1 = jnp.broadcast_to(mod_c[0].reshape(1, 1, d), (bsz, 1, d))
    csc1 = jnp.broadcast_to(mod_c[1].reshape(1, 1, d), (bsz, 1, d))

    w_in_b = w_in.astype(BF16)
    z = _inproj(x, row(norm1_g), sc1, sh1, w_in_b)
    zc = _inproj(ctx, row(norm1_g), csc1, csh1, w_in_b[:, :3 * w])
    o_f = _hgrn(lb_logits[0], zc, z, reverse=False)
    o_b = _hgrn(lb_logits[1], zc, z, reverse=True)

    xm, h2, eid, wts, rank, cnt = _mix(
        o_f, o_b, z, x, g1, sc2, sh2, row(jnp.tile(hgrn_norm_g, heads)), dw_kernel, row(dw_bias),
        row(conv_ln_g), row(conv_ln_b), w_out.astype(BF16), row(norm2_g), wr, br)

    tb = FFN_TB
    n_blk = -(-(n * TOP_K) // tb) + N_EXPERTS
    counts = cnt[:, 0].astype(I32)
    pcounts = (counts + tb - 1) // tb * tb
    pends = jnp.cumsum(pcounts)
    pstarts = pends - pcounts
    pos = _positions(pstarts.astype(I32), eid, rank)
    blk_start = jnp.arange(n_blk, dtype=I32) * tb
    blk_e = jnp.minimum(jnp.sum((pends[None, :] <= blk_start[:, None]).astype(I32), axis=1), N_EXPERTS - 1)
    nblk_used = (pends[-1:] // tb).astype(I32)

    workers = SC_CORES * SC_SUBCORES
    pos4 = pos.reshape(TOP_K, workers, n // (workers * SC_W), SC_W)
    lane = jnp.arange(tb, dtype=I32)[None, :]
    pad_rows = jnp.where(lane < (pcounts - counts)[:, None], (pstarts + counts)[:, None] + lane, n_blk * tb)
    padidx = pad_rows.astype(I32).reshape(workers, N_EXPERTS // workers * tb // SC_W, SC_W)
    xbuf = _sc_dispatch(h2.reshape(n, d // 2), pos4, padidx, jnp.zeros((SC_W, d // 2), F32), n_blk * tb + 8)
    ybuf = _ffn(blk_e, nblk_used, xbuf, w_gate, w_up, w_down, tb)
    yg = _sc_combine_gather(ybuf, pos4)
    return _final(yg, xm, wts.T, g2, row(final_norm_g))


def kernel(x, c, ctx, c_ctx, w_ada, b_ada, norm1_g, w_in, lb_logits, hgrn_norm_g, dw_kernel, dw_bias,
           conv_ln_g, conv_ln_b, w_out, norm2_g, router_group_w, router_group_b, router_expert_w,
           router_expert_b, w_expert_gate, w_expert_up, w_expert_down, final_norm_g):
    depth = w_ada.shape[0]
    assert depth == 1, "context tokens are only updated between layers; a single layer is implemented"
    bsz, l, d = x.shape
    l0 = 0
    rows = -(-(bsz + 1) // 8) * 8
    cc = jnp.zeros((rows, d), F32).at[:bsz].set(c).at[bsz].set(c_ctx)
    mod_all = _ada(cc, w_ada[l0], b_ada[l0].reshape(1, -1))
    mod = mod_all[:bsz].reshape(bsz, N_MOD, d)
    mod_c = mod_all[bsz].reshape(N_MOD, d)
    pad8 = jnp.zeros((d, 8 - N_GROUPS), F32)
    wr = jnp.concatenate([router_group_w[l0], pad8, router_expert_w[l0], jnp.zeros((d, 8), F32)], axis=1).T
    br = jnp.concatenate([router_group_b[l0], jnp.zeros((8 - N_GROUPS,), F32), router_expert_b[l0],
                          jnp.zeros((8,), F32)]).reshape(ROUTER_ROWS, 1)
    return _layer(x, ctx, mod, mod_c, norm1_g[l0], w_in[l0], lb_logits, hgrn_norm_g[l0], dw_kernel[l0], dw_bias[l0], conv_ln_g[l0], conv_ln_b[l0], w_out[l0],
                  norm2_g[l0], wr, br, w_expert_gate[l0], w_expert_up[l0], w_expert_down[l0], final_norm_g)
```

```python
import functools

import jax
import jax.numpy as jnp
from jax import lax
from jax.experimental import pallas as pl
from jax.experimental.pallas import tpu as pltpu
from jax.experimental.pallas import tpu_sc as plsc

F32 = jnp.float32
BF16 = jnp.bfloat16
I32 = jnp.int32
U32 = jnp.uint32

EPS = 1e-6
HEAD_DIM = 128
CHUNK = 64
GRID_W = 64
CONV_K = 31
CONV_PAD = CONV_K // 2
N_GROUPS = 4
EXPERTS_PER_GROUP = 8
N_EXPERTS = N_GROUPS * EXPERTS_PER_GROUP
TOP_K = 2
N_MOD = 6
ROUTER_ROWS = 48

ADA_TN = 1024
INPROJ_TM = 512
HGRN_TT = 256
MIX_TM = 256
MOE_TM = 256
FFN_TB = 512
VMEM_LIMIT = 48 * 1024 * 1024
SC_CORES = 2
SC_SUBCORES = 16
SC_W = 128


def _cparams(sem):
    return pltpu.CompilerParams(dimension_semantics=sem, vmem_limit_bytes=VMEM_LIMIT)


def _silu(v):
    return v * jax.nn.sigmoid(v)


def _dot(a, b):
    return jnp.dot(a, b, preferred_element_type=F32)


def _dot_nt(a, b):
    return lax.dot_general(a, b, (((1,), (1,)), ((), ())), preferred_element_type=F32)


def _dot_tn(a, b):
    return lax.dot_general(a, b, (((0,), (0,)), ((), ())), preferred_element_type=F32)


def _pack_halves(v):
    m = v.shape[1] // 2
    lo = lax.bitcast_convert_type(v[:, :m].astype(BF16).astype(F32), U32) >> 16
    hi = lax.bitcast_convert_type(v[:, m:].astype(BF16).astype(F32), U32) & jnp.uint32(0xFFFF0000)
    return lax.bitcast_convert_type(lo | hi, F32)


def _unpack_halves(p):
    u = lax.bitcast_convert_type(p, U32)
    lo = lax.bitcast_convert_type(u << 16, F32)
    hi = lax.bitcast_convert_type(u & jnp.uint32(0xFFFF0000), F32)
    return lo.astype(BF16), hi.astype(BF16)


def _split3(a):
    hi = a.astype(BF16)
    r = a - hi.astype(F32)
    mid = r.astype(BF16)
    lo = (r - mid.astype(F32)).astype(BF16)
    return hi, mid, lo


def _ada_kernel(c_ref, w_ref, b_ref, o_ref):
    s = _silu(c_ref[...]).astype(BF16)
    o_ref[...] = _dot(s, w_ref[...].astype(BF16)) + b_ref[...]


def _ada(cc, w, b):
    r, d = cc.shape
    nc = w.shape[1]
    return pl.pallas_call(
        _ada_kernel,
        grid=(nc // ADA_TN,),
        in_specs=[pl.BlockSpec((r, d), lambda j: (0, 0)),
                  pl.BlockSpec((d, ADA_TN), lambda j: (0, j)),
                  pl.BlockSpec((1, ADA_TN), lambda j: (0, j))],
        out_specs=pl.BlockSpec((r, ADA_TN), lambda j: (0, j)),
        out_shape=jax.ShapeDtypeStruct((r, nc), F32),
        compiler_params=_cparams(("arbitrary",)),
        name="ada",
    )(cc, w, b)


def _inproj_kernel(x_ref, g_ref, sc_ref, sh_ref, w_ref, zf_ref, zr_ref):
    x = x_ref[0]
    ms = jnp.mean(x * x, axis=-1, keepdims=True)
    gs = g_ref[...] * (1.0 + sc_ref[0])
    h = x * lax.rsqrt(ms + EPS) * gs + sh_ref[0]
    z = _dot(h.astype(BF16), w_ref[...])
    nf = zf_ref.shape[2]
    zf_ref[0] = z[:, :nf]
    zr_ref[0] = z[:, nf:].astype(BF16)


def _inproj(x, g, sc, sh, w, nf):
    bsz, l, d = x.shape
    nc = w.shape[1]
    tm = min(INPROJ_TM, l)
    return pl.pallas_call(
        _inproj_kernel,
        grid=(bsz, l // tm),
        in_specs=[pl.BlockSpec((1, tm, d), lambda b, i: (b, i, 0)),
                  pl.BlockSpec((1, d), lambda b, i: (0, 0)),
                  pl.BlockSpec((1, 1, d), lambda b, i: (b, 0, 0)),
                  pl.BlockSpec((1, 1, d), lambda b, i: (b, 0, 0)),
                  pl.BlockSpec((d, nc), lambda b, i: (0, 0))],
        out_specs=[pl.BlockSpec((1, tm, nf), lambda b, i: (b, i, 0)),
                   pl.BlockSpec((1, tm, nc - nf), lambda b, i: (b, i, 0))],
        out_shape=[jax.ShapeDtypeStruct((bsz, l, nf), F32),
                   jax.ShapeDtypeStruct((bsz, l, nc - nf), BF16)],
        compiler_params=_cparams(("arbitrary", "arbitrary")),
        name="inproj",
    )(x, g, sc, sh, w)


class _HgrnDirection:
    def __init__(self, lbl, st_ref, reverse):
        self.st_ref, self.reverse = st_ref, reverse
        e = jnp.exp(lbl - jnp.max(lbl, axis=0, keepdims=True))
        self.lb = e[0:1] / jnp.sum(e, axis=0, keepdims=True)
        row = lax.broadcasted_iota(I32, (CHUNK, CHUNK), 0)
        col = lax.broadcasted_iota(I32, (CHUNK, CHUNK), 1)
        self.tri = (col >= row) if reverse else (col <= row)
        self.tm = jnp.where(self.tri, 1.0, 0.0).astype(BF16)

    def tile(self, f_ref, v_ref, q_ref=None, o_ref=None):
        heads = self.st_ref.shape[0]
        n_chunks = f_ref.shape[1] // CHUNK
        crow = lambda c: slice(c * CHUNK, (c + 1) * CHUNK)
        hsl = lambda h: slice(h * HEAD_DIM, (h + 1) * HEAD_DIM)
        f = self.lb + (1.0 - self.lb) * jax.nn.sigmoid(f_ref[0])
        k = 1.0 - f
        hi, mid, lo = _split3(jnp.log(f))
        cums = [_dot(self.tm, hi[crow(c)]) + _dot(self.tm, mid[crow(c)]) + _dot(self.tm, lo[crow(c)])
                for c in range(n_chunks)]
        lasts = [cm[0:1] if self.reverse else cm[CHUNK - 1:CHUNK] for cm in cums]
        cum = jnp.concatenate(cums, axis=0)
        last = jnp.concatenate([jnp.broadcast_to(la, (CHUNK, la.shape[1])) for la in lasts], axis=0)
        ks = (k * jnp.exp(last - cum)).astype(BF16)
        vb = v_ref[0].astype(BF16)
        kv = [[_dot_tn(vb[crow(c), hsl(h)], ks[crow(c), hsl(h)]) for h in range(heads)] for c in range(n_chunks)]
        dec = [jnp.exp(la) for la in lasts]
        emit = o_ref is not None
        if emit:
            qd = (_silu(q_ref[0].astype(F32)) * jnp.exp(cum)).astype(BF16)
            ki = (k * jnp.exp(-cum)).astype(BF16)
            sc = [[jnp.where(self.tri, _dot_nt(qd[crow(c), hsl(h)], ki[crow(c), hsl(h)]), 0.0).astype(BF16)
                   for h in range(heads)] for c in range(n_chunks)]
            intra = [[_dot(sc[c][h], vb[crow(c), hsl(h)]) for h in range(heads)] for c in range(n_chunks)]
        state = [self.st_ref[h] for h in range(heads)]
        for c in (range(n_chunks - 1, -1, -1) if self.reverse else range(n_chunks)):
            for h in range(heads):
                if emit:
                    o_ref[0, crow(c), hsl(h)] = intra[c][h] + _dot_nt(qd[crow(c), hsl(h)], state[h].astype(BF16))
                state[h] = state[h] * dec[c][:, hsl(h)] + kv[c][h]
        for h in range(heads):
            self.st_ref[h] = state[h]


def _hgrn_kernel(lbl_ref, cff_ref, cfb_ref, cv_ref, ff_ref, vf_ref, qf_ref, fb_ref, vb_ref, qb_ref,
                 of_ref, ob_ref, stf_ref, stb_ref):
    fwd = _HgrnDirection(lbl_ref[0], stf_ref, reverse=False)
    bwd = _HgrnDirection(lbl_ref[1], stb_ref, reverse=True)

    @pl.when(pl.program_id(1) == 0)
    def _():
        stf_ref[...] = jnp.zeros_like(stf_ref)
        stb_ref[...] = jnp.zeros_like(stb_ref)
        fwd.tile(cff_ref, cv_ref)
        bwd.tile(cfb_ref, cv_ref)

    fwd.tile(ff_ref, vf_ref, qf_ref, of_ref)
    bwd.tile(fb_ref, vb_ref, qb_ref, ob_ref)


def _hgrn(lbl, zcf, zcr, zf, zr):
    bsz, l, _ = zf.shape
    n_ctx_tok = zcf.shape[1]
    w = lbl.shape[2]
    heads = w // HEAD_DIM
    tt = min(HGRN_TT, l)
    nt = l // tt
    ctx_col = lambda c: pl.BlockSpec((1, n_ctx_tok, w), lambda b, i: (b, 0, c))
    fwd_col = lambda c: pl.BlockSpec((1, tt, w), lambda b, i: (b, i, c))
    bwd_col = lambda c: pl.BlockSpec((1, tt, w), lambda b, i: (b, nt - 1 - i, c))
    state = pltpu.VMEM((heads, HEAD_DIM, HEAD_DIM), F32)
    return pl.pallas_call(
        _hgrn_kernel,
        grid=(bsz, nt),
        in_specs=[pl.BlockSpec(lbl.shape, lambda b, i: (0, 0, 0)),
                  ctx_col(0), ctx_col(1), ctx_col(0),
                  fwd_col(0), fwd_col(0), fwd_col(1),
                  bwd_col(1), bwd_col(0), bwd_col(1)],
        out_specs=[fwd_col(0), bwd_col(0)],
        out_shape=[jax.ShapeDtypeStruct((bsz, l, w), F32), jax.ShapeDtypeStruct((bsz, l, w), F32)],
        scratch_shapes=[state, state],
        compiler_params=_cparams(("arbitrary", "arbitrary")),
        name="hgrn",
    )(lbl, zcf, zcf, zcr, zf, zr, zr, zf, zr, zr)


def _mix_kernel(of_ref, ob_ref, zg_ref, za_ref, zt_ref, x_ref, g1_ref, sc2_ref, sh2_ref,
                hg_ref, dwk_ref, dwb_ref, lng_ref, lnb_ref, wo_ref, n2g_ref, wr_ref, br_ref,
                xm_ref, h2_ref, eid_ref, wts_ref, rank_ref, cnt_ref,
                pad_ref, shf_ref, cv_ref, carry_ref, *, tm):
    w = of_ref.shape[2]
    heads = w // HEAD_DIM

    @pl.when((pl.program_id(0) == 0) & (pl.program_id(1) == 0))
    def _():
        carry_ref[...] = jnp.zeros_like(carry_ref)

    o = of_ref[0] + ob_ref[0]
    parts = []
    for h in range(heads):
        oh = o[:, h * HEAD_DIM:(h + 1) * HEAD_DIM]
        parts.append(oh * lax.rsqrt(jnp.mean(oh * oh, axis=-1, keepdims=True) + EPS))
    o_mix = jnp.concatenate(parts, axis=-1) * hg_ref[...] * _silu(zg_ref[0].astype(F32))

    u = za_ref[0].astype(F32) * jax.nn.sigmoid(zt_ref[0].astype(F32))
    lo, hi = 16, 16 + GRID_W
    pad_ref[0:lo, :] = jnp.zeros((lo, u.shape[1]), F32)
    pad_ref[hi:hi + 16, :] = jnp.zeros((16, u.shape[1]), F32)
    span = shf_ref.shape[1]
    for s in range(tm // GRID_W):
        pad_ref[lo:hi, :] = u[s * GRID_W:(s + 1) * GRID_W]
        for b in range(1, 8):
            shf_ref[b] = pad_ref[b:b + span, :]
        acc = jnp.zeros((GRID_W, u.shape[1]), F32) + dwb_ref[...]
        for k in range(CONV_K):
            a, b = divmod(lo - CONV_PAD + k, 8)
            src = pad_ref if b == 0 else shf_ref.at[b]
            acc = acc + src[8 * a:8 * a + GRID_W, :] * dwk_ref[k:k + 1, :]
        cv_ref[s * GRID_W:(s + 1) * GRID_W, :] = acc
    cv = cv_ref[...]
    mu = jnp.mean(cv, axis=-1, keepdims=True)
    cen = cv - mu
    var = jnp.mean(cen * cen, axis=-1, keepdims=True)
    c_mix = _silu(cen * lax.rsqrt(var + EPS) * lng_ref[...] + lnb_ref[...])

    mix = _dot(o_mix.astype(BF16), wo_ref[0:w, :]) + _dot(c_mix.astype(BF16), wo_ref[w:, :])
    xm = x_ref[0] + g1_ref[0] * mix
    xm_ref[0] = xm

    ms = jnp.mean(xm * xm, axis=-1, keepdims=True)
    h2 = xm * lax.rsqrt(ms + EPS) * (n2g_ref[...] * (1.0 + sc2_ref[0])) + sh2_ref[0]
    h2_ref[0] = _pack_halves(h2)

    hh = h2.astype(BF16)
    hl = (h2 - hh.astype(F32)).astype(BF16)
    wr = wr_ref[...]
    wh = wr.astype(BF16)
    wl = (wr - wh.astype(F32)).astype(BF16)
    lg = _dot_nt(wh, hh) + _dot_nt(wh, hl) + _dot_nt(wl, hh) + br_ref[...]

    neg = -jnp.inf
    r8 = lax.broadcasted_iota(I32, (EXPERTS_PER_GROUP, tm), 0).astype(F32)
    gl = jnp.where(r8 < N_GROUPS, lg[0:8], neg)
    gmax = jnp.max(gl, axis=0, keepdims=True)
    grp = jnp.min(jnp.where(gl == gmax, r8, 8.0), axis=0, keepdims=True)
    p_grp = 1.0 / jnp.sum(jnp.exp(gl - gmax), axis=0, keepdims=True)
    es = lg[8:16]
    for g in range(1, N_GROUPS):
        es = jnp.where(grp == float(g), lg[8 + 8 * g:16 + 8 * g], es)
    m1 = jnp.max(es, axis=0, keepdims=True)
    i1 = jnp.min(jnp.where(es == m1, r8, 8.0), axis=0, keepdims=True)
    es2 = jnp.where(r8 == i1, neg, es)
    m2 = jnp.max(es2, axis=0, keepdims=True)
    i2 = jnp.min(jnp.where(es2 == m2, r8, 8.0), axis=0, keepdims=True)
    e2 = jnp.exp(m2 - m1)
    w1 = p_grp / (1.0 + e2)
    w2 = p_grp * e2 / (1.0 + e2)
    eid0 = grp * float(EXPERTS_PER_GROUP) + i1
    eid1 = grp * float(EXPERTS_PER_GROUP) + i2
    eid_ref[0:1, :] = eid0.astype(I32)
    eid_ref[1:2, :] = eid1.astype(I32)
    wts_ref[0:1, :] = w1
    wts_ref[1:2, :] = w2

    r32 = lax.broadcasted_iota(I32, (N_EXPERTS, tm), 0).astype(F32)
    oh0 = jnp.where(r32 == eid0, 1.0, 0.0)
    oh1 = jnp.where(r32 == eid1, 1.0, 0.0)
    src = lax.broadcasted_iota(I32, (tm, tm), 0)
    dst = lax.broadcasted_iota(I32, (tm, tm), 1)
    before = jnp.where(src < dst, 1.0, 0.0).astype(BF16)
    ex = _dot(jnp.concatenate([oh0, oh1], axis=0).astype(BF16), before)
    base = ex[0:N_EXPERTS] + ex[N_EXPERTS:] + carry_ref[...]
    rank_ref[0:1, :] = jnp.sum(oh0 * base, axis=0, keepdims=True).astype(I32)
    rank_ref[1:2, :] = jnp.sum(oh1 * base, axis=0, keepdims=True).astype(I32)
    carry = carry_ref[...] + jnp.sum(oh0 + oh1, axis=1, keepdims=True)
    carry_ref[...] = carry
    cnt_ref[...] = jnp.broadcast_to(carry, cnt_ref.shape)


def _mix(o_f, o_b, z, x, g1, sc2, sh2, hg, dwk, dwb, lng, lnb, wo, n2g, wr, br):
    bsz, l, d = x.shape
    w = o_f.shape[2]
    cw = dwk.shape[1]
    tm = min(MIX_TM, l)
    nt = l // tm
    n = bsz * l
    tok = lambda b, i: (b, i, 0)
    vec = lambda b, i: (0, 0)
    per_b = lambda b, i: (b, 0, 0)
    flat = lambda b, i: (0, b * nt + i)
    kern = functools.partial(_mix_kernel, tm=tm)
    return pl.pallas_call(
        kern,
        grid=(bsz, nt),
        in_specs=[pl.BlockSpec((1, tm, w), tok),
                  pl.BlockSpec((1, tm, w), tok),
                  pl.BlockSpec((1, tm, w), lambda b, i: (b, i, 2)),
                  pl.BlockSpec((1, tm, cw), lambda b, i: (b, i, 3)),
                  pl.BlockSpec((1, tm, cw), lambda b, i: (b, i, 4)),
                  pl.BlockSpec((1, tm, d), tok),
                  pl.BlockSpec((1, 1, d), per_b),
                  pl.BlockSpec((1, 1, d), per_b),
                  pl.BlockSpec((1, 1, d), per_b),
                  pl.BlockSpec((1, w), vec),
                  pl.BlockSpec((CONV_K, cw), vec),
                  pl.BlockSpec((1, cw), vec),
                  pl.BlockSpec((1, cw), vec),
                  pl.BlockSpec((1, cw), vec),
                  pl.BlockSpec((w + cw, d), vec),
                  pl.BlockSpec((1, d), vec),
                  pl.BlockSpec((ROUTER_ROWS, d), vec),
                  pl.BlockSpec((ROUTER_ROWS, 1), vec)],
        out_specs=[pl.BlockSpec((1, tm, d), tok),
                   pl.BlockSpec((1, tm, d // 2), tok),
                   pl.BlockSpec((TOP_K, tm), flat),
                   pl.BlockSpec((TOP_K, tm), flat),
                   pl.BlockSpec((TOP_K, tm), flat),
                   pl.BlockSpec((N_EXPERTS, 128), vec)],
        out_shape=[jax.ShapeDtypeStruct((bsz, l, d), F32),
                   jax.ShapeDtypeStruct((bsz, l, d // 2), F32),
                   jax.ShapeDtypeStruct((TOP_K, n), I32),
                   jax.ShapeDtypeStruct((TOP_K, n), F32),
                   jax.ShapeDtypeStruct((TOP_K, n), I32),
                   jax.ShapeDtypeStruct((N_EXPERTS, 128), F32)],
        scratch_shapes=[pltpu.VMEM((GRID_W + 32, cw), F32),
                        pltpu.VMEM((8, GRID_W + 24, cw), F32),
                        pltpu.VMEM((tm, cw), F32),
                        pltpu.VMEM((N_EXPERTS, 1), F32)],
        compiler_params=_cparams(("arbitrary", "arbitrary")),
        name="mix",
    )(o_f, o_b, z, z, z, x, g1, sc2, sh2, hg, dwk, dwb, lng, lnb, wo, n2g, wr, br)


def _pos_kernel(ps_ref, eid_ref, rank_ref, pos_ref):
    eid = eid_ref[...]
    acc = rank_ref[...]
    for e in range(N_EXPERTS):
        acc = acc + jnp.where(eid == e, ps_ref[e], 0)
    pos_ref[...] = acc


def _positions(pstarts, eid, rank):
    shape = eid.shape
    eid2 = eid.reshape(-1, 128)
    full = pl.BlockSpec(eid2.shape, lambda i: (0, 0))
    pos = pl.pallas_call(
        _pos_kernel,
        grid=(1,),
        in_specs=[pl.BlockSpec(memory_space=pltpu.SMEM), full, full],
        out_specs=full,
        out_shape=jax.ShapeDtypeStruct(eid2.shape, I32),
        name="positions",
    )(pstarts, eid2, rank.reshape(-1, 128))
    return pos.reshape(shape)


def _sc_mesh():
    return plsc.VectorSubcoreMesh(core_axis_name="c", subcore_axis_name="s",
                                  num_cores=SC_CORES, num_subcores=SC_SUBCORES)


def _sc_worker():
    return lax.axis_index("s") * SC_CORES + lax.axis_index("c")


def _sc_dispatch(h2p, pos4, padidx, zero_rows, n_rows):
    n, dw = h2p.shape
    _, workers, chunks, _ = pos4.shape
    pad_chunks = padidx.shape[1]
    per_worker = n // workers

    def body(h_hbm, pos_hbm, pad_hbm, zero_hbm, out_hbm, idx_v, pad_v, rows_v):
        wid = _sc_worker()
        base = wid * per_worker
        for k in range(TOP_K):
            pltpu.sync_copy(pos_hbm.at[k, wid], idx_v.at[k])
        pltpu.sync_copy(pad_hbm.at[wid], pad_v)
        pltpu.sync_copy(zero_hbm, rows_v)
        for j in range(pad_chunks):
            pltpu.sync_copy(rows_v, out_hbm.at[pad_v.at[j]])
        for j in range(chunks):
            pltpu.sync_copy(h_hbm.at[pl.ds(base + j * SC_W, SC_W)], rows_v)
            for k in range(TOP_K):
                pltpu.sync_copy(rows_v, out_hbm.at[idx_v.at[k, j]])

    return pl.kernel(
        body,
        out_type=jax.ShapeDtypeStruct((n_rows, dw), F32),
        mesh=_sc_mesh(),
        scratch_types=[pltpu.VMEM((TOP_K, chunks, SC_W), I32),
                       pltpu.VMEM((pad_chunks, SC_W), I32),
                       pltpu.VMEM((SC_W, dw), F32)],
        name="sc_dispatch",
    )(h2p, pos4, padidx, zero_rows)


def _sc_combine_gather(ybuf, pos4):
    _, dw = ybuf.shape
    _, workers, chunks, _ = pos4.shape
    per_worker = chunks * SC_W

    def body(y_hbm, pos_hbm, out_hbm, idx_v, rows_v):
        wid = _sc_worker()
        base = wid * per_worker
        for k in range(TOP_K):
            pltpu.sync_copy(pos_hbm.at[k, wid], idx_v.at[k])
        for k in range(TOP_K):
            for j in range(chunks):
                pltpu.sync_copy(y_hbm.at[idx_v.at[k, j]], rows_v)
                pltpu.sync_copy(rows_v, out_hbm.at[k, pl.ds(base + j * SC_W, SC_W)])

    return pl.kernel(
        body,
        out_type=jax.ShapeDtypeStruct((TOP_K, workers * per_worker, dw), F32),
        mesh=_sc_mesh(),
        scratch_types=[pltpu.VMEM((TOP_K, chunks, SC_W), I32),
                       pltpu.VMEM((SC_W, dw), F32)],
        name="sc_combine",
    )(ybuf, pos4)


def _ffn_kernel(blk_e_ref, nblk_ref, x_ref, wg_ref, wu_ref, wd_ref, y_ref, wgb_ref, wub_ref, wdb_ref):
    i = pl.program_id(0)
    used = i < nblk_ref[0]
    new_expert = (i == 0) | (blk_e_ref[i] != blk_e_ref[jnp.maximum(i - 1, 0)])
    half = x_ref.shape[1]

    @pl.when(used & new_expert)
    def _():
        wgb_ref[...] = wg_ref[0].astype(BF16)
        wub_ref[...] = wu_ref[0].astype(BF16)
        wdb_ref[...] = wd_ref[0].astype(BF16)

    @pl.when(used)
    def _():
        x_lo, x_hi = _unpack_halves(x_ref[...])
        g = _dot(x_lo, wgb_ref[0:half, :]) + _dot(x_hi, wgb_ref[half:, :])
        u = _dot(x_lo, wub_ref[0:half, :]) + _dot(x_hi, wub_ref[half:, :])
        y_ref[...] = _pack_halves(_dot((_silu(g) * u).astype(BF16), wdb_ref[...]))

    @pl.when(jnp.logical_not(used))
    def _():
        y_ref[...] = jnp.zeros_like(y_ref)


def _ffn(blk_e, nblk, xbuf, wg, wu, wd, tb):
    _, dw = xbuf.shape
    _, d, de = wg.shape
    n_blk = blk_e.shape[0]
    x_blk = lambda i, be, nb: (jnp.minimum(i, nb[0] - 1), 0)
    grid_spec = pltpu.PrefetchScalarGridSpec(
        num_scalar_prefetch=2,
        grid=(n_blk,),
        in_specs=[pl.BlockSpec((tb, dw), x_blk),
                  pl.BlockSpec((1, d, de), lambda i, be, nb: (be[i], 0, 0)),
                  pl.BlockSpec((1, d, de), lambda i, be, nb: (be[i], 0, 0)),
                  pl.BlockSpec((1, de, d), lambda i, be, nb: (be[i], 0, 0))],
        out_specs=pl.BlockSpec((tb, dw), lambda i, be, nb: (i, 0)),
        scratch_shapes=[pltpu.VMEM((d, de), BF16),
                        pltpu.VMEM((d, de), BF16),
                        pltpu.VMEM((de, d), BF16)],
    )
    return pl.pallas_call(
        _ffn_kernel,
        grid_spec=grid_spec,
        out_shape=jax.ShapeDtypeStruct((n_blk * tb, dw), F32),
        compiler_params=_cparams(("arbitrary",)),
        name="ffn",
    )(blk_e, nblk, xbuf, wg, wu, wd)


def _final_kernel(yg_ref, xm_ref, wt_ref, g2_ref, fg_ref, o_ref):
    wt = wt_ref[...]
    y = None
    for k in range(TOP_K):
        lo, hi = _unpack_halves(yg_ref[k])
        yk = jnp.concatenate([lo, hi], axis=-1).astype(F32) * wt[:, k:k + 1]
        y = yk if y is None else y + yk
    x = xm_ref[0] + g2_ref[0] * y
    ms = jnp.mean(x * x, axis=-1, keepdims=True)
    o_ref[0] = x * lax.rsqrt(ms + EPS) * fg_ref[...]


def _final(yg, xm, wt, g2, fg):
    bsz, l, d = xm.shape
    dw = yg.shape[2]
    tm = min(MOE_TM, l)
    nt = l // tm
    return pl.pallas_call(
        _final_kernel,
        grid=(bsz, nt),
        in_specs=[pl.BlockSpec((TOP_K, tm, dw), lambda b, i: (0, b * nt + i, 0)),
                  pl.BlockSpec((1, tm, d), lambda b, i: (b, i, 0)),
                  pl.BlockSpec((tm, TOP_K), lambda b, i: (b * nt + i, 0)),
                  pl.BlockSpec((1, 1, d), lambda b, i: (b, 0, 0)),
                  pl.BlockSpec((1, d), lambda b, i: (0, 0))],
        out_specs=pl.BlockSpec((1, tm, d), lambda b, i: (b, i, 0)),
        out_shape=jax.ShapeDtypeStruct((bsz, l, d), F32),
        compiler_params=_cparams(("arbitrary", "arbitrary")),
        name="final",
    )(yg, xm, wt, g2, fg)


def _layer(x, ctx, mod, mod_c, norm1_g, w_in, lb_logits, hgrn_norm_g, dw_kernel, dw_bias, conv_ln_g,
           conv_ln_b, w_out, norm2_g, wr, br, w_gate, w_up, w_down, final_norm_g):
    bsz, l, d = x.shape
    w = lb_logits.shape[2]
    heads = w // HEAD_DIM
    n = bsz * l
    row = lambda v: v.reshape(1, -1)
    per_b = lambda v: v.reshape(bsz, 1, d)
    sh1, sc1, g1, sh2, sc2, g2 = [per_b(mod[:, j]) for j in range(N_MOD)]
    csh1 = jnp.broadcast_to(mod_c[0].reshape(1, 1, d), (bsz, 1, d))
    csc1 = jnp.broadcast_to(mod_c[1].reshape(1, 1, d), (bsz, 1, d))

    w_in_b = w_in.astype(BF16)
    zf, zr = _inproj(x, row(norm1_g), sc1, sh1, w_in_b, 2 * w)
    zcf, zcr = _inproj(ctx, row(norm1_g), csc1, csh1, w_in_b[:, :3 * w], 2 * w)
    o_f, o_b = _hgrn(lb_logits, zcf, zcr, zf, zr)
    z = zr

    xm, h2, eid, wts, rank, cnt = _mix(
        o_f, o_b, z, x, g1, sc2, sh2, row(jnp.tile(hgrn_norm_g, heads)), dw_kernel, row(dw_bias),
        row(conv_ln_g), row(conv_ln_b), w_out.astype(BF16), row(norm2_g), wr, br)

    tb = FFN_TB
    n_blk = -(-(n * TOP_K) // tb) + N_EXPERTS
    counts = cnt[:, 0].astype(I32)
    pcounts = (counts + tb - 1) // tb * tb
    pends = jnp.cumsum(pcounts)
    pstarts = pends - pcounts
    pos = _positions(pstarts.astype(I32), eid, rank)
    blk_start = jnp.arange(n_blk, dtype=I32) * tb
    blk_e = jnp.minimum(jnp.sum((pends[None, :] <= blk_start[:, None]).astype(I32), axis=1), N_EXPERTS - 1)
    nblk_used = (pends[-1:] // tb).astype(I32)

    workers = SC_CORES * SC_SUBCORES
    pos4 = pos.reshape(TOP_K, workers, n // (workers * SC_W), SC_W)
    lane = jnp.arange(tb, dtype=I32)[None, :]
    pad_rows = jnp.where(lane < (pcounts - counts)[:, None], (pstarts + counts)[:, None] + lane,
                         n_blk * tb + lane)
    padidx = pad_rows.astype(I32).reshape(workers, N_EXPERTS // workers * tb // SC_W, SC_W)
    xbuf = _sc_dispatch(h2.reshape(n, d // 2), pos4, padidx, jnp.zeros((SC_W, d // 2), F32), (n_blk + 1) * tb)
    ybuf = _ffn(blk_e, nblk_used, xbuf, w_gate, w_up, w_down, tb)
    yg = _sc_combine_gather(ybuf, pos4)
    return _final(yg, xm, wts.T, g2, row(final_norm_g))


def kernel(x, c, ctx, c_ctx, w_ada, b_ada, norm1_g, w_in, lb_logits, hgrn_norm_g, dw_kernel, dw_bias,
           conv_ln_g, conv_ln_b, w_out, norm2_g, router_group_w, router_group_b, router_expert_w,
           router_expert_b, w_expert_gate, w_expert_up, w_expert_down, final_norm_g):
    depth = w_ada.shape[0]
    assert depth == 1, "context tokens are only updated between layers; a single layer is implemented"
    bsz, l, d = x.shape
    l0 = 0
    rows = -(-(bsz + 1) // 8) * 8
    cc = jnp.zeros((rows, d), F32).at[:bsz].set(c).at[bsz].set(c_ctx)
    mod_all = _ada(cc, w_ada[l0], b_ada[l0].reshape(1, -1))
    mod = mod_all[:bsz].reshape(bsz, N_MOD, d)
    mod_c = mod_all[bsz].reshape(N_MOD, d)
    pad8 = jnp.zeros((d, 8 - N_GROUPS), F32)
    wr = jnp.concatenate([router_group_w[l0], pad8, router_expert_w[l0], jnp.zeros((d, 8), F32)], axis=1).T
    br = jnp.concatenate([router_group_b[l0], jnp.zeros((8 - N_GROUPS,), F32), router_expert_b[l0],
                          jnp.zeros((8,), F32)]).reshape(ROUTER_ROWS, 1)
    return _layer(x, ctx, mod, mod_c, norm1_g[l0], w_in[l0], lb_logits, hgrn_norm_g[l0], dw_kernel[l0], dw_bias[l0], conv_ln_g[l0], conv_ln_b[l0], w_out[l0],
                  norm2_g[l0], wr, br, w_expert_gate[l0], w_expert_up[l0], w_expert_down[l0], final_norm_g)
```

```python
import functools

import jax
import jax.numpy as jnp
from jax import lax
from jax.experimental import pallas as pl
from jax.experimental.pallas import tpu as pltpu
from jax.experimental.pallas import tpu_sc as plsc

F32 = jnp.float32
BF16 = jnp.bfloat16
I32 = jnp.int32
U32 = jnp.uint32

EPS = 1e-6
HEAD_DIM = 128
CHUNK = 64
GRID_W = 64
CONV_K = 31
CONV_PAD = CONV_K // 2
N_GROUPS = 4
EXPERTS_PER_GROUP = 8
N_EXPERTS = N_GROUPS * EXPERTS_PER_GROUP
TOP_K = 2
N_MOD = 6
ROUTER_ROWS = 48

ADA_TN = 1024
INPROJ_TM = 512
HGRN_TT = 256
MIX_TM = 256
MOE_TM = 256
FFN_TB = 512
VMEM_LIMIT = 48 * 1024 * 1024
SC_CORES = 2
SC_SUBCORES = 16
SC_W = 128
MOE_GROUPS = 2


def _cparams(sem):
    return pltpu.CompilerParams(dimension_semantics=sem, vmem_limit_bytes=VMEM_LIMIT)


def _silu(v):
    return v * jax.nn.sigmoid(v)


def _dot(a, b):
    return jnp.dot(a, b, preferred_element_type=F32)


def _dot_nt(a, b):
    return lax.dot_general(a, b, (((1,), (1,)), ((), ())), preferred_element_type=F32)


def _dot_tn(a, b):
    return lax.dot_general(a, b, (((0,), (0,)), ((), ())), preferred_element_type=F32)


def _pack_halves(v):
    m = v.shape[1] // 2
    lo = lax.bitcast_convert_type(v[:, :m].astype(BF16).astype(F32), U32) >> 16
    hi = lax.bitcast_convert_type(v[:, m:].astype(BF16).astype(F32), U32) & jnp.uint32(0xFFFF0000)
    return lax.bitcast_convert_type(lo | hi, F32)


def _unpack_halves(p):
    u = lax.bitcast_convert_type(p, U32)
    lo = lax.bitcast_convert_type(u << 16, F32)
    hi = lax.bitcast_convert_type(u & jnp.uint32(0xFFFF0000), F32)
    return lo.astype(BF16), hi.astype(BF16)


def _split3(a):
    hi = a.astype(BF16)
    r = a - hi.astype(F32)
    mid = r.astype(BF16)
    lo = (r - mid.astype(F32)).astype(BF16)
    return hi, mid, lo


def _ada_kernel(c_ref, w_ref, b_ref, o_ref):
    s = _silu(c_ref[...]).astype(BF16)
    o_ref[...] = _dot(s, w_ref[...].astype(BF16)) + b_ref[...]


def _ada(cc, w, b):
    r, d = cc.shape
    nc = w.shape[1]
    return pl.pallas_call(
        _ada_kernel,
        grid=(nc // ADA_TN,),
        in_specs=[pl.BlockSpec((r, d), lambda j: (0, 0)),
                  pl.BlockSpec((d, ADA_TN), lambda j: (0, j)),
                  pl.BlockSpec((1, ADA_TN), lambda j: (0, j))],
        out_specs=pl.BlockSpec((r, ADA_TN), lambda j: (0, j)),
        out_shape=jax.ShapeDtypeStruct((r, nc), F32),
        compiler_params=_cparams(("arbitrary",)),
        name="ada",
    )(cc, w, b)


def _inproj_kernel(x_ref, g_ref, sc_ref, sh_ref, w_ref, zf_ref, zr_ref):
    x = x_ref[0]
    ms = jnp.mean(x * x, axis=-1, keepdims=True)
    gs = g_ref[...] * (1.0 + sc_ref[0])
    h = x * lax.rsqrt(ms + EPS) * gs + sh_ref[0]
    z = _dot(h.astype(BF16), w_ref[...])
    nf = zf_ref.shape[2]
    zf_ref[0] = z[:, :nf]
    zr_ref[0] = z[:, nf:].astype(BF16)


def _inproj(x, g, sc, sh, w, nf):
    bsz, l, d = x.shape
    nc = w.shape[1]
    tm = min(INPROJ_TM, l)
    return pl.pallas_call(
        _inproj_kernel,
        grid=(bsz, l // tm),
        in_specs=[pl.BlockSpec((1, tm, d), lambda b, i: (b, i, 0)),
                  pl.BlockSpec((1, d), lambda b, i: (0, 0)),
                  pl.BlockSpec((1, 1, d), lambda b, i: (b, 0, 0)),
                  pl.BlockSpec((1, 1, d), lambda b, i: (b, 0, 0)),
                  pl.BlockSpec((d, nc), lambda b, i: (0, 0))],
        out_specs=[pl.BlockSpec((1, tm, nf), lambda b, i: (b, i, 0)),
                   pl.BlockSpec((1, tm, nc - nf), lambda b, i: (b, i, 0))],
        out_shape=[jax.ShapeDtypeStruct((bsz, l, nf), F32),
                   jax.ShapeDtypeStruct((bsz, l, nc - nf), BF16)],
        compiler_params=_cparams(("arbitrary", "arbitrary")),
        name="inproj",
    )(x, g, sc, sh, w)


class _HgrnDirection:
    def __init__(self, lbl, st_ref, reverse):
        self.st_ref, self.reverse = st_ref, reverse
        e = jnp.exp(lbl - jnp.max(lbl, axis=0, keepdims=True))
        self.lb = e[0:1] / jnp.sum(e, axis=0, keepdims=True)
        row = lax.broadcasted_iota(I32, (CHUNK, CHUNK), 0)
        col = lax.broadcasted_iota(I32, (CHUNK, CHUNK), 1)
        self.tri = (col >= row) if reverse else (col <= row)
        self.tm = jnp.where(self.tri, 1.0, 0.0).astype(BF16)

    def tile(self, f_ref, v_ref, q_ref=None, o_ref=None):
        heads = self.st_ref.shape[0]
        n_chunks = f_ref.shape[1] // CHUNK
        crow = lambda c: slice(c * CHUNK, (c + 1) * CHUNK)
        hsl = lambda h: slice(h * HEAD_DIM, (h + 1) * HEAD_DIM)
        f = self.lb + (1.0 - self.lb) * jax.nn.sigmoid(f_ref[0])
        k = 1.0 - f
        hi, mid, lo = _split3(jnp.log(f))
        cums = [_dot(self.tm, hi[crow(c)]) + _dot(self.tm, mid[crow(c)]) + _dot(self.tm, lo[crow(c)])
                for c in range(n_chunks)]
        lasts = [cm[0:1] if self.reverse else cm[CHUNK - 1:CHUNK] for cm in cums]
        cum = jnp.concatenate(cums, axis=0)
        last = jnp.concatenate([jnp.broadcast_to(la, (CHUNK, la.shape[1])) for la in lasts], axis=0)
        ks = (k * jnp.exp(last - cum)).astype(BF16)
        vb = v_ref[0].astype(BF16)
        kv = [[_dot_tn(vb[crow(c), hsl(h)], ks[crow(c), hsl(h)]) for h in range(heads)] for c in range(n_chunks)]
        dec = [jnp.exp(la) for la in lasts]
        emit = o_ref is not None
        if emit:
            qd = (_silu(q_ref[0].astype(F32)) * jnp.exp(cum)).astype(BF16)
            ki = (k * jnp.exp(-cum)).astype(BF16)
            sc = [[jnp.where(self.tri, _dot_nt(qd[crow(c), hsl(h)], ki[crow(c), hsl(h)]), 0.0).astype(BF16)
                   for h in range(heads)] for c in range(n_chunks)]
            intra = [[_dot(sc[c][h], vb[crow(c), hsl(h)]) for h in range(heads)] for c in range(n_chunks)]
        state = [self.st_ref[h] for h in range(heads)]
        for c in (range(n_chunks - 1, -1, -1) if self.reverse else range(n_chunks)):
            for h in range(heads):
                if emit:
                    inter = _dot_nt(qd[crow(c), hsl(h)], state[h].astype(BF16))
                    o_ref[0, crow(c), hsl(h)] = (intra[c][h] + inter).astype(o_ref.dtype)
                state[h] = state[h] * dec[c][:, hsl(h)] + kv[c][h]
        for h in range(heads):
            self.st_ref[h] = state[h]


def _hgrn_kernel(lbl_ref, cff_ref, cfb_ref, cv_ref, ff_ref, vf_ref, qf_ref, fb_ref, vb_ref, qb_ref,
                 of_ref, ob_ref, stf_ref, stb_ref):
    fwd = _HgrnDirection(lbl_ref[0], stf_ref, reverse=False)
    bwd = _HgrnDirection(lbl_ref[1], stb_ref, reverse=True)

    @pl.when(pl.program_id(1) == 0)
    def _():
        stf_ref[...] = jnp.zeros_like(stf_ref)
        stb_ref[...] = jnp.zeros_like(stb_ref)
        fwd.tile(cff_ref, cv_ref)
        bwd.tile(cfb_ref, cv_ref)

    fwd.tile(ff_ref, vf_ref, qf_ref, of_ref)
    bwd.tile(fb_ref, vb_ref, qb_ref, ob_ref)


def _hgrn(lbl, zcf, zcr, zf, zr):
    bsz, l, _ = zf.shape
    n_ctx_tok = zcf.shape[1]
    w = lbl.shape[2]
    heads = w // HEAD_DIM
    tt = min(HGRN_TT, l)
    nt = l // tt
    ctx_col = lambda c: pl.BlockSpec((1, n_ctx_tok, w), lambda b, i: (b, 0, c))
    fwd_col = lambda c: pl.BlockSpec((1, tt, w), lambda b, i: (b, i, c))
    bwd_col = lambda c: pl.BlockSpec((1, tt, w), lambda b, i: (b, nt - 1 - i, c))
    state = pltpu.VMEM((heads, HEAD_DIM, HEAD_DIM), F32)
    return pl.pallas_call(
        _hgrn_kernel,
        grid=(bsz, nt),
        in_specs=[pl.BlockSpec(lbl.shape, lambda b, i: (0, 0, 0)),
                  ctx_col(0), ctx_col(1), ctx_col(0),
                  fwd_col(0), fwd_col(0), fwd_col(1),
                  bwd_col(1), bwd_col(0), bwd_col(1)],
        out_specs=[fwd_col(0), bwd_col(0)],
        out_shape=[jax.ShapeDtypeStruct((bsz, l, w), BF16), jax.ShapeDtypeStruct((bsz, l, w), BF16)],
        scratch_shapes=[state, state],
        compiler_params=_cparams(("arbitrary", "arbitrary")),
        name="hgrn",
    )(lbl, zcf, zcf, zcr, zf, zr, zr, zf, zr, zr)


def _mix_kernel(of_ref, ob_ref, zg_ref, za_ref, zt_ref, x_ref, g1_ref, sc2_ref, sh2_ref,
                hg_ref, dwk_ref, dwb_ref, lng_ref, lnb_ref, wo_ref, n2g_ref, wr_ref, br_ref,
                xm_ref, h2_ref, eid_ref, wts_ref, rank_ref, cnt_ref,
                pad_ref, shf_ref, cv_ref, carry_ref, *, tm):
    w = of_ref.shape[2]
    heads = w // HEAD_DIM

    @pl.when((pl.program_id(0) == 0) & (pl.program_id(1) == 0))
    def _():
        carry_ref[...] = jnp.zeros_like(carry_ref)

    o = of_ref[0].astype(F32) + ob_ref[0].astype(F32)
    parts = []
    for h in range(heads):
        oh = o[:, h * HEAD_DIM:(h + 1) * HEAD_DIM]
        parts.append(oh * lax.rsqrt(jnp.mean(oh * oh, axis=-1, keepdims=True) + EPS))
    o_mix = jnp.concatenate(parts, axis=-1) * hg_ref[...] * _silu(zg_ref[0].astype(F32))

    u = za_ref[0].astype(F32) * jax.nn.sigmoid(zt_ref[0].astype(F32))
    lo, hi = 16, 16 + GRID_W
    pad_ref[0:lo, :] = jnp.zeros((lo, u.shape[1]), F32)
    pad_ref[hi:hi + 16, :] = jnp.zeros((16, u.shape[1]), F32)
    span = shf_ref.shape[1]
    for s in range(tm // GRID_W):
        pad_ref[lo:hi, :] = u[s * GRID_W:(s + 1) * GRID_W]
        for b in range(1, 8):
            shf_ref[b] = pad_ref[b:b + span, :]
        acc = jnp.zeros((GRID_W, u.shape[1]), F32) + dwb_ref[...]
        for k in range(CONV_K):
            a, b = divmod(lo - CONV_PAD + k, 8)
            src = pad_ref if b == 0 else shf_ref.at[b]
            acc = acc + src[8 * a:8 * a + GRID_W, :] * dwk_ref[k:k + 1, :]
        cv_ref[s * GRID_W:(s + 1) * GRID_W, :] = acc
    cv = cv_ref[...]
    mu = jnp.mean(cv, axis=-1, keepdims=True)
    cen = cv - mu
    var = jnp.mean(cen * cen, axis=-1, keepdims=True)
    c_mix = _silu(cen * lax.rsqrt(var + EPS) * lng_ref[...] + lnb_ref[...])

    mix = _dot(o_mix.astype(BF16), wo_ref[0:w, :]) + _dot(c_mix.astype(BF16), wo_ref[w:, :])
    xm = x_ref[0] + g1_ref[0] * mix
    xm_ref[0] = xm

    ms = jnp.mean(xm * xm, axis=-1, keepdims=True)
    h2 = xm * lax.rsqrt(ms + EPS) * (n2g_ref[...] * (1.0 + sc2_ref[0])) + sh2_ref[0]
    h2_ref[0] = _pack_halves(h2)

    hh = h2.astype(BF16)
    hl = (h2 - hh.astype(F32)).astype(BF16)
    wr = wr_ref[...]
    wh = wr.astype(BF16)
    wl = (wr - wh.astype(F32)).astype(BF16)
    lg = _dot_nt(wh, hh) + _dot_nt(wh, hl) + _dot_nt(wl, hh) + br_ref[...]

    neg = -jnp.inf
    r8 = lax.broadcasted_iota(I32, (EXPERTS_PER_GROUP, tm), 0).astype(F32)
    gl = jnp.where(r8 < N_GROUPS, lg[0:8], neg)
    gmax = jnp.max(gl, axis=0, keepdims=True)
    grp = jnp.min(jnp.where(gl == gmax, r8, 8.0), axis=0, keepdims=True)
    p_grp = 1.0 / jnp.sum(jnp.exp(gl - gmax), axis=0, keepdims=True)
    es = lg[8:16]
    for g in range(1, N_GROUPS):
        es = jnp.where(grp == float(g), lg[8 + 8 * g:16 + 8 * g], es)
    m1 = jnp.max(es, axis=0, keepdims=True)
    i1 = jnp.min(jnp.where(es == m1, r8, 8.0), axis=0, keepdims=True)
    es2 = jnp.where(r8 == i1, neg, es)
    m2 = jnp.max(es2, axis=0, keepdims=True)
    i2 = jnp.min(jnp.where(es2 == m2, r8, 8.0), axis=0, keepdims=True)
    e2 = jnp.exp(m2 - m1)
    w1 = p_grp / (1.0 + e2)
    w2 = p_grp * e2 / (1.0 + e2)
    eid0 = grp * float(EXPERTS_PER_GROUP) + i1
    eid1 = grp * float(EXPERTS_PER_GROUP) + i2
    eid_ref[0:1, :] = eid0.astype(I32)
    eid_ref[1:2, :] = eid1.astype(I32)
    wts_ref[0:1, :] = w1
    wts_ref[1:2, :] = w2

    r32 = lax.broadcasted_iota(I32, (N_EXPERTS, tm), 0).astype(F32)
    oh0 = jnp.where(r32 == eid0, 1.0, 0.0)
    oh1 = jnp.where(r32 == eid1, 1.0, 0.0)
    src = lax.broadcasted_iota(I32, (tm, tm), 0)
    dst = lax.broadcasted_iota(I32, (tm, tm), 1)
    before = jnp.where(src < dst, 1.0, 0.0).astype(BF16)
    ex = _dot(jnp.concatenate([oh0, oh1], axis=0).astype(BF16), before)
    base = ex[0:N_EXPERTS] + ex[N_EXPERTS:] + carry_ref[...]
    rank_ref[0:1, :] = jnp.sum(oh0 * base, axis=0, keepdims=True).astype(I32)
    rank_ref[1:2, :] = jnp.sum(oh1 * base, axis=0, keepdims=True).astype(I32)
    carry = carry_ref[...] + jnp.sum(oh0 + oh1, axis=1, keepdims=True)
    carry_ref[...] = carry
    cnt_ref[...] = jnp.broadcast_to(carry, cnt_ref.shape)


def _mix(o_f, o_b, z, x, g1, sc2, sh2, hg, dwk, dwb, lng, lnb, wo, n2g, wr, br, b0, bsz):
    _, l, d = x.shape
    w = o_f.shape[2]
    cw = dwk.shape[1]
    tm = min(MIX_TM, l)
    nt = l // tm
    n = bsz * l
    src = lambda c: (lambda b, i: (b0 + b, i, c))
    tok = lambda b, i: (b, i, 0)
    vec = lambda b, i: (0, 0)
    per_b = lambda b, i: (b0 + b, 0, 0)
    flat = lambda b, i: (0, b * nt + i)
    kern = functools.partial(_mix_kernel, tm=tm)
    return pl.pallas_call(
        kern,
        grid=(bsz, nt),
        in_specs=[pl.BlockSpec((1, tm, w), src(0)),
                  pl.BlockSpec((1, tm, w), src(0)),
                  pl.BlockSpec((1, tm, w), src(2)),
                  pl.BlockSpec((1, tm, cw), src(3)),
                  pl.BlockSpec((1, tm, cw), src(4)),
                  pl.BlockSpec((1, tm, d), src(0)),
                  pl.BlockSpec((1, 1, d), per_b),
                  pl.BlockSpec((1, 1, d), per_b),
                  pl.BlockSpec((1, 1, d), per_b),
                  pl.BlockSpec((1, w), vec),
                  pl.BlockSpec((CONV_K, cw), vec),
                  pl.BlockSpec((1, cw), vec),
                  pl.BlockSpec((1, cw), vec),
                  pl.BlockSpec((1, cw), vec),
                  pl.BlockSpec((w + cw, d), vec),
                  pl.BlockSpec((1, d), vec),
                  pl.BlockSpec((ROUTER_ROWS, d), vec),
                  pl.BlockSpec((ROUTER_ROWS, 1), vec)],
        out_specs=[pl.BlockSpec((1, tm, d), tok),
                   pl.BlockSpec((1, tm, d // 2), tok),
                   pl.BlockSpec((TOP_K, tm), flat),
                   pl.BlockSpec((TOP_K, tm), flat),
                   pl.BlockSpec((TOP_K, tm), flat),
                   pl.BlockSpec((N_EXPERTS, 128), vec)],
        out_shape=[jax.ShapeDtypeStruct((bsz, l, d), F32),
                   jax.ShapeDtypeStruct((bsz, l, d // 2), F32),
                   jax.ShapeDtypeStruct((TOP_K, n), I32),
                   jax.ShapeDtypeStruct((TOP_K, n), F32),
                   jax.ShapeDtypeStruct((TOP_K, n), I32),
                   jax.ShapeDtypeStruct((N_EXPERTS, 128), F32)],
        scratch_shapes=[pltpu.VMEM((GRID_W + 32, cw), F32),
                        pltpu.VMEM((8, GRID_W + 24, cw), F32),
                        pltpu.VMEM((tm, cw), F32),
                        pltpu.VMEM((N_EXPERTS, 1), F32)],
        compiler_params=_cparams(("arbitrary", "arbitrary")),
        name="mix",
    )(o_f, o_b, z, z, z, x, g1, sc2, sh2, hg, dwk, dwb, lng, lnb, wo, n2g, wr, br)


def _pos_kernel(ps_ref, eid_ref, rank_ref, pos_ref):
    eid = eid_ref[...]
    acc = rank_ref[...]
    for e in range(N_EXPERTS):
        acc = acc + jnp.where(eid == e, ps_ref[e], 0)
    pos_ref[...] = acc


def _positions(pstarts, eid, rank):
    shape = eid.shape
    eid2 = eid.reshape(-1, 128)
    full = pl.BlockSpec(eid2.shape, lambda i: (0, 0))
    pos = pl.pallas_call(
        _pos_kernel,
        grid=(1,),
        in_specs=[pl.BlockSpec(memory_space=pltpu.SMEM), full, full],
        out_specs=full,
        out_shape=jax.ShapeDtypeStruct(eid2.shape, I32),
        name="positions",
    )(pstarts, eid2, rank.reshape(-1, 128))
    return pos.reshape(shape)


def _sc_mesh():
    return plsc.VectorSubcoreMesh(core_axis_name="c", subcore_axis_name="s",
                                  num_cores=SC_CORES, num_subcores=SC_SUBCORES)


def _sc_worker():
    return lax.axis_index("s") * SC_CORES + lax.axis_index("c")


def _sc_dispatch(h2p, pos4, padidx, zero_rows, n_rows):
    n, dw = h2p.shape
    _, workers, chunks, _ = pos4.shape
    pad_chunks = padidx.shape[1]
    per_worker = n // workers

    def body(h_hbm, pos_hbm, pad_hbm, zero_hbm, out_hbm, idx_v, pad_v, rows_v):
        wid = _sc_worker()
        base = wid * per_worker
        for k in range(TOP_K):
            pltpu.sync_copy(pos_hbm.at[k, wid], idx_v.at[k])
        pltpu.sync_copy(pad_hbm.at[wid], pad_v)
        pltpu.sync_copy(zero_hbm, rows_v)
        for j in range(pad_chunks):
            pltpu.sync_copy(rows_v, out_hbm.at[pad_v.at[j]])
        for j in range(chunks):
            pltpu.sync_copy(h_hbm.at[pl.ds(base + j * SC_W, SC_W)], rows_v)
            for k in range(TOP_K):
                pltpu.sync_copy(rows_v, out_hbm.at[idx_v.at[k, j]])

    return pl.kernel(
        body,
        out_type=jax.ShapeDtypeStruct((n_rows, dw), F32),
        mesh=_sc_mesh(),
        scratch_types=[pltpu.VMEM((TOP_K, chunks, SC_W), I32),
                       pltpu.VMEM((pad_chunks, SC_W), I32),
                       pltpu.VMEM((SC_W, dw), F32)],
        name="sc_dispatch",
    )(h2p, pos4, padidx, zero_rows)


def _sc_combine_gather(ybuf, pos4):
    _, dw = ybuf.shape
    _, workers, chunks, _ = pos4.shape
    per_worker = chunks * SC_W

    def body(y_hbm, pos_hbm, out_hbm, idx_v, rows_v):
        wid = _sc_worker()
        base = wid * per_worker
        for k in range(TOP_K):
            pltpu.sync_copy(pos_hbm.at[k, wid], idx_v.at[k])
        for k in range(TOP_K):
            for j in range(chunks):
                pltpu.sync_copy(y_hbm.at[idx_v.at[k, j]], rows_v)
                pltpu.sync_copy(rows_v, out_hbm.at[k, pl.ds(base + j * SC_W, SC_W)])

    return pl.kernel(
        body,
        out_type=jax.ShapeDtypeStruct((TOP_K, workers * per_worker, dw), F32),
        mesh=_sc_mesh(),
        scratch_types=[pltpu.VMEM((TOP_K, chunks, SC_W), I32),
                       pltpu.VMEM((SC_W, dw), F32)],
        name="sc_combine",
    )(ybuf, pos4)


def _ffn_kernel(blk_e_ref, nblk_ref, x_ref, wg_ref, wu_ref, wd_ref, y_ref, wgb_ref, wub_ref, wdb_ref):
    i = pl.program_id(0)
    used = i < nblk_ref[0]
    new_expert = (i == 0) | (blk_e_ref[i] != blk_e_ref[jnp.maximum(i - 1, 0)])
    half = x_ref.shape[1]

    @pl.when(used & new_expert)
    def _():
        wgb_ref[...] = wg_ref[0].astype(BF16)
        wub_ref[...] = wu_ref[0].astype(BF16)
        wdb_ref[...] = wd_ref[0].astype(BF16)

    @pl.when(used)
    def _():
        x_lo, x_hi = _unpack_halves(x_ref[...])
        g = _dot(x_lo, wgb_ref[0:half, :]) + _dot(x_hi, wgb_ref[half:, :])
        u = _dot(x_lo, wub_ref[0:half, :]) + _dot(x_hi, wub_ref[half:, :])
        y_ref[...] = _pack_halves(_dot((_silu(g) * u).astype(BF16), wdb_ref[...]))

    @pl.when(jnp.logical_not(used))
    def _():
        y_ref[...] = jnp.zeros_like(y_ref)


def _ffn(blk_e, nblk, xbuf, wg, wu, wd, tb):
    _, dw = xbuf.shape
    _, d, de = wg.shape
    n_blk = blk_e.shape[0]
    x_blk = lambda i, be, nb: (jnp.minimum(i, nb[0] - 1), 0)
    grid_spec = pltpu.PrefetchScalarGridSpec(
        num_scalar_prefetch=2,
        grid=(n_blk,),
        in_specs=[pl.BlockSpec((tb, dw), x_blk),
                  pl.BlockSpec((1, d, de), lambda i, be, nb: (be[i], 0, 0)),
                  pl.BlockSpec((1, d, de), lambda i, be, nb: (be[i], 0, 0)),
                  pl.BlockSpec((1, de, d), lambda i, be, nb: (be[i], 0, 0))],
        out_specs=pl.BlockSpec((tb, dw), lambda i, be, nb: (i, 0)),
        scratch_shapes=[pltpu.VMEM((d, de), BF16),
                        pltpu.VMEM((d, de), BF16),
                        pltpu.VMEM((de, d), BF16)],
    )
    return pl.pallas_call(
        _ffn_kernel,
        grid_spec=grid_spec,
        out_shape=jax.ShapeDtypeStruct((n_blk * tb, dw), F32),
        compiler_params=_cparams(("arbitrary",)),
        name="ffn",
    )(blk_e, nblk, xbuf, wg, wu, wd)


def _final_kernel(yg_ref, xm_ref, wt_ref, g2_ref, fg_ref, *out_refs):
    o_ref = out_refs[-1]
    wt = wt_ref[...]
    y = None
    for k in range(TOP_K):
        lo, hi = _unpack_halves(yg_ref[k])
        yk = jnp.concatenate([lo, hi], axis=-1).astype(F32) * wt[:, k:k + 1]
        y = yk if y is None else y + yk
    x = xm_ref[0] + g2_ref[0] * y
    ms = jnp.mean(x * x, axis=-1, keepdims=True)
    o_ref[0] = x * lax.rsqrt(ms + EPS) * fg_ref[...]


def _final(yg, xm, wt, g2, fg, out_prev, b0, total):
    bsz, l, d = xm.shape
    dw = yg.shape[2]
    tm = min(MOE_TM, l)
    nt = l // tm
    in_specs = [pl.BlockSpec((TOP_K, tm, dw), lambda b, i: (0, b * nt + i, 0)),
                pl.BlockSpec((1, tm, d), lambda b, i: (b, i, 0)),
                pl.BlockSpec((tm, TOP_K), lambda b, i: (b * nt + i, 0)),
                pl.BlockSpec((1, 1, d), lambda b, i: (b0 + b, 0, 0)),
                pl.BlockSpec((1, d), lambda b, i: (0, 0))]
    args = [yg, xm, wt, g2, fg]
    aliases = {}
    if out_prev is not None:
        in_specs.append(pl.BlockSpec(memory_space=pl.ANY))
        args.append(out_prev)
        aliases = {5: 0}
    return pl.pallas_call(
        _final_kernel,
        grid=(bsz, nt),
        in_specs=in_specs,
        out_specs=pl.BlockSpec((1, tm, d), lambda b, i: (b0 + b, i, 0)),
        out_shape=jax.ShapeDtypeStruct((total, l, d), F32),
        input_output_aliases=aliases,
        compiler_params=_cparams(("arbitrary", "arbitrary")),
        name="final",
    )(*args)


def _layer(x, ctx, mod, mod_c, norm1_g, w_in, lb_logits, hgrn_norm_g, dw_kernel, dw_bias, conv_ln_g,
           conv_ln_b, w_out, norm2_g, wr, br, w_gate, w_up, w_down, final_norm_g):
    bsz, l, d = x.shape
    w = lb_logits.shape[2]
    heads = w // HEAD_DIM
    row = lambda v: v.reshape(1, -1)
    per_b = lambda v: v.reshape(bsz, 1, d)
    sh1, sc1, g1, sh2, sc2, g2 = [per_b(mod[:, j]) for j in range(N_MOD)]
    csh1 = jnp.broadcast_to(mod_c[0].reshape(1, 1, d), (bsz, 1, d))
    csc1 = jnp.broadcast_to(mod_c[1].reshape(1, 1, d), (bsz, 1, d))

    w_in_b = w_in.astype(BF16)
    zf, zr = _inproj(x, row(norm1_g), sc1, sh1, w_in_b, 2 * w)
    zcf, zcr = _inproj(ctx, row(norm1_g), csc1, csh1, w_in_b[:, :3 * w], 2 * w)
    o_f, o_b = _hgrn(lb_logits, zcf, zcr, zf, zr)
    w_out_b = w_out.astype(BF16)

    groups = MOE_GROUPS if bsz % MOE_GROUPS == 0 else 1
    out = None
    for gi in range(groups):
        b0, gb = gi * (bsz // groups), bsz // groups
        mixed = _mix(o_f, o_b, zr, x, g1, sc2, sh2, row(jnp.tile(hgrn_norm_g, heads)), dw_kernel,
                     row(dw_bias), row(conv_ln_g), row(conv_ln_b), w_out_b, row(norm2_g), wr, br, b0, gb)
        out = _moe_group(mixed, g2, w_gate, w_up, w_down, row(final_norm_g), out, b0, bsz)
    return out


def _moe_group(mixed, g2, w_gate, w_up, w_down, final_g, out_prev, b0, total):
    xm, h2, eid, wts, rank, cnt = mixed
    gb, l, d = xm.shape
    n = gb * l

    tb = FFN_TB
    n_blk = -(-(n * TOP_K) // tb) + N_EXPERTS
    counts = cnt[:, 0].astype(I32)
    pcounts = (counts + tb - 1) // tb * tb
    pends = jnp.cumsum(pcounts)
    pstarts = pends - pcounts
    pos = _positions(pstarts.astype(I32), eid, rank)
    blk_start = jnp.arange(n_blk, dtype=I32) * tb
    blk_e = jnp.minimum(jnp.sum((pends[None, :] <= blk_start[:, None]).astype(I32), axis=1), N_EXPERTS - 1)
    nblk_used = (pends[-1:] // tb).astype(I32)

    workers = SC_CORES * SC_SUBCORES
    pos4 = pos.reshape(TOP_K, workers, n // (workers * SC_W), SC_W)
    lane = jnp.arange(tb, dtype=I32)[None, :]
    pad_rows = jnp.where(lane < (pcounts - counts)[:, None], (pstarts + counts)[:, None] + lane,
                         n_blk * tb + lane)
    padidx = pad_rows.astype(I32).reshape(workers, N_EXPERTS // workers * tb // SC_W, SC_W)
    xbuf = _sc_dispatch(h2.reshape(n, d // 2), pos4, padidx, jnp.zeros((SC_W, d // 2), F32), (n_blk + 1) * tb)
    ybuf = _ffn(blk_e, nblk_used, xbuf, w_gate, w_up, w_down, tb)
    yg = _sc_combine_gather(ybuf, pos4)
    return _final(yg, xm, wts.T, g2, final_g, out_prev, b0, total)


def kernel(x, c, ctx, c_ctx, w_ada, b_ada, norm1_g, w_in, lb_logits, hgrn_norm_g, dw_kernel, dw_bias,
           conv_ln_g, conv_ln_b, w_out, norm2_g, router_group_w, router_group_b, router_expert_w,
           router_expert_b, w_expert_gate, w_expert_up, w_expert_down, final_norm_g):
    depth = w_ada.shape[0]
    assert depth == 1, "context tokens are only updated between layers; a single layer is implemented"
    bsz, l, d = x.shape
    l0 = 0
    rows = -(-(bsz + 1) // 8) * 8
    cc = jnp.zeros((rows, d), F32).at[:bsz].set(c).at[bsz].set(c_ctx)
    mod_all = _ada(cc, w_ada[l0], b_ada[l0].reshape(1, -1))
    mod = mod_all[:bsz].reshape(bsz, N_MOD, d)
    mod_c = mod_all[bsz].reshape(N_MOD, d)
    pad8 = jnp.zeros((d, 8 - N_GROUPS), F32)
    wr = jnp.concatenate([router_group_w[l0], pad8, router_expert_w[l0], jnp.zeros((d, 8), F32)], axis=1).T
    br = jnp.concatenate([router_group_b[l0], jnp.zeros((8 - N_GROUPS,), F32), router_expert_b[l0],
                          jnp.zeros((8,), F32)]).reshape(ROUTER_ROWS, 1)
    return _layer(x, ctx, mod, mod_c, norm1_g[l0], w_in[l0], lb_logits, hgrn_norm_g[l0], dw_kernel[l0], dw_bias[l0], conv_ln_g[l0], conv_ln_b[l0], w_out[l0],
                  norm2_g[l0], wr, br, w_expert_gate[l0], w_expert_up[l0], w_expert_down[l0], final_norm_g)
```

```python
import functools

import numpy as np

import jax
import jax.numpy as jnp
from jax import lax
from jax.experimental import pallas as pl
from jax.experimental.pallas import tpu as pltpu
from jax.experimental.pallas import tpu_sc as plsc

F32 = jnp.float32
BF16 = jnp.bfloat16
I32 = jnp.int32
U32 = jnp.uint32

EPS = 1e-6
HEAD_DIM = 128
CHUNK = 64
GRID_W = 64
CONV_K = 31
CONV_PAD = CONV_K // 2
CONV_LEAD = 16
CONV_SPAN = GRID_W + 24
N_GROUPS = 4
EXPERTS_PER_GROUP = 8
N_EXPERTS = N_GROUPS * EXPERTS_PER_GROUP
TOP_K = 2
N_MOD = 6
ROUTER_ROWS = 48

ADA_TN = 1024
INPROJ_TM = 512
HGRN_TT = 256
MIX_TM = 256
MOE_TM = 256
FFN_TB = 512
VMEM_LIMIT = 48 * 1024 * 1024
SC_CORES = 2
SC_SUBCORES = 16
SC_W = 128
MOE_GROUPS = 1


def _cparams(sem):
    return pltpu.CompilerParams(dimension_semantics=sem, vmem_limit_bytes=VMEM_LIMIT)


def _silu(v):
    return v * jax.nn.sigmoid(v)


def _dot(a, b):
    return jnp.dot(a, b, preferred_element_type=F32)


def _dot_nt(a, b):
    return lax.dot_general(a, b, (((1,), (1,)), ((), ())), preferred_element_type=F32)


def _dot_tn(a, b):
    return lax.dot_general(a, b, (((0,), (0,)), ((), ())), preferred_element_type=F32)


def _pack_halves(vr):
    m = vr.shape[1] // 2
    bits = lax.bitcast_convert_type(vr, U32)
    return lax.bitcast_convert_type((bits[:, :m] >> 16) | bits[:, m:], F32)


def _unpack_halves(p):
    u = lax.bitcast_convert_type(p, U32)
    lo = lax.bitcast_convert_type(u << 16, F32)
    hi = lax.bitcast_convert_type(u & jnp.uint32(0xFFFF0000), F32)
    return lo.astype(BF16), hi.astype(BF16)


def _split2(a):
    hi = a.astype(BF16)
    lo = (a - hi.astype(F32)).astype(BF16)
    return hi, lo


def _ada_kernel(c_ref, w_ref, b_ref, o_ref):
    s = _silu(c_ref[...]).astype(BF16)
    o_ref[...] = _dot(s, w_ref[...].astype(BF16)) + b_ref[...]


def _ada(cc, w, b):
    r, d = cc.shape
    nc = w.shape[1]
    return pl.pallas_call(
        _ada_kernel,
        grid=(nc // ADA_TN,),
        in_specs=[pl.BlockSpec((r, d), lambda j: (0, 0)),
                  pl.BlockSpec((d, ADA_TN), lambda j: (0, j)),
                  pl.BlockSpec((1, ADA_TN), lambda j: (0, j))],
        out_specs=pl.BlockSpec((r, ADA_TN), lambda j: (0, j)),
        out_shape=jax.ShapeDtypeStruct((r, nc), F32),
        compiler_params=_cparams(("arbitrary",)),
        name="ada",
    )(cc, w, b)


def _inproj_kernel(x_ref, g_ref, sc_ref, sh_ref, w_ref, zf_ref, zr_ref):
    x = x_ref[0]
    ms = jnp.mean(x * x, axis=-1, keepdims=True)
    gs = g_ref[...] * (1.0 + sc_ref[0])
    h = x * lax.rsqrt(ms + EPS) * gs + sh_ref[0]
    z = _dot(h.astype(BF16), w_ref[...])
    nf = zf_ref.shape[2]
    zf_ref[0] = z[:, :nf]
    zr_ref[0] = z[:, nf:].astype(BF16)


def _inproj(x, g, sc, sh, w, nf):
    bsz, l, d = x.shape
    nc = w.shape[1]
    tm = min(INPROJ_TM, l)
    return pl.pallas_call(
        _inproj_kernel,
        grid=(bsz, l // tm),
        in_specs=[pl.BlockSpec((1, tm, d), lambda b, i: (b, i, 0)),
                  pl.BlockSpec((1, d), lambda b, i: (0, 0)),
                  pl.BlockSpec((1, 1, d), lambda b, i: (b, 0, 0)),
                  pl.BlockSpec((1, 1, d), lambda b, i: (b, 0, 0)),
                  pl.BlockSpec((d, nc), lambda b, i: (0, 0))],
        out_specs=[pl.BlockSpec((1, tm, nf), lambda b, i: (b, i, 0)),
                   pl.BlockSpec((1, tm, nc - nf), lambda b, i: (b, i, 0))],
        out_shape=[jax.ShapeDtypeStruct((bsz, l, nf), F32),
                   jax.ShapeDtypeStruct((bsz, l, nc - nf), BF16)],
        compiler_params=_cparams(("arbitrary", "arbitrary")),
        name="inproj",
    )(x, g, sc, sh, w)


class _HgrnDirection:
    def __init__(self, lbl, st_ref, reverse):
        self.st_ref, self.reverse = st_ref, reverse
        e = jnp.exp(lbl - jnp.max(lbl, axis=0, keepdims=True))
        self.lb = e[0:1] / jnp.sum(e, axis=0, keepdims=True)
        row = lax.broadcasted_iota(I32, (CHUNK, CHUNK), 0)
        col = lax.broadcasted_iota(I32, (CHUNK, CHUNK), 1)
        self.tri = (col >= row) if reverse else (col <= row)
        self.tm = jnp.where(self.tri, 1.0, 0.0).astype(BF16)

    def tile(self, f_ref, v_ref, q_ref=None, o_ref=None):
        heads = self.st_ref.shape[0]
        n_chunks = f_ref.shape[1] // CHUNK
        crow = lambda c: slice(c * CHUNK, (c + 1) * CHUNK)
        hsl = lambda h: slice(h * HEAD_DIM, (h + 1) * HEAD_DIM)
        f = self.lb + (1.0 - self.lb) * jax.nn.sigmoid(f_ref[0])
        k = 1.0 - f
        hi, lo = _split2(jnp.log(f))
        cums = [_dot(self.tm, hi[crow(c)]) + _dot(self.tm, lo[crow(c)])
                for c in range(n_chunks)]
        lasts = [cm[0:1] if self.reverse else cm[CHUNK - 1:CHUNK] for cm in cums]
        cum = jnp.concatenate(cums, axis=0)
        last = jnp.concatenate([jnp.broadcast_to(la, (CHUNK, la.shape[1])) for la in lasts], axis=0)
        ks = (k * jnp.exp(last - cum)).astype(BF16)
        vb = v_ref[0].astype(BF16)
        kv = [[_dot_tn(vb[crow(c), hsl(h)], ks[crow(c), hsl(h)]) for h in range(heads)] for c in range(n_chunks)]
        dec = [jnp.exp(la) for la in lasts]
        emit = o_ref is not None
        if emit:
            qd = (_silu(q_ref[0].astype(F32)) * jnp.exp(cum)).astype(BF16)
            ki = (k * jnp.exp(-cum)).astype(BF16)
            sc = [[jnp.where(self.tri, _dot_nt(qd[crow(c), hsl(h)], ki[crow(c), hsl(h)]), 0.0).astype(BF16)
                   for h in range(heads)] for c in range(n_chunks)]
            intra = [[_dot(sc[c][h], vb[crow(c), hsl(h)]) for h in range(heads)] for c in range(n_chunks)]
        state = [self.st_ref[h] for h in range(heads)]
        for c in (range(n_chunks - 1, -1, -1) if self.reverse else range(n_chunks)):
            for h in range(heads):
                if emit:
                    inter = _dot_nt(qd[crow(c), hsl(h)], state[h].astype(BF16))
                    o_ref[0, crow(c), hsl(h)] = (intra[c][h] + inter).astype(o_ref.dtype)
                state[h] = state[h] * dec[c][:, hsl(h)] + kv[c][h]
        for h in range(heads):
            self.st_ref[h] = state[h]


def _hgrn_kernel(lbl_ref, cff_ref, cfb_ref, cv_ref, ff_ref, vf_ref, qf_ref, fb_ref, vb_ref, qb_ref,
                 of_ref, ob_ref, stf_ref, stb_ref):
    fwd = _HgrnDirection(lbl_ref[0], stf_ref, reverse=False)
    bwd = _HgrnDirection(lbl_ref[1], stb_ref, reverse=True)

    @pl.when(pl.program_id(1) == 0)
    def _():
        stf_ref[...] = jnp.zeros_like(stf_ref)
        stb_ref[...] = jnp.zeros_like(stb_ref)
        fwd.tile(cff_ref, cv_ref)
        bwd.tile(cfb_ref, cv_ref)

    fwd.tile(ff_ref, vf_ref, qf_ref, of_ref)
    bwd.tile(fb_ref, vb_ref, qb_ref, ob_ref)


def _hgrn(lbl, zcf, zcr, zf, zr):
    bsz, l, _ = zf.shape
    n_ctx_tok = zcf.shape[1]
    w = lbl.shape[2]
    heads = w // HEAD_DIM
    tt = min(HGRN_TT, l)
    nt = l // tt
    ctx_col = lambda c: pl.BlockSpec((1, n_ctx_tok, w), lambda b, i: (b, 0, c))
    fwd_col = lambda c: pl.BlockSpec((1, tt, w), lambda b, i: (b, i, c))
    bwd_col = lambda c: pl.BlockSpec((1, tt, w), lambda b, i: (b, nt - 1 - i, c))
    state = pltpu.VMEM((heads, HEAD_DIM, HEAD_DIM), F32)
    return pl.pallas_call(
        _hgrn_kernel,
        grid=(bsz, nt),
        in_specs=[pl.BlockSpec(lbl.shape, lambda b, i: (0, 0, 0)),
                  ctx_col(0), ctx_col(1), ctx_col(0),
                  fwd_col(0), fwd_col(0), fwd_col(1),
                  bwd_col(1), bwd_col(0), bwd_col(1)],
        out_specs=[fwd_col(0), bwd_col(0)],
        out_shape=[jax.ShapeDtypeStruct((bsz, l, w), BF16), jax.ShapeDtypeStruct((bsz, l, w), BF16)],
        scratch_shapes=[state, state],
        compiler_params=_cparams(("arbitrary", "arbitrary")),
        name="hgrn",
    )(lbl, zcf, zcf, zcr, zf, zr, zr, zf, zr, zr)


def _mix_kernel(of_ref, ob_ref, zg_ref, za_ref, zt_ref, x_ref, g1_ref, sc2_ref, sh2_ref,
                hg_ref, dwk_ref, dwb_ref, lng_ref, lnb_ref, wo_ref, n2g_ref, wr_ref, br_ref, shm_ref,
                xm_ref, h2_ref, eid_ref, wts_ref, rank_ref, cnt_ref,
                shf_ref, cv_ref, carry_ref, hh_ref, hl_ref, *, tm):
    w = of_ref.shape[2]
    heads = w // HEAD_DIM
    step = pl.program_id(0)

    @pl.when(step == 0)
    def _():
        carry_ref[...] = jnp.zeros_like(carry_ref)
        hh_ref[...] = jnp.zeros_like(hh_ref)
        hl_ref[...] = jnp.zeros_like(hl_ref)

    router = _route(hh_ref[...], hl_ref[...], wr_ref, br_ref, carry_ref, jnp.where(step > 0, 1.0, 0.0),
                    eid_ref, wts_ref, rank_ref, cnt_ref)
    next(router)

    o = of_ref[0].astype(F32) + ob_ref[0].astype(F32)
    parts = []
    for h in range(heads):
        oh = o[:, h * HEAD_DIM:(h + 1) * HEAD_DIM]
        parts.append(oh * lax.rsqrt(jnp.mean(oh * oh, axis=-1, keepdims=True) + EPS))
    o_mix = jnp.concatenate(parts, axis=-1) * hg_ref[...] * _silu(zg_ref[0].astype(F32))

    u = za_ref[0].astype(F32) * jax.nn.sigmoid(zt_ref[0].astype(F32))
    span = shf_ref.shape[2]
    u_hi, u_lo = _split2(u)
    for s in range(tm // GRID_W):
        rows = slice(s * GRID_W, (s + 1) * GRID_W)
        shifted = _dot(shm_ref[...], jnp.concatenate([u_hi[rows], u_lo[rows]], axis=0))
        for b in range(8):
            shf_ref[s, b] = shifted[b * span:(b + 1) * span]
    next(router)
    for s in range(tm // GRID_W):
        acc = jnp.zeros((GRID_W, u.shape[1]), F32) + dwb_ref[...]
        for k in range(CONV_K):
            a, b = divmod(CONV_LEAD - CONV_PAD + k, 8)
            acc = acc + shf_ref[s, b, 8 * a:8 * a + GRID_W, :] * dwk_ref[k:k + 1, :]
        cv_ref[s * GRID_W:(s + 1) * GRID_W, :] = acc
        if s == 0:
            next(router)
    cv = cv_ref[...]
    mu = jnp.mean(cv, axis=-1, keepdims=True)
    cen = cv - mu
    var = jnp.mean(cen * cen, axis=-1, keepdims=True)
    c_mix = _silu(cen * lax.rsqrt(var + EPS) * lng_ref[...] + lnb_ref[...])
    next(router, None)

    mix = _dot(o_mix.astype(BF16), wo_ref[0:w, :]) + _dot(c_mix.astype(BF16), wo_ref[w:, :])
    xm = x_ref[0] + g1_ref[0] * mix
    xm_ref[0] = xm

    ms = jnp.mean(xm * xm, axis=-1, keepdims=True)
    h2 = xm * lax.rsqrt(ms + EPS) * (n2g_ref[...] * (1.0 + sc2_ref[0])) + sh2_ref[0]
    hh = h2.astype(BF16)
    h2_rounded = hh.astype(F32)
    h2_ref[0] = _pack_halves(h2_rounded)
    hh_ref[...] = hh
    hl_ref[...] = (h2 - h2_rounded).astype(BF16)


def _route(hh, hl, wr_ref, br_ref, carry_ref, live, eid_ref, wts_ref, rank_ref, cnt_ref):
    tm = hh.shape[0]
    wr = wr_ref[...]
    wh = wr.astype(BF16)
    wl = (wr - wh.astype(F32)).astype(BF16)
    lg = _dot_nt(wh, hh) + _dot_nt(wh, hl) + _dot_nt(wl, hh) + br_ref[...]
    yield

    neg = -jnp.inf
    r8 = lax.broadcasted_iota(I32, (EXPERTS_PER_GROUP, tm), 0).astype(F32)
    gl = jnp.where(r8 < N_GROUPS, lg[0:8], neg)
    gmax = jnp.max(gl, axis=0, keepdims=True)
    grp = jnp.min(jnp.where(gl == gmax, r8, 8.0), axis=0, keepdims=True)
    p_grp = 1.0 / jnp.sum(jnp.exp(gl - gmax), axis=0, keepdims=True)
    es = lg[8:16]
    for g in range(1, N_GROUPS):
        es = jnp.where(grp == float(g), lg[8 + 8 * g:16 + 8 * g], es)
    m1 = jnp.max(es, axis=0, keepdims=True)
    i1 = jnp.min(jnp.where(es == m1, r8, 8.0), axis=0, keepdims=True)
    es2 = jnp.where(r8 == i1, neg, es)
    m2 = jnp.max(es2, axis=0, keepdims=True)
    i2 = jnp.min(jnp.where(es2 == m2, r8, 8.0), axis=0, keepdims=True)
    e2 = jnp.exp(m2 - m1)
    w1 = p_grp / (1.0 + e2)
    w2 = p_grp * e2 / (1.0 + e2)
    eid0 = grp * float(EXPERTS_PER_GROUP) + i1
    eid1 = grp * float(EXPERTS_PER_GROUP) + i2
    eid_ref[0:1, :] = eid0.astype(I32)
    eid_ref[1:2, :] = eid1.astype(I32)
    wts_ref[0:1, :] = w1
    wts_ref[1:2, :] = w2
    yield

    r32 = lax.broadcasted_iota(I32, (N_EXPERTS, tm), 0).astype(F32)
    oh0 = jnp.where(r32 == eid0, live, 0.0)
    oh1 = jnp.where(r32 == eid1, live, 0.0)
    src = lax.broadcasted_iota(I32, (tm, tm), 0)
    dst = lax.broadcasted_iota(I32, (tm, tm), 1)
    before = jnp.where(src < dst, 1.0, 0.0).astype(BF16)
    ex = _dot(jnp.concatenate([oh0, oh1], axis=0).astype(BF16), before)
    yield
    base = ex[0:N_EXPERTS] + ex[N_EXPERTS:] + carry_ref[...]
    rank_ref[0:1, :] = jnp.sum(oh0 * base, axis=0, keepdims=True).astype(I32)
    rank_ref[1:2, :] = jnp.sum(oh1 * base, axis=0, keepdims=True).astype(I32)
    carry = carry_ref[...] + jnp.sum(oh0 + oh1, axis=1, keepdims=True)
    carry_ref[...] = carry
    cnt_ref[...] = jnp.broadcast_to(carry, cnt_ref.shape)


def _conv_shift_matrix():
    r = np.arange(8 * CONV_SPAN)
    src = r // CONV_SPAN + r % CONV_SPAN - CONV_LEAD
    sel = (src[:, None] == np.arange(GRID_W)[None, :]).astype(np.float32)
    return jnp.asarray(np.concatenate([sel, sel], axis=1), dtype=BF16)


def _mix(o_f, o_b, z, x, g1, sc2, sh2, hg, dwk, dwb, lng, lnb, wo, n2g, wr, br, b0, bsz):
    _, l, d = x.shape
    shm = _conv_shift_matrix()
    w = o_f.shape[2]
    cw = dwk.shape[1]
    tm = min(MIX_TM, l)
    nt = l // tm
    n = bsz * l
    tiles = bsz * nt
    mixed = lambda s: jnp.minimum(s, tiles - 1)
    src = lambda c: (lambda s: (b0 + mixed(s) // nt, mixed(s) % nt, c))
    tok = lambda s: (mixed(s) // nt, mixed(s) % nt, 0)
    vec = lambda s: (0, 0)
    per_b = lambda s: (b0 + mixed(s) // nt, 0, 0)
    flat = lambda s: (0, jnp.maximum(s - 1, 0))
    kern = functools.partial(_mix_kernel, tm=tm)
    return pl.pallas_call(
        kern,
        grid=(tiles + 1,),
        in_specs=[pl.BlockSpec((1, tm, w), src(0)),
                  pl.BlockSpec((1, tm, w), src(0)),
                  pl.BlockSpec((1, tm, w), src(2)),
                  pl.BlockSpec((1, tm, cw), src(3)),
                  pl.BlockSpec((1, tm, cw), src(4)),
                  pl.BlockSpec((1, tm, d), src(0)),
                  pl.BlockSpec((1, 1, d), per_b),
                  pl.BlockSpec((1, 1, d), per_b),
                  pl.BlockSpec((1, 1, d), per_b),
                  pl.BlockSpec((1, w), vec),
                  pl.BlockSpec((CONV_K, cw), vec),
                  pl.BlockSpec((1, cw), vec),
                  pl.BlockSpec((1, cw), vec),
                  pl.BlockSpec((1, cw), vec),
                  pl.BlockSpec((w + cw, d), vec),
                  pl.BlockSpec((1, d), vec),
                  pl.BlockSpec((ROUTER_ROWS, d), vec),
                  pl.BlockSpec((ROUTER_ROWS, 1), vec),
                  pl.BlockSpec(shm.shape, vec)],
        out_specs=[pl.BlockSpec((1, tm, d), tok),
                   pl.BlockSpec((1, tm, d // 2), tok),
                   pl.BlockSpec((TOP_K, tm), flat),
                   pl.BlockSpec((TOP_K, tm), flat),
                   pl.BlockSpec((TOP_K, tm), flat),
                   pl.BlockSpec((N_EXPERTS, 128), vec)],
        out_shape=[jax.ShapeDtypeStruct((bsz, l, d), F32),
                   jax.ShapeDtypeStruct((bsz, l, d // 2), F32),
                   jax.ShapeDtypeStruct((TOP_K, n), I32),
                   jax.ShapeDtypeStruct((TOP_K, n), F32),
                   jax.ShapeDtypeStruct((TOP_K, n), I32),
                   jax.ShapeDtypeStruct((N_EXPERTS, 128), F32)],
        scratch_shapes=[pltpu.VMEM((tm // GRID_W, 8, CONV_SPAN, cw), F32),
                        pltpu.VMEM((tm, cw), F32),
                        pltpu.VMEM((N_EXPERTS, 1), F32),
                        pltpu.VMEM((tm, d), BF16),
                        pltpu.VMEM((tm, d), BF16)],
        compiler_params=_cparams(("arbitrary",)),
        name="mix",
    )(o_f, o_b, z, z, z, x, g1, sc2, sh2, hg, dwk, dwb, lng, lnb, wo, n2g, wr, br, shm)


def _pos_kernel(ps_ref, eid_ref, rank_ref, pos_ref):
    eid = eid_ref[...]
    acc = rank_ref[...]
    for e in range(N_EXPERTS):
        acc = acc + jnp.where(eid == e, ps_ref[e], 0)
    pos_ref[...] = acc


def _positions(pstarts, eid, rank):
    shape = eid.shape
    eid2 = eid.reshape(-1, 128)
    full = pl.BlockSpec(eid2.shape, lambda i: (0, 0))
    pos = pl.pallas_call(
        _pos_kernel,
        grid=(1,),
        in_specs=[pl.BlockSpec(memory_space=pltpu.SMEM), full, full],
        out_specs=full,
        out_shape=jax.ShapeDtypeStruct(eid2.shape, I32),
        name="positions",
    )(pstarts, eid2, rank.reshape(-1, 128))
    return pos.reshape(shape)


def _sc_mesh():
    return plsc.VectorSubcoreMesh(core_axis_name="c", subcore_axis_name="s",
                                  num_cores=SC_CORES, num_subcores=SC_SUBCORES)


def _sc_worker():
    return lax.axis_index("s") * SC_CORES + lax.axis_index("c")


def _sc_dispatch(h2p, pos4, padidx, zero_rows, n_rows):
    n, dw = h2p.shape
    _, workers, chunks, _ = pos4.shape
    pad_chunks = padidx.shape[1]
    per_worker = n // workers

    def body(h_hbm, pos_hbm, pad_hbm, zero_hbm, out_hbm, idx_v, pad_v, rows_v):
        wid = _sc_worker()
        base = wid * per_worker
        for k in range(TOP_K):
            pltpu.sync_copy(pos_hbm.at[k, wid], idx_v.at[k])
        pltpu.sync_copy(pad_hbm.at[wid], pad_v)
        pltpu.sync_copy(zero_hbm, rows_v)
        for j in range(pad_chunks):
            pltpu.sync_copy(rows_v, out_hbm.at[pad_v.at[j]])
        for j in range(chunks):
            pltpu.sync_copy(h_hbm.at[pl.ds(base + j * SC_W, SC_W)], rows_v)
            for k in range(TOP_K):
                pltpu.sync_copy(rows_v, out_hbm.at[idx_v.at[k, j]])

    return pl.kernel(
        body,
        out_type=jax.ShapeDtypeStruct((n_rows, dw), F32),
        mesh=_sc_mesh(),
        scratch_types=[pltpu.VMEM((TOP_K, chunks, SC_W), I32),
                       pltpu.VMEM((pad_chunks, SC_W), I32),
                       pltpu.VMEM((SC_W, dw), F32)],
        name="sc_dispatch",
    )(h2p, pos4, padidx, zero_rows)


def _sc_combine_gather(ybuf, pos4):
    _, dw = ybuf.shape
    _, workers, chunks, _ = pos4.shape
    per_worker = chunks * SC_W

    def body(y_hbm, pos_hbm, out_hbm, idx_v, rows_v):
        wid = _sc_worker()
        base = wid * per_worker
        for k in range(TOP_K):
            pltpu.sync_copy(pos_hbm.at[k, wid], idx_v.at[k])
        for k in range(TOP_K):
            for j in range(chunks):
                pltpu.sync_copy(y_hbm.at[idx_v.at[k, j]], rows_v)
                pltpu.sync_copy(rows_v, out_hbm.at[k, pl.ds(base + j * SC_W, SC_W)])

    return pl.kernel(
        body,
        out_type=jax.ShapeDtypeStruct((TOP_K, workers * per_worker, dw), F32),
        mesh=_sc_mesh(),
        scratch_types=[pltpu.VMEM((TOP_K, chunks, SC_W), I32),
                       pltpu.VMEM((SC_W, dw), F32)],
        name="sc_combine",
    )(ybuf, pos4)


def _ffn_kernel(blk_e_ref, nblk_ref, x_ref, wg_ref, wu_ref, wd_ref, y_ref):
    del blk_e_ref
    used = pl.program_id(0) < nblk_ref[0]
    half = x_ref.shape[1]

    @pl.when(used)
    def _():
        x_lo, x_hi = _unpack_halves(x_ref[...])
        g = _dot(x_lo, wg_ref[0, 0:half, :]) + _dot(x_hi, wg_ref[0, half:, :])
        u = _dot(x_lo, wu_ref[0, 0:half, :]) + _dot(x_hi, wu_ref[0, half:, :])
        y = _dot((_silu(g) * u).astype(BF16), wd_ref[0])
        y_ref[...] = _pack_halves(y.astype(BF16).astype(F32))

    @pl.when(jnp.logical_not(used))
    def _():
        y_ref[...] = jnp.zeros_like(y_ref)


def _ffn(blk_e, nblk, xbuf, wg, wu, wd, tb):
    _, dw = xbuf.shape
    _, d, de = wg.shape
    n_blk = blk_e.shape[0]
    x_blk = lambda i, be, nb: (jnp.minimum(i, nb[0] - 1), 0)
    grid_spec = pltpu.PrefetchScalarGridSpec(
        num_scalar_prefetch=2,
        grid=(n_blk,),
        in_specs=[pl.BlockSpec((tb, dw), x_blk),
                  pl.BlockSpec((1, d, de), lambda i, be, nb: (be[i], 0, 0)),
                  pl.BlockSpec((1, d, de), lambda i, be, nb: (be[i], 0, 0)),
                  pl.BlockSpec((1, de, d), lambda i, be, nb: (be[i], 0, 0))],
        out_specs=pl.BlockSpec((tb, dw), lambda i, be, nb: (i, 0)),
    )
    return pl.pallas_call(
        _ffn_kernel,
        grid_spec=grid_spec,
        out_shape=jax.ShapeDtypeStruct((n_blk * tb, dw), F32),
        compiler_params=_cparams(("arbitrary",)),
        name="ffn",
    )(blk_e, nblk, xbuf, wg, wu, wd)


def _final_kernel(yg_ref, xm_ref, wt_ref, g2_ref, fg_ref, *out_refs):
    o_ref = out_refs[-1]
    wt = wt_ref[...]
    y = None
    for k in range(TOP_K):
        lo, hi = _unpack_halves(yg_ref[k])
        yk = jnp.concatenate([lo, hi], axis=-1).astype(F32) * wt[:, k:k + 1]
        y = yk if y is None else y + yk
    x = xm_ref[0] + g2_ref[0] * y
    ms = jnp.mean(x * x, axis=-1, keepdims=True)
    o_ref[0] = x * lax.rsqrt(ms + EPS) * fg_ref[...]


def _final(yg, xm, wt, g2, fg, out_prev, b0, total):
    bsz, l, d = xm.shape
    dw = yg.shape[2]
    tm = min(MOE_TM, l)
    nt = l // tm
    in_specs = [pl.BlockSpec((TOP_K, tm, dw), lambda b, i: (0, b * nt + i, 0)),
                pl.BlockSpec((1, tm, d), lambda b, i: (b, i, 0)),
                pl.BlockSpec((tm, TOP_K), lambda b, i: (b * nt + i, 0)),
                pl.BlockSpec((1, 1, d), lambda b, i: (b0 + b, 0, 0)),
                pl.BlockSpec((1, d), lambda b, i: (0, 0))]
    args = [yg, xm, wt, g2, fg]
    aliases = {}
    if out_prev is not None:
        in_specs.append(pl.BlockSpec(memory_space=pl.ANY))
        args.append(out_prev)
        aliases = {5: 0}
    return pl.pallas_call(
        _final_kernel,
        grid=(bsz, nt),
        in_specs=in_specs,
        out_specs=pl.BlockSpec((1, tm, d), lambda b, i: (b0 + b, i, 0)),
        out_shape=jax.ShapeDtypeStruct((total, l, d), F32),
        input_output_aliases=aliases,
        compiler_params=_cparams(("arbitrary", "arbitrary")),
        name="final",
    )(*args)


def _layer(x, ctx, mod, mod_c, norm1_g, w_in, lb_logits, hgrn_norm_g, dw_kernel, dw_bias, conv_ln_g,
           conv_ln_b, w_out, norm2_g, wr, br, w_gate, w_up, w_down, final_norm_g):
    bsz, l, d = x.shape
    w = lb_logits.shape[2]
    heads = w // HEAD_DIM
    row = lambda v: v.reshape(1, -1)
    per_b = lambda v: v.reshape(bsz, 1, d)
    sh1, sc1, g1, sh2, sc2, g2 = [per_b(mod[:, j]) for j in range(N_MOD)]
    csh1 = jnp.broadcast_to(mod_c[0].reshape(1, 1, d), (bsz, 1, d))
    csc1 = jnp.broadcast_to(mod_c[1].reshape(1, 1, d), (bsz, 1, d))

    w_in_b = w_in.astype(BF16)
    zf, zr = _inproj(x, row(norm1_g), sc1, sh1, w_in_b, 2 * w)
    zcf, zcr = _inproj(ctx, row(norm1_g), csc1, csh1, w_in_b[:, :3 * w], 2 * w)
    o_f, o_b = _hgrn(lb_logits, zcf, zcr, zf, zr)
    w_out_b = w_out.astype(BF16)

    groups = MOE_GROUPS if bsz % MOE_GROUPS == 0 else 1
    out = None
    for gi in range(groups):
        b0, gb = gi * (bsz // groups), bsz // groups
        mixed = _mix(o_f, o_b, zr, x, g1, sc2, sh2, row(jnp.tile(hgrn_norm_g, heads)), dw_kernel,
                     row(dw_bias), row(conv_ln_g), row(conv_ln_b), w_out_b, row(norm2_g), wr, br, b0, gb)
        out = _moe_group(mixed, g2, w_gate.astype(BF16), w_up.astype(BF16), w_down.astype(BF16),
                         row(final_norm_g), out, b0, bsz)
    return out


def _moe_group(mixed, g2, w_gate, w_up, w_down, final_g, out_prev, b0, total):
    xm, h2, eid, wts, rank, cnt = mixed
    gb, l, d = xm.shape
    n = gb * l

    tb = FFN_TB
    n_blk = -(-(n * TOP_K) // tb) + N_EXPERTS
    counts = cnt[:, 0].astype(I32)
    pcounts = (counts + tb - 1) // tb * tb
    pends = jnp.cumsum(pcounts)
    pstarts = pends - pcounts
    pos = _positions(pstarts.astype(I32), eid, rank)
    blk_start = jnp.arange(n_blk, dtype=I32) * tb
    blk_e = jnp.minimum(jnp.sum((pends[None, :] <= blk_start[:, None]).astype(I32), axis=1), N_EXPERTS - 1)
    nblk_used = (pends[-1:] // tb).astype(I32)

    workers = SC_CORES * SC_SUBCORES
    pos4 = pos.reshape(TOP_K, workers, n // (workers * SC_W), SC_W)
    lane = jnp.arange(tb, dtype=I32)[None, :]
    pad_rows = jnp.where(lane < (pcounts - counts)[:, None], (pstarts + counts)[:, None] + lane,
                         n_blk * tb + lane)
    padidx = pad_rows.astype(I32).reshape(workers, N_EXPERTS // workers * tb // SC_W, SC_W)
    xbuf = _sc_dispatch(h2.reshape(n, d // 2), pos4, padidx, jnp.zeros((SC_W, d // 2), F32), (n_blk + 1) * tb)
    ybuf = _ffn(blk_e, nblk_used, xbuf, w_gate, w_up, w_down, tb)
    yg = _sc_combine_gather(ybuf, pos4)
    return _final(yg, xm, wts.T, g2, final_g, out_prev, b0, total)


def kernel(x, c, ctx, c_ctx, w_ada, b_ada, norm1_g, w_in, lb_logits, hgrn_norm_g, dw_kernel, dw_bias,
           conv_ln_g, conv_ln_b, w_out, norm2_g, router_group_w, router_group_b, router_expert_w,
           router_expert_b, w_expert_gate, w_expert_up, w_expert_down, final_norm_g):
    depth = w_ada.shape[0]
    assert depth == 1, "context tokens are only updated between layers; a single layer is implemented"
    bsz, l, d = x.shape
    l0 = 0
    rows = -(-(bsz + 1) // 8) * 8
    cc = jnp.zeros((rows, d), F32).at[:bsz].set(c).at[bsz].set(c_ctx)
    mod_all = _ada(cc, w_ada[l0], b_ada[l0].reshape(1, -1))
    mod = mod_all[:bsz].reshape(bsz, N_MOD, d)
    mod_c = mod_all[bsz].reshape(N_MOD, d)
    pad8 = jnp.zeros((d, 8 - N_GROUPS), F32)
    wr = jnp.concatenate([router_group_w[l0], pad8, router_expert_w[l0], jnp.zeros((d, 8), F32)], axis=1).T
    br = jnp.concatenate([router_group_b[l0], jnp.zeros((8 - N_GROUPS,), F32), router_expert_b[l0],
                          jnp.zeros((8,), F32)]).reshape(ROUTER_ROWS, 1)
    return _layer(x, ctx, mod, mod_c, norm1_g[l0], w_in[l0], lb_logits, hgrn_norm_g[l0], dw_kernel[l0], dw_bias[l0], conv_ln_g[l0], conv_ln_b[l0], w_out[l0],
                  norm2_g[l0], wr, br, w_expert_gate[l0], w_expert_up[l0], w_expert_down[l0], final_norm_g)
```

```python
import functools

import numpy as np

import jax
import jax.numpy as jnp
from jax import lax
from jax.experimental import pallas as pl
from jax.experimental.pallas import tpu as pltpu
from jax.experimental.pallas import tpu_sc as plsc

F32 = jnp.float32
BF16 = jnp.bfloat16
I32 = jnp.int32
U32 = jnp.uint32

EPS = 1e-6
HEAD_DIM = 128
CHUNK = 64
GRID_W = 64
CONV_K = 31
CONV_PAD = CONV_K // 2
CONV_LEAD = 16
CONV_SPAN = GRID_W + 24
N_GROUPS = 4
EXPERTS_PER_GROUP = 8
N_EXPERTS = N_GROUPS * EXPERTS_PER_GROUP
TOP_K = 2
N_MOD = 6
ROUTER_ROWS = 48

ADA_TN = 1024
INPROJ_TM = 1024
HGRN_TT = 256
MIX_TM = 256
MOE_TM = 256
FFN_TB = 1024
CAST_EXPERTS = 2
VMEM_LIMIT = 48 * 1024 * 1024
SC_CORES = 2
SC_SUBCORES = 16
SC_W = 128
MOE_GROUPS = 1


def _cparams(sem):
    return pltpu.CompilerParams(dimension_semantics=sem, vmem_limit_bytes=VMEM_LIMIT)


def _silu(v):
    return v * jax.nn.sigmoid(v)


def _dot(a, b):
    return jnp.dot(a, b, preferred_element_type=F32)


def _dot_nt(a, b):
    return lax.dot_general(a, b, (((1,), (1,)), ((), ())), preferred_element_type=F32)


def _dot_tn(a, b):
    return lax.dot_general(a, b, (((0,), (0,)), ((), ())), preferred_element_type=F32)


def _pack_halves(vr):
    m = vr.shape[1] // 2
    bits = lax.bitcast_convert_type(vr, U32)
    return lax.bitcast_convert_type((bits[:, :m] >> 16) | bits[:, m:], F32)


def _unpack_halves(p):
    u = lax.bitcast_convert_type(p, U32)
    lo = lax.bitcast_convert_type(u << 16, F32)
    hi = lax.bitcast_convert_type(u & jnp.uint32(0xFFFF0000), F32)
    return lo.astype(BF16), hi.astype(BF16)


def _split2(a):
    hi = a.astype(BF16)
    lo = (a - hi.astype(F32)).astype(BF16)
    return hi, lo


def _ada_kernel(c_ref, w_ref, b_ref, o_ref):
    s = _silu(c_ref[...]).astype(BF16)
    o_ref[...] = _dot(s, w_ref[...].astype(BF16)) + b_ref[...]


def _ada(cc, w, b):
    r, d = cc.shape
    nc = w.shape[1]
    return pl.pallas_call(
        _ada_kernel,
        grid=(nc // ADA_TN,),
        in_specs=[pl.BlockSpec((r, d), lambda j: (0, 0)),
                  pl.BlockSpec((d, ADA_TN), lambda j: (0, j)),
                  pl.BlockSpec((1, ADA_TN), lambda j: (0, j))],
        out_specs=pl.BlockSpec((r, ADA_TN), lambda j: (0, j)),
        out_shape=jax.ShapeDtypeStruct((r, nc), F32),
        compiler_params=_cparams(("arbitrary",)),
        name="ada",
    )(cc, w, b)


def _inproj_kernel(x_ref, g_ref, sc_ref, sh_ref, w_ref, zf_ref, zr_ref):
    x = x_ref[0]
    ms = jnp.mean(x * x, axis=-1, keepdims=True)
    gs = g_ref[...] * (1.0 + sc_ref[0])
    h = x * lax.rsqrt(ms + EPS) * gs + sh_ref[0]
    z = _dot(h.astype(BF16), w_ref[...])
    nf = zf_ref.shape[2]
    zf_ref[0] = z[:, :nf]
    zr_ref[0] = z[:, nf:].astype(BF16)


def _inproj(x, g, sc, sh, w, nf):
    bsz, l, d = x.shape
    nc = w.shape[1]
    tm = min(INPROJ_TM, l)
    return pl.pallas_call(
        _inproj_kernel,
        grid=(bsz, l // tm),
        in_specs=[pl.BlockSpec((1, tm, d), lambda b, i: (b, i, 0)),
                  pl.BlockSpec((1, d), lambda b, i: (0, 0)),
                  pl.BlockSpec((1, 1, d), lambda b, i: (b, 0, 0)),
                  pl.BlockSpec((1, 1, d), lambda b, i: (b, 0, 0)),
                  pl.BlockSpec((d, nc), lambda b, i: (0, 0))],
        out_specs=[pl.BlockSpec((1, tm, nf), lambda b, i: (b, i, 0)),
                   pl.BlockSpec((1, tm, nc - nf), lambda b, i: (b, i, 0))],
        out_shape=[jax.ShapeDtypeStruct((bsz, l, nf), F32),
                   jax.ShapeDtypeStruct((bsz, l, nc - nf), BF16)],
        compiler_params=_cparams(("arbitrary", "arbitrary")),
        name="inproj",
    )(x, g, sc, sh, w)


class _HgrnDirection:
    def __init__(self, lbl, st_ref, reverse):
        self.st_ref, self.reverse = st_ref, reverse
        e = jnp.exp(lbl - jnp.max(lbl, axis=0, keepdims=True))
        self.lb = e[0:1] / jnp.sum(e, axis=0, keepdims=True)
        row = lax.broadcasted_iota(I32, (CHUNK, CHUNK), 0)
        col = lax.broadcasted_iota(I32, (CHUNK, CHUNK), 1)
        self.tri = (col >= row) if reverse else (col <= row)
        self.tm = jnp.where(self.tri, 1.0, 0.0).astype(BF16)

    def tile(self, f_ref, v_ref, q_ref=None, o_ref=None):
        heads = self.st_ref.shape[0]
        n_chunks = f_ref.shape[1] // CHUNK
        crow = lambda c: slice(c * CHUNK, (c + 1) * CHUNK)
        hsl = lambda h: slice(h * HEAD_DIM, (h + 1) * HEAD_DIM)
        f = self.lb + (1.0 - self.lb) * jax.nn.sigmoid(f_ref[0])
        k = 1.0 - f
        hi, lo = _split2(jnp.log(f))
        cums = [_dot(self.tm, hi[crow(c)]) + _dot(self.tm, lo[crow(c)])
                for c in range(n_chunks)]
        lasts = [cm[0:1] if self.reverse else cm[CHUNK - 1:CHUNK] for cm in cums]
        cum = jnp.concatenate(cums, axis=0)
        dec = [jnp.exp(la) for la in lasts]
        k_intra = k * jnp.exp(-cum)
        ks = (k_intra * jnp.concatenate([jnp.broadcast_to(dc, (CHUNK, dc.shape[1])) for dc in dec], axis=0)
              ).astype(BF16)
        vb = v_ref[0].astype(BF16)
        kv = [[_dot_tn(vb[crow(c), hsl(h)], ks[crow(c), hsl(h)]) for h in range(heads)] for c in range(n_chunks)]
        emit = o_ref is not None
        if emit:
            qd = (_silu(q_ref[0].astype(F32)) * jnp.exp(cum)).astype(BF16)
            ki = k_intra.astype(BF16)
            sc = [[jnp.where(self.tri, _dot_nt(qd[crow(c), hsl(h)], ki[crow(c), hsl(h)]), 0.0).astype(BF16)
                   for h in range(heads)] for c in range(n_chunks)]
            intra = [[_dot(sc[c][h], vb[crow(c), hsl(h)]) for h in range(heads)] for c in range(n_chunks)]
        state = [self.st_ref[h] for h in range(heads)]
        for c in (range(n_chunks - 1, -1, -1) if self.reverse else range(n_chunks)):
            for h in range(heads):
                if emit:
                    inter = _dot_nt(qd[crow(c), hsl(h)], state[h].astype(BF16))
                    o_ref[0, crow(c), hsl(h)] = (intra[c][h] + inter).astype(o_ref.dtype)
                state[h] = state[h] * dec[c][:, hsl(h)] + kv[c][h]
        for h in range(heads):
            self.st_ref[h] = state[h]


def _hgrn_kernel(lbl_ref, cff_ref, cfb_ref, cv_ref, ff_ref, vf_ref, qf_ref, fb_ref, vb_ref, qb_ref,
                 of_ref, ob_ref, stf_ref, stb_ref):
    fwd = _HgrnDirection(lbl_ref[0], stf_ref, reverse=False)
    bwd = _HgrnDirection(lbl_ref[1], stb_ref, reverse=True)

    @pl.when(pl.program_id(1) == 0)
    def _():
        stf_ref[...] = jnp.zeros_like(stf_ref)
        stb_ref[...] = jnp.zeros_like(stb_ref)
        fwd.tile(cff_ref, cv_ref)
        bwd.tile(cfb_ref, cv_ref)

    fwd.tile(ff_ref, vf_ref, qf_ref, of_ref)
    bwd.tile(fb_ref, vb_ref, qb_ref, ob_ref)


def _hgrn(lbl, zcf, zcr, zf, zr):
    bsz, l, _ = zf.shape
    n_ctx_tok = zcf.shape[1]
    w = lbl.shape[2]
    heads = w // HEAD_DIM
    tt = min(HGRN_TT, l)
    nt = l // tt
    ctx_col = lambda c: pl.BlockSpec((1, n_ctx_tok, w), lambda b, i: (b, 0, c))
    fwd_col = lambda c: pl.BlockSpec((1, tt, w), lambda b, i: (b, i, c))
    bwd_col = lambda c: pl.BlockSpec((1, tt, w), lambda b, i: (b, nt - 1 - i, c))
    state = pltpu.VMEM((heads, HEAD_DIM, HEAD_DIM), F32)
    return pl.pallas_call(
        _hgrn_kernel,
        grid=(bsz, nt),
        in_specs=[pl.BlockSpec(lbl.shape, lambda b, i: (0, 0, 0)),
                  ctx_col(0), ctx_col(1), ctx_col(0),
                  fwd_col(0), fwd_col(0), fwd_col(1),
                  bwd_col(1), bwd_col(0), bwd_col(1)],
        out_specs=[fwd_col(0), bwd_col(0)],
        out_shape=[jax.ShapeDtypeStruct((bsz, l, w), BF16), jax.ShapeDtypeStruct((bsz, l, w), BF16)],
        scratch_shapes=[state, state],
        compiler_params=_cparams(("arbitrary", "arbitrary")),
        name="hgrn",
    )(lbl, zcf, zcf, zcr, zf, zr, zr, zf, zr, zr)


def _mix_kernel(of_ref, ob_ref, zg_ref, za_ref, zt_ref, x_ref, g1_ref, sc2_ref, sh2_ref,
                hg_ref, dwk_ref, dwb_ref, lng_ref, lnb_ref, wo_ref, n2g_ref, wr_ref, br_ref, shm_ref,
                xm_ref, h2_ref, eid_ref, wts_ref, rank_ref, cnt_ref,
                shf_ref, cv_ref, carry_ref, hh_ref, hl_ref, *, tm):
    w = of_ref.shape[2]
    heads = w // HEAD_DIM
    step = pl.program_id(0)

    @pl.when(step == 0)
    def _():
        carry_ref[...] = jnp.zeros_like(carry_ref)
        hh_ref[...] = jnp.zeros_like(hh_ref)
        hl_ref[...] = jnp.zeros_like(hl_ref)

    router = _route(hh_ref[...], hl_ref[...], wr_ref, br_ref, carry_ref, jnp.where(step > 0, 1.0, 0.0),
                    eid_ref, wts_ref, rank_ref, cnt_ref)
    next(router)

    o = of_ref[0].astype(F32) + ob_ref[0].astype(F32)
    parts = []
    for h in range(heads):
        oh = o[:, h * HEAD_DIM:(h + 1) * HEAD_DIM]
        parts.append(oh * lax.rsqrt(jnp.mean(oh * oh, axis=-1, keepdims=True) + EPS))
    o_mix = jnp.concatenate(parts, axis=-1) * hg_ref[...] * _silu(zg_ref[0].astype(F32))

    u = za_ref[0].astype(F32) * jax.nn.sigmoid(zt_ref[0].astype(F32))
    span = shf_ref.shape[2]
    u_hi, u_lo = _split2(u)
    for s in range(tm // GRID_W):
        rows = slice(s * GRID_W, (s + 1) * GRID_W)
        shifted = _dot(shm_ref[...], jnp.concatenate([u_hi[rows], u_lo[rows]], axis=0))
        for b in range(8):
            shf_ref[s, b] = shifted[b * span:(b + 1) * span]
    next(router)
    for s in range(tm // GRID_W):
        acc = jnp.zeros((GRID_W, u.shape[1]), F32) + dwb_ref[...]
        for k in range(CONV_K):
            a, b = divmod(CONV_LEAD - CONV_PAD + k, 8)
            acc = acc + shf_ref[s, b, 8 * a:8 * a + GRID_W, :] * dwk_ref[k:k + 1, :]
        cv_ref[s * GRID_W:(s + 1) * GRID_W, :] = acc
        if s == 0:
            next(router)
    cv = cv_ref[...]
    mu = jnp.mean(cv, axis=-1, keepdims=True)
    cen = cv - mu
    var = jnp.mean(cen * cen, axis=-1, keepdims=True)
    c_mix = _silu(cen * lax.rsqrt(var + EPS) * lng_ref[...] + lnb_ref[...])
    next(router, None)

    mix = _dot(o_mix.astype(BF16), wo_ref[0:w, :]) + _dot(c_mix.astype(BF16), wo_ref[w:, :])
    xm = x_ref[0] + g1_ref[0] * mix
    xm_ref[0] = xm

    ms = jnp.mean(xm * xm, axis=-1, keepdims=True)
    h2 = xm * lax.rsqrt(ms + EPS) * (n2g_ref[...] * (1.0 + sc2_ref[0])) + sh2_ref[0]
    hh = h2.astype(BF16)
    h2_rounded = hh.astype(F32)
    h2_ref[0] = _pack_halves(h2_rounded)
    hh_ref[...] = hh
    hl_ref[...] = (h2 - h2_rounded).astype(BF16)


def _route(hh, hl, wr_ref, br_ref, carry_ref, live, eid_ref, wts_ref, rank_ref, cnt_ref):
    tm = hh.shape[0]
    wr = wr_ref[...]
    wh = wr.astype(BF16)
    wl = (wr - wh.astype(F32)).astype(BF16)
    lg = _dot_nt(wh, hh) + _dot_nt(wh, hl) + _dot_nt(wl, hh) + br_ref[...]
    yield

    neg = -jnp.inf
    r8 = lax.broadcasted_iota(I32, (EXPERTS_PER_GROUP, tm), 0).astype(F32)
    gl = jnp.where(r8 < N_GROUPS, lg[0:8], neg)
    gmax = jnp.max(gl, axis=0, keepdims=True)
    grp = jnp.min(jnp.where(gl == gmax, r8, 8.0), axis=0, keepdims=True)
    p_grp = 1.0 / jnp.sum(jnp.exp(gl - gmax), axis=0, keepdims=True)
    es = lg[8:16]
    for g in range(1, N_GROUPS):
        es = jnp.where(grp == float(g), lg[8 + 8 * g:16 + 8 * g], es)
    m1 = jnp.max(es, axis=0, keepdims=True)
    i1 = jnp.min(jnp.where(es == m1, r8, 8.0), axis=0, keepdims=True)
    es2 = jnp.where(r8 == i1, neg, es)
    m2 = jnp.max(es2, axis=0, keepdims=True)
    i2 = jnp.min(jnp.where(es2 == m2, r8, 8.0), axis=0, keepdims=True)
    e2 = jnp.exp(m2 - m1)
    w1 = p_grp / (1.0 + e2)
    w2 = p_grp * e2 / (1.0 + e2)
    eid0 = grp * float(EXPERTS_PER_GROUP) + i1
    eid1 = grp * float(EXPERTS_PER_GROUP) + i2
    eid_ref[0:1, :] = eid0.astype(I32)
    eid_ref[1:2, :] = eid1.astype(I32)
    wts_ref[0:1, :] = w1
    wts_ref[1:2, :] = w2
    yield

    r32 = lax.broadcasted_iota(I32, (N_EXPERTS, tm), 0).astype(F32)
    oh0 = jnp.where(r32 == eid0, live, 0.0)
    oh1 = jnp.where(r32 == eid1, live, 0.0)
    src = lax.broadcasted_iota(I32, (tm, tm), 0)
    dst = lax.broadcasted_iota(I32, (tm, tm), 1)
    before = jnp.where(src < dst, 1.0, 0.0).astype(BF16)
    ex = _dot(jnp.concatenate([oh0, oh1], axis=0).astype(BF16), before)
    yield
    base = ex[0:N_EXPERTS] + ex[N_EXPERTS:] + carry_ref[...]
    rank_ref[0:1, :] = jnp.sum(oh0 * base, axis=0, keepdims=True).astype(I32)
    rank_ref[1:2, :] = jnp.sum(oh1 * base, axis=0, keepdims=True).astype(I32)
    carry = carry_ref[...] + jnp.sum(oh0 + oh1, axis=1, keepdims=True)
    carry_ref[...] = carry
    cnt_ref[...] = jnp.broadcast_to(carry, cnt_ref.shape)


def _conv_shift_matrix():
    r = np.arange(8 * CONV_SPAN)
    src = r // CONV_SPAN + r % CONV_SPAN - CONV_LEAD
    sel = (src[:, None] == np.arange(GRID_W)[None, :]).astype(np.float32)
    return jnp.asarray(np.concatenate([sel, sel], axis=1), dtype=BF16)


def _mix(o_f, o_b, z, x, g1, sc2, sh2, hg, dwk, dwb, lng, lnb, wo, n2g, wr, br, b0, bsz):
    _, l, d = x.shape
    shm = _conv_shift_matrix()
    w = o_f.shape[2]
    cw = dwk.shape[1]
    tm = min(MIX_TM, l)
    nt = l // tm
    n = bsz * l
    tiles = bsz * nt
    mixed = lambda s: jnp.minimum(s, tiles - 1)
    src = lambda c: (lambda s: (b0 + mixed(s) // nt, mixed(s) % nt, c))
    tok = lambda s: (mixed(s) // nt, mixed(s) % nt, 0)
    vec = lambda s: (0, 0)
    per_b = lambda s: (b0 + mixed(s) // nt, 0, 0)
    flat = lambda s: (0, jnp.maximum(s - 1, 0))
    kern = functools.partial(_mix_kernel, tm=tm)
    return pl.pallas_call(
        kern,
        grid=(tiles + 1,),
        in_specs=[pl.BlockSpec((1, tm, w), src(0)),
                  pl.BlockSpec((1, tm, w), src(0)),
                  pl.BlockSpec((1, tm, w), src(2)),
                  pl.BlockSpec((1, tm, cw), src(3)),
                  pl.BlockSpec((1, tm, cw), src(4)),
                  pl.BlockSpec((1, tm, d), src(0)),
                  pl.BlockSpec((1, 1, d), per_b),
                  pl.BlockSpec((1, 1, d), per_b),
                  pl.BlockSpec((1, 1, d), per_b),
                  pl.BlockSpec((1, w), vec),
                  pl.BlockSpec((CONV_K, cw), vec),
                  pl.BlockSpec((1, cw), vec),
                  pl.BlockSpec((1, cw), vec),
                  pl.BlockSpec((1, cw), vec),
                  pl.BlockSpec((w + cw, d), vec),
                  pl.BlockSpec((1, d), vec),
                  pl.BlockSpec((ROUTER_ROWS, d), vec),
                  pl.BlockSpec((ROUTER_ROWS, 1), vec),
                  pl.BlockSpec(shm.shape, vec)],
        out_specs=[pl.BlockSpec((1, tm, d), tok),
                   pl.BlockSpec((1, tm, d // 2), tok),
                   pl.BlockSpec((TOP_K, tm), flat),
                   pl.BlockSpec((TOP_K, tm), flat),
                   pl.BlockSpec((TOP_K, tm), flat),
                   pl.BlockSpec((N_EXPERTS, 128), vec)],
        out_shape=[jax.ShapeDtypeStruct((bsz, l, d), F32),
                   jax.ShapeDtypeStruct((bsz, l, d // 2), F32),
                   jax.ShapeDtypeStruct((TOP_K, n), I32),
                   jax.ShapeDtypeStruct((TOP_K, n), F32),
                   jax.ShapeDtypeStruct((TOP_K, n), I32),
                   jax.ShapeDtypeStruct((N_EXPERTS, 128), F32)],
        scratch_shapes=[pltpu.VMEM((tm // GRID_W, 8, CONV_SPAN, cw), F32),
                        pltpu.VMEM((tm, cw), F32),
                        pltpu.VMEM((N_EXPERTS, 1), F32),
                        pltpu.VMEM((tm, d), BF16),
                        pltpu.VMEM((tm, d), BF16)],
        compiler_params=_cparams(("arbitrary",)),
        name="mix",
    )(o_f, o_b, z, z, z, x, g1, sc2, sh2, hg, dwk, dwb, lng, lnb, wo, n2g, wr, br, shm)


def _pos_kernel(ps_ref, eid_ref, rank_ref, pos_ref):
    eid = eid_ref[...]
    acc = rank_ref[...]
    for e in range(N_EXPERTS):
        acc = acc + jnp.where(eid == e, ps_ref[e], 0)
    pos_ref[...] = acc


def _positions(pstarts, eid, rank):
    shape = eid.shape
    eid2 = eid.reshape(-1, 128)
    full = pl.BlockSpec(eid2.shape, lambda i: (0, 0))
    pos = pl.pallas_call(
        _pos_kernel,
        grid=(1,),
        in_specs=[pl.BlockSpec(memory_space=pltpu.SMEM), full, full],
        out_specs=full,
        out_shape=jax.ShapeDtypeStruct(eid2.shape, I32),
        name="positions",
    )(pstarts, eid2, rank.reshape(-1, 128))
    return pos.reshape(shape)


def _sc_mesh():
    return plsc.VectorSubcoreMesh(core_axis_name="c", subcore_axis_name="s",
                                  num_cores=SC_CORES, num_subcores=SC_SUBCORES)


def _sc_worker():
    return lax.axis_index("s") * SC_CORES + lax.axis_index("c")


def _sc_dispatch(h2p, pos4, padidx, zero_rows, n_rows):
    n, dw = h2p.shape
    _, workers, chunks, _ = pos4.shape
    pad_chunks = padidx.shape[1]
    per_worker = n // workers

    def body(h_hbm, pos_hbm, pad_hbm, zero_hbm, out_hbm, idx_v, pad_v, rows_v):
        wid = _sc_worker()
        base = wid * per_worker
        for k in range(TOP_K):
            pltpu.sync_copy(pos_hbm.at[k, wid], idx_v.at[k])
        pltpu.sync_copy(pad_hbm.at[wid], pad_v)
        pltpu.sync_copy(zero_hbm, rows_v)
        for j in range(pad_chunks):
            pltpu.sync_copy(rows_v, out_hbm.at[pad_v.at[j]])
        for j in range(chunks):
            pltpu.sync_copy(h_hbm.at[pl.ds(base + j * SC_W, SC_W)], rows_v)
            for k in range(TOP_K):
                pltpu.sync_copy(rows_v, out_hbm.at[idx_v.at[k, j]])

    return pl.kernel(
        body,
        out_type=jax.ShapeDtypeStruct((n_rows, dw), F32),
        mesh=_sc_mesh(),
        scratch_types=[pltpu.VMEM((TOP_K, chunks, SC_W), I32),
                       pltpu.VMEM((pad_chunks, SC_W), I32),
                       pltpu.VMEM((SC_W, dw), F32)],
        name="sc_dispatch",
    )(h2p, pos4, padidx, zero_rows)


def _sc_combine_gather(ybuf, pos4):
    _, dw = ybuf.shape
    _, workers, chunks, _ = pos4.shape
    per_worker = chunks * SC_W

    def body(y_hbm, pos_hbm, out_hbm, idx_v, rows_v):
        wid = _sc_worker()
        base = wid * per_worker
        for k in range(TOP_K):
            pltpu.sync_copy(pos_hbm.at[k, wid], idx_v.at[k])
        for k in range(TOP_K):
            for j in range(chunks):
                pltpu.sync_copy(y_hbm.at[idx_v.at[k, j]], rows_v)
                pltpu.sync_copy(rows_v, out_hbm.at[k, pl.ds(base + j * SC_W, SC_W)])

    return pl.kernel(
        body,
        out_type=jax.ShapeDtypeStruct((TOP_K, workers * per_worker, dw), F32),
        mesh=_sc_mesh(),
        scratch_types=[pltpu.VMEM((TOP_K, chunks, SC_W), I32),
                       pltpu.VMEM((SC_W, dw), F32)],
        name="sc_combine",
    )(ybuf, pos4)


def _cast_kernel(*refs):
    n = len(refs) // 2
    for src, dst in zip(refs[:n], refs[n:]):
        dst[...] = src[...].astype(dst.dtype)


def _expert_weights_bf16(*ws):
    ne = ws[0].shape[0]
    ce = CAST_EXPERTS if ne % CAST_EXPERTS == 0 else 1
    specs = [pl.BlockSpec((ce,) + w.shape[1:], lambda i: (i, 0, 0)) for w in ws]
    return pl.pallas_call(
        _cast_kernel,
        grid=(ne // ce,),
        in_specs=specs,
        out_specs=specs,
        out_shape=[jax.ShapeDtypeStruct(w.shape, BF16) for w in ws],
        compiler_params=_cparams(("arbitrary",)),
        name="expert_weights_bf16",
    )(*ws)


def _ffn_kernel(blk_e_ref, nblk_ref, x_ref, wg_ref, wu_ref, wd_ref, y_ref):
    del blk_e_ref
    used = pl.program_id(0) < nblk_ref[0]
    half = x_ref.shape[1]

    @pl.when(used)
    def _():
        x_lo, x_hi = _unpack_halves(x_ref[...])
        g = _dot(x_lo, wg_ref[0, 0:half, :]) + _dot(x_hi, wg_ref[0, half:, :])
        u = _dot(x_lo, wu_ref[0, 0:half, :]) + _dot(x_hi, wu_ref[0, half:, :])
        y = _dot((_silu(g) * u).astype(BF16), wd_ref[0])
        y_ref[...] = _pack_halves(y.astype(BF16).astype(F32))

    @pl.when(jnp.logical_not(used))
    def _():
        y_ref[...] = jnp.zeros_like(y_ref)


def _ffn(blk_e, nblk, xbuf, wg, wu, wd, tb):
    _, dw = xbuf.shape
    _, d, de = wg.shape
    n_blk = blk_e.shape[0]
    x_blk = lambda i, be, nb: (jnp.minimum(i, nb[0] - 1), 0)
    grid_spec = pltpu.PrefetchScalarGridSpec(
        num_scalar_prefetch=2,
        grid=(n_blk,),
        in_specs=[pl.BlockSpec((tb, dw), x_blk),
                  pl.BlockSpec((1, d, de), lambda i, be, nb: (be[i], 0, 0)),
                  pl.BlockSpec((1, d, de), lambda i, be, nb: (be[i], 0, 0)),
                  pl.BlockSpec((1, de, d), lambda i, be, nb: (be[i], 0, 0))],
        out_specs=pl.BlockSpec((tb, dw), lambda i, be, nb: (i, 0)),
    )
    return pl.pallas_call(
        _ffn_kernel,
        grid_spec=grid_spec,
        out_shape=jax.ShapeDtypeStruct((n_blk * tb, dw), F32),
        compiler_params=_cparams(("arbitrary",)),
        name="ffn",
    )(blk_e, nblk, xbuf, wg, wu, wd)


def _final_kernel(yg_ref, xm_ref, wt_ref, g2_ref, fg_ref, *out_refs):
    o_ref = out_refs[-1]
    wt = wt_ref[...]
    y = None
    for k in range(TOP_K):
        lo, hi = _unpack_halves(yg_ref[k])
        yk = jnp.concatenate([lo, hi], axis=-1).astype(F32) * wt[:, k:k + 1]
        y = yk if y is None else y + yk
    x = xm_ref[0] + g2_ref[0] * y
    ms = jnp.mean(x * x, axis=-1, keepdims=True)
    o_ref[0] = x * lax.rsqrt(ms + EPS) * fg_ref[...]


def _final(yg, xm, wt, g2, fg, out_prev, b0, total):
    bsz, l, d = xm.shape
    dw = yg.shape[2]
    tm = min(MOE_TM, l)
    nt = l // tm
    in_specs = [pl.BlockSpec((TOP_K, tm, dw), lambda b, i: (0, b * nt + i, 0)),
                pl.BlockSpec((1, tm, d), lambda b, i: (b, i, 0)),
                pl.BlockSpec((tm, TOP_K), lambda b, i: (b * nt + i, 0)),
                pl.BlockSpec((1, 1, d), lambda b, i: (b0 + b, 0, 0)),
                pl.BlockSpec((1, d), lambda b, i: (0, 0))]
    args = [yg, xm, wt, g2, fg]
    aliases = {}
    if out_prev is not None:
        in_specs.append(pl.BlockSpec(memory_space=pl.ANY))
        args.append(out_prev)
        aliases = {5: 0}
    return pl.pallas_call(
        _final_kernel,
        grid=(bsz, nt),
        in_specs=in_specs,
        out_specs=pl.BlockSpec((1, tm, d), lambda b, i: (b0 + b, i, 0)),
        out_shape=jax.ShapeDtypeStruct((total, l, d), F32),
        input_output_aliases=aliases,
        compiler_params=_cparams(("arbitrary", "arbitrary")),
        name="final",
    )(*args)


def _layer(x, ctx, mod, mod_c, norm1_g, w_in, lb_logits, hgrn_norm_g, dw_kernel, dw_bias, conv_ln_g,
           conv_ln_b, w_out, norm2_g, wr, br, w_gate, w_up, w_down, final_norm_g):
    bsz, l, d = x.shape
    w = lb_logits.shape[2]
    heads = w // HEAD_DIM
    row = lambda v: v.reshape(1, -1)
    per_b = lambda v: v.reshape(bsz, 1, d)
    sh1, sc1, g1, sh2, sc2, g2 = [per_b(mod[:, j]) for j in range(N_MOD)]
    csh1 = jnp.broadcast_to(mod_c[0].reshape(1, 1, d), (bsz, 1, d))
    csc1 = jnp.broadcast_to(mod_c[1].reshape(1, 1, d), (bsz, 1, d))

    w_in_b = w_in.astype(BF16)
    zf, zr = _inproj(x, row(norm1_g), sc1, sh1, w_in_b, 2 * w)
    zcf, zcr = _inproj(ctx, row(norm1_g), csc1, csh1, w_in_b[:, :3 * w], 2 * w)
    o_f, o_b = _hgrn(lb_logits, zcf, zcr, zf, zr)
    w_out_b = w_out.astype(BF16)

    groups = MOE_GROUPS if bsz % MOE_GROUPS == 0 else 1
    wg_b, wu_b, wd_b = _expert_weights_bf16(w_gate, w_up, w_down)
    out = None
    for gi in range(groups):
        b0, gb = gi * (bsz // groups), bsz // groups
        mixed = _mix(o_f, o_b, zr, x, g1, sc2, sh2, row(jnp.tile(hgrn_norm_g, heads)), dw_kernel,
                     row(dw_bias), row(conv_ln_g), row(conv_ln_b), w_out_b, row(norm2_g), wr, br, b0, gb)
        out = _moe_group(mixed, g2, wg_b, wu_b, wd_b, row(final_norm_g), out, b0, bsz)
    return out


def _moe_group(mixed, g2, w_gate, w_up, w_down, final_g, out_prev, b0, total):
    xm, h2, eid, wts, rank, cnt = mixed
    gb, l, d = xm.shape
    n = gb * l

    tb = FFN_TB
    n_blk = -(-(n * TOP_K) // tb) + N_EXPERTS
    counts = cnt[:, 0].astype(I32)
    pcounts = (counts + tb - 1) // tb * tb
    pends = jnp.cumsum(pcounts)
    pstarts = pends - pcounts
    pos = _positions(pstarts.astype(I32), eid, rank)
    blk_start = jnp.arange(n_blk, dtype=I32) * tb
    blk_e = jnp.minimum(jnp.sum((pends[None, :] <= blk_start[:, None]).astype(I32), axis=1), N_EXPERTS - 1)
    nblk_used = (pends[-1:] // tb).astype(I32)

    workers = SC_CORES * SC_SUBCORES
    pos4 = pos.reshape(TOP_K, workers, n // (workers * SC_W), SC_W)
    lane = jnp.arange(tb, dtype=I32)[None, :]
    pad_rows = jnp.where(lane < (pcounts - counts)[:, None], (pstarts + counts)[:, None] + lane,
                         n_blk * tb + lane)
    padidx = pad_rows.astype(I32).reshape(workers, N_EXPERTS // workers * tb // SC_W, SC_W)
    xbuf = _sc_dispatch(h2.reshape(n, d // 2), pos4, padidx, jnp.zeros((SC_W, d // 2), F32), (n_blk + 1) * tb)
    ybuf = _ffn(blk_e, nblk_used, xbuf, w_gate, w_up, w_down, tb)
    yg = _sc_combine_gather(ybuf, pos4)
    return _final(yg, xm, wts.T, g2, final_g, out_prev, b0, total)


def kernel(x, c, ctx, c_ctx, w_ada, b_ada, norm1_g, w_in, lb_logits, hgrn_norm_g, dw_kernel, dw_bias,
           conv_ln_g, conv_ln_b, w_out, norm2_g, router_group_w, router_group_b, router_expert_w,
           router_expert_b, w_expert_gate, w_expert_up, w_expert_down, final_norm_g):
    depth = w_ada.shape[0]
    assert depth == 1, "context tokens are only updated between layers; a single layer is implemented"
    bsz, l, d = x.shape
    l0 = 0
    rows = -(-(bsz + 1) // 8) * 8
    cc = jnp.zeros((rows, d), F32).at[:bsz].set(c).at[bsz].set(c_ctx)
    mod_all = _ada(cc, w_ada[l0], b_ada[l0].reshape(1, -1))
    mod = mod_all[:bsz].reshape(bsz, N_MOD, d)
    mod_c = mod_all[bsz].reshape(N_MOD, d)
    pad8 = jnp.zeros((d, 8 - N_GROUPS), F32)
    wr = jnp.concatenate([router_group_w[l0], pad8, router_expert_w[l0], jnp.zeros((d, 8), F32)], axis=1).T
    br = jnp.concatenate([router_group_b[l0], jnp.zeros((8 - N_GROUPS,), F32), router_expert_b[l0],
                          jnp.zeros((8,), F32)]).reshape(ROUTER_ROWS, 1)
    return _layer(x, ctx, mod, mod_c, norm1_g[l0], w_in[l0], lb_logits, hgrn_norm_g[l0], dw_kernel[l0], dw_bias[l0], conv_ln_g[l0], conv_ln_b[l0], w_out[l0],
                  norm2_g[l0], wr, br, w_expert_gate[l0], w_expert_up[l0], w_expert_down[l0], final_norm_g)
```

```python
import functools

import numpy as np

import jax
import jax.numpy as jnp
from jax import lax
from jax.experimental import pallas as pl
from jax.experimental.pallas import tpu as pltpu
from jax.experimental.pallas import tpu_sc as plsc

F32 = jnp.float32
BF16 = jnp.bfloat16
I32 = jnp.int32
U32 = jnp.uint32

EPS = 1e-6
HEAD_DIM = 128
CHUNK = 64
GRID_W = 64
CONV_K = 31
CONV_PAD = CONV_K // 2
CONV_LEAD = 16
CONV_SPAN = GRID_W + 24
N_GROUPS = 4
EXPERTS_PER_GROUP = 8
N_EXPERTS = N_GROUPS * EXPERTS_PER_GROUP
TOP_K = 2
N_MOD = 6
ROUTER_ROWS = 48

ADA_TN = 1024
INPROJ_TM = 1024
HGRN_TT = 256
MIX_TM = 256
MOE_TM = 256
FFN_TB = 512
VMEM_LIMIT = 48 * 1024 * 1024
SC_CORES = 2
SC_SUBCORES = 16
SC_W = 128
MOE_GROUPS = 1


def _cparams(sem):
    return pltpu.CompilerParams(dimension_semantics=sem, vmem_limit_bytes=VMEM_LIMIT)


def _silu(v):
    return v * jax.nn.sigmoid(v)


def _dot(a, b):
    return jnp.dot(a, b, preferred_element_type=F32)


def _dot_nt(a, b):
    return lax.dot_general(a, b, (((1,), (1,)), ((), ())), preferred_element_type=F32)


def _dot_tn(a, b):
    return lax.dot_general(a, b, (((0,), (0,)), ((), ())), preferred_element_type=F32)


def _pack_halves(vr):
    m = vr.shape[1] // 2
    bits = lax.bitcast_convert_type(vr, U32)
    return lax.bitcast_convert_type((bits[:, :m] >> 16) | bits[:, m:], F32)


def _unpack_halves(p):
    u = lax.bitcast_convert_type(p, U32)
    lo = lax.bitcast_convert_type(u << 16, F32)
    hi = lax.bitcast_convert_type(u & jnp.uint32(0xFFFF0000), F32)
    return lo.astype(BF16), hi.astype(BF16)


def _split2(a):
    hi = a.astype(BF16)
    lo = (a - hi.astype(F32)).astype(BF16)
    return hi, lo


def _ada_kernel(c_ref, w_ref, b_ref, o_ref):
    s = _silu(c_ref[...]).astype(BF16)
    o_ref[...] = _dot(s, w_ref[...].astype(BF16)) + b_ref[...]


def _ada(cc, w, b):
    r, d = cc.shape
    nc = w.shape[1]
    return pl.pallas_call(
        _ada_kernel,
        grid=(nc // ADA_TN,),
        in_specs=[pl.BlockSpec((r, d), lambda j: (0, 0)),
                  pl.BlockSpec((d, ADA_TN), lambda j: (0, j)),
                  pl.BlockSpec((1, ADA_TN), lambda j: (0, j))],
        out_specs=pl.BlockSpec((r, ADA_TN), lambda j: (0, j)),
        out_shape=jax.ShapeDtypeStruct((r, nc), F32),
        compiler_params=_cparams(("arbitrary",)),
        name="ada",
    )(cc, w, b)


def _inproj_kernel(x_ref, g_ref, sc_ref, sh_ref, w_ref, zf_ref, zr_ref):
    x = x_ref[0]
    ms = jnp.mean(x * x, axis=-1, keepdims=True)
    gs = g_ref[...] * (1.0 + sc_ref[0])
    h = x * lax.rsqrt(ms + EPS) * gs + sh_ref[0]
    z = _dot(h.astype(BF16), w_ref[...])
    nf = zf_ref.shape[2]
    zf_ref[0] = z[:, :nf]
    zr_ref[0] = z[:, nf:].astype(BF16)


def _inproj(x, g, sc, sh, w, nf):
    bsz, l, d = x.shape
    nc = w.shape[1]
    tm = min(INPROJ_TM, l)
    return pl.pallas_call(
        _inproj_kernel,
        grid=(bsz, l // tm),
        in_specs=[pl.BlockSpec((1, tm, d), lambda b, i: (b, i, 0)),
                  pl.BlockSpec((1, d), lambda b, i: (0, 0)),
                  pl.BlockSpec((1, 1, d), lambda b, i: (b, 0, 0)),
                  pl.BlockSpec((1, 1, d), lambda b, i: (b, 0, 0)),
                  pl.BlockSpec((d, nc), lambda b, i: (0, 0))],
        out_specs=[pl.BlockSpec((1, tm, nf), lambda b, i: (b, i, 0)),
                   pl.BlockSpec((1, tm, nc - nf), lambda b, i: (b, i, 0))],
        out_shape=[jax.ShapeDtypeStruct((bsz, l, nf), F32),
                   jax.ShapeDtypeStruct((bsz, l, nc - nf), BF16)],
        compiler_params=_cparams(("arbitrary", "arbitrary")),
        name="inproj",
    )(x, g, sc, sh, w)


class _HgrnDirection:
    def __init__(self, lbl, st_ref, reverse):
        self.st_ref, self.reverse = st_ref, reverse
        e = jnp.exp(lbl - jnp.max(lbl, axis=0, keepdims=True))
        self.lb = e[0:1] / jnp.sum(e, axis=0, keepdims=True)
        row = lax.broadcasted_iota(I32, (CHUNK, CHUNK), 0)
        col = lax.broadcasted_iota(I32, (CHUNK, CHUNK), 1)
        self.tri = (col >= row) if reverse else (col <= row)
        self.tm = jnp.where(self.tri, 1.0, 0.0).astype(BF16)

    def tile(self, f_ref, v_ref, q_ref=None, o_ref=None):
        heads = self.st_ref.shape[0]
        n_chunks = f_ref.shape[1] // CHUNK
        crow = lambda c: slice(c * CHUNK, (c + 1) * CHUNK)
        hsl = lambda h: slice(h * HEAD_DIM, (h + 1) * HEAD_DIM)
        f = self.lb + (1.0 - self.lb) * jax.nn.sigmoid(f_ref[0])
        k = 1.0 - f
        hi, lo = _split2(jnp.log(f))
        cums = [_dot(self.tm, hi[crow(c)]) + _dot(self.tm, lo[crow(c)])
                for c in range(n_chunks)]
        lasts = [cm[0:1] if self.reverse else cm[CHUNK - 1:CHUNK] for cm in cums]
        cum = jnp.concatenate(cums, axis=0)
        dec = [jnp.exp(la) for la in lasts]
        k_intra = k * jnp.exp(-cum)
        ks = (k_intra * jnp.concatenate([jnp.broadcast_to(dc, (CHUNK, dc.shape[1])) for dc in dec], axis=0)
              ).astype(BF16)
        vb = v_ref[0].astype(BF16)
        kv = [[_dot_tn(vb[crow(c), hsl(h)], ks[crow(c), hsl(h)]) for h in range(heads)] for c in range(n_chunks)]
        emit = o_ref is not None
        if emit:
            qd = (_silu(q_ref[0].astype(F32)) * jnp.exp(cum)).astype(BF16)
            ki = k_intra.astype(BF16)
            sc = [[jnp.where(self.tri, _dot_nt(qd[crow(c), hsl(h)], ki[crow(c), hsl(h)]), 0.0).astype(BF16)
                   for h in range(heads)] for c in range(n_chunks)]
            intra = [[_dot(sc[c][h], vb[crow(c), hsl(h)]) for h in range(heads)] for c in range(n_chunks)]
        state = [self.st_ref[h] for h in range(heads)]
        for c in (range(n_chunks - 1, -1, -1) if self.reverse else range(n_chunks)):
            for h in range(heads):
                if emit:
                    inter = _dot_nt(qd[crow(c), hsl(h)], state[h].astype(BF16))
                    o_ref[0, crow(c), hsl(h)] = (intra[c][h] + inter).astype(o_ref.dtype)
                state[h] = state[h] * dec[c][:, hsl(h)] + kv[c][h]
        for h in range(heads):
            self.st_ref[h] = state[h]


def _hgrn_kernel(lbl_ref, cff_ref, cfb_ref, cv_ref, ff_ref, vf_ref, qf_ref, fb_ref, vb_ref, qb_ref,
                 of_ref, ob_ref, stf_ref, stb_ref):
    fwd = _HgrnDirection(lbl_ref[0], stf_ref, reverse=False)
    bwd = _HgrnDirection(lbl_ref[1], stb_ref, reverse=True)

    @pl.when(pl.program_id(1) == 0)
    def _():
        stf_ref[...] = jnp.zeros_like(stf_ref)
        stb_ref[...] = jnp.zeros_like(stb_ref)
        fwd.tile(cff_ref, cv_ref)
        bwd.tile(cfb_ref, cv_ref)

    fwd.tile(ff_ref, vf_ref, qf_ref, of_ref)
    bwd.tile(fb_ref, vb_ref, qb_ref, ob_ref)


def _hgrn(lbl, zcf, zcr, zf, zr):
    bsz, l, _ = zf.shape
    n_ctx_tok = zcf.shape[1]
    w = lbl.shape[2]
    heads = w // HEAD_DIM
    tt = min(HGRN_TT, l)
    nt = l // tt
    ctx_col = lambda c: pl.BlockSpec((1, n_ctx_tok, w), lambda b, i: (b, 0, c))
    fwd_col = lambda c: pl.BlockSpec((1, tt, w), lambda b, i: (b, i, c))
    bwd_col = lambda c: pl.BlockSpec((1, tt, w), lambda b, i: (b, nt - 1 - i, c))
    state = pltpu.VMEM((heads, HEAD_DIM, HEAD_DIM), F32)
    return pl.pallas_call(
        _hgrn_kernel,
        grid=(bsz, nt),
        in_specs=[pl.BlockSpec(lbl.shape, lambda b, i: (0, 0, 0)),
                  ctx_col(0), ctx_col(1), ctx_col(0),
                  fwd_col(0), fwd_col(0), fwd_col(1),
                  bwd_col(1), bwd_col(0), bwd_col(1)],
        out_specs=[fwd_col(0), bwd_col(0)],
        out_shape=[jax.ShapeDtypeStruct((bsz, l, w), BF16), jax.ShapeDtypeStruct((bsz, l, w), BF16)],
        scratch_shapes=[state, state],
        compiler_params=_cparams(("arbitrary", "arbitrary")),
        name="hgrn",
    )(lbl, zcf, zcf, zcr, zf, zr, zr, zf, zr, zr)


def _mix_kernel(of_ref, ob_ref, zg_ref, za_ref, zt_ref, x_ref, g1_ref, sc2_ref, sh2_ref,
                hg_ref, dwk_ref, dwb_ref, lng_ref, lnb_ref, wo_ref, n2g_ref, wr_ref, br_ref, shm_ref,
                xm_ref, h2_ref, eid_ref, wts_ref, rank_ref, cnt_ref,
                shf_ref, cv_ref, carry_ref, hh_ref, hl_ref, *, tm):
    w = of_ref.shape[2]
    heads = w // HEAD_DIM
    step = pl.program_id(0)

    @pl.when(step == 0)
    def _():
        carry_ref[...] = jnp.zeros_like(carry_ref)
        hh_ref[...] = jnp.zeros_like(hh_ref)
        hl_ref[...] = jnp.zeros_like(hl_ref)

    router = _route(hh_ref[...], hl_ref[...], wr_ref, br_ref, carry_ref, jnp.where(step > 0, 1.0, 0.0),
                    eid_ref, wts_ref, rank_ref, cnt_ref)
    next(router)

    o = of_ref[0].astype(F32) + ob_ref[0].astype(F32)
    parts = []
    for h in range(heads):
        oh = o[:, h * HEAD_DIM:(h + 1) * HEAD_DIM]
        parts.append(oh * lax.rsqrt(jnp.mean(oh * oh, axis=-1, keepdims=True) + EPS))
    o_mix = jnp.concatenate(parts, axis=-1) * hg_ref[...] * _silu(zg_ref[0].astype(F32))

    u = za_ref[0].astype(F32) * jax.nn.sigmoid(zt_ref[0].astype(F32))
    span = shf_ref.shape[2]
    u_hi, u_lo = _split2(u)
    for s in range(tm // GRID_W):
        rows = slice(s * GRID_W, (s + 1) * GRID_W)
        shifted = _dot(shm_ref[...], jnp.concatenate([u_hi[rows], u_lo[rows]], axis=0))
        for b in range(8):
            shf_ref[s, b] = shifted[b * span:(b + 1) * span]
    next(router)
    for s in range(tm // GRID_W):
        acc = jnp.zeros((GRID_W, u.shape[1]), F32) + dwb_ref[...]
        for k in range(CONV_K):
            a, b = divmod(CONV_LEAD - CONV_PAD + k, 8)
            acc = acc + shf_ref[s, b, 8 * a:8 * a + GRID_W, :] * dwk_ref[k:k + 1, :]
        cv_ref[s * GRID_W:(s + 1) * GRID_W, :] = acc
        if s == 0:
            next(router)
    cv = cv_ref[...]
    mu = jnp.mean(cv, axis=-1, keepdims=True)
    cen = cv - mu
    var = jnp.mean(cen * cen, axis=-1, keepdims=True)
    c_mix = _silu(cen * lax.rsqrt(var + EPS) * lng_ref[...] + lnb_ref[...])
    next(router, None)

    mix = _dot(o_mix.astype(BF16), wo_ref[0:w, :]) + _dot(c_mix.astype(BF16), wo_ref[w:, :])
    xm = x_ref[0] + g1_ref[0] * mix
    xm_ref[0] = xm

    ms = jnp.mean(xm * xm, axis=-1, keepdims=True)
    h2 = xm * lax.rsqrt(ms + EPS) * (n2g_ref[...] * (1.0 + sc2_ref[0])) + sh2_ref[0]
    hh = h2.astype(BF16)
    h2_rounded = hh.astype(F32)
    h2_ref[0] = _pack_halves(h2_rounded)
    hh_ref[...] = hh
    hl_ref[...] = (h2 - h2_rounded).astype(BF16)


def _route(hh, hl, wr_ref, br_ref, carry_ref, live, eid_ref, wts_ref, rank_ref, cnt_ref):
    tm = hh.shape[0]
    wr = wr_ref[...]
    wh = wr.astype(BF16)
    wl = (wr - wh.astype(F32)).astype(BF16)
    lg = _dot_nt(wh, hh) + _dot_nt(wh, hl) + _dot_nt(wl, hh) + br_ref[...]
    yield

    neg = -jnp.inf
    r8 = lax.broadcasted_iota(I32, (EXPERTS_PER_GROUP, tm), 0).astype(F32)
    gl = jnp.where(r8 < N_GROUPS, lg[0:8], neg)
    gmax = jnp.max(gl, axis=0, keepdims=True)
    grp = jnp.min(jnp.where(gl == gmax, r8, 8.0), axis=0, keepdims=True)
    p_grp = 1.0 / jnp.sum(jnp.exp(gl - gmax), axis=0, keepdims=True)
    es = lg[8:16]
    for g in range(1, N_GROUPS):
        es = jnp.where(grp == float(g), lg[8 + 8 * g:16 + 8 * g], es)
    m1 = jnp.max(es, axis=0, keepdims=True)
    i1 = jnp.min(jnp.where(es == m1, r8, 8.0), axis=0, keepdims=True)
    es2 = jnp.where(r8 == i1, neg, es)
    m2 = jnp.max(es2, axis=0, keepdims=True)
    i2 = jnp.min(jnp.where(es2 == m2, r8, 8.0), axis=0, keepdims=True)
    e2 = jnp.exp(m2 - m1)
    w1 = p_grp / (1.0 + e2)
    w2 = p_grp * e2 / (1.0 + e2)
    eid0 = grp * float(EXPERTS_PER_GROUP) + i1
    eid1 = grp * float(EXPERTS_PER_GROUP) + i2
    eid_ref[0:1, :] = eid0.astype(I32)
    eid_ref[1:2, :] = eid1.astype(I32)
    wts_ref[0:1, :] = w1
    wts_ref[1:2, :] = w2
    yield

    r32 = lax.broadcasted_iota(I32, (N_EXPERTS, tm), 0).astype(F32)
    oh0 = jnp.where(r32 == eid0, live, 0.0)
    oh1 = jnp.where(r32 == eid1, live, 0.0)
    src = lax.broadcasted_iota(I32, (tm, tm), 0)
    dst = lax.broadcasted_iota(I32, (tm, tm), 1)
    before = jnp.where(src < dst, 1.0, 0.0).astype(BF16)
    ex = _dot(jnp.concatenate([oh0, oh1], axis=0).astype(BF16), before)
    yield
    base = ex[0:N_EXPERTS] + ex[N_EXPERTS:] + carry_ref[...]
    rank_ref[0:1, :] = jnp.sum(oh0 * base, axis=0, keepdims=True).astype(I32)
    rank_ref[1:2, :] = jnp.sum(oh1 * base, axis=0, keepdims=True).astype(I32)
    carry = carry_ref[...] + jnp.sum(oh0 + oh1, axis=1, keepdims=True)
    carry_ref[...] = carry
    cnt_ref[...] = jnp.broadcast_to(carry, cnt_ref.shape)


def _conv_shift_matrix():
    r = np.arange(8 * CONV_SPAN)
    src = r // CONV_SPAN + r % CONV_SPAN - CONV_LEAD
    sel = (src[:, None] == np.arange(GRID_W)[None, :]).astype(np.float32)
    return jnp.asarray(np.concatenate([sel, sel], axis=1), dtype=BF16)


def _mix(o_f, o_b, z, x, g1, sc2, sh2, hg, dwk, dwb, lng, lnb, wo, n2g, wr, br, b0, bsz):
    _, l, d = x.shape
    shm = _conv_shift_matrix()
    w = o_f.shape[2]
    cw = dwk.shape[1]
    tm = min(MIX_TM, l)
    nt = l // tm
    n = bsz * l
    tiles = bsz * nt
    mixed = lambda s: jnp.minimum(s, tiles - 1)
    src = lambda c: (lambda s: (b0 + mixed(s) // nt, mixed(s) % nt, c))
    tok = lambda s: (mixed(s) // nt, mixed(s) % nt, 0)
    vec = lambda s: (0, 0)
    per_b = lambda s: (b0 + mixed(s) // nt, 0, 0)
    flat = lambda s: (0, jnp.maximum(s - 1, 0))
    kern = functools.partial(_mix_kernel, tm=tm)
    return pl.pallas_call(
        kern,
        grid=(tiles + 1,),
        in_specs=[pl.BlockSpec((1, tm, w), src(0)),
                  pl.BlockSpec((1, tm, w), src(0)),
                  pl.BlockSpec((1, tm, w), src(2)),
                  pl.BlockSpec((1, tm, cw), src(3)),
                  pl.BlockSpec((1, tm, cw), src(4)),
                  pl.BlockSpec((1, tm, d), src(0)),
                  pl.BlockSpec((1, 1, d), per_b),
                  pl.BlockSpec((1, 1, d), per_b),
                  pl.BlockSpec((1, 1, d), per_b),
                  pl.BlockSpec((1, w), vec),
                  pl.BlockSpec((CONV_K, cw), vec),
                  pl.BlockSpec((1, cw), vec),
                  pl.BlockSpec((1, cw), vec),
                  pl.BlockSpec((1, cw), vec),
                  pl.BlockSpec((w + cw, d), vec),
                  pl.BlockSpec((1, d), vec),
                  pl.BlockSpec((ROUTER_ROWS, d), vec),
                  pl.BlockSpec((ROUTER_ROWS, 1), vec),
                  pl.BlockSpec(shm.shape, vec)],
        out_specs=[pl.BlockSpec((1, tm, d), tok),
                   pl.BlockSpec((1, tm, d // 2), tok),
                   pl.BlockSpec((TOP_K, tm), flat),
                   pl.BlockSpec((TOP_K, tm), flat),
                   pl.BlockSpec((TOP_K, tm), flat),
                   pl.BlockSpec((N_EXPERTS, 128), vec)],
        out_shape=[jax.ShapeDtypeStruct((bsz, l, d), F32),
                   jax.ShapeDtypeStruct((bsz, l, d // 2), F32),
                   jax.ShapeDtypeStruct((TOP_K, n), I32),
                   jax.ShapeDtypeStruct((TOP_K, n), F32),
                   jax.ShapeDtypeStruct((TOP_K, n), I32),
                   jax.ShapeDtypeStruct((N_EXPERTS, 128), F32)],
        scratch_shapes=[pltpu.VMEM((tm // GRID_W, 8, CONV_SPAN, cw), F32),
                        pltpu.VMEM((tm, cw), F32),
                        pltpu.VMEM((N_EXPERTS, 1), F32),
                        pltpu.VMEM((tm, d), BF16),
                        pltpu.VMEM((tm, d), BF16)],
        compiler_params=_cparams(("arbitrary",)),
        name="mix",
    )(o_f, o_b, z, z, z, x, g1, sc2, sh2, hg, dwk, dwb, lng, lnb, wo, n2g, wr, br, shm)


def _pos_kernel(ps_ref, eid_ref, rank_ref, pos_ref):
    eid = eid_ref[...]
    acc = rank_ref[...]
    for e in range(N_EXPERTS):
        acc = acc + jnp.where(eid == e, ps_ref[e], 0)
    pos_ref[...] = acc


def _positions(pstarts, eid, rank):
    shape = eid.shape
    eid2 = eid.reshape(-1, 128)
    full = pl.BlockSpec(eid2.shape, lambda i: (0, 0))
    pos = pl.pallas_call(
        _pos_kernel,
        grid=(1,),
        in_specs=[pl.BlockSpec(memory_space=pltpu.SMEM), full, full],
        out_specs=full,
        out_shape=jax.ShapeDtypeStruct(eid2.shape, I32),
        name="positions",
    )(pstarts, eid2, rank.reshape(-1, 128))
    return pos.reshape(shape)


def _sc_mesh():
    return plsc.VectorSubcoreMesh(core_axis_name="c", subcore_axis_name="s",
                                  num_cores=SC_CORES, num_subcores=SC_SUBCORES)


def _sc_worker():
    return lax.axis_index("s") * SC_CORES + lax.axis_index("c")


def _sc_dispatch(h2p, pos4, padidx, zero_rows, n_rows):
    n, dw = h2p.shape
    _, workers, chunks, _ = pos4.shape
    pad_chunks = padidx.shape[1]
    per_worker = n // workers

    def body(h_hbm, pos_hbm, pad_hbm, zero_hbm, out_hbm, idx_v, pad_v, rows_v):
        wid = _sc_worker()
        base = wid * per_worker
        for k in range(TOP_K):
            pltpu.sync_copy(pos_hbm.at[k, wid], idx_v.at[k])
        pltpu.sync_copy(pad_hbm.at[wid], pad_v)
        pltpu.sync_copy(zero_hbm, rows_v)
        for j in range(pad_chunks):
            pltpu.sync_copy(rows_v, out_hbm.at[pad_v.at[j]])
        for j in range(chunks):
            pltpu.sync_copy(h_hbm.at[pl.ds(base + j * SC_W, SC_W)], rows_v)
            for k in range(TOP_K):
                pltpu.sync_copy(rows_v, out_hbm.at[idx_v.at[k, j]])

    return pl.kernel(
        body,
        out_type=jax.ShapeDtypeStruct((n_rows, dw), F32),
        mesh=_sc_mesh(),
        scratch_types=[pltpu.VMEM((TOP_K, chunks, SC_W), I32),
                       pltpu.VMEM((pad_chunks, SC_W), I32),
                       pltpu.VMEM((SC_W, dw), F32)],
        name="sc_dispatch",
    )(h2p, pos4, padidx, zero_rows)


def _sc_combine_gather(ybuf, pos4):
    _, dw = ybuf.shape
    _, workers, chunks, _ = pos4.shape
    per_worker = chunks * SC_W

    def body(y_hbm, pos_hbm, out_hbm, idx_v, rows_v):
        wid = _sc_worker()
        base = wid * per_worker
        for k in range(TOP_K):
            pltpu.sync_copy(pos_hbm.at[k, wid], idx_v.at[k])
        for k in range(TOP_K):
            for j in range(chunks):
                pltpu.sync_copy(y_hbm.at[idx_v.at[k, j]], rows_v)
                pltpu.sync_copy(rows_v, out_hbm.at[k, pl.ds(base + j * SC_W, SC_W)])

    return pl.kernel(
        body,
        out_type=jax.ShapeDtypeStruct((TOP_K, workers * per_worker, dw), F32),
        mesh=_sc_mesh(),
        scratch_types=[pltpu.VMEM((TOP_K, chunks, SC_W), I32),
                       pltpu.VMEM((SC_W, dw), F32)],
        name="sc_combine",
    )(ybuf, pos4)


def _ffn_kernel(blk_e_ref, nblk_ref, x_ref, wg_ref, wu_ref, wd_ref, y_ref):
    del blk_e_ref
    used = pl.program_id(0) < nblk_ref[0]
    half = x_ref.shape[1]

    @pl.when(used)
    def _():
        x_lo, x_hi = _unpack_halves(x_ref[...])
        bf = lambda wv: wv.astype(BF16)
        g = _dot(x_lo, bf(wg_ref[0, 0:half, :])) + _dot(x_hi, bf(wg_ref[0, half:, :]))
        u = _dot(x_lo, bf(wu_ref[0, 0:half, :])) + _dot(x_hi, bf(wu_ref[0, half:, :]))
        y = _dot((_silu(g) * u).astype(BF16), bf(wd_ref[0]))
        y_ref[...] = _pack_halves(y.astype(BF16).astype(F32))

    @pl.when(jnp.logical_not(used))
    def _():
        y_ref[...] = jnp.zeros_like(y_ref)


def _ffn(blk_e, nblk, xbuf, wg, wu, wd, tb):
    _, dw = xbuf.shape
    _, d, de = wg.shape
    n_blk = blk_e.shape[0]
    x_blk = lambda i, be, nb: (jnp.minimum(i, nb[0] - 1), 0)
    grid_spec = pltpu.PrefetchScalarGridSpec(
        num_scalar_prefetch=2,
        grid=(n_blk,),
        in_specs=[pl.BlockSpec((tb, dw), x_blk),
                  pl.BlockSpec((1, d, de), lambda i, be, nb: (be[i], 0, 0)),
                  pl.BlockSpec((1, d, de), lambda i, be, nb: (be[i], 0, 0)),
                  pl.BlockSpec((1, de, d), lambda i, be, nb: (be[i], 0, 0))],
        out_specs=pl.BlockSpec((tb, dw), lambda i, be, nb: (i, 0)),
    )
    return pl.pallas_call(
        _ffn_kernel,
        grid_spec=grid_spec,
        out_shape=jax.ShapeDtypeStruct((n_blk * tb, dw), F32),
        compiler_params=_cparams(("arbitrary",)),
        name="ffn",
    )(blk_e, nblk, xbuf, wg, wu, wd)


def _final_kernel(yg_ref, xm_ref, wt_ref, g2_ref, fg_ref, *out_refs):
    o_ref = out_refs[-1]
    wt = wt_ref[...]
    y = None
    for k in range(TOP_K):
        lo, hi = _unpack_halves(yg_ref[k])
        yk = jnp.concatenate([lo, hi], axis=-1).astype(F32) * wt[:, k:k + 1]
        y = yk if y is None else y + yk
    x = xm_ref[0] + g2_ref[0] * y
    ms = jnp.mean(x * x, axis=-1, keepdims=True)
    o_ref[0] = x * lax.rsqrt(ms + EPS) * fg_ref[...]


def _final(yg, xm, wt, g2, fg, out_prev, b0, total):
    bsz, l, d = xm.shape
    dw = yg.shape[2]
    tm = min(MOE_TM, l)
    nt = l // tm
    in_specs = [pl.BlockSpec((TOP_K, tm, dw), lambda b, i: (0, b * nt + i, 0)),
                pl.BlockSpec((1, tm, d), lambda b, i: (b, i, 0)),
                pl.BlockSpec((tm, TOP_K), lambda b, i: (b * nt + i, 0)),
                pl.BlockSpec((1, 1, d), lambda b, i: (b0 + b, 0, 0)),
                pl.BlockSpec((1, d), lambda b, i: (0, 0))]
    args = [yg, xm, wt, g2, fg]
    aliases = {}
    if out_prev is not None:
        in_specs.append(pl.BlockSpec(memory_space=pl.ANY))
        args.append(out_prev)
        aliases = {5: 0}
    return pl.pallas_call(
        _final_kernel,
        grid=(bsz, nt),
        in_specs=in_specs,
        out_specs=pl.BlockSpec((1, tm, d), lambda b, i: (b0 + b, i, 0)),
        out_shape=jax.ShapeDtypeStruct((total, l, d), F32),
        input_output_aliases=aliases,
        compiler_params=_cparams(("arbitrary", "arbitrary")),
        name="final",
    )(*args)


def _layer(x, ctx, mod, mod_c, norm1_g, w_in, lb_logits, hgrn_norm_g, dw_kernel, dw_bias, conv_ln_g,
           conv_ln_b, w_out, norm2_g, wr, br, w_gate, w_up, w_down, final_norm_g):
    bsz, l, d = x.shape
    w = lb_logits.shape[2]
    heads = w // HEAD_DIM
    row = lambda v: v.reshape(1, -1)
    per_b = lambda v: v.reshape(bsz, 1, d)
    sh1, sc1, g1, sh2, sc2, g2 = [per_b(mod[:, j]) for j in range(N_MOD)]
    csh1 = jnp.broadcast_to(mod_c[0].reshape(1, 1, d), (bsz, 1, d))
    csc1 = jnp.broadcast_to(mod_c[1].reshape(1, 1, d), (bsz, 1, d))

    w_in_b = w_in.astype(BF16)
    zf, zr = _inproj(x, row(norm1_g), sc1, sh1, w_in_b, 2 * w)
    zcf, zcr = _inproj(ctx, row(norm1_g), csc1, csh1, w_in_b[:, :3 * w], 2 * w)
    o_f, o_b = _hgrn(lb_logits, zcf, zcr, zf, zr)
    w_out_b = w_out.astype(BF16)

    groups = MOE_GROUPS if bsz % MOE_GROUPS == 0 else 1
    out = None
    for gi in range(groups):
        b0, gb = gi * (bsz // groups), bsz // groups
        mixed = _mix(o_f, o_b, zr, x, g1, sc2, sh2, row(jnp.tile(hgrn_norm_g, heads)), dw_kernel,
                     row(dw_bias), row(conv_ln_g), row(conv_ln_b), w_out_b, row(norm2_g), wr, br, b0, gb)
        out = _moe_group(mixed, g2, w_gate, w_up, w_down, row(final_norm_g), out, b0, bsz)
    return out


def _moe_group(mixed, g2, w_gate, w_up, w_down, final_g, out_prev, b0, total):
    xm, h2, eid, wts, rank, cnt = mixed
    gb, l, d = xm.shape
    n = gb * l

    tb = FFN_TB
    n_blk = -(-(n * TOP_K) // tb) + N_EXPERTS
    counts = cnt[:, 0].astype(I32)
    pcounts = (counts + tb - 1) // tb * tb
    pends = jnp.cumsum(pcounts)
    pstarts = pends - pcounts
    pos = _positions(pstarts.astype(I32), eid, rank)
    blk_start = jnp.arange(n_blk, dtype=I32) * tb
    blk_e = jnp.minimum(jnp.sum((pends[None, :] <= blk_start[:, None]).astype(I32), axis=1), N_EXPERTS - 1)
    nblk_used = (pends[-1:] // tb).astype(I32)

    workers = SC_CORES * SC_SUBCORES
    pos4 = pos.reshape(TOP_K, workers, n // (workers * SC_W), SC_W)
    lane = jnp.arange(tb, dtype=I32)[None, :]
    pad_rows = jnp.where(lane < (pcounts - counts)[:, None], (pstarts + counts)[:, None] + lane,
                         n_blk * tb + lane)
    padidx = pad_rows.astype(I32).reshape(workers, N_EXPERTS // workers * tb // SC_W, SC_W)
    xbuf = _sc_dispatch(h2.reshape(n, d // 2), pos4, padidx, jnp.zeros((SC_W, d // 2), F32), (n_blk + 1) * tb)
    ybuf = _ffn(blk_e, nblk_used, xbuf, w_gate, w_up, w_down, tb)
    yg = _sc_combine_gather(ybuf, pos4)
    return _final(yg, xm, wts.T, g2, final_g, out_prev, b0, total)


def kernel(x, c, ctx, c_ctx, w_ada, b_ada, norm1_g, w_in, lb_logits, hgrn_norm_g, dw_kernel, dw_bias,
           conv_ln_g, conv_ln_b, w_out, norm2_g, router_group_w, router_group_b, router_expert_w,
           router_expert_b, w_expert_gate, w_expert_up, w_expert_down, final_norm_g):
    depth = w_ada.shape[0]
    assert depth == 1, "context tokens are only updated between layers; a single layer is implemented"
    bsz, l, d = x.shape
    l0 = 0
    rows = -(-(bsz + 1) // 8) * 8
    cc = jnp.zeros((rows, d), F32).at[:bsz].set(c).at[bsz].set(c_ctx)
    mod_all = _ada(cc, w_ada[l0], b_ada[l0].reshape(1, -1))
    mod = mod_all[:bsz].reshape(bsz, N_MOD, d)
    mod_c = mod_all[bsz].reshape(N_MOD, d)
    pad8 = jnp.zeros((d, 8 - N_GROUPS), F32)
    wr = jnp.concatenate([router_group_w[l0], pad8, router_expert_w[l0], jnp.zeros((d, 8), F32)], axis=1).T
    br = jnp.concatenate([router_group_b[l0], jnp.zeros((8 - N_GROUPS,), F32), router_expert_b[l0],
                          jnp.zeros((8,), F32)]).reshape(ROUTER_ROWS, 1)
    return _layer(x, ctx, mod, mod_c, norm1_g[l0], w_in[l0], lb_logits, hgrn_norm_g[l0], dw_kernel[l0], dw_bias[l0], conv_ln_g[l0], conv_ln_b[l0], w_out[l0],
                  norm2_g[l0], wr, br, w_expert_gate[l0], w_expert_up[l0], w_expert_down[l0], final_norm_g)
```

```python
import functools

import numpy as np

import jax
import jax.numpy as jnp
from jax import lax
from jax.experimental import pallas as pl
from jax.experimental.pallas import tpu as pltpu
from jax.experimental.pallas import tpu_sc as plsc

F32 = jnp.float32
BF16 = jnp.bfloat16
I32 = jnp.int32
U32 = jnp.uint32

EPS = 1e-6
HEAD_DIM = 128
CHUNK = 64
GRID_W = 64
CONV_K = 31
CONV_PAD = CONV_K // 2
CONV_LEAD = 16
CONV_SPAN = GRID_W + 24
N_GROUPS = 4
EXPERTS_PER_GROUP = 8
N_EXPERTS = N_GROUPS * EXPERTS_PER_GROUP
TOP_K = 2
N_MOD = 6
ROUTER_ROWS = 48

ADA_TN = 1024
INPROJ_TM = 1024
HGRN_TT = 512
MIX_TM = 256
MOE_TM = 512
FFN_TB = 1024
VMEM_LIMIT = 48 * 1024 * 1024
SC_CORES = 2
SC_SUBCORES = 16
SC_W = 128
MOE_GROUPS = 1


def _cparams(sem):
    return pltpu.CompilerParams(dimension_semantics=sem, vmem_limit_bytes=VMEM_LIMIT)


def _silu(v):
    return v * jax.nn.sigmoid(v)


def _dot(a, b):
    return jnp.dot(a, b, preferred_element_type=F32)


def _dot_nt(a, b):
    return lax.dot_general(a, b, (((1,), (1,)), ((), ())), preferred_element_type=F32)


def _dot_tn(a, b):
    return lax.dot_general(a, b, (((0,), (0,)), ((), ())), preferred_element_type=F32)


def _pack_halves(vr):
    m = vr.shape[1] // 2
    bits = lax.bitcast_convert_type(vr, U32)
    return lax.bitcast_convert_type((bits[:, :m] >> 16) | bits[:, m:], F32)


def _unpack_halves(p):
    u = lax.bitcast_convert_type(p, U32)
    lo = lax.bitcast_convert_type(u << 16, F32)
    hi = lax.bitcast_convert_type(u & jnp.uint32(0xFFFF0000), F32)
    return lo.astype(BF16), hi.astype(BF16)


def _split2(a):
    hi = a.astype(BF16)
    lo = (a - hi.astype(F32)).astype(BF16)
    return hi, lo


def _ada_kernel(c_ref, w_ref, b_ref, o_ref):
    s = _silu(c_ref[...]).astype(BF16)
    o_ref[...] = _dot(s, w_ref[...].astype(BF16)) + b_ref[...]


def _ada(cc, w, b):
    r, d = cc.shape
    nc = w.shape[1]
    return pl.pallas_call(
        _ada_kernel,
        grid=(nc // ADA_TN,),
        in_specs=[pl.BlockSpec((r, d), lambda j: (0, 0)),
                  pl.BlockSpec((d, ADA_TN), lambda j: (0, j)),
                  pl.BlockSpec((1, ADA_TN), lambda j: (0, j))],
        out_specs=pl.BlockSpec((r, ADA_TN), lambda j: (0, j)),
        out_shape=jax.ShapeDtypeStruct((r, nc), F32),
        compiler_params=_cparams(("arbitrary",)),
        name="ada",
    )(cc, w, b)


def _inproj_kernel(x_ref, g_ref, sc_ref, sh_ref, w_ref, zf_ref, zr_ref):
    x = x_ref[0]
    ms = jnp.mean(x * x, axis=-1, keepdims=True)
    gs = g_ref[...] * (1.0 + sc_ref[0])
    h = x * lax.rsqrt(ms + EPS) * gs + sh_ref[0]
    z = _dot(h.astype(BF16), w_ref[...])
    nf = zf_ref.shape[2]
    zf_ref[0] = z[:, :nf]
    zr_ref[0] = z[:, nf:].astype(BF16)


def _inproj(x, g, sc, sh, w, nf):
    bsz, l, d = x.shape
    nc = w.shape[1]
    tm = min(INPROJ_TM, l)
    return pl.pallas_call(
        _inproj_kernel,
        grid=(bsz, l // tm),
        in_specs=[pl.BlockSpec((1, tm, d), lambda b, i: (b, i, 0)),
                  pl.BlockSpec((1, d), lambda b, i: (0, 0)),
                  pl.BlockSpec((1, 1, d), lambda b, i: (b, 0, 0)),
                  pl.BlockSpec((1, 1, d), lambda b, i: (b, 0, 0)),
                  pl.BlockSpec((d, nc), lambda b, i: (0, 0))],
        out_specs=[pl.BlockSpec((1, tm, nf), lambda b, i: (b, i, 0)),
                   pl.BlockSpec((1, tm, nc - nf), lambda b, i: (b, i, 0))],
        out_shape=[jax.ShapeDtypeStruct((bsz, l, nf), F32),
                   jax.ShapeDtypeStruct((bsz, l, nc - nf), BF16)],
        compiler_params=_cparams(("arbitrary", "arbitrary")),
        name="inproj",
    )(x, g, sc, sh, w)


class _HgrnDirection:
    def __init__(self, lbl, st_ref, reverse):
        self.st_ref, self.reverse = st_ref, reverse
        e = jnp.exp(lbl - jnp.max(lbl, axis=0, keepdims=True))
        self.lb = e[0:1] / jnp.sum(e, axis=0, keepdims=True)
        row = lax.broadcasted_iota(I32, (CHUNK, CHUNK), 0)
        col = lax.broadcasted_iota(I32, (CHUNK, CHUNK), 1)
        self.tri = (col >= row) if reverse else (col <= row)
        self.tm = jnp.where(self.tri, 1.0, 0.0).astype(BF16)

    def tile(self, f_ref, v_ref, q_ref=None, o_ref=None):
        heads = self.st_ref.shape[0]
        n_chunks = f_ref.shape[1] // CHUNK
        crow = lambda c: slice(c * CHUNK, (c + 1) * CHUNK)
        hsl = lambda h: slice(h * HEAD_DIM, (h + 1) * HEAD_DIM)
        f = self.lb + (1.0 - self.lb) * jax.nn.sigmoid(f_ref[0])
        k = 1.0 - f
        hi, lo = _split2(jnp.log(f))
        cums = [_dot(self.tm, hi[crow(c)]) + _dot(self.tm, lo[crow(c)])
                for c in range(n_chunks)]
        lasts = [cm[0:1] if self.reverse else cm[CHUNK - 1:CHUNK] for cm in cums]
        cum = jnp.concatenate(cums, axis=0)
        dec = [jnp.exp(la) for la in lasts]
        k_intra = k * jnp.exp(-cum)
        ks = (k_intra * jnp.concatenate([jnp.broadcast_to(dc, (CHUNK, dc.shape[1])) for dc in dec], axis=0)
              ).astype(BF16)
        vb = v_ref[0].astype(BF16)
        kv = [[_dot_tn(vb[crow(c), hsl(h)], ks[crow(c), hsl(h)]) for h in range(heads)] for c in range(n_chunks)]
        emit = o_ref is not None
        if emit:
            qd = (_silu(q_ref[0].astype(F32)) * jnp.exp(cum)).astype(BF16)
            ki = k_intra.astype(BF16)
            sc = [[jnp.where(self.tri, _dot_nt(qd[crow(c), hsl(h)], ki[crow(c), hsl(h)]), 0.0).astype(BF16)
                   for h in range(heads)] for c in range(n_chunks)]
            intra = [[_dot(sc[c][h], vb[crow(c), hsl(h)]) for h in range(heads)] for c in range(n_chunks)]
        state = [self.st_ref[h] for h in range(heads)]
        for c in (range(n_chunks - 1, -1, -1) if self.reverse else range(n_chunks)):
            for h in range(heads):
                if emit:
                    inter = _dot_nt(qd[crow(c), hsl(h)], state[h].astype(BF16))
                    o_ref[0, crow(c), hsl(h)] = (intra[c][h] + inter).astype(o_ref.dtype)
                state[h] = state[h] * dec[c][:, hsl(h)] + kv[c][h]
        for h in range(heads):
            self.st_ref[h] = state[h]


def _hgrn_kernel(lbl_ref, cff_ref, cfb_ref, cv_ref, ff_ref, vf_ref, qf_ref, fb_ref, vb_ref, qb_ref,
                 of_ref, ob_ref, stf_ref, stb_ref):
    fwd = _HgrnDirection(lbl_ref[0], stf_ref, reverse=False)
    bwd = _HgrnDirection(lbl_ref[1], stb_ref, reverse=True)

    @pl.when(pl.program_id(1) == 0)
    def _():
        stf_ref[...] = jnp.zeros_like(stf_ref)
        stb_ref[...] = jnp.zeros_like(stb_ref)
        fwd.tile(cff_ref, cv_ref)
        bwd.tile(cfb_ref, cv_ref)

    fwd.tile(ff_ref, vf_ref, qf_ref, of_ref)
    bwd.tile(fb_ref, vb_ref, qb_ref, ob_ref)


def _hgrn(lbl, zcf, zcr, zf, zr):
    bsz, l, _ = zf.shape
    n_ctx_tok = zcf.shape[1]
    w = lbl.shape[2]
    heads = w // HEAD_DIM
    tt = min(HGRN_TT, l)
    nt = l // tt
    ctx_col = lambda c: pl.BlockSpec((1, n_ctx_tok, w), lambda b, i: (b, 0, c))
    fwd_col = lambda c: pl.BlockSpec((1, tt, w), lambda b, i: (b, i, c))
    bwd_col = lambda c: pl.BlockSpec((1, tt, w), lambda b, i: (b, nt - 1 - i, c))
    state = pltpu.VMEM((heads, HEAD_DIM, HEAD_DIM), F32)
    return pl.pallas_call(
        _hgrn_kernel,
        grid=(bsz, nt),
        in_specs=[pl.BlockSpec(lbl.shape, lambda b, i: (0, 0, 0)),
                  ctx_col(0), ctx_col(1), ctx_col(0),
                  fwd_col(0), fwd_col(0), fwd_col(1),
                  bwd_col(1), bwd_col(0), bwd_col(1)],
        out_specs=[fwd_col(0), bwd_col(0)],
        out_shape=[jax.ShapeDtypeStruct((bsz, l, w), BF16), jax.ShapeDtypeStruct((bsz, l, w), BF16)],
        scratch_shapes=[state, state],
        compiler_params=_cparams(("arbitrary", "arbitrary")),
        name="hgrn",
    )(lbl, zcf, zcf, zcr, zf, zr, zr, zf, zr, zr)


def _mix_kernel(of_ref, ob_ref, zg_ref, za_ref, zt_ref, x_ref, g1_ref, sc2_ref, sh2_ref,
                hg_ref, dwk_ref, dwb_ref, lng_ref, lnb_ref, wo_ref, n2g_ref, wr_ref, br_ref, shm_ref,
                xm_ref, h2_ref, eid_ref, wts_ref, rank_ref, cnt_ref,
                shf_ref, cv_ref, carry_ref, hh_ref, hl_ref, *, tm):
    w = of_ref.shape[2]
    heads = w // HEAD_DIM
    step = pl.program_id(0)

    @pl.when(step == 0)
    def _():
        carry_ref[...] = jnp.zeros_like(carry_ref)
        hh_ref[...] = jnp.zeros_like(hh_ref)
        hl_ref[...] = jnp.zeros_like(hl_ref)

    router = _route(hh_ref[...], hl_ref[...], wr_ref, br_ref, carry_ref, jnp.where(step > 0, 1.0, 0.0),
                    eid_ref, wts_ref, rank_ref, cnt_ref)
    next(router)

    o = of_ref[0].astype(F32) + ob_ref[0].astype(F32)
    parts = []
    for h in range(heads):
        oh = o[:, h * HEAD_DIM:(h + 1) * HEAD_DIM]
        parts.append(oh * lax.rsqrt(jnp.mean(oh * oh, axis=-1, keepdims=True) + EPS))
    o_mix = jnp.concatenate(parts, axis=-1) * hg_ref[...] * _silu(zg_ref[0].astype(F32))

    u = za_ref[0].astype(F32) * jax.nn.sigmoid(zt_ref[0].astype(F32))
    span = shf_ref.shape[2]
    u_hi, u_lo = _split2(u)
    for s in range(tm // GRID_W):
        rows = slice(s * GRID_W, (s + 1) * GRID_W)
        shifted = _dot(shm_ref[...], jnp.concatenate([u_hi[rows], u_lo[rows]], axis=0))
        for b in range(8):
            shf_ref[s, b] = shifted[b * span:(b + 1) * span]
    next(router)
    for s in range(tm // GRID_W):
        acc = jnp.zeros((GRID_W, u.shape[1]), F32) + dwb_ref[...]
        for k in range(CONV_K):
            a, b = divmod(CONV_LEAD - CONV_PAD + k, 8)
            acc = acc + shf_ref[s, b, 8 * a:8 * a + GRID_W, :] * dwk_ref[k:k + 1, :]
        cv_ref[s * GRID_W:(s + 1) * GRID_W, :] = acc
        if s == 0:
            next(router)
    cv = cv_ref[...]
    mu = jnp.mean(cv, axis=-1, keepdims=True)
    cen = cv - mu
    var = jnp.mean(cen * cen, axis=-1, keepdims=True)
    c_mix = _silu(cen * lax.rsqrt(var + EPS) * lng_ref[...] + lnb_ref[...])
    next(router, None)

    mix = _dot(o_mix.astype(BF16), wo_ref[0:w, :]) + _dot(c_mix.astype(BF16), wo_ref[w:, :])
    xm = x_ref[0] + g1_ref[0] * mix
    xm_ref[0] = xm

    ms = jnp.mean(xm * xm, axis=-1, keepdims=True)
    h2 = xm * lax.rsqrt(ms + EPS) * (n2g_ref[...] * (1.0 + sc2_ref[0])) + sh2_ref[0]
    hh = h2.astype(BF16)
    h2_rounded = hh.astype(F32)
    h2_ref[0] = _pack_halves(h2_rounded)
    hh_ref[...] = hh
    hl_ref[...] = (h2 - h2_rounded).astype(BF16)


def _route(hh, hl, wr_ref, br_ref, carry_ref, live, eid_ref, wts_ref, rank_ref, cnt_ref):
    tm = hh.shape[0]
    wr = wr_ref[...]
    wh = wr.astype(BF16)
    wl = (wr - wh.astype(F32)).astype(BF16)
    lg = _dot_nt(wh, hh) + _dot_nt(wh, hl) + _dot_nt(wl, hh) + br_ref[...]
    yield

    neg = -jnp.inf
    r8 = lax.broadcasted_iota(I32, (EXPERTS_PER_GROUP, tm), 0).astype(F32)
    gl = jnp.where(r8 < N_GROUPS, lg[0:8], neg)
    gmax = jnp.max(gl, axis=0, keepdims=True)
    grp = jnp.min(jnp.where(gl == gmax, r8, 8.0), axis=0, keepdims=True)
    p_grp = 1.0 / jnp.sum(jnp.exp(gl - gmax), axis=0, keepdims=True)
    es = lg[8:16]
    for g in range(1, N_GROUPS):
        es = jnp.where(grp == float(g), lg[8 + 8 * g:16 + 8 * g], es)
    m1 = jnp.max(es, axis=0, keepdims=True)
    i1 = jnp.min(jnp.where(es == m1, r8, 8.0), axis=0, keepdims=True)
    es2 = jnp.where(r8 == i1, neg, es)
    m2 = jnp.max(es2, axis=0, keepdims=True)
    i2 = jnp.min(jnp.where(es2 == m2, r8, 8.0), axis=0, keepdims=True)
    e2 = jnp.exp(m2 - m1)
    w1 = p_grp / (1.0 + e2)
    w2 = p_grp * e2 / (1.0 + e2)
    eid0 = grp * float(EXPERTS_PER_GROUP) + i1
    eid1 = grp * float(EXPERTS_PER_GROUP) + i2
    eid_ref[0:1, :] = eid0.astype(I32)
    eid_ref[1:2, :] = eid1.astype(I32)
    wts_ref[0:1, :] = w1
    wts_ref[1:2, :] = w2
    yield

    r32 = lax.broadcasted_iota(I32, (N_EXPERTS, tm), 0).astype(F32)
    oh0 = jnp.where(r32 == eid0, live, 0.0)
    oh1 = jnp.where(r32 == eid1, live, 0.0)
    src = lax.broadcasted_iota(I32, (tm, tm), 0)
    dst = lax.broadcasted_iota(I32, (tm, tm), 1)
    before = jnp.where(src < dst, 1.0, 0.0).astype(BF16)
    ex = _dot(jnp.concatenate([oh0, oh1], axis=0).astype(BF16), before)
    yield
    base = ex[0:N_EXPERTS] + ex[N_EXPERTS:] + carry_ref[...]
    rank_ref[0:1, :] = jnp.sum(oh0 * base, axis=0, keepdims=True).astype(I32)
    rank_ref[1:2, :] = jnp.sum(oh1 * base, axis=0, keepdims=True).astype(I32)
    carry = carry_ref[...] + jnp.sum(oh0 + oh1, axis=1, keepdims=True)
    carry_ref[...] = carry
    cnt_ref[...] = jnp.broadcast_to(carry, cnt_ref.shape)


def _conv_shift_matrix():
    r = np.arange(8 * CONV_SPAN)
    src = r // CONV_SPAN + r % CONV_SPAN - CONV_LEAD
    sel = (src[:, None] == np.arange(GRID_W)[None, :]).astype(np.float32)
    return jnp.asarray(np.concatenate([sel, sel], axis=1), dtype=BF16)


def _mix(o_f, o_b, z, x, g1, sc2, sh2, hg, dwk, dwb, lng, lnb, wo, n2g, wr, br, b0, bsz):
    _, l, d = x.shape
    shm = _conv_shift_matrix()
    w = o_f.shape[2]
    cw = dwk.shape[1]
    tm = min(MIX_TM, l)
    nt = l // tm
    n = bsz * l
    tiles = bsz * nt
    mixed = lambda s: jnp.minimum(s, tiles - 1)
    src = lambda c: (lambda s: (b0 + mixed(s) // nt, mixed(s) % nt, c))
    tok = lambda s: (mixed(s) // nt, mixed(s) % nt, 0)
    vec = lambda s: (0, 0)
    per_b = lambda s: (b0 + mixed(s) // nt, 0, 0)
    flat = lambda s: (0, jnp.maximum(s - 1, 0))
    kern = functools.partial(_mix_kernel, tm=tm)
    return pl.pallas_call(
        kern,
        grid=(tiles + 1,),
        in_specs=[pl.BlockSpec((1, tm, w), src(0)),
                  pl.BlockSpec((1, tm, w), src(0)),
                  pl.BlockSpec((1, tm, w), src(2)),
                  pl.BlockSpec((1, tm, cw), src(3)),
                  pl.BlockSpec((1, tm, cw), src(4)),
                  pl.BlockSpec((1, tm, d), src(0)),
                  pl.BlockSpec((1, 1, d), per_b),
                  pl.BlockSpec((1, 1, d), per_b),
                  pl.BlockSpec((1, 1, d), per_b),
                  pl.BlockSpec((1, w), vec),
                  pl.BlockSpec((CONV_K, cw), vec),
                  pl.BlockSpec((1, cw), vec),
                  pl.BlockSpec((1, cw), vec),
                  pl.BlockSpec((1, cw), vec),
                  pl.BlockSpec((w + cw, d), vec),
                  pl.BlockSpec((1, d), vec),
                  pl.BlockSpec((ROUTER_ROWS, d), vec),
                  pl.BlockSpec((ROUTER_ROWS, 1), vec),
                  pl.BlockSpec(shm.shape, vec)],
        out_specs=[pl.BlockSpec((1, tm, d), tok),
                   pl.BlockSpec((1, tm, d // 2), tok),
                   pl.BlockSpec((TOP_K, tm), flat),
                   pl.BlockSpec((TOP_K, tm), flat),
                   pl.BlockSpec((TOP_K, tm), flat),
                   pl.BlockSpec((N_EXPERTS, 128), vec)],
        out_shape=[jax.ShapeDtypeStruct((bsz, l, d), F32),
                   jax.ShapeDtypeStruct((bsz, l, d // 2), F32),
                   jax.ShapeDtypeStruct((TOP_K, n), I32),
                   jax.ShapeDtypeStruct((TOP_K, n), F32),
                   jax.ShapeDtypeStruct((TOP_K, n), I32),
                   jax.ShapeDtypeStruct((N_EXPERTS, 128), F32)],
        scratch_shapes=[pltpu.VMEM((tm // GRID_W, 8, CONV_SPAN, cw), F32),
                        pltpu.VMEM((tm, cw), F32),
                        pltpu.VMEM((N_EXPERTS, 1), F32),
                        pltpu.VMEM((tm, d), BF16),
                        pltpu.VMEM((tm, d), BF16)],
        compiler_params=_cparams(("arbitrary",)),
        name="mix",
    )(o_f, o_b, z, z, z, x, g1, sc2, sh2, hg, dwk, dwb, lng, lnb, wo, n2g, wr, br, shm)


def _pos_kernel(ps_ref, eid_ref, rank_ref, pos_ref):
    eid = eid_ref[...]
    acc = rank_ref[...]
    for e in range(N_EXPERTS):
        acc = acc + jnp.where(eid == e, ps_ref[e], 0)
    pos_ref[...] = acc


def _positions(pstarts, eid, rank):
    shape = eid.shape
    eid2 = eid.reshape(-1, 128)
    full = pl.BlockSpec(eid2.shape, lambda i: (0, 0))
    pos = pl.pallas_call(
        _pos_kernel,
        grid=(1,),
        in_specs=[pl.BlockSpec(memory_space=pltpu.SMEM), full, full],
        out_specs=full,
        out_shape=jax.ShapeDtypeStruct(eid2.shape, I32),
        name="positions",
    )(pstarts, eid2, rank.reshape(-1, 128))
    return pos.reshape(shape)


def _sc_mesh():
    return plsc.VectorSubcoreMesh(core_axis_name="c", subcore_axis_name="s",
                                  num_cores=SC_CORES, num_subcores=SC_SUBCORES)


def _sc_worker():
    return lax.axis_index("s") * SC_CORES + lax.axis_index("c")


def _sc_dispatch(h2p, pos4, padidx, zero_rows, n_rows):
    n, dw = h2p.shape
    _, workers, chunks, _ = pos4.shape
    pad_chunks = padidx.shape[1]
    per_worker = n // workers

    def body(h_hbm, pos_hbm, pad_hbm, zero_hbm, out_hbm, idx_v, pad_v, rows_v):
        wid = _sc_worker()
        base = wid * per_worker
        for k in range(TOP_K):
            pltpu.sync_copy(pos_hbm.at[k, wid], idx_v.at[k])
        pltpu.sync_copy(pad_hbm.at[wid], pad_v)
        pltpu.sync_copy(zero_hbm, rows_v)
        for j in range(pad_chunks):
            pltpu.sync_copy(rows_v, out_hbm.at[pad_v.at[j]])
        for j in range(chunks):
            pltpu.sync_copy(h_hbm.at[pl.ds(base + j * SC_W, SC_W)], rows_v)
            for k in range(TOP_K):
                pltpu.sync_copy(rows_v, out_hbm.at[idx_v.at[k, j]])

    return pl.kernel(
        body,
        out_type=jax.ShapeDtypeStruct((n_rows, dw), F32),
        mesh=_sc_mesh(),
        scratch_types=[pltpu.VMEM((TOP_K, chunks, SC_W), I32),
                       pltpu.VMEM((pad_chunks, SC_W), I32),
                       pltpu.VMEM((SC_W, dw), F32)],
        name="sc_dispatch",
    )(h2p, pos4, padidx, zero_rows)


def _sc_combine_gather(ybuf, pos4):
    _, dw = ybuf.shape
    _, workers, chunks, _ = pos4.shape
    per_worker = chunks * SC_W

    def body(y_hbm, pos_hbm, out_hbm, idx_v, rows_v):
        wid = _sc_worker()
        base = wid * per_worker
        for k in range(TOP_K):
            pltpu.sync_copy(pos_hbm.at[k, wid], idx_v.at[k])
        for k in range(TOP_K):
            for j in range(chunks):
                pltpu.sync_copy(y_hbm.at[idx_v.at[k, j]], rows_v)
                pltpu.sync_copy(rows_v, out_hbm.at[k, pl.ds(base + j * SC_W, SC_W)])

    return pl.kernel(
        body,
        out_type=jax.ShapeDtypeStruct((TOP_K, workers * per_worker, dw), F32),
        mesh=_sc_mesh(),
        scratch_types=[pltpu.VMEM((TOP_K, chunks, SC_W), I32),
                       pltpu.VMEM((SC_W, dw), F32)],
        name="sc_combine",
    )(ybuf, pos4)


def _ffn_kernel(blk_e_ref, nblk_ref, x_ref, wg_ref, wu_ref, wd_ref, y_ref):
    del blk_e_ref
    used = pl.program_id(0) < nblk_ref[0]
    half = x_ref.shape[1]

    @pl.when(used)
    def _():
        x_lo, x_hi = _unpack_halves(x_ref[...])
        bf = lambda wv: wv.astype(BF16)
        g = _dot(x_lo, bf(wg_ref[0, 0:half, :])) + _dot(x_hi, bf(wg_ref[0, half:, :]))
        u = _dot(x_lo, bf(wu_ref[0, 0:half, :])) + _dot(x_hi, bf(wu_ref[0, half:, :]))
        y = _dot((_silu(g) * u).astype(BF16), bf(wd_ref[0]))
        y_ref[...] = _pack_halves(y.astype(BF16).astype(F32))

    @pl.when(jnp.logical_not(used))
    def _():
        y_ref[...] = jnp.zeros_like(y_ref)


def _ffn(blk_e, nblk, xbuf, wg, wu, wd, tb):
    _, dw = xbuf.shape
    _, d, de = wg.shape
    n_blk = blk_e.shape[0]
    x_blk = lambda i, be, nb: (jnp.minimum(i, nb[0] - 1), 0)
    grid_spec = pltpu.PrefetchScalarGridSpec(
        num_scalar_prefetch=2,
        grid=(n_blk,),
        in_specs=[pl.BlockSpec((tb, dw), x_blk),
                  pl.BlockSpec((1, d, de), lambda i, be, nb: (be[i], 0, 0)),
                  pl.BlockSpec((1, d, de), lambda i, be, nb: (be[i], 0, 0)),
                  pl.BlockSpec((1, de, d), lambda i, be, nb: (be[i], 0, 0))],
        out_specs=pl.BlockSpec((tb, dw), lambda i, be, nb: (i, 0)),
    )
    return pl.pallas_call(
        _ffn_kernel,
        grid_spec=grid_spec,
        out_shape=jax.ShapeDtypeStruct((n_blk * tb, dw), F32),
        compiler_params=_cparams(("arbitrary",)),
        name="ffn",
    )(blk_e, nblk, xbuf, wg, wu, wd)


def _final_kernel(yg_ref, xm_ref, wt_ref, g2_ref, fg_ref, *out_refs):
    o_ref = out_refs[-1]
    wt = wt_ref[...]
    y = None
    for k in range(TOP_K):
        lo, hi = _unpack_halves(yg_ref[k])
        yk = jnp.concatenate([lo, hi], axis=-1).astype(F32) * wt[:, k:k + 1]
        y = yk if y is None else y + yk
    x = xm_ref[0] + g2_ref[0] * y
    ms = jnp.mean(x * x, axis=-1, keepdims=True)
    o_ref[0] = x * lax.rsqrt(ms + EPS) * fg_ref[...]


def _final(yg, xm, wt, g2, fg, out_prev, b0, total):
    bsz, l, d = xm.shape
    dw = yg.shape[2]
    tm = min(MOE_TM, l)
    nt = l // tm
    in_specs = [pl.BlockSpec((TOP_K, tm, dw), lambda b, i: (0, b * nt + i, 0)),
                pl.BlockSpec((1, tm, d), lambda b, i: (b, i, 0)),
                pl.BlockSpec((tm, TOP_K), lambda b, i: (b * nt + i, 0)),
                pl.BlockSpec((1, 1, d), lambda b, i: (b0 + b, 0, 0)),
                pl.BlockSpec((1, d), lambda b, i: (0, 0))]
    args = [yg, xm, wt, g2, fg]
    aliases = {}
    if out_prev is not None:
        in_specs.append(pl.BlockSpec(memory_space=pl.ANY))
        args.append(out_prev)
        aliases = {5: 0}
    return pl.pallas_call(
        _final_kernel,
        grid=(bsz, nt),
        in_specs=in_specs,
        out_specs=pl.BlockSpec((1, tm, d), lambda b, i: (b0 + b, i, 0)),
        out_shape=jax.ShapeDtypeStruct((total, l, d), F32),
        input_output_aliases=aliases,
        compiler_params=_cparams(("arbitrary", "arbitrary")),
        name="final",
    )(*args)


def _layer(x, ctx, mod, mod_c, norm1_g, w_in, lb_logits, hgrn_norm_g, dw_kernel, dw_bias, conv_ln_g,
           conv_ln_b, w_out, norm2_g, wr, br, w_gate, w_up, w_down, final_norm_g):
    bsz, l, d = x.shape
    w = lb_logits.shape[2]
    heads = w // HEAD_DIM
    row = lambda v: v.reshape(1, -1)
    per_b = lambda v: v.reshape(bsz, 1, d)
    sh1, sc1, g1, sh2, sc2, g2 = [per_b(mod[:, j]) for j in range(N_MOD)]
    csh1 = jnp.broadcast_to(mod_c[0].reshape(1, 1, d), (bsz, 1, d))
    csc1 = jnp.broadcast_to(mod_c[1].reshape(1, 1, d), (bsz, 1, d))

    w_in_b = w_in.astype(BF16)
    zf, zr = _inproj(x, row(norm1_g), sc1, sh1, w_in_b, 2 * w)
    zcf, zcr = _inproj(ctx, row(norm1_g), csc1, csh1, w_in_b[:, :3 * w], 2 * w)
    o_f, o_b = _hgrn(lb_logits, zcf, zcr, zf, zr)
    w_out_b = w_out.astype(BF16)

    groups = MOE_GROUPS if bsz % MOE_GROUPS == 0 else 1
    out = None
    for gi in range(groups):
        b0, gb = gi * (bsz // groups), bsz // groups
        mixed = _mix(o_f, o_b, zr, x, g1, sc2, sh2, row(jnp.tile(hgrn_norm_g, heads)), dw_kernel,
                     row(dw_bias), row(conv_ln_g), row(conv_ln_b), w_out_b, row(norm2_g), wr, br, b0, gb)
        out = _moe_group(mixed, g2, w_gate, w_up, w_down, row(final_norm_g), out, b0, bsz)
    return out


def _moe_group(mixed, g2, w_gate, w_up, w_down, final_g, out_prev, b0, total):
    xm, h2, eid, wts, rank, cnt = mixed
    gb, l, d = xm.shape
    n = gb * l

    tb = FFN_TB
    n_blk = -(-(n * TOP_K) // tb) + N_EXPERTS
    counts = cnt[:, 0].astype(I32)
    pcounts = (counts + tb - 1) // tb * tb
    pends = jnp.cumsum(pcounts)
    pstarts = pends - pcounts
    pos = _positions(pstarts.astype(I32), eid, rank)
    blk_start = jnp.arange(n_blk, dtype=I32) * tb
    blk_e = jnp.minimum(jnp.sum((pends[None, :] <= blk_start[:, None]).astype(I32), axis=1), N_EXPERTS - 1)
    nblk_used = (pends[-1:] // tb).astype(I32)

    workers = SC_CORES * SC_SUBCORES
    pos4 = pos.reshape(TOP_K, workers, n // (workers * SC_W), SC_W)
    lane = jnp.arange(tb, dtype=I32)[None, :]
    pad_rows = jnp.where(lane < (pcounts - counts)[:, None], (pstarts + counts)[:, None] + lane,
                         n_blk * tb + lane)
    padidx = pad_rows.astype(I32).reshape(workers, N_EXPERTS // workers * tb // SC_W, SC_W)
    xbuf = _sc_dispatch(h2.reshape(n, d // 2), pos4, padidx, jnp.zeros((SC_W, d // 2), F32), (n_blk + 1) * tb)
    ybuf = _ffn(blk_e, nblk_used, xbuf, w_gate, w_up, w_down, tb)
    yg = _sc_combine_gather(ybuf, pos4)
    return _final(yg, xm, wts.T, g2, final_g, out_prev, b0, total)


def kernel(x, c, ctx, c_ctx, w_ada, b_ada, norm1_g, w_in, lb_logits, hgrn_norm_g, dw_kernel, dw_bias,
           conv_ln_g, conv_ln_b, w_out, norm2_g, router_group_w, router_group_b, router_expert_w,
           router_expert_b, w_expert_gate, w_expert_up, w_expert_down, final_norm_g):
    depth = w_ada.shape[0]
    assert depth == 1, "context tokens are only updated between layers; a single layer is implemented"
    bsz, l, d = x.shape
    l0 = 0
    rows = -(-(bsz + 1) // 8) * 8
    cc = jnp.zeros((rows, d), F32).at[:bsz].set(c).at[bsz].set(c_ctx)
    mod_all = _ada(cc, w_ada[l0], b_ada[l0].reshape(1, -1))
    mod = mod_all[:bsz].reshape(bsz, N_MOD, d)
    mod_c = mod_all[bsz].reshape(N_MOD, d)
    pad8 = jnp.zeros((d, 8 - N_GROUPS), F32)
    wr = jnp.concatenate([router_group_w[l0], pad8, router_expert_w[l0], jnp.zeros((d, 8), F32)], axis=1).T
    br = jnp.concatenate([router_group_b[l0], jnp.zeros((8 - N_GROUPS,), F32), router_expert_b[l0],
                          jnp.zeros((8,), F32)]).reshape(ROUTER_ROWS, 1)
    return _layer(x, ctx, mod, mod_c, norm1_g[l0], w_in[l0], lb_logits, hgrn_norm_g[l0], dw_kernel[l0], dw_bias[l0], conv_ln_g[l0], conv_ln_b[l0], w_out[l0],
                  norm2_g[l0], wr, br, w_expert_gate[l0], w_expert_up[l0], w_expert_down[l0], final_norm_g)
```

```python
import functools

import numpy as np

import jax
import jax.numpy as jnp
from jax import lax
from jax.experimental import pallas as pl
from jax.experimental.pallas import tpu as pltpu
from jax.experimental.pallas import tpu_sc as plsc

F32 = jnp.float32
BF16 = jnp.bfloat16
I32 = jnp.int32
U32 = jnp.uint32

EPS = 1e-6
HEAD_DIM = 128
CHUNK = 64
GRID_W = 64
CONV_K = 31
CONV_PAD = CONV_K // 2
CONV_LANES = 256
CONV_LEAD = 16
CONV_SPAN = GRID_W + 24
N_GROUPS = 4
EXPERTS_PER_GROUP = 8
N_EXPERTS = N_GROUPS * EXPERTS_PER_GROUP
TOP_K = 2
N_MOD = 6
ROUTER_ROWS = 48

ADA_TN = 1024
INPROJ_TM = 1024
HGRN_TT = 512
MIX_TM = 256
MOE_TM = 512
FFN_TB = 1024
VMEM_LIMIT = 48 * 1024 * 1024
SC_CORES = 2
SC_SUBCORES = 16
SC_W = 128
MOE_GROUPS = 1


def _cparams(sem):
    return pltpu.CompilerParams(dimension_semantics=sem, vmem_limit_bytes=VMEM_LIMIT)


def _silu(v):
    return v * jax.nn.sigmoid(v)


def _dot(a, b):
    return jnp.dot(a, b, preferred_element_type=F32)


def _dot_nt(a, b):
    return lax.dot_general(a, b, (((1,), (1,)), ((), ())), preferred_element_type=F32)


def _dot_tn(a, b):
    return lax.dot_general(a, b, (((0,), (0,)), ((), ())), preferred_element_type=F32)


def _pack_halves(vr):
    m = vr.shape[1] // 2
    bits = lax.bitcast_convert_type(vr, U32)
    return lax.bitcast_convert_type((bits[:, :m] >> 16) | bits[:, m:], F32)


def _unpack_halves(p):
    u = lax.bitcast_convert_type(p, U32)
    lo = lax.bitcast_convert_type(u << 16, F32)
    hi = lax.bitcast_convert_type(u & jnp.uint32(0xFFFF0000), F32)
    return lo.astype(BF16), hi.astype(BF16)


def _split2(a):
    hi = a.astype(BF16)
    lo = (a - hi.astype(F32)).astype(BF16)
    return hi, lo


def _ada_kernel(c_ref, w_ref, b_ref, o_ref):
    s = _silu(c_ref[...]).astype(BF16)
    o_ref[...] = _dot(s, w_ref[...].astype(BF16)) + b_ref[...]


def _ada(cc, w, b):
    r, d = cc.shape
    nc = w.shape[1]
    return pl.pallas_call(
        _ada_kernel,
        grid=(nc // ADA_TN,),
        in_specs=[pl.BlockSpec((r, d), lambda j: (0, 0)),
                  pl.BlockSpec((d, ADA_TN), lambda j: (0, j)),
                  pl.BlockSpec((1, ADA_TN), lambda j: (0, j))],
        out_specs=pl.BlockSpec((r, ADA_TN), lambda j: (0, j)),
        out_shape=jax.ShapeDtypeStruct((r, nc), F32),
        compiler_params=_cparams(("arbitrary",)),
        name="ada",
    )(cc, w, b)


def _inproj_kernel(x_ref, g_ref, sc_ref, sh_ref, w_ref, zf_ref, zr_ref, *, activate):
    x = x_ref[0]
    ms = jnp.mean(x * x, axis=-1, keepdims=True)
    gs = g_ref[...] * (1.0 + sc_ref[0])
    h = x * lax.rsqrt(ms + EPS) * gs + sh_ref[0]
    z = _dot(h.astype(BF16), w_ref[...])
    nf = zf_ref.shape[2]
    zf_ref[0] = z[:, :nf]
    if activate:
        wd = nf // 2
        col = lambda j: z[:, nf + j * wd:nf + (j + 1) * wd]
        zr_ref[0, :, 0:wd] = col(0).astype(BF16)
        zr_ref[0, :, wd:2 * wd] = _silu(col(1)).astype(BF16)
        zr_ref[0, :, 2 * wd:3 * wd] = _silu(col(2)).astype(BF16)
        zr_ref[0, :, 3 * wd:] = (col(3) * jax.nn.sigmoid(col(4))).astype(BF16)
    else:
        zr_ref[0] = z[:, nf:].astype(BF16)


def _inproj(x, g, sc, sh, w, nf, activate):
    bsz, l, d = x.shape
    nc = w.shape[1]
    tm = min(INPROJ_TM, l)
    nr = nc - nf - (nf // 2 if activate else 0)
    return pl.pallas_call(
        functools.partial(_inproj_kernel, activate=activate),
        grid=(bsz, l // tm),
        in_specs=[pl.BlockSpec((1, tm, d), lambda b, i: (b, i, 0)),
                  pl.BlockSpec((1, d), lambda b, i: (0, 0)),
                  pl.BlockSpec((1, 1, d), lambda b, i: (b, 0, 0)),
                  pl.BlockSpec((1, 1, d), lambda b, i: (b, 0, 0)),
                  pl.BlockSpec((d, nc), lambda b, i: (0, 0))],
        out_specs=[pl.BlockSpec((1, tm, nf), lambda b, i: (b, i, 0)),
                   pl.BlockSpec((1, tm, nr), lambda b, i: (b, i, 0))],
        out_shape=[jax.ShapeDtypeStruct((bsz, l, nf), F32),
                   jax.ShapeDtypeStruct((bsz, l, nr), BF16)],
        compiler_params=_cparams(("arbitrary", "arbitrary")),
        name="inproj",
    )(x, g, sc, sh, w)


class _HgrnDirection:
    def __init__(self, lbl, st_ref, reverse):
        self.st_ref, self.reverse = st_ref, reverse
        e = jnp.exp(lbl - jnp.max(lbl, axis=0, keepdims=True))
        self.lb = e[0:1] / jnp.sum(e, axis=0, keepdims=True)
        row = lax.broadcasted_iota(I32, (CHUNK, CHUNK), 0)
        col = lax.broadcasted_iota(I32, (CHUNK, CHUNK), 1)
        self.tri = (col >= row) if reverse else (col <= row)
        self.tm = jnp.where(self.tri, 1.0, 0.0).astype(BF16)

    def tile(self, f_ref, v_ref, q_ref=None, o_ref=None):
        heads = self.st_ref.shape[0]
        n_chunks = f_ref.shape[1] // CHUNK
        crow = lambda c: slice(c * CHUNK, (c + 1) * CHUNK)
        hsl = lambda h: slice(h * HEAD_DIM, (h + 1) * HEAD_DIM)
        f = self.lb + (1.0 - self.lb) * jax.nn.sigmoid(f_ref[0])
        k = 1.0 - f
        hi, lo = _split2(jnp.log(f))
        cums = [_dot(self.tm, hi[crow(c)]) + _dot(self.tm, lo[crow(c)])
                for c in range(n_chunks)]
        lasts = [cm[0:1] if self.reverse else cm[CHUNK - 1:CHUNK] for cm in cums]
        cum = jnp.concatenate(cums, axis=0)
        dec = [jnp.exp(la) for la in lasts]
        k_intra = k * jnp.exp(-cum)
        ks = (k_intra * jnp.concatenate([jnp.broadcast_to(dc, (CHUNK, dc.shape[1])) for dc in dec], axis=0)
              ).astype(BF16)
        vb = v_ref[0].astype(BF16)
        kv = [[_dot_tn(vb[crow(c), hsl(h)], ks[crow(c), hsl(h)]) for h in range(heads)] for c in range(n_chunks)]
        emit = o_ref is not None
        if emit:
            qd = (q_ref[0].astype(F32) * jnp.exp(cum)).astype(BF16)
            ki = k_intra.astype(BF16)
            sc = [[jnp.where(self.tri, _dot_nt(qd[crow(c), hsl(h)], ki[crow(c), hsl(h)]), 0.0).astype(BF16)
                   for h in range(heads)] for c in range(n_chunks)]
            intra = [[_dot(sc[c][h], vb[crow(c), hsl(h)]) for h in range(heads)] for c in range(n_chunks)]
        state = [self.st_ref[h] for h in range(heads)]
        for c in (range(n_chunks - 1, -1, -1) if self.reverse else range(n_chunks)):
            for h in range(heads):
                if emit:
                    inter = _dot_nt(qd[crow(c), hsl(h)], state[h].astype(BF16))
                    o_ref[0, crow(c), hsl(h)] = (intra[c][h] + inter).astype(o_ref.dtype)
                state[h] = state[h] * dec[c][:, hsl(h)] + kv[c][h]
        for h in range(heads):
            self.st_ref[h] = state[h]


def _hgrn_kernel(lbl_ref, cff_ref, cfb_ref, cv_ref, ff_ref, vf_ref, qf_ref, fb_ref, vb_ref, qb_ref,
                 of_ref, ob_ref, stf_ref, stb_ref):
    fwd = _HgrnDirection(lbl_ref[0], stf_ref, reverse=False)
    bwd = _HgrnDirection(lbl_ref[1], stb_ref, reverse=True)

    @pl.when(pl.program_id(1) == 0)
    def _():
        stf_ref[...] = jnp.zeros_like(stf_ref)
        stb_ref[...] = jnp.zeros_like(stb_ref)
        fwd.tile(cff_ref, cv_ref)
        bwd.tile(cfb_ref, cv_ref)

    fwd.tile(ff_ref, vf_ref, qf_ref, of_ref)
    bwd.tile(fb_ref, vb_ref, qb_ref, ob_ref)


def _hgrn(lbl, zcf, zcr, zf, zr):
    bsz, l, _ = zf.shape
    n_ctx_tok = zcf.shape[1]
    w = lbl.shape[2]
    heads = w // HEAD_DIM
    tt = min(HGRN_TT, l)
    nt = l // tt
    ctx_col = lambda c: pl.BlockSpec((1, n_ctx_tok, w), lambda b, i: (b, 0, c))
    fwd_col = lambda c: pl.BlockSpec((1, tt, w), lambda b, i: (b, i, c))
    bwd_col = lambda c: pl.BlockSpec((1, tt, w), lambda b, i: (b, nt - 1 - i, c))
    state = pltpu.VMEM((heads, HEAD_DIM, HEAD_DIM), F32)
    return pl.pallas_call(
        _hgrn_kernel,
        grid=(bsz, nt),
        in_specs=[pl.BlockSpec(lbl.shape, lambda b, i: (0, 0, 0)),
                  ctx_col(0), ctx_col(1), ctx_col(0),
                  fwd_col(0), fwd_col(0), fwd_col(1),
                  bwd_col(1), bwd_col(0), bwd_col(1)],
        out_specs=[fwd_col(0), bwd_col(0)],
        out_shape=[jax.ShapeDtypeStruct((bsz, l, w), BF16), jax.ShapeDtypeStruct((bsz, l, w), BF16)],
        scratch_shapes=[state, state],
        compiler_params=_cparams(("arbitrary", "arbitrary")),
        name="hgrn",
    )(lbl, zcf, zcf, zcr, zf, zr, zr, zf, zr, zr)


def _mix_kernel(of_ref, ob_ref, zg_ref, zu_ref, x_ref, g1_ref, sc2_ref, sh2_ref,
                hg_ref, dwk_ref, dwb_ref, lng_ref, lnb_ref, wo_ref, n2g_ref, wr_ref, br_ref, shm_ref,
                xm_ref, h2_ref, eid_ref, wts_ref, rank_ref, cnt_ref,
                shf_ref, cv_ref, carry_ref, hh_ref, hl_ref, *, tm):
    w = of_ref.shape[2]
    heads = w // HEAD_DIM
    step = pl.program_id(0)

    @pl.when(step == 0)
    def _():
        carry_ref[...] = jnp.zeros_like(carry_ref)
        hh_ref[...] = jnp.zeros_like(hh_ref)
        hl_ref[...] = jnp.zeros_like(hl_ref)

    router = _route(hh_ref[...], hl_ref[...], wr_ref, br_ref, carry_ref, jnp.where(step > 0, 1.0, 0.0),
                    eid_ref, wts_ref, rank_ref, cnt_ref)
    next(router)

    o = of_ref[0].astype(F32) + ob_ref[0].astype(F32)
    parts = []
    for h in range(heads):
        oh = o[:, h * HEAD_DIM:(h + 1) * HEAD_DIM]
        parts.append(oh * lax.rsqrt(jnp.mean(oh * oh, axis=-1, keepdims=True) + EPS))
    o_mix = jnp.concatenate(parts, axis=-1) * hg_ref[...] * zg_ref[0].astype(F32)

    u = zu_ref[0].astype(F32)
    span = shf_ref.shape[2]
    u_hi, u_lo = _split2(u)
    for s in range(tm // GRID_W):
        rows = slice(s * GRID_W, (s + 1) * GRID_W)
        shifted = _dot(shm_ref[...], jnp.concatenate([u_hi[rows], u_lo[rows]], axis=0))
        for b in range(8):
            shf_ref[s, b] = shifted[b * span:(b + 1) * span]
    next(router)
    lanes = CONV_LANES
    for s in range(tm // GRID_W):
        for c0 in range(0, u.shape[1], lanes):
            cs = slice(c0, c0 + lanes)
            acc = jnp.zeros((GRID_W // 8, 8, lanes), F32) + dwb_ref[:, cs][None]
            for k in range(CONV_K):
                a, b = divmod(CONV_LEAD - CONV_PAD + k, 8)
                tap = shf_ref[s, b, 8 * a:8 * a + GRID_W, cs].reshape(GRID_W // 8, 8, lanes)
                acc = acc + tap * dwk_ref[k, :, cs][None]
            cv_ref[s * GRID_W:(s + 1) * GRID_W, cs] = acc.reshape(GRID_W, lanes)
        if s == 0:
            next(router)
    cv = cv_ref[...]
    mu = jnp.mean(cv, axis=-1, keepdims=True)
    cen = cv - mu
    var = jnp.mean(cen * cen, axis=-1, keepdims=True)
    c_mix = _silu(cen * lax.rsqrt(var + EPS) * lng_ref[...] + lnb_ref[...])
    next(router, None)

    mix = _dot(o_mix.astype(BF16), wo_ref[0:w, :]) + _dot(c_mix.astype(BF16), wo_ref[w:, :])
    xm = x_ref[0] + g1_ref[0] * mix
    xm_ref[0] = xm

    ms = jnp.mean(xm * xm, axis=-1, keepdims=True)
    h2 = xm * lax.rsqrt(ms + EPS) * (n2g_ref[...] * (1.0 + sc2_ref[0])) + sh2_ref[0]
    hh = h2.astype(BF16)
    h2_rounded = hh.astype(F32)
    h2_ref[0] = _pack_halves(h2_rounded)
    hh_ref[...] = hh
    hl_ref[...] = (h2 - h2_rounded).astype(BF16)


def _route(hh, hl, wr_ref, br_ref, carry_ref, live, eid_ref, wts_ref, rank_ref, cnt_ref):
    tm = hh.shape[0]
    wr = wr_ref[...]
    wh = wr.astype(BF16)
    wl = (wr - wh.astype(F32)).astype(BF16)
    lg = _dot_nt(wh, hh) + _dot_nt(wh, hl) + _dot_nt(wl, hh) + br_ref[...]
    yield

    neg = -jnp.inf
    r8 = lax.broadcasted_iota(I32, (EXPERTS_PER_GROUP, tm), 0).astype(F32)
    gl = jnp.where(r8 < N_GROUPS, lg[0:8], neg)
    gmax = jnp.max(gl, axis=0, keepdims=True)
    grp = jnp.min(jnp.where(gl == gmax, r8, 8.0), axis=0, keepdims=True)
    p_grp = 1.0 / jnp.sum(jnp.exp(gl - gmax), axis=0, keepdims=True)
    es = lg[8:16]
    for g in range(1, N_GROUPS):
        es = jnp.where(grp == float(g), lg[8 + 8 * g:16 + 8 * g], es)
    m1 = jnp.max(es, axis=0, keepdims=True)
    i1 = jnp.min(jnp.where(es == m1, r8, 8.0), axis=0, keepdims=True)
    es2 = jnp.where(r8 == i1, neg, es)
    m2 = jnp.max(es2, axis=0, keepdims=True)
    i2 = jnp.min(jnp.where(es2 == m2, r8, 8.0), axis=0, keepdims=True)
    e2 = jnp.exp(m2 - m1)
    w1 = p_grp / (1.0 + e2)
    w2 = p_grp * e2 / (1.0 + e2)
    eid0 = grp * float(EXPERTS_PER_GROUP) + i1
    eid1 = grp * float(EXPERTS_PER_GROUP) + i2
    eid_ref[0:1, :] = eid0.astype(I32)
    eid_ref[1:2, :] = eid1.astype(I32)
    wts_ref[0:1, :] = w1
    wts_ref[1:2, :] = w2
    yield

    r32 = lax.broadcasted_iota(I32, (N_EXPERTS, tm), 0).astype(F32)
    oh0 = jnp.where(r32 == eid0, live, 0.0)
    oh1 = jnp.where(r32 == eid1, live, 0.0)
    src = lax.broadcasted_iota(I32, (tm, tm), 0)
    dst = lax.broadcasted_iota(I32, (tm, tm), 1)
    before = jnp.where(src < dst, 1.0, 0.0).astype(BF16)
    ex = _dot(jnp.concatenate([oh0, oh1], axis=0).astype(BF16), before)
    yield
    base = ex[0:N_EXPERTS] + ex[N_EXPERTS:] + carry_ref[...]
    rank_ref[0:1, :] = jnp.sum(oh0 * base, axis=0, keepdims=True).astype(I32)
    rank_ref[1:2, :] = jnp.sum(oh1 * base, axis=0, keepdims=True).astype(I32)
    carry = carry_ref[...] + jnp.sum(oh0 + oh1, axis=1, keepdims=True)
    carry_ref[...] = carry
    cnt_ref[...] = jnp.broadcast_to(carry, cnt_ref.shape)


def _conv_shift_matrix():
    r = np.arange(8 * CONV_SPAN)
    src = r // CONV_SPAN + r % CONV_SPAN - CONV_LEAD
    sel = (src[:, None] == np.arange(GRID_W)[None, :]).astype(np.float32)
    return jnp.asarray(np.concatenate([sel, sel], axis=1), dtype=BF16)


def _mix(o_f, o_b, z, x, g1, sc2, sh2, hg, dwk, dwb, lng, lnb, wo, n2g, wr, br, b0, bsz):
    _, l, d = x.shape
    shm = _conv_shift_matrix()
    w = o_f.shape[2]
    cw = dwk.shape[-1]
    tm = min(MIX_TM, l)
    nt = l // tm
    n = bsz * l
    tiles = bsz * nt
    mixed = lambda s: jnp.minimum(s, tiles - 1)
    src = lambda c: (lambda s: (b0 + mixed(s) // nt, mixed(s) % nt, c))
    tok = lambda s: (mixed(s) // nt, mixed(s) % nt, 0)
    vec = lambda s: (0, 0)
    per_b = lambda s: (b0 + mixed(s) // nt, 0, 0)
    flat = lambda s: (0, jnp.maximum(s - 1, 0))
    kern = functools.partial(_mix_kernel, tm=tm)
    return pl.pallas_call(
        kern,
        grid=(tiles + 1,),
        in_specs=[pl.BlockSpec((1, tm, w), src(0)),
                  pl.BlockSpec((1, tm, w), src(0)),
                  pl.BlockSpec((1, tm, w), src(2)),
                  pl.BlockSpec((1, tm, cw), src(3)),
                  pl.BlockSpec((1, tm, d), src(0)),
                  pl.BlockSpec((1, 1, d), per_b),
                  pl.BlockSpec((1, 1, d), per_b),
                  pl.BlockSpec((1, 1, d), per_b),
                  pl.BlockSpec((1, w), vec),
                  pl.BlockSpec((CONV_K, 8, cw), lambda s: (0, 0, 0)),
                  pl.BlockSpec((8, cw), vec),
                  pl.BlockSpec((1, cw), vec),
                  pl.BlockSpec((1, cw), vec),
                  pl.BlockSpec((w + cw, d), vec),
                  pl.BlockSpec((1, d), vec),
                  pl.BlockSpec((ROUTER_ROWS, d), vec),
                  pl.BlockSpec((ROUTER_ROWS, 1), vec),
                  pl.BlockSpec(shm.shape, vec)],
        out_specs=[pl.BlockSpec((1, tm, d), tok),
                   pl.BlockSpec((1, tm, d // 2), tok),
                   pl.BlockSpec((TOP_K, tm), flat),
                   pl.BlockSpec((TOP_K, tm), flat),
                   pl.BlockSpec((TOP_K, tm), flat),
                   pl.BlockSpec((N_EXPERTS, 128), vec)],
        out_shape=[jax.ShapeDtypeStruct((bsz, l, d), F32),
                   jax.ShapeDtypeStruct((bsz, l, d // 2), F32),
                   jax.ShapeDtypeStruct((TOP_K, n), I32),
                   jax.ShapeDtypeStruct((TOP_K, n), F32),
                   jax.ShapeDtypeStruct((TOP_K, n), I32),
                   jax.ShapeDtypeStruct((N_EXPERTS, 128), F32)],
        scratch_shapes=[pltpu.VMEM((tm // GRID_W, 8, CONV_SPAN, cw), F32),
                        pltpu.VMEM((tm, cw), F32),
                        pltpu.VMEM((N_EXPERTS, 1), F32),
                        pltpu.VMEM((tm, d), BF16),
                        pltpu.VMEM((tm, d), BF16)],
        compiler_params=_cparams(("arbitrary",)),
        name="mix",
    )(o_f, o_b, z, z, x, g1, sc2, sh2, hg, dwk, dwb, lng, lnb, wo, n2g, wr, br, shm)


def _pos_kernel(ps_ref, eid_ref, rank_ref, pos_ref):
    eid = eid_ref[...]
    acc = rank_ref[...]
    for e in range(N_EXPERTS):
        acc = acc + jnp.where(eid == e, ps_ref[e], 0)
    pos_ref[...] = acc


def _positions(pstarts, eid, rank):
    shape = eid.shape
    eid2 = eid.reshape(-1, 128)
    full = pl.BlockSpec(eid2.shape, lambda i: (0, 0))
    pos = pl.pallas_call(
        _pos_kernel,
        grid=(1,),
        in_specs=[pl.BlockSpec(memory_space=pltpu.SMEM), full, full],
        out_specs=full,
        out_shape=jax.ShapeDtypeStruct(eid2.shape, I32),
        name="positions",
    )(pstarts, eid2, rank.reshape(-1, 128))
    return pos.reshape(shape)


def _sc_mesh():
    return plsc.VectorSubcoreMesh(core_axis_name="c", subcore_axis_name="s",
                                  num_cores=SC_CORES, num_subcores=SC_SUBCORES)


def _sc_worker():
    return lax.axis_index("s") * SC_CORES + lax.axis_index("c")


def _sc_dispatch(h2p, pos4, padidx, zero_rows, n_rows):
    n, dw = h2p.shape
    _, workers, chunks, _ = pos4.shape
    pad_chunks = padidx.shape[1]
    per_worker = n // workers

    def body(h_hbm, pos_hbm, pad_hbm, zero_hbm, out_hbm, idx_v, pad_v, rows_v):
        wid = _sc_worker()
        base = wid * per_worker
        for k in range(TOP_K):
            pltpu.sync_copy(pos_hbm.at[k, wid], idx_v.at[k])
        pltpu.sync_copy(pad_hbm.at[wid], pad_v)
        pltpu.sync_copy(zero_hbm, rows_v)
        for j in range(pad_chunks):
            pltpu.sync_copy(rows_v, out_hbm.at[pad_v.at[j]])
        for j in range(chunks):
            pltpu.sync_copy(h_hbm.at[pl.ds(base + j * SC_W, SC_W)], rows_v)
            for k in range(TOP_K):
                pltpu.sync_copy(rows_v, out_hbm.at[idx_v.at[k, j]])

    return pl.kernel(
        body,
        out_type=jax.ShapeDtypeStruct((n_rows, dw), F32),
        mesh=_sc_mesh(),
        scratch_types=[pltpu.VMEM((TOP_K, chunks, SC_W), I32),
                       pltpu.VMEM((pad_chunks, SC_W), I32),
                       pltpu.VMEM((SC_W, dw), F32)],
        name="sc_dispatch",
    )(h2p, pos4, padidx, zero_rows)


def _sc_combine_gather(ybuf, pos4):
    _, dw = ybuf.shape
    _, workers, chunks, _ = pos4.shape
    per_worker = chunks * SC_W

    def body(y_hbm, pos_hbm, out_hbm, idx_v, rows_v):
        wid = _sc_worker()
        base = wid * per_worker
        for k in range(TOP_K):
            pltpu.sync_copy(pos_hbm.at[k, wid], idx_v.at[k])
        for k in range(TOP_K):
            for j in range(chunks):
                pltpu.sync_copy(y_hbm.at[idx_v.at[k, j]], rows_v)
                pltpu.sync_copy(rows_v, out_hbm.at[k, pl.ds(base + j * SC_W, SC_W)])

    return pl.kernel(
        body,
        out_type=jax.ShapeDtypeStruct((TOP_K, workers * per_worker, dw), F32),
        mesh=_sc_mesh(),
        scratch_types=[pltpu.VMEM((TOP_K, chunks, SC_W), I32),
                       pltpu.VMEM((SC_W, dw), F32)],
        name="sc_combine",
    )(ybuf, pos4)


def _ffn_kernel(blk_e_ref, nblk_ref, x_ref, wg_ref, wu_ref, wd_ref, y_ref):
    del blk_e_ref
    used = pl.program_id(0) < nblk_ref[0]
    half = x_ref.shape[1]

    @pl.when(used)
    def _():
        x_lo, x_hi = _unpack_halves(x_ref[...])
        bf = lambda wv: wv.astype(BF16)
        g = _dot(x_lo, bf(wg_ref[0, 0:half, :])) + _dot(x_hi, bf(wg_ref[0, half:, :]))
        u = _dot(x_lo, bf(wu_ref[0, 0:half, :])) + _dot(x_hi, bf(wu_ref[0, half:, :]))
        y = _dot((_silu(g) * u).astype(BF16), bf(wd_ref[0]))
        y_ref[...] = _pack_halves(y.astype(BF16).astype(F32))

    @pl.when(jnp.logical_not(used))
    def _():
        y_ref[...] = jnp.zeros_like(y_ref)


def _ffn(blk_e, nblk, xbuf, wg, wu, wd, tb):
    _, dw = xbuf.shape
    _, d, de = wg.shape
    n_blk = blk_e.shape[0]
    x_blk = lambda i, be, nb: (jnp.minimum(i, nb[0] - 1), 0)
    grid_spec = pltpu.PrefetchScalarGridSpec(
        num_scalar_prefetch=2,
        grid=(n_blk,),
        in_specs=[pl.BlockSpec((tb, dw), x_blk),
                  pl.BlockSpec((1, d, de), lambda i, be, nb: (be[i], 0, 0)),
                  pl.BlockSpec((1, d, de), lambda i, be, nb: (be[i], 0, 0)),
                  pl.BlockSpec((1, de, d), lambda i, be, nb: (be[i], 0, 0))],
        out_specs=pl.BlockSpec((tb, dw), lambda i, be, nb: (i, 0)),
    )
    return pl.pallas_call(
        _ffn_kernel,
        grid_spec=grid_spec,
        out_shape=jax.ShapeDtypeStruct((n_blk * tb, dw), F32),
        compiler_params=_cparams(("arbitrary",)),
        name="ffn",
    )(blk_e, nblk, xbuf, wg, wu, wd)


def _final_kernel(yg_ref, xm_ref, wt_ref, g2_ref, fg_ref, *out_refs):
    o_ref = out_refs[-1]
    wt = wt_ref[...]
    y = None
    for k in range(TOP_K):
        lo, hi = _unpack_halves(yg_ref[k])
        yk = jnp.concatenate([lo, hi], axis=-1).astype(F32) * wt[:, k:k + 1]
        y = yk if y is None else y + yk
    x = xm_ref[0] + g2_ref[0] * y
    ms = jnp.mean(x * x, axis=-1, keepdims=True)
    o_ref[0] = x * lax.rsqrt(ms + EPS) * fg_ref[...]


def _final(yg, xm, wt, g2, fg, out_prev, b0, total):
    bsz, l, d = xm.shape
    dw = yg.shape[2]
    tm = min(MOE_TM, l)
    nt = l // tm
    in_specs = [pl.BlockSpec((TOP_K, tm, dw), lambda b, i: (0, b * nt + i, 0)),
                pl.BlockSpec((1, tm, d), lambda b, i: (b, i, 0)),
                pl.BlockSpec((tm, TOP_K), lambda b, i: (b * nt + i, 0)),
                pl.BlockSpec((1, 1, d), lambda b, i: (b0 + b, 0, 0)),
                pl.BlockSpec((1, d), lambda b, i: (0, 0))]
    args = [yg, xm, wt, g2, fg]
    aliases = {}
    if out_prev is not None:
        in_specs.append(pl.BlockSpec(memory_space=pl.ANY))
        args.append(out_prev)
        aliases = {5: 0}
    return pl.pallas_call(
        _final_kernel,
        grid=(bsz, nt),
        in_specs=in_specs,
        out_specs=pl.BlockSpec((1, tm, d), lambda b, i: (b0 + b, i, 0)),
        out_shape=jax.ShapeDtypeStruct((total, l, d), F32),
        input_output_aliases=aliases,
        compiler_params=_cparams(("arbitrary", "arbitrary")),
        name="final",
    )(*args)


def _layer(x, ctx, mod, mod_c, norm1_g, w_in, lb_logits, hgrn_norm_g, dw_kernel, dw_bias, conv_ln_g,
           conv_ln_b, w_out, norm2_g, wr, br, w_gate, w_up, w_down, final_norm_g):
    bsz, l, d = x.shape
    w = lb_logits.shape[2]
    heads = w // HEAD_DIM
    row = lambda v: v.reshape(1, -1)
    per_b = lambda v: v.reshape(bsz, 1, d)
    sh1, sc1, g1, sh2, sc2, g2 = [per_b(mod[:, j]) for j in range(N_MOD)]
    csh1 = jnp.broadcast_to(mod_c[0].reshape(1, 1, d), (bsz, 1, d))
    csc1 = jnp.broadcast_to(mod_c[1].reshape(1, 1, d), (bsz, 1, d))

    w_in_b = w_in.astype(BF16)
    zf, zr = _inproj(x, row(norm1_g), sc1, sh1, w_in_b, 2 * w, True)
    zcf, zcr = _inproj(ctx, row(norm1_g), csc1, csh1, w_in_b[:, :3 * w], 2 * w, False)
    o_f, o_b = _hgrn(lb_logits, zcf, zcr, zf, zr)
    w_out_b = w_out.astype(BF16)

    groups = MOE_GROUPS if bsz % MOE_GROUPS == 0 else 1
    out = None
    for gi in range(groups):
        b0, gb = gi * (bsz // groups), bsz // groups
        rep8 = lambda v: jnp.broadcast_to(v[..., None, :], v.shape[:-1] + (8, v.shape[-1]))
        mixed = _mix(o_f, o_b, zr, x, g1, sc2, sh2, row(jnp.tile(hgrn_norm_g, heads)), rep8(dw_kernel),
                     rep8(dw_bias), row(conv_ln_g), row(conv_ln_b), w_out_b, row(norm2_g), wr, br, b0, gb)
        out = _moe_group(mixed, g2, w_gate, w_up, w_down, row(final_norm_g), out, b0, bsz)
    return out


def _moe_group(mixed, g2, w_gate, w_up, w_down, final_g, out_prev, b0, total):
    xm, h2, eid, wts, rank, cnt = mixed
    gb, l, d = xm.shape
    n = gb * l

    tb = FFN_TB
    n_blk = -(-(n * TOP_K) // tb) + N_EXPERTS
    counts = cnt[:, 0].astype(I32)
    pcounts = (counts + tb - 1) // tb * tb
    pends = jnp.cumsum(pcounts)
    pstarts = pends - pcounts
    pos = _positions(pstarts.astype(I32), eid, rank)
    blk_start = jnp.arange(n_blk, dtype=I32) * tb
    blk_e = jnp.minimum(jnp.sum((pends[None, :] <= blk_start[:, None]).astype(I32), axis=1), N_EXPERTS - 1)
    nblk_used = (pends[-1:] // tb).astype(I32)

    workers = SC_CORES * SC_SUBCORES
    pos4 = pos.reshape(TOP_K, workers, n // (workers * SC_W), SC_W)
    lane = jnp.arange(tb, dtype=I32)[None, :]
    pad_rows = jnp.where(lane < (pcounts - counts)[:, None], (pstarts + counts)[:, None] + lane,
                         n_blk * tb + lane)
    padidx = pad_rows.astype(I32).reshape(workers, N_EXPERTS // workers * tb // SC_W, SC_W)
    xbuf = _sc_dispatch(h2.reshape(n, d // 2), pos4, padidx, jnp.zeros((SC_W, d // 2), F32), (n_blk + 1) * tb)
    ybuf = _ffn(blk_e, nblk_used, xbuf, w_gate, w_up, w_down, tb)
    yg = _sc_combine_gather(ybuf, pos4)
    return _final(yg, xm, wts.T, g2, final_g, out_prev, b0, total)


def kernel(x, c, ctx, c_ctx, w_ada, b_ada, norm1_g, w_in, lb_logits, hgrn_norm_g, dw_kernel, dw_bias,
           conv_ln_g, conv_ln_b, w_out, norm2_g, router_group_w, router_group_b, router_expert_w,
           router_expert_b, w_expert_gate, w_expert_up, w_expert_down, final_norm_g):
    depth = w_ada.shape[0]
    assert depth == 1, "context tokens are only updated between layers; a single layer is implemented"
    bsz, l, d = x.shape
    l0 = 0
    rows = -(-(bsz + 1) // 8) * 8
    cc = jnp.zeros((rows, d), F32).at[:bsz].set(c).at[bsz].set(c_ctx)
    mod_all = _ada(cc, w_ada[l0], b_ada[l0].reshape(1, -1))
    mod = mod_all[:bsz].reshape(bsz, N_MOD, d)
    mod_c = mod_all[bsz].reshape(N_MOD, d)
    pad8 = jnp.zeros((d, 8 - N_GROUPS), F32)
    wr = jnp.concatenate([router_group_w[l0], pad8, router_expert_w[l0], jnp.zeros((d, 8), F32)], axis=1).T
    br = jnp.concatenate([router_group_b[l0], jnp.zeros((8 - N_GROUPS,), F32), router_expert_b[l0],
                          jnp.zeros((8,), F32)]).reshape(ROUTER_ROWS, 1)
    return _layer(x, ctx, mod, mod_c, norm1_g[l0], w_in[l0], lb_logits, hgrn_norm_g[l0], dw_kernel[l0], dw_bias[l0], conv_ln_g[l0], conv_ln_b[l0], w_out[l0],
                  norm2_g[l0], wr, br, w_expert_gate[l0], w_expert_up[l0], w_expert_down[l0], final_norm_g)
```

```python
import functools

import numpy as np

import jax
import jax.numpy as jnp
from jax import lax
from jax.experimental import pallas as pl
from jax.experimental.pallas import tpu as pltpu
from jax.experimental.pallas import tpu_sc as plsc

F32 = jnp.float32
BF16 = jnp.bfloat16
I32 = jnp.int32
U32 = jnp.uint32

EPS = 1e-6
HEAD_DIM = 128
CHUNK = 64
GRID_W = 64
CONV_K = 31
CONV_PAD = CONV_K // 2
CONV_LANES = 256
CONV_LEAD = 16
CONV_SPAN = GRID_W + 24
N_GROUPS = 4
EXPERTS_PER_GROUP = 8
N_EXPERTS = N_GROUPS * EXPERTS_PER_GROUP
TOP_K = 2
N_MOD = 6
ROUTER_ROWS = 48

ADA_TN = 1024
INPROJ_TM = 1024
HGRN_TT = 512
MIX_TM = 256
MOE_TM = 512
FFN_TB = 512
VMEM_LIMIT = 48 * 1024 * 1024
SC_CORES = 2
SC_SUBCORES = 16
SC_W = 128
MOE_GROUPS = 2


def _cparams(sem):
    return pltpu.CompilerParams(dimension_semantics=sem, vmem_limit_bytes=VMEM_LIMIT)


def _silu(v):
    return v * jax.nn.sigmoid(v)


def _dot(a, b):
    return jnp.dot(a, b, preferred_element_type=F32)


def _dot_nt(a, b):
    return lax.dot_general(a, b, (((1,), (1,)), ((), ())), preferred_element_type=F32)


def _dot_tn(a, b):
    return lax.dot_general(a, b, (((0,), (0,)), ((), ())), preferred_element_type=F32)


def _pack_halves(vr):
    m = vr.shape[1] // 2
    bits = lax.bitcast_convert_type(vr, U32)
    return lax.bitcast_convert_type((bits[:, :m] >> 16) | bits[:, m:], F32)


def _unpack_halves(p):
    u = lax.bitcast_convert_type(p, U32)
    lo = lax.bitcast_convert_type(u << 16, F32)
    hi = lax.bitcast_convert_type(u & jnp.uint32(0xFFFF0000), F32)
    return lo.astype(BF16), hi.astype(BF16)


def _split2(a):
    hi = a.astype(BF16)
    lo = (a - hi.astype(F32)).astype(BF16)
    return hi, lo


def _ada_kernel(c_ref, w_ref, b_ref, o_ref):
    s = _silu(c_ref[...]).astype(BF16)
    o_ref[...] = _dot(s, w_ref[...].astype(BF16)) + b_ref[...]


def _ada(cc, w, b):
    r, d = cc.shape
    nc = w.shape[1]
    return pl.pallas_call(
        _ada_kernel,
        grid=(nc // ADA_TN,),
        in_specs=[pl.BlockSpec((r, d), lambda j: (0, 0)),
                  pl.BlockSpec((d, ADA_TN), lambda j: (0, j)),
                  pl.BlockSpec((1, ADA_TN), lambda j: (0, j))],
        out_specs=pl.BlockSpec((r, ADA_TN), lambda j: (0, j)),
        out_shape=jax.ShapeDtypeStruct((r, nc), F32),
        compiler_params=_cparams(("arbitrary",)),
        name="ada",
    )(cc, w, b)


def _inproj_kernel(x_ref, g_ref, sc_ref, sh_ref, w_ref, zf_ref, zr_ref, *, activate):
    x = x_ref[0]
    ms = jnp.mean(x * x, axis=-1, keepdims=True)
    gs = g_ref[...] * (1.0 + sc_ref[0])
    h = x * lax.rsqrt(ms + EPS) * gs + sh_ref[0]
    z = _dot(h.astype(BF16), w_ref[...])
    nf = zf_ref.shape[2]
    zf_ref[0] = z[:, :nf]
    if activate:
        wd = nf // 2
        col = lambda j: z[:, nf + j * wd:nf + (j + 1) * wd]
        zr_ref[0, :, 0:wd] = col(0).astype(BF16)
        zr_ref[0, :, wd:2 * wd] = _silu(col(1)).astype(BF16)
        zr_ref[0, :, 2 * wd:3 * wd] = _silu(col(2)).astype(BF16)
        zr_ref[0, :, 3 * wd:] = (col(3) * jax.nn.sigmoid(col(4))).astype(BF16)
    else:
        zr_ref[0] = z[:, nf:].astype(BF16)


def _inproj(x, g, sc, sh, w, nf, activate):
    bsz, l, d = x.shape
    nc = w.shape[1]
    tm = min(INPROJ_TM, l)
    nr = nc - nf - (nf // 2 if activate else 0)
    return pl.pallas_call(
        functools.partial(_inproj_kernel, activate=activate),
        grid=(bsz, l // tm),
        in_specs=[pl.BlockSpec((1, tm, d), lambda b, i: (b, i, 0)),
                  pl.BlockSpec((1, d), lambda b, i: (0, 0)),
                  pl.BlockSpec((1, 1, d), lambda b, i: (b, 0, 0)),
                  pl.BlockSpec((1, 1, d), lambda b, i: (b, 0, 0)),
                  pl.BlockSpec((d, nc), lambda b, i: (0, 0))],
        out_specs=[pl.BlockSpec((1, tm, nf), lambda b, i: (b, i, 0)),
                   pl.BlockSpec((1, tm, nr), lambda b, i: (b, i, 0))],
        out_shape=[jax.ShapeDtypeStruct((bsz, l, nf), F32),
                   jax.ShapeDtypeStruct((bsz, l, nr), BF16)],
        compiler_params=_cparams(("arbitrary", "arbitrary")),
        name="inproj",
    )(x, g, sc, sh, w)


class _HgrnDirection:
    def __init__(self, lbl, st_ref, reverse):
        self.st_ref, self.reverse = st_ref, reverse
        e = jnp.exp(lbl - jnp.max(lbl, axis=0, keepdims=True))
        self.lb = e[0:1] / jnp.sum(e, axis=0, keepdims=True)
        row = lax.broadcasted_iota(I32, (CHUNK, CHUNK), 0)
        col = lax.broadcasted_iota(I32, (CHUNK, CHUNK), 1)
        self.tri = (col >= row) if reverse else (col <= row)
        self.tm = jnp.where(self.tri, 1.0, 0.0).astype(BF16)

    def tile(self, f_ref, v_ref, q_ref=None, o_ref=None):
        heads = self.st_ref.shape[0]
        n_chunks = f_ref.shape[1] // CHUNK
        crow = lambda c: slice(c * CHUNK, (c + 1) * CHUNK)
        hsl = lambda h: slice(h * HEAD_DIM, (h + 1) * HEAD_DIM)
        f = self.lb + (1.0 - self.lb) * jax.nn.sigmoid(f_ref[0])
        k = 1.0 - f
        hi, lo = _split2(jnp.log(f))
        cums = [_dot(self.tm, hi[crow(c)]) + _dot(self.tm, lo[crow(c)])
                for c in range(n_chunks)]
        lasts = [cm[0:1] if self.reverse else cm[CHUNK - 1:CHUNK] for cm in cums]
        cum = jnp.concatenate(cums, axis=0)
        dec = [jnp.exp(la) for la in lasts]
        k_intra = k * jnp.exp(-cum)
        ks = (k_intra * jnp.concatenate([jnp.broadcast_to(dc, (CHUNK, dc.shape[1])) for dc in dec], axis=0)
              ).astype(BF16)
        vb = v_ref[0].astype(BF16)
        kv = [[_dot_tn(vb[crow(c), hsl(h)], ks[crow(c), hsl(h)]) for h in range(heads)] for c in range(n_chunks)]
        emit = o_ref is not None
        if emit:
            qd = (q_ref[0].astype(F32) * jnp.exp(cum)).astype(BF16)
            ki = k_intra.astype(BF16)
            sc = [[jnp.where(self.tri, _dot_nt(qd[crow(c), hsl(h)], ki[crow(c), hsl(h)]), 0.0).astype(BF16)
                   for h in range(heads)] for c in range(n_chunks)]
            intra = [[_dot(sc[c][h], vb[crow(c), hsl(h)]) for h in range(heads)] for c in range(n_chunks)]
        state = [self.st_ref[h] for h in range(heads)]
        for c in (range(n_chunks - 1, -1, -1) if self.reverse else range(n_chunks)):
            for h in range(heads):
                if emit:
                    inter = _dot_nt(qd[crow(c), hsl(h)], state[h].astype(BF16))
                    o_ref[0, crow(c), hsl(h)] = (intra[c][h] + inter).astype(o_ref.dtype)
                state[h] = state[h] * dec[c][:, hsl(h)] + kv[c][h]
        for h in range(heads):
            self.st_ref[h] = state[h]


def _hgrn_kernel(lbl_ref, cff_ref, cfb_ref, cv_ref, ff_ref, vf_ref, qf_ref, fb_ref, vb_ref, qb_ref,
                 of_ref, ob_ref, stf_ref, stb_ref):
    fwd = _HgrnDirection(lbl_ref[0], stf_ref, reverse=False)
    bwd = _HgrnDirection(lbl_ref[1], stb_ref, reverse=True)

    @pl.when(pl.program_id(1) == 0)
    def _():
        stf_ref[...] = jnp.zeros_like(stf_ref)
        stb_ref[...] = jnp.zeros_like(stb_ref)
        fwd.tile(cff_ref, cv_ref)
        bwd.tile(cfb_ref, cv_ref)

    fwd.tile(ff_ref, vf_ref, qf_ref, of_ref)
    bwd.tile(fb_ref, vb_ref, qb_ref, ob_ref)


def _hgrn(lbl, zcf, zcr, zf, zr):
    bsz, l, _ = zf.shape
    n_ctx_tok = zcf.shape[1]
    w = lbl.shape[2]
    heads = w // HEAD_DIM
    tt = min(HGRN_TT, l)
    nt = l // tt
    ctx_col = lambda c: pl.BlockSpec((1, n_ctx_tok, w), lambda b, i: (b, 0, c))
    fwd_col = lambda c: pl.BlockSpec((1, tt, w), lambda b, i: (b, i, c))
    bwd_col = lambda c: pl.BlockSpec((1, tt, w), lambda b, i: (b, nt - 1 - i, c))
    state = pltpu.VMEM((heads, HEAD_DIM, HEAD_DIM), F32)
    return pl.pallas_call(
        _hgrn_kernel,
        grid=(bsz, nt),
        in_specs=[pl.BlockSpec(lbl.shape, lambda b, i: (0, 0, 0)),
                  ctx_col(0), ctx_col(1), ctx_col(0),
                  fwd_col(0), fwd_col(0), fwd_col(1),
                  bwd_col(1), bwd_col(0), bwd_col(1)],
        out_specs=[fwd_col(0), bwd_col(0)],
        out_shape=[jax.ShapeDtypeStruct((bsz, l, w), BF16), jax.ShapeDtypeStruct((bsz, l, w), BF16)],
        scratch_shapes=[state, state],
        compiler_params=_cparams(("arbitrary", "arbitrary")),
        name="hgrn",
    )(lbl, zcf, zcf, zcr, zf, zr, zr, zf, zr, zr)


def _mix_kernel(of_ref, ob_ref, zg_ref, zu_ref, x_ref, g1_ref, sc2_ref, sh2_ref,
                hg_ref, dwk_ref, dwb_ref, lng_ref, lnb_ref, wo_ref, n2g_ref, wr_ref, br_ref, shm_ref,
                xm_ref, h2_ref, eid_ref, wts_ref, rank_ref, cnt_ref,
                shf_ref, cv_ref, carry_ref, hh_ref, hl_ref, *, tm):
    w = of_ref.shape[2]
    heads = w // HEAD_DIM
    step = pl.program_id(0)

    @pl.when(step == 0)
    def _():
        carry_ref[...] = jnp.zeros_like(carry_ref)
        hh_ref[...] = jnp.zeros_like(hh_ref)
        hl_ref[...] = jnp.zeros_like(hl_ref)

    router = _route(hh_ref[...], hl_ref[...], wr_ref, br_ref, carry_ref, jnp.where(step > 0, 1.0, 0.0),
                    eid_ref, wts_ref, rank_ref, cnt_ref)
    next(router)

    o = of_ref[0].astype(F32) + ob_ref[0].astype(F32)
    parts = []
    for h in range(heads):
        oh = o[:, h * HEAD_DIM:(h + 1) * HEAD_DIM]
        parts.append(oh * lax.rsqrt(jnp.mean(oh * oh, axis=-1, keepdims=True) + EPS))
    o_mix = jnp.concatenate(parts, axis=-1) * hg_ref[...] * zg_ref[0].astype(F32)

    u = zu_ref[0].astype(F32)
    span = shf_ref.shape[2]
    u_hi, u_lo = _split2(u)
    for s in range(tm // GRID_W):
        rows = slice(s * GRID_W, (s + 1) * GRID_W)
        shifted = _dot(shm_ref[...], jnp.concatenate([u_hi[rows], u_lo[rows]], axis=0))
        for b in range(8):
            shf_ref[s, b] = shifted[b * span:(b + 1) * span]
    next(router)
    lanes = CONV_LANES
    for s in range(tm // GRID_W):
        for c0 in range(0, u.shape[1], lanes):
            cs = slice(c0, c0 + lanes)
            acc = jnp.zeros((GRID_W // 8, 8, lanes), F32) + dwb_ref[:, cs][None]
            for k in range(CONV_K):
                a, b = divmod(CONV_LEAD - CONV_PAD + k, 8)
                tap = shf_ref[s, b, 8 * a:8 * a + GRID_W, cs].reshape(GRID_W // 8, 8, lanes)
                acc = acc + tap * dwk_ref[k, :, cs][None]
            cv_ref[s * GRID_W:(s + 1) * GRID_W, cs] = acc.reshape(GRID_W, lanes)
        if s == 0:
            next(router)
    cv = cv_ref[...]
    mu = jnp.mean(cv, axis=-1, keepdims=True)
    cen = cv - mu
    var = jnp.mean(cen * cen, axis=-1, keepdims=True)
    c_mix = _silu(cen * lax.rsqrt(var + EPS) * lng_ref[...] + lnb_ref[...])
    next(router, None)

    mix = _dot(o_mix.astype(BF16), wo_ref[0:w, :]) + _dot(c_mix.astype(BF16), wo_ref[w:, :])
    xm = x_ref[0] + g1_ref[0] * mix
    xm_ref[0] = xm

    ms = jnp.mean(xm * xm, axis=-1, keepdims=True)
    h2 = xm * lax.rsqrt(ms + EPS) * (n2g_ref[...] * (1.0 + sc2_ref[0])) + sh2_ref[0]
    hh = h2.astype(BF16)
    h2_rounded = hh.astype(F32)
    h2_ref[0] = _pack_halves(h2_rounded)
    hh_ref[...] = hh
    hl_ref[...] = (h2 - h2_rounded).astype(BF16)


def _route(hh, hl, wr_ref, br_ref, carry_ref, live, eid_ref, wts_ref, rank_ref, cnt_ref):
    tm = hh.shape[0]
    wr = wr_ref[...]
    wh = wr.astype(BF16)
    wl = (wr - wh.astype(F32)).astype(BF16)
    lg = _dot_nt(wh, hh) + _dot_nt(wh, hl) + _dot_nt(wl, hh) + br_ref[...]
    yield

    neg = -jnp.inf
    r8 = lax.broadcasted_iota(I32, (EXPERTS_PER_GROUP, tm), 0).astype(F32)
    gl = jnp.where(r8 < N_GROUPS, lg[0:8], neg)
    gmax = jnp.max(gl, axis=0, keepdims=True)
    grp = jnp.min(jnp.where(gl == gmax, r8, 8.0), axis=0, keepdims=True)
    p_grp = 1.0 / jnp.sum(jnp.exp(gl - gmax), axis=0, keepdims=True)
    es = lg[8:16]
    for g in range(1, N_GROUPS):
        es = jnp.where(grp == float(g), lg[8 + 8 * g:16 + 8 * g], es)
    m1 = jnp.max(es, axis=0, keepdims=True)
    i1 = jnp.min(jnp.where(es == m1, r8, 8.0), axis=0, keepdims=True)
    es2 = jnp.where(r8 == i1, neg, es)
    m2 = jnp.max(es2, axis=0, keepdims=True)
    i2 = jnp.min(jnp.where(es2 == m2, r8, 8.0), axis=0, keepdims=True)
    e2 = jnp.exp(m2 - m1)
    w1 = p_grp / (1.0 + e2)
    w2 = p_grp * e2 / (1.0 + e2)
    eid0 = grp * float(EXPERTS_PER_GROUP) + i1
    eid1 = grp * float(EXPERTS_PER_GROUP) + i2
    eid_ref[0:1, :] = eid0.astype(I32)
    eid_ref[1:2, :] = eid1.astype(I32)
    wts_ref[0:1, :] = w1
    wts_ref[1:2, :] = w2
    yield

    r32 = lax.broadcasted_iota(I32, (N_EXPERTS, tm), 0).astype(F32)
    oh0 = jnp.where(r32 == eid0, live, 0.0)
    oh1 = jnp.where(r32 == eid1, live, 0.0)
    src = lax.broadcasted_iota(I32, (tm, tm), 0)
    dst = lax.broadcasted_iota(I32, (tm, tm), 1)
    before = jnp.where(src < dst, 1.0, 0.0).astype(BF16)
    ex = _dot(jnp.concatenate([oh0, oh1], axis=0).astype(BF16), before)
    yield
    base = ex[0:N_EXPERTS] + ex[N_EXPERTS:] + carry_ref[...]
    rank_ref[0:1, :] = jnp.sum(oh0 * base, axis=0, keepdims=True).astype(I32)
    rank_ref[1:2, :] = jnp.sum(oh1 * base, axis=0, keepdims=True).astype(I32)
    carry = carry_ref[...] + jnp.sum(oh0 + oh1, axis=1, keepdims=True)
    carry_ref[...] = carry
    cnt_ref[...] = jnp.broadcast_to(carry, cnt_ref.shape)


def _conv_shift_matrix():
    r = np.arange(8 * CONV_SPAN)
    src = r // CONV_SPAN + r % CONV_SPAN - CONV_LEAD
    sel = (src[:, None] == np.arange(GRID_W)[None, :]).astype(np.float32)
    return jnp.asarray(np.concatenate([sel, sel], axis=1), dtype=BF16)


def _mix(o_f, o_b, z, x, g1, sc2, sh2, hg, dwk, dwb, lng, lnb, wo, n2g, wr, br, b0, bsz):
    _, l, d = x.shape
    shm = _conv_shift_matrix()
    w = o_f.shape[2]
    cw = dwk.shape[-1]
    tm = min(MIX_TM, l)
    nt = l // tm
    n = bsz * l
    tiles = bsz * nt
    mixed = lambda s: jnp.minimum(s, tiles - 1)
    src = lambda c: (lambda s: (b0 + mixed(s) // nt, mixed(s) % nt, c))
    tok = lambda s: (mixed(s) // nt, mixed(s) % nt, 0)
    vec = lambda s: (0, 0)
    per_b = lambda s: (b0 + mixed(s) // nt, 0, 0)
    flat = lambda s: (0, jnp.maximum(s - 1, 0))
    kern = functools.partial(_mix_kernel, tm=tm)
    return pl.pallas_call(
        kern,
        grid=(tiles + 1,),
        in_specs=[pl.BlockSpec((1, tm, w), src(0)),
                  pl.BlockSpec((1, tm, w), src(0)),
                  pl.BlockSpec((1, tm, w), src(2)),
                  pl.BlockSpec((1, tm, cw), src(3)),
                  pl.BlockSpec((1, tm, d), src(0)),
                  pl.BlockSpec((1, 1, d), per_b),
                  pl.BlockSpec((1, 1, d), per_b),
                  pl.BlockSpec((1, 1, d), per_b),
                  pl.BlockSpec((1, w), vec),
                  pl.BlockSpec((CONV_K, 8, cw), lambda s: (0, 0, 0)),
                  pl.BlockSpec((8, cw), vec),
                  pl.BlockSpec((1, cw), vec),
                  pl.BlockSpec((1, cw), vec),
                  pl.BlockSpec((w + cw, d), vec),
                  pl.BlockSpec((1, d), vec),
                  pl.BlockSpec((ROUTER_ROWS, d), vec),
                  pl.BlockSpec((ROUTER_ROWS, 1), vec),
                  pl.BlockSpec(shm.shape, vec)],
        out_specs=[pl.BlockSpec((1, tm, d), tok),
                   pl.BlockSpec((1, tm, d // 2), tok),
                   pl.BlockSpec((TOP_K, tm), flat),
                   pl.BlockSpec((TOP_K, tm), flat),
                   pl.BlockSpec((TOP_K, tm), flat),
                   pl.BlockSpec((N_EXPERTS, 128), vec)],
        out_shape=[jax.ShapeDtypeStruct((bsz, l, d), F32),
                   jax.ShapeDtypeStruct((bsz, l, d // 2), F32),
                   jax.ShapeDtypeStruct((TOP_K, n), I32),
                   jax.ShapeDtypeStruct((TOP_K, n), F32),
                   jax.ShapeDtypeStruct((TOP_K, n), I32),
                   jax.ShapeDtypeStruct((N_EXPERTS, 128), F32)],
        scratch_shapes=[pltpu.VMEM((tm // GRID_W, 8, CONV_SPAN, cw), F32),
                        pltpu.VMEM((tm, cw), F32),
                        pltpu.VMEM((N_EXPERTS, 1), F32),
                        pltpu.VMEM((tm, d), BF16),
                        pltpu.VMEM((tm, d), BF16)],
        compiler_params=_cparams(("arbitrary",)),
        name="mix",
    )(o_f, o_b, z, z, x, g1, sc2, sh2, hg, dwk, dwb, lng, lnb, wo, n2g, wr, br, shm)


def _pos_kernel(ps_ref, eid_ref, rank_ref, pos_ref):
    eid = eid_ref[...]
    acc = rank_ref[...]
    for e in range(N_EXPERTS):
        acc = acc + jnp.where(eid == e, ps_ref[e], 0)
    pos_ref[...] = acc


def _positions(pstarts, eid, rank):
    shape = eid.shape
    eid2 = eid.reshape(-1, 128)
    full = pl.BlockSpec(eid2.shape, lambda i: (0, 0))
    pos = pl.pallas_call(
        _pos_kernel,
        grid=(1,),
        in_specs=[pl.BlockSpec(memory_space=pltpu.SMEM), full, full],
        out_specs=full,
        out_shape=jax.ShapeDtypeStruct(eid2.shape, I32),
        name="positions",
    )(pstarts, eid2, rank.reshape(-1, 128))
    return pos.reshape(shape)


def _sc_mesh():
    return plsc.VectorSubcoreMesh(core_axis_name="c", subcore_axis_name="s",
                                  num_cores=SC_CORES, num_subcores=SC_SUBCORES)


def _sc_worker():
    return lax.axis_index("s") * SC_CORES + lax.axis_index("c")


def _sc_dispatch(h2p, pos4, padidx, zero_rows, n_rows):
    n, dw = h2p.shape
    _, workers, chunks, _ = pos4.shape
    pad_chunks = padidx.shape[1]
    per_worker = n // workers

    def body(h_hbm, pos_hbm, pad_hbm, zero_hbm, out_hbm, idx_v, pad_v, rows_v):
        wid = _sc_worker()
        base = wid * per_worker
        for k in range(TOP_K):
            pltpu.sync_copy(pos_hbm.at[k, wid], idx_v.at[k])
        pltpu.sync_copy(pad_hbm.at[wid], pad_v)
        pltpu.sync_copy(zero_hbm, rows_v)
        for j in range(pad_chunks):
            pltpu.sync_copy(rows_v, out_hbm.at[pad_v.at[j]])
        for j in range(chunks):
            pltpu.sync_copy(h_hbm.at[pl.ds(base + j * SC_W, SC_W)], rows_v)
            for k in range(TOP_K):
                pltpu.sync_copy(rows_v, out_hbm.at[idx_v.at[k, j]])

    return pl.kernel(
        body,
        out_type=jax.ShapeDtypeStruct((n_rows, dw), F32),
        mesh=_sc_mesh(),
        scratch_types=[pltpu.VMEM((TOP_K, chunks, SC_W), I32),
                       pltpu.VMEM((pad_chunks, SC_W), I32),
                       pltpu.VMEM((SC_W, dw), F32)],
        name="sc_dispatch",
    )(h2p, pos4, padidx, zero_rows)


def _sc_combine_gather(ybuf, pos4):
    _, dw = ybuf.shape
    _, workers, chunks, _ = pos4.shape
    per_worker = chunks * SC_W

    def body(y_hbm, pos_hbm, out_hbm, idx_v, rows_v):
        wid = _sc_worker()
        base = wid * per_worker
        for k in range(TOP_K):
            pltpu.sync_copy(pos_hbm.at[k, wid], idx_v.at[k])
        for k in range(TOP_K):
            for j in range(chunks):
                pltpu.sync_copy(y_hbm.at[idx_v.at[k, j]], rows_v)
                pltpu.sync_copy(rows_v, out_hbm.at[k, pl.ds(base + j * SC_W, SC_W)])

    return pl.kernel(
        body,
        out_type=jax.ShapeDtypeStruct((TOP_K, workers * per_worker, dw), F32),
        mesh=_sc_mesh(),
        scratch_types=[pltpu.VMEM((TOP_K, chunks, SC_W), I32),
                       pltpu.VMEM((SC_W, dw), F32)],
        name="sc_combine",
    )(ybuf, pos4)


def _ffn_kernel(blk_e_ref, nblk_ref, x_ref, wg_ref, wu_ref, wd_ref, y_ref):
    del blk_e_ref
    used = pl.program_id(0) < nblk_ref[0]
    half = x_ref.shape[1]

    @pl.when(used)
    def _():
        x_lo, x_hi = _unpack_halves(x_ref[...])
        bf = lambda wv: wv.astype(BF16)
        g = _dot(x_lo, bf(wg_ref[0, 0:half, :])) + _dot(x_hi, bf(wg_ref[0, half:, :]))
        u = _dot(x_lo, bf(wu_ref[0, 0:half, :])) + _dot(x_hi, bf(wu_ref[0, half:, :]))
        y = _dot((_silu(g) * u).astype(BF16), bf(wd_ref[0]))
        y_ref[...] = _pack_halves(y.astype(BF16).astype(F32))

    @pl.when(jnp.logical_not(used))
    def _():
        y_ref[...] = jnp.zeros_like(y_ref)


def _ffn(blk_e, nblk, xbuf, wg, wu, wd, tb):
    _, dw = xbuf.shape
    _, d, de = wg.shape
    n_blk = blk_e.shape[0]
    x_blk = lambda i, be, nb: (jnp.minimum(i, nb[0] - 1), 0)
    grid_spec = pltpu.PrefetchScalarGridSpec(
        num_scalar_prefetch=2,
        grid=(n_blk,),
        in_specs=[pl.BlockSpec((tb, dw), x_blk),
                  pl.BlockSpec((1, d, de), lambda i, be, nb: (be[i], 0, 0)),
                  pl.BlockSpec((1, d, de), lambda i, be, nb: (be[i], 0, 0)),
                  pl.BlockSpec((1, de, d), lambda i, be, nb: (be[i], 0, 0))],
        out_specs=pl.BlockSpec((tb, dw), lambda i, be, nb: (i, 0)),
    )
    return pl.pallas_call(
        _ffn_kernel,
        grid_spec=grid_spec,
        out_shape=jax.ShapeDtypeStruct((n_blk * tb, dw), F32),
        compiler_params=_cparams(("arbitrary",)),
        name="ffn",
    )(blk_e, nblk, xbuf, wg, wu, wd)


def _final_kernel(yg_ref, xm_ref, wt_ref, g2_ref, fg_ref, *out_refs):
    o_ref = out_refs[-1]
    wt = wt_ref[...]
    y = None
    for k in range(TOP_K):
        lo, hi = _unpack_halves(yg_ref[k])
        yk = jnp.concatenate([lo, hi], axis=-1).astype(F32) * wt[:, k:k + 1]
        y = yk if y is None else y + yk
    x = xm_ref[0] + g2_ref[0] * y
    ms = jnp.mean(x * x, axis=-1, keepdims=True)
    o_ref[0] = x * lax.rsqrt(ms + EPS) * fg_ref[...]


def _final(yg, xm, wt, g2, fg, out_prev, b0, total):
    bsz, l, d = xm.shape
    dw = yg.shape[2]
    tm = min(MOE_TM, l)
    nt = l // tm
    in_specs = [pl.BlockSpec((TOP_K, tm, dw), lambda b, i: (0, b * nt + i, 0)),
                pl.BlockSpec((1, tm, d), lambda b, i: (b, i, 0)),
                pl.BlockSpec((tm, TOP_K), lambda b, i: (b * nt + i, 0)),
                pl.BlockSpec((1, 1, d), lambda b, i: (b0 + b, 0, 0)),
                pl.BlockSpec((1, d), lambda b, i: (0, 0))]
    args = [yg, xm, wt, g2, fg]
    aliases = {}
    if out_prev is not None:
        in_specs.append(pl.BlockSpec(memory_space=pl.ANY))
        args.append(out_prev)
        aliases = {5: 0}
    return pl.pallas_call(
        _final_kernel,
        grid=(bsz, nt),
        in_specs=in_specs,
        out_specs=pl.BlockSpec((1, tm, d), lambda b, i: (b0 + b, i, 0)),
        out_shape=jax.ShapeDtypeStruct((total, l, d), F32),
        input_output_aliases=aliases,
        compiler_params=_cparams(("arbitrary", "arbitrary")),
        name="final",
    )(*args)


def _layer(x, ctx, mod, mod_c, norm1_g, w_in, lb_logits, hgrn_norm_g, dw_kernel, dw_bias, conv_ln_g,
           conv_ln_b, w_out, norm2_g, wr, br, w_gate, w_up, w_down, final_norm_g):
    bsz, l, d = x.shape
    w = lb_logits.shape[2]
    heads = w // HEAD_DIM
    row = lambda v: v.reshape(1, -1)
    per_b = lambda v: v.reshape(bsz, 1, d)
    sh1, sc1, g1, sh2, sc2, g2 = [per_b(mod[:, j]) for j in range(N_MOD)]
    csh1 = jnp.broadcast_to(mod_c[0].reshape(1, 1, d), (bsz, 1, d))
    csc1 = jnp.broadcast_to(mod_c[1].reshape(1, 1, d), (bsz, 1, d))

    w_in_b = w_in.astype(BF16)
    zf, zr = _inproj(x, row(norm1_g), sc1, sh1, w_in_b, 2 * w, True)
    zcf, zcr = _inproj(ctx, row(norm1_g), csc1, csh1, w_in_b[:, :3 * w], 2 * w, False)
    o_f, o_b = _hgrn(lb_logits, zcf, zcr, zf, zr)
    w_out_b = w_out.astype(BF16)

    groups = MOE_GROUPS if bsz % MOE_GROUPS == 0 else 1
    out = None
    for gi in range(groups):
        b0, gb = gi * (bsz // groups), bsz // groups
        rep8 = lambda v: jnp.broadcast_to(v[..., None, :], v.shape[:-1] + (8, v.shape[-1]))
        mixed = _mix(o_f, o_b, zr, x, g1, sc2, sh2, row(jnp.tile(hgrn_norm_g, heads)), rep8(dw_kernel),
                     rep8(dw_bias), row(conv_ln_g), row(conv_ln_b), w_out_b, row(norm2_g), wr, br, b0, gb)
        out = _moe_group(mixed, g2, w_gate, w_up, w_down, row(final_norm_g), out, b0, bsz)
    return out


def _moe_group(mixed, g2, w_gate, w_up, w_down, final_g, out_prev, b0, total):
    xm, h2, eid, wts, rank, cnt = mixed
    gb, l, d = xm.shape
    n = gb * l

    tb = FFN_TB
    n_blk = -(-(n * TOP_K) // tb) + N_EXPERTS
    counts = cnt[:, 0].astype(I32)
    pcounts = (counts + tb - 1) // tb * tb
    pends = jnp.cumsum(pcounts)
    pstarts = pends - pcounts
    pos = _positions(pstarts.astype(I32), eid, rank)
    blk_start = jnp.arange(n_blk, dtype=I32) * tb
    blk_e = jnp.minimum(jnp.sum((pends[None, :] <= blk_start[:, None]).astype(I32), axis=1), N_EXPERTS - 1)
    nblk_used = (pends[-1:] // tb).astype(I32)

    workers = SC_CORES * SC_SUBCORES
    pos4 = pos.reshape(TOP_K, workers, n // (workers * SC_W), SC_W)
    lane = jnp.arange(tb, dtype=I32)[None, :]
    pad_rows = jnp.where(lane < (pcounts - counts)[:, None], (pstarts + counts)[:, None] + lane,
                         n_blk * tb + lane)
    padidx = pad_rows.astype(I32).reshape(workers, N_EXPERTS // workers * tb // SC_W, SC_W)
    xbuf = _sc_dispatch(h2.reshape(n, d // 2), pos4, padidx, jnp.zeros((SC_W, d // 2), F32), (n_blk + 1) * tb)
    ybuf = _ffn(blk_e, nblk_used, xbuf, w_gate, w_up, w_down, tb)
    yg = _sc_combine_gather(ybuf, pos4)
    return _final(yg, xm, wts.T, g2, final_g, out_prev, b0, total)


def kernel(x, c, ctx, c_ctx, w_ada, b_ada, norm1_g, w_in, lb_logits, hgrn_norm_g, dw_kernel, dw_bias,
           conv_ln_g, conv_ln_b, w_out, norm2_g, router_group_w, router_group_b, router_expert_w,
           router_expert_b, w_expert_gate, w_expert_up, w_expert_down, final_norm_g):
    depth = w_ada.shape[0]
    assert depth == 1, "context tokens are only updated between layers; a single layer is implemented"
    bsz, l, d = x.shape
    l0 = 0
    rows = -(-(bsz + 1) // 8) * 8
    cc = jnp.zeros((rows, d), F32).at[:bsz].set(c).at[bsz].set(c_ctx)
    mod_all = _ada(cc, w_ada[l0], b_ada[l0].reshape(1, -1))
    mod = mod_all[:bsz].reshape(bsz, N_MOD, d)
    mod_c = mod_all[bsz].reshape(N_MOD, d)
    pad8 = jnp.zeros((d, 8 - N_GROUPS), F32)
    wr = jnp.concatenate([router_group_w[l0], pad8, router_expert_w[l0], jnp.zeros((d, 8), F32)], axis=1).T
    br = jnp.concatenate([router_group_b[l0], jnp.zeros((8 - N_GROUPS,), F32), router_expert_b[l0],
                          jnp.zeros((8,), F32)]).reshape(ROUTER_ROWS, 1)
    return _layer(x, ctx, mod, mod_c, norm1_g[l0], w_in[l0], lb_logits, hgrn_norm_g[l0], dw_kernel[l0], dw_bias[l0], conv_ln_g[l0], conv_ln_b[l0], w_out[l0],
                  norm2_g[l0], wr, br, w_expert_gate[l0], w_expert_up[l0], w_expert_down[l0], final_norm_g)
```

```python
import functools

import numpy as np

import jax
import jax.numpy as jnp
from jax import lax
from jax.experimental import pallas as pl
from jax.experimental.pallas import tpu as pltpu
from jax.experimental.pallas import tpu_sc as plsc

F32 = jnp.float32
BF16 = jnp.bfloat16
I32 = jnp.int32
U32 = jnp.uint32

EPS = 1e-6
HEAD_DIM = 128
CHUNK = 64
GRID_W = 64
CONV_K = 31
CONV_PAD = CONV_K // 2
CONV_LANES = 256
CONV_LEAD = 16
CONV_SPAN = GRID_W + 24
N_GROUPS = 4
EXPERTS_PER_GROUP = 8
N_EXPERTS = N_GROUPS * EXPERTS_PER_GROUP
TOP_K = 2
N_MOD = 6
ROUTER_ROWS = 48

ADA_TN = 1024
INPROJ_TM = 1024
HGRN_TT = 512
MIX_TM = 256
MOE_TM = 512
FFN_TB = 1024
VMEM_LIMIT = 48 * 1024 * 1024
SC_CORES = 2
SC_SUBCORES = 16
SC_W = 128
MOE_GROUPS = 1


def _cparams(sem):
    return pltpu.CompilerParams(dimension_semantics=sem, vmem_limit_bytes=VMEM_LIMIT)


def _silu(v):
    return v * jax.nn.sigmoid(v)


def _dot(a, b):
    return jnp.dot(a, b, preferred_element_type=F32)


def _dot_nt(a, b):
    return lax.dot_general(a, b, (((1,), (1,)), ((), ())), preferred_element_type=F32)


def _dot_tn(a, b):
    return lax.dot_general(a, b, (((0,), (0,)), ((), ())), preferred_element_type=F32)


def _pack_halves(vr):
    m = vr.shape[1] // 2
    bits = lax.bitcast_convert_type(vr, U32)
    return lax.bitcast_convert_type((bits[:, :m] >> 16) | bits[:, m:], F32)


def _unpack_halves(p):
    u = lax.bitcast_convert_type(p, U32)
    lo = lax.bitcast_convert_type(u << 16, F32)
    hi = lax.bitcast_convert_type(u & jnp.uint32(0xFFFF0000), F32)
    return lo.astype(BF16), hi.astype(BF16)


def _split2(a):
    hi = a.astype(BF16)
    lo = (a - hi.astype(F32)).astype(BF16)
    return hi, lo


def _ada_kernel(c_ref, w_ref, b_ref, o_ref):
    s = _silu(c_ref[...]).astype(BF16)
    o_ref[...] = _dot(s, w_ref[...].astype(BF16)) + b_ref[...]


def _ada(cc, w, b):
    r, d = cc.shape
    nc = w.shape[1]
    return pl.pallas_call(
        _ada_kernel,
        grid=(nc // ADA_TN,),
        in_specs=[pl.BlockSpec((r, d), lambda j: (0, 0)),
                  pl.BlockSpec((d, ADA_TN), lambda j: (0, j)),
                  pl.BlockSpec((1, ADA_TN), lambda j: (0, j))],
        out_specs=pl.BlockSpec((r, ADA_TN), lambda j: (0, j)),
        out_shape=jax.ShapeDtypeStruct((r, nc), F32),
        compiler_params=_cparams(("arbitrary",)),
        name="ada",
    )(cc, w, b)


def _inproj_kernel(x_ref, g_ref, sc_ref, sh_ref, w_ref, zf_ref, zr_ref, *, activate):
    x = x_ref[0]
    ms = jnp.mean(x * x, axis=-1, keepdims=True)
    gs = g_ref[...] * (1.0 + sc_ref[0])
    h = x * lax.rsqrt(ms + EPS) * gs + sh_ref[0]
    z = _dot(h.astype(BF16), w_ref[...])
    nf = zf_ref.shape[2]
    zf_ref[0] = z[:, :nf]
    if activate:
        wd = nf // 2
        col = lambda j: z[:, nf + j * wd:nf + (j + 1) * wd]
        zr_ref[0, :, 0:wd] = col(0).astype(BF16)
        zr_ref[0, :, wd:2 * wd] = _silu(col(1)).astype(BF16)
        zr_ref[0, :, 2 * wd:3 * wd] = _silu(col(2)).astype(BF16)
        zr_ref[0, :, 3 * wd:] = (col(3) * jax.nn.sigmoid(col(4))).astype(BF16)
    else:
        zr_ref[0] = z[:, nf:].astype(BF16)


def _inproj(x, g, sc, sh, w, nf, activate):
    bsz, l, d = x.shape
    nc = w.shape[1]
    tm = min(INPROJ_TM, l)
    nr = nc - nf - (nf // 2 if activate else 0)
    return pl.pallas_call(
        functools.partial(_inproj_kernel, activate=activate),
        grid=(bsz, l // tm),
        in_specs=[pl.BlockSpec((1, tm, d), lambda b, i: (b, i, 0)),
                  pl.BlockSpec((1, d), lambda b, i: (0, 0)),
                  pl.BlockSpec((1, 1, d), lambda b, i: (b, 0, 0)),
                  pl.BlockSpec((1, 1, d), lambda b, i: (b, 0, 0)),
                  pl.BlockSpec((d, nc), lambda b, i: (0, 0))],
        out_specs=[pl.BlockSpec((1, tm, nf), lambda b, i: (b, i, 0)),
                   pl.BlockSpec((1, tm, nr), lambda b, i: (b, i, 0))],
        out_shape=[jax.ShapeDtypeStruct((bsz, l, nf), F32),
                   jax.ShapeDtypeStruct((bsz, l, nr), BF16)],
        compiler_params=_cparams(("arbitrary", "arbitrary")),
        name="inproj",
    )(x, g, sc, sh, w)


class _HgrnDirection:
    def __init__(self, lbl, st_ref, reverse):
        self.st_ref, self.reverse = st_ref, reverse
        e = jnp.exp(lbl - jnp.max(lbl, axis=0, keepdims=True))
        self.lb = e[0:1] / jnp.sum(e, axis=0, keepdims=True)
        row = lax.broadcasted_iota(I32, (CHUNK, CHUNK), 0)
        col = lax.broadcasted_iota(I32, (CHUNK, CHUNK), 1)
        self.tri = (col >= row) if reverse else (col <= row)
        self.tm = jnp.where(self.tri, 1.0, 0.0).astype(BF16)

    def tile(self, f_ref, v_ref, q_ref=None, o_ref=None):
        heads = self.st_ref.shape[0]
        n_chunks = f_ref.shape[1] // CHUNK
        crow = lambda c: slice(c * CHUNK, (c + 1) * CHUNK)
        hsl = lambda h: slice(h * HEAD_DIM, (h + 1) * HEAD_DIM)
        f = self.lb + (1.0 - self.lb) * jax.nn.sigmoid(f_ref[0])
        k = 1.0 - f
        hi, lo = _split2(jnp.log2(f))
        cums = [_dot(self.tm, hi[crow(c)]) + _dot(self.tm, lo[crow(c)])
                for c in range(n_chunks)]
        lasts = [cm[0:1] if self.reverse else cm[CHUNK - 1:CHUNK] for cm in cums]
        cum = jnp.concatenate(cums, axis=0)
        dec = [jnp.exp2(la) for la in lasts]
        k_intra = k * jnp.exp2(-cum)
        ks = (k_intra * jnp.concatenate([jnp.broadcast_to(dc, (CHUNK, dc.shape[1])) for dc in dec], axis=0)
              ).astype(BF16)
        vb = v_ref[0].astype(BF16)
        kv = [[_dot_tn(vb[crow(c), hsl(h)], ks[crow(c), hsl(h)]) for h in range(heads)] for c in range(n_chunks)]
        emit = o_ref is not None
        if emit:
            qd = (q_ref[0].astype(F32) * jnp.exp2(cum)).astype(BF16)
            ki = k_intra.astype(BF16)
            sc = [[jnp.where(self.tri, _dot_nt(qd[crow(c), hsl(h)], ki[crow(c), hsl(h)]), 0.0).astype(BF16)
                   for h in range(heads)] for c in range(n_chunks)]
            lhs = [[jnp.concatenate([qd[crow(c), hsl(h)], sc[c][h]], axis=1) for h in range(heads)]
                   for c in range(n_chunks)]
            v32 = v_ref[0].astype(F32)
            vt = [[v32[crow(c), hsl(h)].T.astype(BF16) for h in range(heads)] for c in range(n_chunks)]
        state = [self.st_ref[h] for h in range(heads)]
        for c in (range(n_chunks - 1, -1, -1) if self.reverse else range(n_chunks)):
            for h in range(heads):
                if emit:
                    rhs = jnp.concatenate([state[h].astype(BF16), vt[c][h]], axis=1)
                    o_ref[0, crow(c), hsl(h)] = _dot_nt(lhs[c][h], rhs).astype(o_ref.dtype)
                state[h] = state[h] * dec[c][:, hsl(h)] + kv[c][h]
        for h in range(heads):
            self.st_ref[h] = state[h]


def _hgrn_kernel(lbl_ref, cff_ref, cfb_ref, cv_ref, ff_ref, vf_ref, qf_ref, fb_ref, vb_ref, qb_ref,
                 of_ref, ob_ref, stf_ref, stb_ref):
    fwd = _HgrnDirection(lbl_ref[0], stf_ref, reverse=False)
    bwd = _HgrnDirection(lbl_ref[1], stb_ref, reverse=True)

    @pl.when(pl.program_id(1) == 0)
    def _():
        stf_ref[...] = jnp.zeros_like(stf_ref)
        stb_ref[...] = jnp.zeros_like(stb_ref)
        fwd.tile(cff_ref, cv_ref)
        bwd.tile(cfb_ref, cv_ref)

    fwd.tile(ff_ref, vf_ref, qf_ref, of_ref)
    bwd.tile(fb_ref, vb_ref, qb_ref, ob_ref)


def _hgrn(lbl, zcf, zcr, zf, zr):
    bsz, l, _ = zf.shape
    n_ctx_tok = zcf.shape[1]
    w = lbl.shape[2]
    heads = w // HEAD_DIM
    tt = min(HGRN_TT, l)
    nt = l // tt
    ctx_col = lambda c: pl.BlockSpec((1, n_ctx_tok, w), lambda b, i: (b, 0, c))
    fwd_col = lambda c: pl.BlockSpec((1, tt, w), lambda b, i: (b, i, c))
    bwd_col = lambda c: pl.BlockSpec((1, tt, w), lambda b, i: (b, nt - 1 - i, c))
    state = pltpu.VMEM((heads, HEAD_DIM, HEAD_DIM), F32)
    return pl.pallas_call(
        _hgrn_kernel,
        grid=(bsz, nt),
        in_specs=[pl.BlockSpec(lbl.shape, lambda b, i: (0, 0, 0)),
                  ctx_col(0), ctx_col(1), ctx_col(0),
                  fwd_col(0), fwd_col(0), fwd_col(1),
                  bwd_col(1), bwd_col(0), bwd_col(1)],
        out_specs=[fwd_col(0), bwd_col(0)],
        out_shape=[jax.ShapeDtypeStruct((bsz, l, w), BF16), jax.ShapeDtypeStruct((bsz, l, w), BF16)],
        scratch_shapes=[state, state],
        compiler_params=_cparams(("arbitrary", "arbitrary")),
        name="hgrn",
    )(lbl, zcf, zcf, zcr, zf, zr, zr, zf, zr, zr)


def _mix_kernel(of_ref, ob_ref, zg_ref, zu_ref, x_ref, g1_ref, sc2_ref, sh2_ref,
                hg_ref, dwk_ref, dwb_ref, lng_ref, lnb_ref, wo_ref, n2g_ref, wr_ref, br_ref, shm_ref,
                xm_ref, h2_ref, eid_ref, wts_ref, rank_ref, cnt_ref,
                shf_ref, cv_ref, carry_ref, hh_ref, hl_ref, *, tm):
    w = of_ref.shape[2]
    heads = w // HEAD_DIM
    step = pl.program_id(0)

    @pl.when(step == 0)
    def _():
        carry_ref[...] = jnp.zeros_like(carry_ref)
        hh_ref[...] = jnp.zeros_like(hh_ref)
        hl_ref[...] = jnp.zeros_like(hl_ref)

    router = _route(hh_ref[...], hl_ref[...], wr_ref, br_ref, carry_ref, jnp.where(step > 0, 1.0, 0.0),
                    eid_ref, wts_ref, rank_ref, cnt_ref)
    next(router)

    o = of_ref[0].astype(F32) + ob_ref[0].astype(F32)
    parts = []
    for h in range(heads):
        oh = o[:, h * HEAD_DIM:(h + 1) * HEAD_DIM]
        parts.append(oh * lax.rsqrt(jnp.mean(oh * oh, axis=-1, keepdims=True) + EPS))
    o_mix = jnp.concatenate(parts, axis=-1) * hg_ref[...] * zg_ref[0].astype(F32)

    u = zu_ref[0].astype(F32)
    span = shf_ref.shape[2]
    u_hi, u_lo = _split2(u)
    for s in range(tm // GRID_W):
        rows = slice(s * GRID_W, (s + 1) * GRID_W)
        shifted = _dot(shm_ref[...], jnp.concatenate([u_hi[rows], u_lo[rows]], axis=0))
        for b in range(8):
            shf_ref[s, b] = shifted[b * span:(b + 1) * span]
    next(router)
    lanes = CONV_LANES
    for s in range(tm // GRID_W):
        for c0 in range(0, u.shape[1], lanes):
            cs = slice(c0, c0 + lanes)
            acc = jnp.zeros((GRID_W // 8, 8, lanes), F32) + dwb_ref[:, cs][None]
            for k in range(CONV_K):
                a, b = divmod(CONV_LEAD - CONV_PAD + k, 8)
                tap = shf_ref[s, b, 8 * a:8 * a + GRID_W, cs].reshape(GRID_W // 8, 8, lanes)
                acc = acc + tap * dwk_ref[k, :, cs][None]
            cv_ref[s * GRID_W:(s + 1) * GRID_W, cs] = acc.reshape(GRID_W, lanes)
        if s == 0:
            next(router)
    cv = cv_ref[...]
    mu = jnp.mean(cv, axis=-1, keepdims=True)
    cen = cv - mu
    var = jnp.mean(cen * cen, axis=-1, keepdims=True)
    c_mix = _silu(cen * lax.rsqrt(var + EPS) * lng_ref[...] + lnb_ref[...])
    next(router, None)

    mix = _dot(o_mix.astype(BF16), wo_ref[0:w, :]) + _dot(c_mix.astype(BF16), wo_ref[w:, :])
    xm = x_ref[0] + g1_ref[0] * mix
    xm_ref[0] = xm

    ms = jnp.mean(xm * xm, axis=-1, keepdims=True)
    h2 = xm * lax.rsqrt(ms + EPS) * (n2g_ref[...] * (1.0 + sc2_ref[0])) + sh2_ref[0]
    hh = h2.astype(BF16)
    h2_rounded = hh.astype(F32)
    h2_ref[0] = _pack_halves(h2_rounded)
    hh_ref[...] = hh
    hl_ref[...] = (h2 - h2_rounded).astype(BF16)


def _route(hh, hl, wr_ref, br_ref, carry_ref, live, eid_ref, wts_ref, rank_ref, cnt_ref):
    tm = hh.shape[0]
    wr = wr_ref[...]
    wh = wr.astype(BF16)
    wl = (wr - wh.astype(F32)).astype(BF16)
    lg = _dot_nt(wh, hh) + _dot_nt(wh, hl) + _dot_nt(wl, hh) + br_ref[...]
    yield

    neg = -jnp.inf
    r8 = lax.broadcasted_iota(I32, (EXPERTS_PER_GROUP, tm), 0).astype(F32)
    gl = jnp.where(r8 < N_GROUPS, lg[0:8], neg)
    gmax = jnp.max(gl, axis=0, keepdims=True)
    grp = jnp.min(jnp.where(gl == gmax, r8, 8.0), axis=0, keepdims=True)
    p_grp = 1.0 / jnp.sum(jnp.exp(gl - gmax), axis=0, keepdims=True)
    es = lg[8:16]
    for g in range(1, N_GROUPS):
        es = jnp.where(grp == float(g), lg[8 + 8 * g:16 + 8 * g], es)
    m1 = jnp.max(es, axis=0, keepdims=True)
    i1 = jnp.min(jnp.where(es == m1, r8, 8.0), axis=0, keepdims=True)
    es2 = jnp.where(r8 == i1, neg, es)
    m2 = jnp.max(es2, axis=0, keepdims=True)
    i2 = jnp.min(jnp.where(es2 == m2, r8, 8.0), axis=0, keepdims=True)
    e2 = jnp.exp(m2 - m1)
    w1 = p_grp / (1.0 + e2)
    w2 = p_grp * e2 / (1.0 + e2)
    eid0 = grp * float(EXPERTS_PER_GROUP) + i1
    eid1 = grp * float(EXPERTS_PER_GROUP) + i2
    eid_ref[0:1, :] = eid0.astype(I32)
    eid_ref[1:2, :] = eid1.astype(I32)
    wts_ref[0:1, :] = w1
    wts_ref[1:2, :] = w2
    yield

    r32 = lax.broadcasted_iota(I32, (N_EXPERTS, tm), 0).astype(F32)
    oh0 = jnp.where(r32 == eid0, live, 0.0)
    oh1 = jnp.where(r32 == eid1, live, 0.0)
    src = lax.broadcasted_iota(I32, (tm, tm), 0)
    dst = lax.broadcasted_iota(I32, (tm, tm), 1)
    before = jnp.where(src < dst, 1.0, 0.0).astype(BF16)
    ex = _dot(jnp.concatenate([oh0, oh1], axis=0).astype(BF16), before)
    yield
    base = ex[0:N_EXPERTS] + ex[N_EXPERTS:] + carry_ref[...]
    rank_ref[0:1, :] = jnp.sum(oh0 * base, axis=0, keepdims=True).astype(I32)
    rank_ref[1:2, :] = jnp.sum(oh1 * base, axis=0, keepdims=True).astype(I32)
    carry = carry_ref[...] + jnp.sum(oh0 + oh1, axis=1, keepdims=True)
    carry_ref[...] = carry
    cnt_ref[...] = jnp.broadcast_to(carry, cnt_ref.shape)


def _conv_shift_matrix():
    r = np.arange(8 * CONV_SPAN)
    src = r // CONV_SPAN + r % CONV_SPAN - CONV_LEAD
    sel = (src[:, None] == np.arange(GRID_W)[None, :]).astype(np.float32)
    return jnp.asarray(np.concatenate([sel, sel], axis=1), dtype=BF16)


def _mix(o_f, o_b, z, x, g1, sc2, sh2, hg, dwk, dwb, lng, lnb, wo, n2g, wr, br, b0, bsz):
    _, l, d = x.shape
    shm = _conv_shift_matrix()
    w = o_f.shape[2]
    cw = dwk.shape[-1]
    tm = min(MIX_TM, l)
    nt = l // tm
    n = bsz * l
    tiles = bsz * nt
    mixed = lambda s: jnp.minimum(s, tiles - 1)
    src = lambda c: (lambda s: (b0 + mixed(s) // nt, mixed(s) % nt, c))
    tok = lambda s: (mixed(s) // nt, mixed(s) % nt, 0)
    vec = lambda s: (0, 0)
    per_b = lambda s: (b0 + mixed(s) // nt, 0, 0)
    flat = lambda s: (0, jnp.maximum(s - 1, 0))
    kern = functools.partial(_mix_kernel, tm=tm)
    return pl.pallas_call(
        kern,
        grid=(tiles + 1,),
        in_specs=[pl.BlockSpec((1, tm, w), src(0)),
                  pl.BlockSpec((1, tm, w), src(0)),
                  pl.BlockSpec((1, tm, w), src(2)),
                  pl.BlockSpec((1, tm, cw), src(3)),
                  pl.BlockSpec((1, tm, d), src(0)),
                  pl.BlockSpec((1, 1, d), per_b),
                  pl.BlockSpec((1, 1, d), per_b),
                  pl.BlockSpec((1, 1, d), per_b),
                  pl.BlockSpec((1, w), vec),
                  pl.BlockSpec((CONV_K, 8, cw), lambda s: (0, 0, 0)),
                  pl.BlockSpec((8, cw), vec),
                  pl.BlockSpec((1, cw), vec),
                  pl.BlockSpec((1, cw), vec),
                  pl.BlockSpec((w + cw, d), vec),
                  pl.BlockSpec((1, d), vec),
                  pl.BlockSpec((ROUTER_ROWS, d), vec),
                  pl.BlockSpec((ROUTER_ROWS, 1), vec),
                  pl.BlockSpec(shm.shape, vec)],
        out_specs=[pl.BlockSpec((1, tm, d), tok),
                   pl.BlockSpec((1, tm, d // 2), tok),
                   pl.BlockSpec((TOP_K, tm), flat),
                   pl.BlockSpec((TOP_K, tm), flat),
                   pl.BlockSpec((TOP_K, tm), flat),
                   pl.BlockSpec((N_EXPERTS, 128), vec)],
        out_shape=[jax.ShapeDtypeStruct((bsz, l, d), F32),
                   jax.ShapeDtypeStruct((bsz, l, d // 2), F32),
                   jax.ShapeDtypeStruct((TOP_K, n), I32),
                   jax.ShapeDtypeStruct((TOP_K, n), F32),
                   jax.ShapeDtypeStruct((TOP_K, n), I32),
                   jax.ShapeDtypeStruct((N_EXPERTS, 128), F32)],
        scratch_shapes=[pltpu.VMEM((tm // GRID_W, 8, CONV_SPAN, cw), F32),
                        pltpu.VMEM((tm, cw), F32),
                        pltpu.VMEM((N_EXPERTS, 1), F32),
                        pltpu.VMEM((tm, d), BF16),
                        pltpu.VMEM((tm, d), BF16)],
        compiler_params=_cparams(("arbitrary",)),
        name="mix",
    )(o_f, o_b, z, z, x, g1, sc2, sh2, hg, dwk, dwb, lng, lnb, wo, n2g, wr, br, shm)


def _pos_kernel(ps_ref, eid_ref, rank_ref, pos_ref):
    eid = eid_ref[...]
    acc = rank_ref[...]
    for e in range(N_EXPERTS):
        acc = acc + jnp.where(eid == e, ps_ref[e], 0)
    pos_ref[...] = acc


def _positions(pstarts, eid, rank):
    shape = eid.shape
    eid2 = eid.reshape(-1, 128)
    full = pl.BlockSpec(eid2.shape, lambda i: (0, 0))
    pos = pl.pallas_call(
        _pos_kernel,
        grid=(1,),
        in_specs=[pl.BlockSpec(memory_space=pltpu.SMEM), full, full],
        out_specs=full,
        out_shape=jax.ShapeDtypeStruct(eid2.shape, I32),
        name="positions",
    )(pstarts, eid2, rank.reshape(-1, 128))
    return pos.reshape(shape)


def _sc_mesh():
    return plsc.VectorSubcoreMesh(core_axis_name="c", subcore_axis_name="s",
                                  num_cores=SC_CORES, num_subcores=SC_SUBCORES)


def _sc_worker():
    return lax.axis_index("s") * SC_CORES + lax.axis_index("c")


def _sc_dispatch(h2p, pos4, padidx, zero_rows, n_rows):
    n, dw = h2p.shape
    _, workers, chunks, _ = pos4.shape
    pad_chunks = padidx.shape[1]
    per_worker = n // workers

    def body(h_hbm, pos_hbm, pad_hbm, zero_hbm, out_hbm, idx_v, pad_v, rows_v):
        wid = _sc_worker()
        base = wid * per_worker
        for k in range(TOP_K):
            pltpu.sync_copy(pos_hbm.at[k, wid], idx_v.at[k])
        pltpu.sync_copy(pad_hbm.at[wid], pad_v)
        pltpu.sync_copy(zero_hbm, rows_v)
        for j in range(pad_chunks):
            pltpu.sync_copy(rows_v, out_hbm.at[pad_v.at[j]])
        for j in range(chunks):
            pltpu.sync_copy(h_hbm.at[pl.ds(base + j * SC_W, SC_W)], rows_v)
            for k in range(TOP_K):
                pltpu.sync_copy(rows_v, out_hbm.at[idx_v.at[k, j]])

    return pl.kernel(
        body,
        out_type=jax.ShapeDtypeStruct((n_rows, dw), F32),
        mesh=_sc_mesh(),
        scratch_types=[pltpu.VMEM((TOP_K, chunks, SC_W), I32),
                       pltpu.VMEM((pad_chunks, SC_W), I32),
                       pltpu.VMEM((SC_W, dw), F32)],
        name="sc_dispatch",
    )(h2p, pos4, padidx, zero_rows)


def _sc_combine_gather(ybuf, pos4):
    _, dw = ybuf.shape
    _, workers, chunks, _ = pos4.shape
    per_worker = chunks * SC_W

    def body(y_hbm, pos_hbm, out_hbm, idx_v, rows_v):
        wid = _sc_worker()
        base = wid * per_worker
        for k in range(TOP_K):
            pltpu.sync_copy(pos_hbm.at[k, wid], idx_v.at[k])
        for k in range(TOP_K):
            for j in range(chunks):
                pltpu.sync_copy(y_hbm.at[idx_v.at[k, j]], rows_v)
                pltpu.sync_copy(rows_v, out_hbm.at[k, pl.ds(base + j * SC_W, SC_W)])

    return pl.kernel(
        body,
        out_type=jax.ShapeDtypeStruct((TOP_K, workers * per_worker, dw), F32),
        mesh=_sc_mesh(),
        scratch_types=[pltpu.VMEM((TOP_K, chunks, SC_W), I32),
                       pltpu.VMEM((SC_W, dw), F32)],
        name="sc_combine",
    )(ybuf, pos4)


def _ffn_kernel(blk_e_ref, nblk_ref, x_ref, wg_ref, wu_ref, wd_ref, y_ref):
    del blk_e_ref
    used = pl.program_id(0) < nblk_ref[0]
    half = x_ref.shape[1]

    @pl.when(used)
    def _():
        x_lo, x_hi = _unpack_halves(x_ref[...])
        bf = lambda wv: wv.astype(BF16)
        g = _dot(x_lo, bf(wg_ref[0, 0:half, :])) + _dot(x_hi, bf(wg_ref[0, half:, :]))
        u = _dot(x_lo, bf(wu_ref[0, 0:half, :])) + _dot(x_hi, bf(wu_ref[0, half:, :]))
        y = _dot((_silu(g) * u).astype(BF16), bf(wd_ref[0]))
        y_ref[...] = _pack_halves(y.astype(BF16).astype(F32))

    @pl.when(jnp.logical_not(used))
    def _():
        y_ref[...] = jnp.zeros_like(y_ref)


def _ffn(blk_e, nblk, xbuf, wg, wu, wd, tb):
    _, dw = xbuf.shape
    _, d, de = wg.shape
    n_blk = blk_e.shape[0]
    x_blk = lambda i, be, nb: (jnp.minimum(i, nb[0] - 1), 0)
    grid_spec = pltpu.PrefetchScalarGridSpec(
        num_scalar_prefetch=2,
        grid=(n_blk,),
        in_specs=[pl.BlockSpec((tb, dw), x_blk),
                  pl.BlockSpec((1, d, de), lambda i, be, nb: (be[i], 0, 0)),
                  pl.BlockSpec((1, d, de), lambda i, be, nb: (be[i], 0, 0)),
                  pl.BlockSpec((1, de, d), lambda i, be, nb: (be[i], 0, 0))],
        out_specs=pl.BlockSpec((tb, dw), lambda i, be, nb: (i, 0)),
    )
    return pl.pallas_call(
        _ffn_kernel,
        grid_spec=grid_spec,
        out_shape=jax.ShapeDtypeStruct((n_blk * tb, dw), F32),
        compiler_params=_cparams(("arbitrary",)),
        name="ffn",
    )(blk_e, nblk, xbuf, wg, wu, wd)


def _final_kernel(yg_ref, xm_ref, wt_ref, g2_ref, fg_ref, *out_refs):
    o_ref = out_refs[-1]
    wt = wt_ref[...]
    y = None
    for k in range(TOP_K):
        lo, hi = _unpack_halves(yg_ref[k])
        yk = jnp.concatenate([lo, hi], axis=-1).astype(F32) * wt[:, k:k + 1]
        y = yk if y is None else y + yk
    x = xm_ref[0] + g2_ref[0] * y
    ms = jnp.mean(x * x, axis=-1, keepdims=True)
    o_ref[0] = x * lax.rsqrt(ms + EPS) * fg_ref[...]


def _final(yg, xm, wt, g2, fg, out_prev, b0, total):
    bsz, l, d = xm.shape
    dw = yg.shape[2]
    tm = min(MOE_TM, l)
    nt = l // tm
    in_specs = [pl.BlockSpec((TOP_K, tm, dw), lambda b, i: (0, b * nt + i, 0)),
                pl.BlockSpec((1, tm, d), lambda b, i: (b, i, 0)),
                pl.BlockSpec((tm, TOP_K), lambda b, i: (b * nt + i, 0)),
                pl.BlockSpec((1, 1, d), lambda b, i: (b0 + b, 0, 0)),
                pl.BlockSpec((1, d), lambda b, i: (0, 0))]
    args = [yg, xm, wt, g2, fg]
    aliases = {}
    if out_prev is not None:
        in_specs.append(pl.BlockSpec(memory_space=pl.ANY))
        args.append(out_prev)
        aliases = {5: 0}
    return pl.pallas_call(
        _final_kernel,
        grid=(bsz, nt),
        in_specs=in_specs,
        out_specs=pl.BlockSpec((1, tm, d), lambda b, i: (b0 + b, i, 0)),
        out_shape=jax.ShapeDtypeStruct((total, l, d), F32),
        input_output_aliases=aliases,
        compiler_params=_cparams(("arbitrary", "arbitrary")),
        name="final",
    )(*args)


def _layer(x, ctx, mod, mod_c, norm1_g, w_in, lb_logits, hgrn_norm_g, dw_kernel, dw_bias, conv_ln_g,
           conv_ln_b, w_out, norm2_g, wr, br, w_gate, w_up, w_down, final_norm_g):
    bsz, l, d = x.shape
    w = lb_logits.shape[2]
    heads = w // HEAD_DIM
    row = lambda v: v.reshape(1, -1)
    per_b = lambda v: v.reshape(bsz, 1, d)
    sh1, sc1, g1, sh2, sc2, g2 = [per_b(mod[:, j]) for j in range(N_MOD)]
    csh1 = jnp.broadcast_to(mod_c[0].reshape(1, 1, d), (bsz, 1, d))
    csc1 = jnp.broadcast_to(mod_c[1].reshape(1, 1, d), (bsz, 1, d))

    w_in_b = w_in.astype(BF16)
    zf, zr = _inproj(x, row(norm1_g), sc1, sh1, w_in_b, 2 * w, True)
    zcf, zcr = _inproj(ctx, row(norm1_g), csc1, csh1, w_in_b[:, :3 * w], 2 * w, False)
    o_f, o_b = _hgrn(lb_logits, zcf, zcr, zf, zr)
    w_out_b = w_out.astype(BF16)

    groups = MOE_GROUPS if bsz % MOE_GROUPS == 0 else 1
    out = None
    for gi in range(groups):
        b0, gb = gi * (bsz // groups), bsz // groups
        rep8 = lambda v: jnp.broadcast_to(v[..., None, :], v.shape[:-1] + (8, v.shape[-1]))
        mixed = _mix(o_f, o_b, zr, x, g1, sc2, sh2, row(jnp.tile(hgrn_norm_g, heads)), rep8(dw_kernel),
                     rep8(dw_bias), row(conv_ln_g), row(conv_ln_b), w_out_b, row(norm2_g), wr, br, b0, gb)
        out = _moe_group(mixed, g2, w_gate, w_up, w_down, row(final_norm_g), out, b0, bsz)
    return out


def _moe_group(mixed, g2, w_gate, w_up, w_down, final_g, out_prev, b0, total):
    xm, h2, eid, wts, rank, cnt = mixed
    gb, l, d = xm.shape
    n = gb * l

    tb = FFN_TB
    n_blk = -(-(n * TOP_K) // tb) + N_EXPERTS
    counts = cnt[:, 0].astype(I32)
    pcounts = (counts + tb - 1) // tb * tb
    pends = jnp.cumsum(pcounts)
    pstarts = pends - pcounts
    pos = _positions(pstarts.astype(I32), eid, rank)
    blk_start = jnp.arange(n_blk, dtype=I32) * tb
    blk_e = jnp.minimum(jnp.sum((pends[None, :] <= blk_start[:, None]).astype(I32), axis=1), N_EXPERTS - 1)
    nblk_used = (pends[-1:] // tb).astype(I32)

    workers = SC_CORES * SC_SUBCORES
    pos4 = pos.reshape(TOP_K, workers, n // (workers * SC_W), SC_W)
    lane = jnp.arange(tb, dtype=I32)[None, :]
    pad_rows = jnp.where(lane < (pcounts - counts)[:, None], (pstarts + counts)[:, None] + lane,
                         n_blk * tb + lane)
    padidx = pad_rows.astype(I32).reshape(workers, N_EXPERTS // workers * tb // SC_W, SC_W)
    xbuf = _sc_dispatch(h2.reshape(n, d // 2), pos4, padidx, jnp.zeros((SC_W, d // 2), F32), (n_blk + 1) * tb)
    ybuf = _ffn(blk_e, nblk_used, xbuf, w_gate, w_up, w_down, tb)
    yg = _sc_combine_gather(ybuf, pos4)
    return _final(yg, xm, wts.T, g2, final_g, out_prev, b0, total)


def kernel(x, c, ctx, c_ctx, w_ada, b_ada, norm1_g, w_in, lb_logits, hgrn_norm_g, dw_kernel, dw_bias,
           conv_ln_g, conv_ln_b, w_out, norm2_g, router_group_w, router_group_b, router_expert_w,
           router_expert_b, w_expert_gate, w_expert_up, w_expert_down, final_norm_g):
    depth = w_ada.shape[0]
    assert depth == 1, "context tokens are only updated between layers; a single layer is implemented"
    bsz, l, d = x.shape
    l0 = 0
    rows = -(-(bsz + 1) // 8) * 8
    cc = jnp.zeros((rows, d), F32).at[:bsz].set(c).at[bsz].set(c_ctx)
    mod_all = _ada(cc, w_ada[l0], b_ada[l0].reshape(1, -1))
    mod = mod_all[:bsz].reshape(bsz, N_MOD, d)
    mod_c = mod_all[bsz].reshape(N_MOD, d)
    pad8 = jnp.zeros((d, 8 - N_GROUPS), F32)
    wr = jnp.concatenate([router_group_w[l0], pad8, router_expert_w[l0], jnp.zeros((d, 8), F32)], axis=1).T
    br = jnp.concatenate([router_group_b[l0], jnp.zeros((8 - N_GROUPS,), F32), router_expert_b[l0],
                          jnp.zeros((8,), F32)]).reshape(ROUTER_ROWS, 1)
    return _layer(x, ctx, mod, mod_c, norm1_g[l0], w_in[l0], lb_logits, hgrn_norm_g[l0], dw_kernel[l0], dw_bias[l0], conv_ln_g[l0], conv_ln_b[l0], w_out[l0],
                  norm2_g[l0], wr, br, w_expert_gate[l0], w_expert_up[l0], w_expert_down[l0], final_norm_g)
```

```python
import functools

import numpy as np

import jax
import jax.numpy as jnp
from jax import lax
from jax.experimental import pallas as pl
from jax.experimental.pallas import tpu as pltpu
from jax.experimental.pallas import tpu_sc as plsc

F32 = jnp.float32
BF16 = jnp.bfloat16
I32 = jnp.int32
U32 = jnp.uint32

EPS = 1e-6
HEAD_DIM = 128
CHUNK = 64
GRID_W = 64
CONV_K = 31
CONV_PAD = CONV_K // 2
CONV_LANES = 256
CONV_LEAD = 16
CONV_SPAN = GRID_W + 24
N_GROUPS = 4
EXPERTS_PER_GROUP = 8
N_EXPERTS = N_GROUPS * EXPERTS_PER_GROUP
TOP_K = 2
N_MOD = 6
ROUTER_ROWS = 48

ADA_TN = 1024
INPROJ_TM = 1024
HGRN_TT = 512
MIX_TM = 256
MOE_TM = 512
FFN_TB = 1024
VMEM_LIMIT = 48 * 1024 * 1024
SC_CORES = 2
SC_SUBCORES = 16
SC_W = 128
MOE_GROUPS = 1


def _cparams(sem):
    return pltpu.CompilerParams(dimension_semantics=sem, vmem_limit_bytes=VMEM_LIMIT)


def _silu(v):
    return v * jax.nn.sigmoid(v)


def _dot(a, b):
    return jnp.dot(a, b, preferred_element_type=F32)


def _dot_nt(a, b):
    return lax.dot_general(a, b, (((1,), (1,)), ((), ())), preferred_element_type=F32)


def _dot_tn(a, b):
    return lax.dot_general(a, b, (((0,), (0,)), ((), ())), preferred_element_type=F32)


def _pack_halves(vr):
    m = vr.shape[1] // 2
    bits = lax.bitcast_convert_type(vr, U32)
    return lax.bitcast_convert_type((bits[:, :m] >> 16) | bits[:, m:], F32)


def _unpack_halves(p):
    u = lax.bitcast_convert_type(p, U32)
    lo = lax.bitcast_convert_type(u << 16, F32)
    hi = lax.bitcast_convert_type(u & jnp.uint32(0xFFFF0000), F32)
    return lo.astype(BF16), hi.astype(BF16)


def _split2(a):
    hi = a.astype(BF16)
    lo = (a - hi.astype(F32)).astype(BF16)
    return hi, lo


def _ada_kernel(c_ref, w_ref, b_ref, o_ref):
    s = _silu(c_ref[...]).astype(BF16)
    o_ref[...] = _dot(s, w_ref[...].astype(BF16)) + b_ref[...]


def _ada(cc, w, b):
    r, d = cc.shape
    nc = w.shape[1]
    return pl.pallas_call(
        _ada_kernel,
        grid=(nc // ADA_TN,),
        in_specs=[pl.BlockSpec((r, d), lambda j: (0, 0)),
                  pl.BlockSpec((d, ADA_TN), lambda j: (0, j)),
                  pl.BlockSpec((1, ADA_TN), lambda j: (0, j))],
        out_specs=pl.BlockSpec((r, ADA_TN), lambda j: (0, j)),
        out_shape=jax.ShapeDtypeStruct((r, nc), F32),
        compiler_params=_cparams(("arbitrary",)),
        name="ada",
    )(cc, w, b)


def _inproj_kernel(x_ref, g_ref, sc_ref, sh_ref, w_ref, zf_ref, zr_ref, *, activate):
    x = x_ref[0]
    ms = jnp.mean(x * x, axis=-1, keepdims=True)
    gs = g_ref[...] * (1.0 + sc_ref[0])
    h = x * lax.rsqrt(ms + EPS) * gs + sh_ref[0]
    z = _dot(h.astype(BF16), w_ref[...])
    nf = zf_ref.shape[2]
    zf_ref[0] = z[:, :nf]
    if activate:
        wd = nf // 2
        col = lambda j: z[:, nf + j * wd:nf + (j + 1) * wd]
        zr_ref[0, :, 0:wd] = col(0).astype(BF16)
        zr_ref[0, :, wd:2 * wd] = _silu(col(1)).astype(BF16)
        zr_ref[0, :, 2 * wd:3 * wd] = _silu(col(2)).astype(BF16)
        zr_ref[0, :, 3 * wd:] = (col(3) * jax.nn.sigmoid(col(4))).astype(BF16)
    else:
        zr_ref[0] = z[:, nf:].astype(BF16)


def _inproj(x, g, sc, sh, w, nf, activate):
    bsz, l, d = x.shape
    nc = w.shape[1]
    tm = min(INPROJ_TM, l)
    nr = nc - nf - (nf // 2 if activate else 0)
    return pl.pallas_call(
        functools.partial(_inproj_kernel, activate=activate),
        grid=(bsz, l // tm),
        in_specs=[pl.BlockSpec((1, tm, d), lambda b, i: (b, i, 0)),
                  pl.BlockSpec((1, d), lambda b, i: (0, 0)),
                  pl.BlockSpec((1, 1, d), lambda b, i: (b, 0, 0)),
                  pl.BlockSpec((1, 1, d), lambda b, i: (b, 0, 0)),
                  pl.BlockSpec((d, nc), lambda b, i: (0, 0))],
        out_specs=[pl.BlockSpec((1, tm, nf), lambda b, i: (b, i, 0)),
                   pl.BlockSpec((1, tm, nr), lambda b, i: (b, i, 0))],
        out_shape=[jax.ShapeDtypeStruct((bsz, l, nf), F32),
                   jax.ShapeDtypeStruct((bsz, l, nr), BF16)],
        compiler_params=_cparams(("arbitrary", "arbitrary")),
        name="inproj",
    )(x, g, sc, sh, w)


class _HgrnDirection:
    def __init__(self, lbl, st_ref, reverse):
        self.st_ref, self.reverse = st_ref, reverse
        e = jnp.exp(lbl - jnp.max(lbl, axis=0, keepdims=True))
        self.lb = e[0:1] / jnp.sum(e, axis=0, keepdims=True)
        row = lax.broadcasted_iota(I32, (CHUNK, CHUNK), 0)
        col = lax.broadcasted_iota(I32, (CHUNK, CHUNK), 1)
        self.tri = (col >= row) if reverse else (col <= row)
        tm = jnp.where(self.tri, 1.0, 0.0).astype(BF16)
        self.tm2 = jnp.concatenate([tm, tm], axis=1)

    def tile(self, f_ref, v_ref, q_ref=None, o_ref=None):
        heads = self.st_ref.shape[0]
        n_chunks = f_ref.shape[1] // CHUNK
        crow = lambda c: slice(c * CHUNK, (c + 1) * CHUNK)
        hsl = lambda h: slice(h * HEAD_DIM, (h + 1) * HEAD_DIM)
        f = self.lb + (1.0 - self.lb) * jax.nn.sigmoid(f_ref[0])
        k = 1.0 - f
        hi, lo = _split2(jnp.log2(f))
        cums = [_dot(self.tm2, jnp.concatenate([hi[crow(c)], lo[crow(c)]], axis=0))
                for c in range(n_chunks)]
        lasts = [cm[0:1] if self.reverse else cm[CHUNK - 1:CHUNK] for cm in cums]
        cum = jnp.concatenate(cums, axis=0)
        dec = [jnp.exp2(la) for la in lasts]
        k_intra = k * jnp.exp2(-cum)
        ks = (k_intra * jnp.concatenate([jnp.broadcast_to(dc, (CHUNK, dc.shape[1])) for dc in dec], axis=0)
              ).astype(BF16)
        vb = v_ref[0].astype(BF16)
        kv = [[_dot_tn(vb[crow(c), hsl(h)], ks[crow(c), hsl(h)]) for h in range(heads)] for c in range(n_chunks)]
        emit = o_ref is not None
        if emit:
            qd = (q_ref[0].astype(F32) * jnp.exp2(cum)).astype(BF16)
            ki = k_intra.astype(BF16)
            sc = [[jnp.where(self.tri, _dot_nt(qd[crow(c), hsl(h)], ki[crow(c), hsl(h)]), 0.0).astype(BF16)
                   for h in range(heads)] for c in range(n_chunks)]
            lhs = [[jnp.concatenate([qd[crow(c), hsl(h)], sc[c][h]], axis=1) for h in range(heads)]
                   for c in range(n_chunks)]
            v32 = v_ref[0].astype(F32)
            vt = [[v32[crow(c), hsl(h)].T.astype(BF16) for h in range(heads)] for c in range(n_chunks)]
        state = [self.st_ref[h] for h in range(heads)]
        for c in (range(n_chunks - 1, -1, -1) if self.reverse else range(n_chunks)):
            for h in range(heads):
                if emit:
                    rhs = jnp.concatenate([state[h].astype(BF16), vt[c][h]], axis=1)
                    o_ref[0, crow(c), hsl(h)] = _dot_nt(lhs[c][h], rhs).astype(o_ref.dtype)
                state[h] = state[h] * dec[c][:, hsl(h)] + kv[c][h]
        for h in range(heads):
            self.st_ref[h] = state[h]


def _hgrn_kernel(lbl_ref, cff_ref, cfb_ref, cv_ref, ff_ref, vf_ref, qf_ref, fb_ref, vb_ref, qb_ref,
                 of_ref, ob_ref, stf_ref, stb_ref):
    fwd = _HgrnDirection(lbl_ref[0], stf_ref, reverse=False)
    bwd = _HgrnDirection(lbl_ref[1], stb_ref, reverse=True)

    @pl.when(pl.program_id(1) == 0)
    def _():
        stf_ref[...] = jnp.zeros_like(stf_ref)
        stb_ref[...] = jnp.zeros_like(stb_ref)
        fwd.tile(cff_ref, cv_ref)
        bwd.tile(cfb_ref, cv_ref)

    fwd.tile(ff_ref, vf_ref, qf_ref, of_ref)
    bwd.tile(fb_ref, vb_ref, qb_ref, ob_ref)


def _hgrn(lbl, zcf, zcr, zf, zr):
    bsz, l, _ = zf.shape
    n_ctx_tok = zcf.shape[1]
    w = lbl.shape[2]
    heads = w // HEAD_DIM
    tt = min(HGRN_TT, l)
    nt = l // tt
    ctx_col = lambda c: pl.BlockSpec((1, n_ctx_tok, w), lambda b, i: (b, 0, c))
    fwd_col = lambda c: pl.BlockSpec((1, tt, w), lambda b, i: (b, i, c))
    bwd_col = lambda c: pl.BlockSpec((1, tt, w), lambda b, i: (b, nt - 1 - i, c))
    state = pltpu.VMEM((heads, HEAD_DIM, HEAD_DIM), F32)
    return pl.pallas_call(
        _hgrn_kernel,
        grid=(bsz, nt),
        in_specs=[pl.BlockSpec(lbl.shape, lambda b, i: (0, 0, 0)),
                  ctx_col(0), ctx_col(1), ctx_col(0),
                  fwd_col(0), fwd_col(0), fwd_col(1),
                  bwd_col(1), bwd_col(0), bwd_col(1)],
        out_specs=[fwd_col(0), bwd_col(0)],
        out_shape=[jax.ShapeDtypeStruct((bsz, l, w), BF16), jax.ShapeDtypeStruct((bsz, l, w), BF16)],
        scratch_shapes=[state, state],
        compiler_params=_cparams(("arbitrary", "arbitrary")),
        name="hgrn",
    )(lbl, zcf, zcf, zcr, zf, zr, zr, zf, zr, zr)


def _mix_kernel(of_ref, ob_ref, zg_ref, zu_ref, x_ref, g1_ref, sc2_ref, sh2_ref,
                hg_ref, dwk_ref, dwb_ref, lng_ref, lnb_ref, wo_ref, n2g_ref, wr_ref, br_ref, shm_ref,
                xm_ref, h2_ref, eid_ref, wts_ref, rank_ref, cnt_ref,
                shf_ref, cv_ref, carry_ref, hh_ref, hl_ref, *, tm):
    w = of_ref.shape[2]
    heads = w // HEAD_DIM
    step = pl.program_id(0)

    @pl.when(step == 0)
    def _():
        carry_ref[...] = jnp.zeros_like(carry_ref)
        hh_ref[...] = jnp.zeros_like(hh_ref)
        hl_ref[...] = jnp.zeros_like(hl_ref)

    router = _route(hh_ref[...], hl_ref[...], wr_ref, br_ref, carry_ref, jnp.where(step > 0, 1.0, 0.0),
                    eid_ref, wts_ref, rank_ref, cnt_ref)
    next(router)

    o = of_ref[0].astype(F32) + ob_ref[0].astype(F32)
    parts = []
    for h in range(heads):
        oh = o[:, h * HEAD_DIM:(h + 1) * HEAD_DIM]
        parts.append(oh * lax.rsqrt(jnp.mean(oh * oh, axis=-1, keepdims=True) + EPS))
    o_mix = jnp.concatenate(parts, axis=-1) * hg_ref[...] * zg_ref[0].astype(F32)

    u = zu_ref[0].astype(F32)
    span = shf_ref.shape[2]
    u_hi, u_lo = _split2(u)
    for s in range(tm // GRID_W):
        rows = slice(s * GRID_W, (s + 1) * GRID_W)
        shifted = _dot(shm_ref[...], jnp.concatenate([u_hi[rows], u_lo[rows]], axis=0))
        for b in range(8):
            shf_ref[s, b] = shifted[b * span:(b + 1) * span]
    next(router)
    lanes = CONV_LANES
    for s in range(tm // GRID_W):
        for c0 in range(0, u.shape[1], lanes):
            cs = slice(c0, c0 + lanes)
            acc = jnp.zeros((GRID_W // 8, 8, lanes), F32) + dwb_ref[:, cs][None]
            for k in range(CONV_K):
                a, b = divmod(CONV_LEAD - CONV_PAD + k, 8)
                tap = shf_ref[s, b, 8 * a:8 * a + GRID_W, cs].reshape(GRID_W // 8, 8, lanes)
                acc = acc + tap * dwk_ref[k, :, cs][None]
            cv_ref[s * GRID_W:(s + 1) * GRID_W, cs] = acc.reshape(GRID_W, lanes)
        if s == 0:
            next(router)
    cv = cv_ref[...]
    mu = jnp.mean(cv, axis=-1, keepdims=True)
    cen = cv - mu
    var = jnp.mean(cen * cen, axis=-1, keepdims=True)
    c_mix = _silu(cen * lax.rsqrt(var + EPS) * lng_ref[...] + lnb_ref[...])
    next(router, None)

    mix = _dot(jnp.concatenate([o_mix.astype(BF16), c_mix.astype(BF16)], axis=1), wo_ref[...])
    xm = x_ref[0] + g1_ref[0] * mix
    xm_ref[0] = xm

    ms = jnp.mean(xm * xm, axis=-1, keepdims=True)
    h2 = xm * lax.rsqrt(ms + EPS) * (n2g_ref[...] * (1.0 + sc2_ref[0])) + sh2_ref[0]
    hh = h2.astype(BF16)
    h2_rounded = hh.astype(F32)
    h2_ref[0] = _pack_halves(h2_rounded)
    hh_ref[...] = hh
    hl_ref[...] = (h2 - h2_rounded).astype(BF16)


def _route(hh, hl, wr_ref, br_ref, carry_ref, live, eid_ref, wts_ref, rank_ref, cnt_ref):
    tm = hh.shape[0]
    wr = wr_ref[...]
    wh = wr.astype(BF16)
    wl = (wr - wh.astype(F32)).astype(BF16)
    lg = _dot_nt(wh, hh) + _dot_nt(wh, hl) + _dot_nt(wl, hh) + br_ref[...]
    yield

    neg = -jnp.inf
    r8 = lax.broadcasted_iota(I32, (EXPERTS_PER_GROUP, tm), 0).astype(F32)
    gl = jnp.where(r8 < N_GROUPS, lg[0:8], neg)
    gmax = jnp.max(gl, axis=0, keepdims=True)
    grp = jnp.min(jnp.where(gl == gmax, r8, 8.0), axis=0, keepdims=True)
    p_grp = 1.0 / jnp.sum(jnp.exp(gl - gmax), axis=0, keepdims=True)
    es = lg[8:16]
    for g in range(1, N_GROUPS):
        es = jnp.where(grp == float(g), lg[8 + 8 * g:16 + 8 * g], es)
    m1 = jnp.max(es, axis=0, keepdims=True)
    i1 = jnp.min(jnp.where(es == m1, r8, 8.0), axis=0, keepdims=True)
    es2 = jnp.where(r8 == i1, neg, es)
    m2 = jnp.max(es2, axis=0, keepdims=True)
    i2 = jnp.min(jnp.where(es2 == m2, r8, 8.0), axis=0, keepdims=True)
    e2 = jnp.exp(m2 - m1)
    w1 = p_grp / (1.0 + e2)
    w2 = p_grp * e2 / (1.0 + e2)
    eid0 = grp * float(EXPERTS_PER_GROUP) + i1
    eid1 = grp * float(EXPERTS_PER_GROUP) + i2
    eid_ref[0:1, :] = eid0.astype(I32)
    eid_ref[1:2, :] = eid1.astype(I32)
    wts_ref[0:1, :] = w1
    wts_ref[1:2, :] = w2
    yield

    r32 = lax.broadcasted_iota(I32, (N_EXPERTS, tm), 0).astype(F32)
    oh0 = jnp.where(r32 == eid0, live, 0.0)
    oh1 = jnp.where(r32 == eid1, live, 0.0)
    src = lax.broadcasted_iota(I32, (tm, tm), 0)
    dst = lax.broadcasted_iota(I32, (tm, tm), 1)
    before = jnp.where(src < dst, 1.0, 0.0).astype(BF16)
    ex = _dot(jnp.concatenate([oh0, oh1], axis=0).astype(BF16), before)
    yield
    base = ex[0:N_EXPERTS] + ex[N_EXPERTS:] + carry_ref[...]
    rank_ref[0:1, :] = jnp.sum(oh0 * base, axis=0, keepdims=True).astype(I32)
    rank_ref[1:2, :] = jnp.sum(oh1 * base, axis=0, keepdims=True).astype(I32)
    carry = carry_ref[...] + jnp.sum(oh0 + oh1, axis=1, keepdims=True)
    carry_ref[...] = carry
    cnt_ref[...] = jnp.broadcast_to(carry, cnt_ref.shape)


def _conv_shift_matrix():
    r = np.arange(8 * CONV_SPAN)
    src = r // CONV_SPAN + r % CONV_SPAN - CONV_LEAD
    sel = (src[:, None] == np.arange(GRID_W)[None, :]).astype(np.float32)
    return jnp.asarray(np.concatenate([sel, sel], axis=1), dtype=BF16)


def _mix(o_f, o_b, z, x, g1, sc2, sh2, hg, dwk, dwb, lng, lnb, wo, n2g, wr, br, b0, bsz):
    _, l, d = x.shape
    shm = _conv_shift_matrix()
    w = o_f.shape[2]
    cw = dwk.shape[-1]
    tm = min(MIX_TM, l)
    nt = l // tm
    n = bsz * l
    tiles = bsz * nt
    mixed = lambda s: jnp.minimum(s, tiles - 1)
    src = lambda c: (lambda s: (b0 + mixed(s) // nt, mixed(s) % nt, c))
    tok = lambda s: (mixed(s) // nt, mixed(s) % nt, 0)
    vec = lambda s: (0, 0)
    per_b = lambda s: (b0 + mixed(s) // nt, 0, 0)
    flat = lambda s: (0, jnp.maximum(s - 1, 0))
    kern = functools.partial(_mix_kernel, tm=tm)
    return pl.pallas_call(
        kern,
        grid=(tiles + 1,),
        in_specs=[pl.BlockSpec((1, tm, w), src(0)),
                  pl.BlockSpec((1, tm, w), src(0)),
                  pl.BlockSpec((1, tm, w), src(2)),
                  pl.BlockSpec((1, tm, cw), src(3)),
                  pl.BlockSpec((1, tm, d), src(0)),
                  pl.BlockSpec((1, 1, d), per_b),
                  pl.BlockSpec((1, 1, d), per_b),
                  pl.BlockSpec((1, 1, d), per_b),
                  pl.BlockSpec((1, w), vec),
                  pl.BlockSpec((CONV_K, 8, cw), lambda s: (0, 0, 0)),
                  pl.BlockSpec((8, cw), vec),
                  pl.BlockSpec((1, cw), vec),
                  pl.BlockSpec((1, cw), vec),
                  pl.BlockSpec((w + cw, d), vec),
                  pl.BlockSpec((1, d), vec),
                  pl.BlockSpec((ROUTER_ROWS, d), vec),
                  pl.BlockSpec((ROUTER_ROWS, 1), vec),
                  pl.BlockSpec(shm.shape, vec)],
        out_specs=[pl.BlockSpec((1, tm, d), tok),
                   pl.BlockSpec((1, tm, d // 2), tok),
                   pl.BlockSpec((TOP_K, tm), flat),
                   pl.BlockSpec((TOP_K, tm), flat),
                   pl.BlockSpec((TOP_K, tm), flat),
                   pl.BlockSpec((N_EXPERTS, 128), vec)],
        out_shape=[jax.ShapeDtypeStruct((bsz, l, d), F32),
                   jax.ShapeDtypeStruct((bsz, l, d // 2), F32),
                   jax.ShapeDtypeStruct((TOP_K, n), I32),
                   jax.ShapeDtypeStruct((TOP_K, n), F32),
                   jax.ShapeDtypeStruct((TOP_K, n), I32),
                   jax.ShapeDtypeStruct((N_EXPERTS, 128), F32)],
        scratch_shapes=[pltpu.VMEM((tm // GRID_W, 8, CONV_SPAN, cw), F32),
                        pltpu.VMEM((tm, cw), F32),
                        pltpu.VMEM((N_EXPERTS, 1), F32),
                        pltpu.VMEM((tm, d), BF16),
                        pltpu.VMEM((tm, d), BF16)],
        compiler_params=_cparams(("arbitrary",)),
        name="mix",
    )(o_f, o_b, z, z, x, g1, sc2, sh2, hg, dwk, dwb, lng, lnb, wo, n2g, wr, br, shm)


def _pos_kernel(ps_ref, eid_ref, rank_ref, pos_ref):
    eid = eid_ref[...]
    acc = rank_ref[...]
    for e in range(N_EXPERTS):
        acc = acc + jnp.where(eid == e, ps_ref[e], 0)
    pos_ref[...] = acc


def _positions(pstarts, eid, rank):
    shape = eid.shape
    eid2 = eid.reshape(-1, 128)
    full = pl.BlockSpec(eid2.shape, lambda i: (0, 0))
    pos = pl.pallas_call(
        _pos_kernel,
        grid=(1,),
        in_specs=[pl.BlockSpec(memory_space=pltpu.SMEM), full, full],
        out_specs=full,
        out_shape=jax.ShapeDtypeStruct(eid2.shape, I32),
        name="positions",
    )(pstarts, eid2, rank.reshape(-1, 128))
    return pos.reshape(shape)


def _sc_mesh():
    return plsc.VectorSubcoreMesh(core_axis_name="c", subcore_axis_name="s",
                                  num_cores=SC_CORES, num_subcores=SC_SUBCORES)


def _sc_worker():
    return lax.axis_index("s") * SC_CORES + lax.axis_index("c")


def _sc_dispatch(h2p, pos4, padidx, zero_rows, n_rows):
    n, dw = h2p.shape
    _, workers, chunks, _ = pos4.shape
    pad_chunks = padidx.shape[1]
    per_worker = n // workers

    def body(h_hbm, pos_hbm, pad_hbm, zero_hbm, out_hbm, idx_v, pad_v, rows_v):
        wid = _sc_worker()
        base = wid * per_worker
        for k in range(TOP_K):
            pltpu.sync_copy(pos_hbm.at[k, wid], idx_v.at[k])
        pltpu.sync_copy(pad_hbm.at[wid], pad_v)
        pltpu.sync_copy(zero_hbm, rows_v)
        for j in range(pad_chunks):
            pltpu.sync_copy(rows_v, out_hbm.at[pad_v.at[j]])
        for j in range(chunks):
            pltpu.sync_copy(h_hbm.at[pl.ds(base + j * SC_W, SC_W)], rows_v)
            for k in range(TOP_K):
                pltpu.sync_copy(rows_v, out_hbm.at[idx_v.at[k, j]])

    return pl.kernel(
        body,
        out_type=jax.ShapeDtypeStruct((n_rows, dw), F32),
        mesh=_sc_mesh(),
        scratch_types=[pltpu.VMEM((TOP_K, chunks, SC_W), I32),
                       pltpu.VMEM((pad_chunks, SC_W), I32),
                       pltpu.VMEM((SC_W, dw), F32)],
        name="sc_dispatch",
    )(h2p, pos4, padidx, zero_rows)


def _sc_combine_gather(ybuf, pos4):
    _, dw = ybuf.shape
    _, workers, chunks, _ = pos4.shape
    per_worker = chunks * SC_W

    def body(y_hbm, pos_hbm, out_hbm, idx_v, rows_v):
        wid = _sc_worker()
        base = wid * per_worker
        for k in range(TOP_K):
            pltpu.sync_copy(pos_hbm.at[k, wid], idx_v.at[k])
        for k in range(TOP_K):
            for j in range(chunks):
                pltpu.sync_copy(y_hbm.at[idx_v.at[k, j]], rows_v)
                pltpu.sync_copy(rows_v, out_hbm.at[k, pl.ds(base + j * SC_W, SC_W)])

    return pl.kernel(
        body,
        out_type=jax.ShapeDtypeStruct((TOP_K, workers * per_worker, dw), F32),
        mesh=_sc_mesh(),
        scratch_types=[pltpu.VMEM((TOP_K, chunks, SC_W), I32),
                       pltpu.VMEM((SC_W, dw), F32)],
        name="sc_combine",
    )(ybuf, pos4)


def _ffn_kernel(blk_e_ref, nblk_ref, x_ref, wg_ref, wu_ref, wd_ref, y_ref):
    del blk_e_ref
    used = pl.program_id(0) < nblk_ref[0]
    half = x_ref.shape[1]

    @pl.when(used)
    def _():
        x_lo, x_hi = _unpack_halves(x_ref[...])
        bf = lambda wv: wv.astype(BF16)
        g = _dot(x_lo, bf(wg_ref[0, 0:half, :])) + _dot(x_hi, bf(wg_ref[0, half:, :]))
        u = _dot(x_lo, bf(wu_ref[0, 0:half, :])) + _dot(x_hi, bf(wu_ref[0, half:, :]))
        y = _dot((_silu(g) * u).astype(BF16), bf(wd_ref[0]))
        y_ref[...] = _pack_halves(y.astype(BF16).astype(F32))

    @pl.when(jnp.logical_not(used))
    def _():
        y_ref[...] = jnp.zeros_like(y_ref)


def _ffn(blk_e, nblk, xbuf, wg, wu, wd, tb):
    _, dw = xbuf.shape
    _, d, de = wg.shape
    n_blk = blk_e.shape[0]
    x_blk = lambda i, be, nb: (jnp.minimum(i, nb[0] - 1), 0)
    grid_spec = pltpu.PrefetchScalarGridSpec(
        num_scalar_prefetch=2,
        grid=(n_blk,),
        in_specs=[pl.BlockSpec((tb, dw), x_blk),
                  pl.BlockSpec((1, d, de), lambda i, be, nb: (be[i], 0, 0)),
                  pl.BlockSpec((1, d, de), lambda i, be, nb: (be[i], 0, 0)),
                  pl.BlockSpec((1, de, d), lambda i, be, nb: (be[i], 0, 0))],
        out_specs=pl.BlockSpec((tb, dw), lambda i, be, nb: (i, 0)),
    )
    return pl.pallas_call(
        _ffn_kernel,
        grid_spec=grid_spec,
        out_shape=jax.ShapeDtypeStruct((n_blk * tb, dw), F32),
        compiler_params=_cparams(("arbitrary",)),
        name="ffn",
    )(blk_e, nblk, xbuf, wg, wu, wd)


def _final_kernel(yg_ref, xm_ref, wt_ref, g2_ref, fg_ref, *out_refs):
    o_ref = out_refs[-1]
    wt = wt_ref[...]
    y = None
    for k in range(TOP_K):
        lo, hi = _unpack_halves(yg_ref[k])
        yk = jnp.concatenate([lo, hi], axis=-1).astype(F32) * wt[:, k:k + 1]
        y = yk if y is None else y + yk
    x = xm_ref[0] + g2_ref[0] * y
    ms = jnp.mean(x * x, axis=-1, keepdims=True)
    o_ref[0] = x * lax.rsqrt(ms + EPS) * fg_ref[...]


def _final(yg, xm, wt, g2, fg, out_prev, b0, total):
    bsz, l, d = xm.shape
    dw = yg.shape[2]
    tm = min(MOE_TM, l)
    nt = l // tm
    in_specs = [pl.BlockSpec((TOP_K, tm, dw), lambda b, i: (0, b * nt + i, 0)),
                pl.BlockSpec((1, tm, d), lambda b, i: (b, i, 0)),
                pl.BlockSpec((tm, TOP_K), lambda b, i: (b * nt + i, 0)),
                pl.BlockSpec((1, 1, d), lambda b, i: (b0 + b, 0, 0)),
                pl.BlockSpec((1, d), lambda b, i: (0, 0))]
    args = [yg, xm, wt, g2, fg]
    aliases = {}
    if out_prev is not None:
        in_specs.append(pl.BlockSpec(memory_space=pl.ANY))
        args.append(out_prev)
        aliases = {5: 0}
    return pl.pallas_call(
        _final_kernel,
        grid=(bsz, nt),
        in_specs=in_specs,
        out_specs=pl.BlockSpec((1, tm, d), lambda b, i: (b0 + b, i, 0)),
        out_shape=jax.ShapeDtypeStruct((total, l, d), F32),
        input_output_aliases=aliases,
        compiler_params=_cparams(("arbitrary", "arbitrary")),
        name="final",
    )(*args)


def _layer(x, ctx, mod, mod_c, norm1_g, w_in, lb_logits, hgrn_norm_g, dw_kernel, dw_bias, conv_ln_g,
           conv_ln_b, w_out, norm2_g, wr, br, w_gate, w_up, w_down, final_norm_g):
    bsz, l, d = x.shape
    w = lb_logits.shape[2]
    heads = w // HEAD_DIM
    row = lambda v: v.reshape(1, -1)
    per_b = lambda v: v.reshape(bsz, 1, d)
    sh1, sc1, g1, sh2, sc2, g2 = [per_b(mod[:, j]) for j in range(N_MOD)]
    csh1 = jnp.broadcast_to(mod_c[0].reshape(1, 1, d), (bsz, 1, d))
    csc1 = jnp.broadcast_to(mod_c[1].reshape(1, 1, d), (bsz, 1, d))

    w_in_b = w_in.astype(BF16)
    zf, zr = _inproj(x, row(norm1_g), sc1, sh1, w_in_b, 2 * w, True)
    zcf, zcr = _inproj(ctx, row(norm1_g), csc1, csh1, w_in_b[:, :3 * w], 2 * w, False)
    o_f, o_b = _hgrn(lb_logits, zcf, zcr, zf, zr)
    w_out_b = w_out.astype(BF16)

    groups = MOE_GROUPS if bsz % MOE_GROUPS == 0 else 1
    out = None
    for gi in range(groups):
        b0, gb = gi * (bsz // groups), bsz // groups
        rep8 = lambda v: jnp.broadcast_to(v[..., None, :], v.shape[:-1] + (8, v.shape[-1]))
        mixed = _mix(o_f, o_b, zr, x, g1, sc2, sh2, row(jnp.tile(hgrn_norm_g, heads)), rep8(dw_kernel),
                     rep8(dw_bias), row(conv_ln_g), row(conv_ln_b), w_out_b, row(norm2_g), wr, br, b0, gb)
        out = _moe_group(mixed, g2, w_gate, w_up, w_down, row(final_norm_g), out, b0, bsz)
    return out


def _moe_group(mixed, g2, w_gate, w_up, w_down, final_g, out_prev, b0, total):
    xm, h2, eid, wts, rank, cnt = mixed
    gb, l, d = xm.shape
    n = gb * l

    tb = FFN_TB
    n_blk = -(-(n * TOP_K) // tb) + N_EXPERTS
    counts = cnt[:, 0].astype(I32)
    pcounts = (counts + tb - 1) // tb * tb
    pends = jnp.cumsum(pcounts)
    pstarts = pends - pcounts
    pos = _positions(pstarts.astype(I32), eid, rank)
    blk_start = jnp.arange(n_blk, dtype=I32) * tb
    blk_e = jnp.minimum(jnp.sum((pends[None, :] <= blk_start[:, None]).astype(I32), axis=1), N_EXPERTS - 1)
    nblk_used = (pends[-1:] // tb).astype(I32)

    workers = SC_CORES * SC_SUBCORES
    pos4 = pos.reshape(TOP_K, workers, n // (workers * SC_W), SC_W)
    lane = jnp.arange(tb, dtype=I32)[None, :]
    pad_rows = jnp.where(lane < (pcounts - counts)[:, None], (pstarts + counts)[:, None] + lane,
                         n_blk * tb + lane)
    padidx = pad_rows.astype(I32).reshape(workers, N_EXPERTS // workers * tb // SC_W, SC_W)
    xbuf = _sc_dispatch(h2.reshape(n, d // 2), pos4, padidx, jnp.zeros((SC_W, d // 2), F32), (n_blk + 1) * tb)
    ybuf = _ffn(blk_e, nblk_used, xbuf, w_gate, w_up, w_down, tb)
    yg = _sc_combine_gather(ybuf, pos4)
    return _final(yg, xm, wts.T, g2, final_g, out_prev, b0, total)


def kernel(x, c, ctx, c_ctx, w_ada, b_ada, norm1_g, w_in, lb_logits, hgrn_norm_g, dw_kernel, dw_bias,
           conv_ln_g, conv_ln_b, w_out, norm2_g, router_group_w, router_group_b, router_expert_w,
           router_expert_b, w_expert_gate, w_expert_up, w_expert_down, final_norm_g):
    depth = w_ada.shape[0]
    assert depth == 1, "context tokens are only updated between layers; a single layer is implemented"
    bsz, l, d = x.shape
    l0 = 0
    rows = -(-(bsz + 1) // 8) * 8
    cc = jnp.zeros((rows, d), F32).at[:bsz].set(c).at[bsz].set(c_ctx)
    mod_all = _ada(cc, w_ada[l0], b_ada[l0].reshape(1, -1))
    mod = mod_all[:bsz].reshape(bsz, N_MOD, d)
    mod_c = mod_all[bsz].reshape(N_MOD, d)
    pad8 = jnp.zeros((d, 8 - N_GROUPS), F32)
    wr = jnp.concatenate([router_group_w[l0], pad8, router_expert_w[l0], jnp.zeros((d, 8), F32)], axis=1).T
    br = jnp.concatenate([router_group_b[l0], jnp.zeros((8 - N_GROUPS,), F32), router_expert_b[l0],
                          jnp.zeros((8,), F32)]).reshape(ROUTER_ROWS, 1)
    return _layer(x, ctx, mod, mod_c, norm1_g[l0], w_in[l0], lb_logits, hgrn_norm_g[l0], dw_kernel[l0], dw_bias[l0], conv_ln_g[l0], conv_ln_b[l0], w_out[l0],
                  norm2_g[l0], wr, br, w_expert_gate[l0], w_expert_up[l0], w_expert_down[l0], final_norm_g)
```

```python
import functools

import numpy as np

import jax
import jax.numpy as jnp
from jax import lax
from jax.experimental import pallas as pl
from jax.experimental.pallas import tpu as pltpu
from jax.experimental.pallas import tpu_sc as plsc

F32 = jnp.float32
BF16 = jnp.bfloat16
I32 = jnp.int32
U32 = jnp.uint32

EPS = 1e-6
HEAD_DIM = 128
CHUNK = 64
GRID_W = 64
CONV_K = 31
CONV_PAD = CONV_K // 2
CONV_LANES = 256
CONV_LEAD = 16
CONV_SPAN = GRID_W + 24
N_GROUPS = 4
EXPERTS_PER_GROUP = 8
N_EXPERTS = N_GROUPS * EXPERTS_PER_GROUP
TOP_K = 2
N_MOD = 6
ROUTER_ROWS = 48

ADA_TN = 1024
INPROJ_TM = 1024
HGRN_TT = 1024
MIX_TM = 256
MOE_TM = 1024
FFN_TB = 1024
VMEM_LIMIT = 48 * 1024 * 1024
SC_CORES = 2
SC_SUBCORES = 16
SC_W = 128
MOE_GROUPS = 1


def _cparams(sem):
    return pltpu.CompilerParams(dimension_semantics=sem, vmem_limit_bytes=VMEM_LIMIT)


def _silu(v):
    return v * jax.nn.sigmoid(v)


def _dot(a, b):
    return jnp.dot(a, b, preferred_element_type=F32)


def _dot_nt(a, b):
    return lax.dot_general(a, b, (((1,), (1,)), ((), ())), preferred_element_type=F32)


def _dot_tn(a, b):
    return lax.dot_general(a, b, (((0,), (0,)), ((), ())), preferred_element_type=F32)


def _pack_halves(vr):
    m = vr.shape[1] // 2
    bits = lax.bitcast_convert_type(vr, U32)
    return lax.bitcast_convert_type((bits[:, :m] >> 16) | bits[:, m:], F32)


def _unpack_halves(p):
    u = lax.bitcast_convert_type(p, U32)
    lo = lax.bitcast_convert_type(u << 16, F32)
    hi = lax.bitcast_convert_type(u & jnp.uint32(0xFFFF0000), F32)
    return lo.astype(BF16), hi.astype(BF16)


def _split2(a):
    hi = a.astype(BF16)
    lo = (a - hi.astype(F32)).astype(BF16)
    return hi, lo


def _ada_kernel(c_ref, w_ref, b_ref, o_ref):
    s = _silu(c_ref[...]).astype(BF16)
    o_ref[...] = _dot(s, w_ref[...].astype(BF16)) + b_ref[...]


def _ada(cc, w, b):
    r, d = cc.shape
    nc = w.shape[1]
    return pl.pallas_call(
        _ada_kernel,
        grid=(nc // ADA_TN,),
        in_specs=[pl.BlockSpec((r, d), lambda j: (0, 0)),
                  pl.BlockSpec((d, ADA_TN), lambda j: (0, j)),
                  pl.BlockSpec((1, ADA_TN), lambda j: (0, j))],
        out_specs=pl.BlockSpec((r, ADA_TN), lambda j: (0, j)),
        out_shape=jax.ShapeDtypeStruct((r, nc), F32),
        compiler_params=_cparams(("arbitrary",)),
        name="ada",
    )(cc, w, b)


def _inproj_kernel(x_ref, g_ref, sc_ref, sh_ref, w_ref, zf_ref, zr_ref, *, activate):
    x = x_ref[0]
    ms = jnp.mean(x * x, axis=-1, keepdims=True)
    gs = g_ref[...] * (1.0 + sc_ref[0])
    h = x * lax.rsqrt(ms + EPS) * gs + sh_ref[0]
    z = _dot(h.astype(BF16), w_ref[...])
    nf = zf_ref.shape[2]
    zf_ref[0] = z[:, :nf]
    if activate:
        wd = nf // 2
        col = lambda j: z[:, nf + j * wd:nf + (j + 1) * wd]
        zr_ref[0, :, 0:wd] = col(0).astype(BF16)
        zr_ref[0, :, wd:2 * wd] = _silu(col(1)).astype(BF16)
        zr_ref[0, :, 2 * wd:3 * wd] = _silu(col(2)).astype(BF16)
        zr_ref[0, :, 3 * wd:] = (col(3) * jax.nn.sigmoid(col(4))).astype(BF16)
    else:
        zr_ref[0] = z[:, nf:].astype(BF16)


def _inproj(x, g, sc, sh, w, nf, activate):
    bsz, l, d = x.shape
    nc = w.shape[1]
    tm = min(INPROJ_TM, l)
    nr = nc - nf - (nf // 2 if activate else 0)
    return pl.pallas_call(
        functools.partial(_inproj_kernel, activate=activate),
        grid=(bsz, l // tm),
        in_specs=[pl.BlockSpec((1, tm, d), lambda b, i: (b, i, 0)),
                  pl.BlockSpec((1, d), lambda b, i: (0, 0)),
                  pl.BlockSpec((1, 1, d), lambda b, i: (b, 0, 0)),
                  pl.BlockSpec((1, 1, d), lambda b, i: (b, 0, 0)),
                  pl.BlockSpec((d, nc), lambda b, i: (0, 0))],
        out_specs=[pl.BlockSpec((1, tm, nf), lambda b, i: (b, i, 0)),
                   pl.BlockSpec((1, tm, nr), lambda b, i: (b, i, 0))],
        out_shape=[jax.ShapeDtypeStruct((bsz, l, nf), F32),
                   jax.ShapeDtypeStruct((bsz, l, nr), BF16)],
        compiler_params=_cparams(("arbitrary", "arbitrary")),
        name="inproj",
    )(x, g, sc, sh, w)


class _HgrnDirection:
    def __init__(self, lbl, st_ref, reverse):
        self.st_ref, self.reverse = st_ref, reverse
        e = jnp.exp(lbl - jnp.max(lbl, axis=0, keepdims=True))
        self.lb = e[0:1] / jnp.sum(e, axis=0, keepdims=True)
        row = lax.broadcasted_iota(I32, (CHUNK, CHUNK), 0)
        col = lax.broadcasted_iota(I32, (CHUNK, CHUNK), 1)
        self.tri = (col >= row) if reverse else (col <= row)
        tm = jnp.where(self.tri, 1.0, 0.0).astype(BF16)
        self.tm2 = jnp.concatenate([tm, tm], axis=1)

    def tile(self, f_ref, v_ref, q_ref=None, o_ref=None):
        heads = self.st_ref.shape[0]
        n_chunks = f_ref.shape[1] // CHUNK
        crow = lambda c: slice(c * CHUNK, (c + 1) * CHUNK)
        hsl = lambda h: slice(h * HEAD_DIM, (h + 1) * HEAD_DIM)
        f = self.lb + (1.0 - self.lb) * jax.nn.sigmoid(f_ref[0])
        k = 1.0 - f
        hi, lo = _split2(jnp.log2(f))
        cums = [_dot(self.tm2, jnp.concatenate([hi[crow(c)], lo[crow(c)]], axis=0))
                for c in range(n_chunks)]
        lasts = [cm[0:1] if self.reverse else cm[CHUNK - 1:CHUNK] for cm in cums]
        cum = jnp.concatenate(cums, axis=0)
        dec = [jnp.exp2(la) for la in lasts]
        k_intra = k * jnp.exp2(-cum)
        ks = (k_intra * jnp.concatenate([jnp.broadcast_to(dc, (CHUNK, dc.shape[1])) for dc in dec], axis=0)
              ).astype(BF16)
        vb = v_ref[0].astype(BF16)
        kv = [[_dot_tn(vb[crow(c), hsl(h)], ks[crow(c), hsl(h)]) for h in range(heads)] for c in range(n_chunks)]
        emit = o_ref is not None
        if emit:
            qd = (q_ref[0].astype(F32) * jnp.exp2(cum)).astype(BF16)
            ki = k_intra.astype(BF16)
            sc = [[jnp.where(self.tri, _dot_nt(qd[crow(c), hsl(h)], ki[crow(c), hsl(h)]), 0.0).astype(BF16)
                   for h in range(heads)] for c in range(n_chunks)]
            lhs = [[jnp.concatenate([qd[crow(c), hsl(h)], sc[c][h]], axis=1) for h in range(heads)]
                   for c in range(n_chunks)]
            vt = [[vb[crow(c), hsl(h)].T for h in range(heads)] for c in range(n_chunks)]
        state = [self.st_ref[h] for h in range(heads)]
        for c in (range(n_chunks - 1, -1, -1) if self.reverse else range(n_chunks)):
            for h in range(heads):
                if emit:
                    rhs = jnp.concatenate([state[h].astype(BF16), vt[c][h]], axis=1)
                    o_ref[0, crow(c), hsl(h)] = _dot_nt(lhs[c][h], rhs).astype(o_ref.dtype)
                state[h] = state[h] * dec[c][:, hsl(h)] + kv[c][h]
        for h in range(heads):
            self.st_ref[h] = state[h]


def _hgrn_kernel(lbl_ref, cff_ref, cfb_ref, cv_ref, ff_ref, vf_ref, qf_ref, fb_ref, vb_ref, qb_ref,
                 of_ref, ob_ref, stf_ref, stb_ref):
    fwd = _HgrnDirection(lbl_ref[0], stf_ref, reverse=False)
    bwd = _HgrnDirection(lbl_ref[1], stb_ref, reverse=True)

    @pl.when(pl.program_id(1) == 0)
    def _():
        stf_ref[...] = jnp.zeros_like(stf_ref)
        stb_ref[...] = jnp.zeros_like(stb_ref)
        fwd.tile(cff_ref, cv_ref)
        bwd.tile(cfb_ref, cv_ref)

    fwd.tile(ff_ref, vf_ref, qf_ref, of_ref)
    bwd.tile(fb_ref, vb_ref, qb_ref, ob_ref)


def _hgrn(lbl, zcf, zcr, zf, zr):
    bsz, l, _ = zf.shape
    n_ctx_tok = zcf.shape[1]
    w = lbl.shape[2]
    heads = w // HEAD_DIM
    tt = min(HGRN_TT, l)
    nt = l // tt
    ctx_col = lambda c: pl.BlockSpec((1, n_ctx_tok, w), lambda b, i: (b, 0, c))
    fwd_col = lambda c: pl.BlockSpec((1, tt, w), lambda b, i: (b, i, c))
    bwd_col = lambda c: pl.BlockSpec((1, tt, w), lambda b, i: (b, nt - 1 - i, c))
    state = pltpu.VMEM((heads, HEAD_DIM, HEAD_DIM), F32)
    return pl.pallas_call(
        _hgrn_kernel,
        grid=(bsz, nt),
        in_specs=[pl.BlockSpec(lbl.shape, lambda b, i: (0, 0, 0)),
                  ctx_col(0), ctx_col(1), ctx_col(0),
                  fwd_col(0), fwd_col(0), fwd_col(1),
                  bwd_col(1), bwd_col(0), bwd_col(1)],
        out_specs=[fwd_col(0), bwd_col(0)],
        out_shape=[jax.ShapeDtypeStruct((bsz, l, w), BF16), jax.ShapeDtypeStruct((bsz, l, w), BF16)],
        scratch_shapes=[state, state],
        compiler_params=_cparams(("arbitrary", "arbitrary")),
        name="hgrn",
    )(lbl, zcf, zcf, zcr, zf, zr, zr, zf, zr, zr)


def _mix_kernel(of_ref, ob_ref, zg_ref, zu_ref, x_ref, g1_ref, sc2_ref, sh2_ref,
                hg_ref, dwk_ref, dwb_ref, lng_ref, lnb_ref, wo_ref, n2g_ref, wr_ref, br_ref, shm_ref,
                xm_ref, h2_ref, eid_ref, wts_ref, rank_ref, cnt_ref,
                shf_ref, cv_ref, carry_ref, hh_ref, hl_ref, *, tm):
    w = of_ref.shape[2]
    heads = w // HEAD_DIM
    step = pl.program_id(0)

    @pl.when(step == 0)
    def _():
        carry_ref[...] = jnp.zeros_like(carry_ref)
        hh_ref[...] = jnp.zeros_like(hh_ref)
        hl_ref[...] = jnp.zeros_like(hl_ref)

    router = _route(hh_ref[...], hl_ref[...], wr_ref, br_ref, carry_ref, jnp.where(step > 0, 1.0, 0.0),
                    eid_ref, wts_ref, rank_ref, cnt_ref)
    next(router)

    o = of_ref[0].astype(F32) + ob_ref[0].astype(F32)
    parts = []
    for h in range(heads):
        oh = o[:, h * HEAD_DIM:(h + 1) * HEAD_DIM]
        parts.append(oh * lax.rsqrt(jnp.mean(oh * oh, axis=-1, keepdims=True) + EPS))
    o_mix = jnp.concatenate(parts, axis=-1) * hg_ref[...] * zg_ref[0].astype(F32)

    u = zu_ref[0].astype(F32)
    span = shf_ref.shape[2]
    u_hi, u_lo = _split2(u)
    for s in range(tm // GRID_W):
        rows = slice(s * GRID_W, (s + 1) * GRID_W)
        shifted = _dot(shm_ref[...], jnp.concatenate([u_hi[rows], u_lo[rows]], axis=0))
        for b in range(8):
            shf_ref[s, b] = shifted[b * span:(b + 1) * span]
    next(router)
    lanes = CONV_LANES
    for s in range(tm // GRID_W):
        for c0 in range(0, u.shape[1], lanes):
            cs = slice(c0, c0 + lanes)
            acc = jnp.zeros((GRID_W // 8, 8, lanes), F32) + dwb_ref[:, cs][None]
            for k in range(CONV_K):
                a, b = divmod(CONV_LEAD - CONV_PAD + k, 8)
                tap = shf_ref[s, b, 8 * a:8 * a + GRID_W, cs].reshape(GRID_W // 8, 8, lanes)
                acc = acc + tap * dwk_ref[k, :, cs][None]
            cv_ref[s * GRID_W:(s + 1) * GRID_W, cs] = acc.reshape(GRID_W, lanes)
        if s == 0:
            next(router)
    cv = cv_ref[...]
    mu = jnp.mean(cv, axis=-1, keepdims=True)
    cen = cv - mu
    var = jnp.mean(cen * cen, axis=-1, keepdims=True)
    c_mix = _silu(cen * lax.rsqrt(var + EPS) * lng_ref[...] + lnb_ref[...])
    next(router, None)

    mix = _dot(jnp.concatenate([o_mix.astype(BF16), c_mix.astype(BF16)], axis=1), wo_ref[...])
    xm = x_ref[0] + g1_ref[0] * mix
    xm_ref[0] = xm

    ms = jnp.mean(xm * xm, axis=-1, keepdims=True)
    h2 = xm * lax.rsqrt(ms + EPS) * (n2g_ref[...] * (1.0 + sc2_ref[0])) + sh2_ref[0]
    hh = h2.astype(BF16)
    h2_rounded = hh.astype(F32)
    h2_ref[0] = _pack_halves(h2_rounded)
    hh_ref[...] = hh
    hl_ref[...] = (h2 - h2_rounded).astype(BF16)


def _route(hh, hl, wr_ref, br_ref, carry_ref, live, eid_ref, wts_ref, rank_ref, cnt_ref):
    tm = hh.shape[0]
    wr = wr_ref[...]
    wh = wr.astype(BF16)
    wl = (wr - wh.astype(F32)).astype(BF16)
    lg = _dot_nt(wh, hh) + _dot_nt(wh, hl) + _dot_nt(wl, hh) + br_ref[...]
    yield

    neg = -jnp.inf
    r8 = lax.broadcasted_iota(I32, (EXPERTS_PER_GROUP, tm), 0).astype(F32)
    gl = jnp.where(r8 < N_GROUPS, lg[0:8], neg)
    gmax = jnp.max(gl, axis=0, keepdims=True)
    grp = jnp.min(jnp.where(gl == gmax, r8, 8.0), axis=0, keepdims=True)
    p_grp = 1.0 / jnp.sum(jnp.exp(gl - gmax), axis=0, keepdims=True)
    es = lg[8:16]
    for g in range(1, N_GROUPS):
        es = jnp.where(grp == float(g), lg[8 + 8 * g:16 + 8 * g], es)
    m1 = jnp.max(es, axis=0, keepdims=True)
    i1 = jnp.min(jnp.where(es == m1, r8, 8.0), axis=0, keepdims=True)
    es2 = jnp.where(r8 == i1, neg, es)
    m2 = jnp.max(es2, axis=0, keepdims=True)
    i2 = jnp.min(jnp.where(es2 == m2, r8, 8.0), axis=0, keepdims=True)
    e2 = jnp.exp(m2 - m1)
    w1 = p_grp / (1.0 + e2)
    w2 = p_grp * e2 / (1.0 + e2)
    eid0 = grp * float(EXPERTS_PER_GROUP) + i1
    eid1 = grp * float(EXPERTS_PER_GROUP) + i2
    eid_ref[0:1, :] = eid0.astype(I32)
    eid_ref[1:2, :] = eid1.astype(I32)
    wts_ref[0:1, :] = w1
    wts_ref[1:2, :] = w2
    yield

    r32 = lax.broadcasted_iota(I32, (N_EXPERTS, tm), 0).astype(F32)
    oh0 = jnp.where(r32 == eid0, live, 0.0)
    oh1 = jnp.where(r32 == eid1, live, 0.0)
    src = lax.broadcasted_iota(I32, (tm, tm), 0)
    dst = lax.broadcasted_iota(I32, (tm, tm), 1)
    before = jnp.where(src < dst, 1.0, 0.0).astype(BF16)
    ex = _dot(jnp.concatenate([oh0, oh1], axis=0).astype(BF16), before)
    yield
    base = ex[0:N_EXPERTS] + ex[N_EXPERTS:] + carry_ref[...]
    rank_ref[0:1, :] = jnp.sum(oh0 * base, axis=0, keepdims=True).astype(I32)
    rank_ref[1:2, :] = jnp.sum(oh1 * base, axis=0, keepdims=True).astype(I32)
    carry = carry_ref[...] + jnp.sum(oh0 + oh1, axis=1, keepdims=True)
    carry_ref[...] = carry
    cnt_ref[...] = jnp.broadcast_to(carry, cnt_ref.shape)


def _conv_shift_matrix():
    r = np.arange(8 * CONV_SPAN)
    src = r // CONV_SPAN + r % CONV_SPAN - CONV_LEAD
    sel = (src[:, None] == np.arange(GRID_W)[None, :]).astype(np.float32)
    return jnp.asarray(np.concatenate([sel, sel], axis=1), dtype=BF16)


def _mix(o_f, o_b, z, x, g1, sc2, sh2, hg, dwk, dwb, lng, lnb, wo, n2g, wr, br, b0, bsz):
    _, l, d = x.shape
    shm = _conv_shift_matrix()
    w = o_f.shape[2]
    cw = dwk.shape[-1]
    tm = min(MIX_TM, l)
    nt = l // tm
    n = bsz * l
    tiles = bsz * nt
    mixed = lambda s: jnp.minimum(s, tiles - 1)
    src = lambda c: (lambda s: (b0 + mixed(s) // nt, mixed(s) % nt, c))
    tok = lambda s: (mixed(s) // nt, mixed(s) % nt, 0)
    vec = lambda s: (0, 0)
    per_b = lambda s: (b0 + mixed(s) // nt, 0, 0)
    flat = lambda s: (0, jnp.maximum(s - 1, 0))
    kern = functools.partial(_mix_kernel, tm=tm)
    return pl.pallas_call(
        kern,
        grid=(tiles + 1,),
        in_specs=[pl.BlockSpec((1, tm, w), src(0)),
                  pl.BlockSpec((1, tm, w), src(0)),
                  pl.BlockSpec((1, tm, w), src(2)),
                  pl.BlockSpec((1, tm, cw), src(3)),
                  pl.BlockSpec((1, tm, d), src(0)),
                  pl.BlockSpec((1, 1, d), per_b),
                  pl.BlockSpec((1, 1, d), per_b),
                  pl.BlockSpec((1, 1, d), per_b),
                  pl.BlockSpec((1, w), vec),
                  pl.BlockSpec((CONV_K, 8, cw), lambda s: (0, 0, 0)),
                  pl.BlockSpec((8, cw), vec),
                  pl.BlockSpec((1, cw), vec),
                  pl.BlockSpec((1, cw), vec),
                  pl.BlockSpec((w + cw, d), vec),
                  pl.BlockSpec((1, d), vec),
                  pl.BlockSpec((ROUTER_ROWS, d), vec),
                  pl.BlockSpec((ROUTER_ROWS, 1), vec),
                  pl.BlockSpec(shm.shape, vec)],
        out_specs=[pl.BlockSpec((1, tm, d), tok),
                   pl.BlockSpec((1, tm, d // 2), tok),
                   pl.BlockSpec((TOP_K, tm), flat),
                   pl.BlockSpec((TOP_K, tm), flat),
                   pl.BlockSpec((TOP_K, tm), flat),
                   pl.BlockSpec((N_EXPERTS, 128), vec)],
        out_shape=[jax.ShapeDtypeStruct((bsz, l, d), F32),
                   jax.ShapeDtypeStruct((bsz, l, d // 2), F32),
                   jax.ShapeDtypeStruct((TOP_K, n), I32),
                   jax.ShapeDtypeStruct((TOP_K, n), F32),
                   jax.ShapeDtypeStruct((TOP_K, n), I32),
                   jax.ShapeDtypeStruct((N_EXPERTS, 128), F32)],
        scratch_shapes=[pltpu.VMEM((tm // GRID_W, 8, CONV_SPAN, cw), F32),
                        pltpu.VMEM((tm, cw), F32),
                        pltpu.VMEM((N_EXPERTS, 1), F32),
                        pltpu.VMEM((tm, d), BF16),
                        pltpu.VMEM((tm, d), BF16)],
        compiler_params=_cparams(("arbitrary",)),
        name="mix",
    )(o_f, o_b, z, z, x, g1, sc2, sh2, hg, dwk, dwb, lng, lnb, wo, n2g, wr, br, shm)


def _pos_kernel(ps_ref, eid_ref, rank_ref, pos_ref):
    eid = eid_ref[...]
    acc = rank_ref[...]
    for e in range(N_EXPERTS):
        acc = acc + jnp.where(eid == e, ps_ref[e], 0)
    pos_ref[...] = acc


def _positions(pstarts, eid, rank):
    shape = eid.shape
    eid2 = eid.reshape(-1, 128)
    full = pl.BlockSpec(eid2.shape, lambda i: (0, 0))
    pos = pl.pallas_call(
        _pos_kernel,
        grid=(1,),
        in_specs=[pl.BlockSpec(memory_space=pltpu.SMEM), full, full],
        out_specs=full,
        out_shape=jax.ShapeDtypeStruct(eid2.shape, I32),
        name="positions",
    )(pstarts, eid2, rank.reshape(-1, 128))
    return pos.reshape(shape)


def _sc_mesh():
    return plsc.VectorSubcoreMesh(core_axis_name="c", subcore_axis_name="s",
                                  num_cores=SC_CORES, num_subcores=SC_SUBCORES)


def _sc_worker():
    return lax.axis_index("s") * SC_CORES + lax.axis_index("c")


def _sc_dispatch(h2p, pos4, padidx, zero_rows, n_rows):
    n, dw = h2p.shape
    _, workers, chunks, _ = pos4.shape
    pad_chunks = padidx.shape[1]
    per_worker = n // workers

    def body(h_hbm, pos_hbm, pad_hbm, zero_hbm, out_hbm, idx_v, pad_v, rows_v):
        wid = _sc_worker()
        base = wid * per_worker
        for k in range(TOP_K):
            pltpu.sync_copy(pos_hbm.at[k, wid], idx_v.at[k])
        pltpu.sync_copy(pad_hbm.at[wid], pad_v)
        pltpu.sync_copy(zero_hbm, rows_v)
        for j in range(pad_chunks):
            pltpu.sync_copy(rows_v, out_hbm.at[pad_v.at[j]])
        for j in range(chunks):
            pltpu.sync_copy(h_hbm.at[pl.ds(base + j * SC_W, SC_W)], rows_v)
            for k in range(TOP_K):
                pltpu.sync_copy(rows_v, out_hbm.at[idx_v.at[k, j]])

    return pl.kernel(
        body,
        out_type=jax.ShapeDtypeStruct((n_rows, dw), F32),
        mesh=_sc_mesh(),
        scratch_types=[pltpu.VMEM((TOP_K, chunks, SC_W), I32),
                       pltpu.VMEM((pad_chunks, SC_W), I32),
                       pltpu.VMEM((SC_W, dw), F32)],
        name="sc_dispatch",
    )(h2p, pos4, padidx, zero_rows)


def _sc_combine_gather(ybuf, pos4):
    _, dw = ybuf.shape
    _, workers, chunks, _ = pos4.shape
    per_worker = chunks * SC_W

    def body(y_hbm, pos_hbm, out_hbm, idx_v, rows_v):
        wid = _sc_worker()
        base = wid * per_worker
        for k in range(TOP_K):
            pltpu.sync_copy(pos_hbm.at[k, wid], idx_v.at[k])
        for k in range(TOP_K):
            for j in range(chunks):
                pltpu.sync_copy(y_hbm.at[idx_v.at[k, j]], rows_v)
                pltpu.sync_copy(rows_v, out_hbm.at[k, pl.ds(base + j * SC_W, SC_W)])

    return pl.kernel(
        body,
        out_type=jax.ShapeDtypeStruct((TOP_K, workers * per_worker, dw), F32),
        mesh=_sc_mesh(),
        scratch_types=[pltpu.VMEM((TOP_K, chunks, SC_W), I32),
                       pltpu.VMEM((SC_W, dw), F32)],
        name="sc_combine",
    )(ybuf, pos4)


def _ffn_kernel(blk_e_ref, nblk_ref, x_ref, wg_ref, wu_ref, wd_ref, y_ref):
    del blk_e_ref
    used = pl.program_id(0) < nblk_ref[0]
    half = x_ref.shape[1]

    @pl.when(used)
    def _():
        x_lo, x_hi = _unpack_halves(x_ref[...])
        bf = lambda wv: wv.astype(BF16)
        g = _dot(x_lo, bf(wg_ref[0, 0:half, :])) + _dot(x_hi, bf(wg_ref[0, half:, :]))
        u = _dot(x_lo, bf(wu_ref[0, 0:half, :])) + _dot(x_hi, bf(wu_ref[0, half:, :]))
        y = _dot((_silu(g) * u).astype(BF16), bf(wd_ref[0]))
        y_ref[...] = _pack_halves(y.astype(BF16).astype(F32))

    @pl.when(jnp.logical_not(used))
    def _():
        y_ref[...] = jnp.zeros_like(y_ref)


def _ffn(blk_e, nblk, xbuf, wg, wu, wd, tb):
    _, dw = xbuf.shape
    _, d, de = wg.shape
    n_blk = blk_e.shape[0]
    x_blk = lambda i, be, nb: (jnp.minimum(i, nb[0] - 1), 0)
    grid_spec = pltpu.PrefetchScalarGridSpec(
        num_scalar_prefetch=2,
        grid=(n_blk,),
        in_specs=[pl.BlockSpec((tb, dw), x_blk),
                  pl.BlockSpec((1, d, de), lambda i, be, nb: (be[i], 0, 0)),
                  pl.BlockSpec((1, d, de), lambda i, be, nb: (be[i], 0, 0)),
                  pl.BlockSpec((1, de, d), lambda i, be, nb: (be[i], 0, 0))],
        out_specs=pl.BlockSpec((tb, dw), lambda i, be, nb: (i, 0)),
    )
    return pl.pallas_call(
        _ffn_kernel,
        grid_spec=grid_spec,
        out_shape=jax.ShapeDtypeStruct((n_blk * tb, dw), F32),
        compiler_params=_cparams(("arbitrary",)),
        name="ffn",
    )(blk_e, nblk, xbuf, wg, wu, wd)


def _final_kernel(yg_ref, xm_ref, wt_ref, g2_ref, fg_ref, *out_refs):
    o_ref = out_refs[-1]
    wt = wt_ref[...]
    y = None
    for k in range(TOP_K):
        lo, hi = _unpack_halves(yg_ref[k])
        yk = jnp.concatenate([lo, hi], axis=-1).astype(F32) * wt[:, k:k + 1]
        y = yk if y is None else y + yk
    x = xm_ref[0] + g2_ref[0] * y
    ms = jnp.mean(x * x, axis=-1, keepdims=True)
    o_ref[0] = x * lax.rsqrt(ms + EPS) * fg_ref[...]


def _final(yg, xm, wt, g2, fg, out_prev, b0, total):
    bsz, l, d = xm.shape
    dw = yg.shape[2]
    tm = min(MOE_TM, l)
    nt = l // tm
    in_specs = [pl.BlockSpec((TOP_K, tm, dw), lambda b, i: (0, b * nt + i, 0)),
                pl.BlockSpec((1, tm, d), lambda b, i: (b, i, 0)),
                pl.BlockSpec((tm, TOP_K), lambda b, i: (b * nt + i, 0)),
                pl.BlockSpec((1, 1, d), lambda b, i: (b0 + b, 0, 0)),
                pl.BlockSpec((1, d), lambda b, i: (0, 0))]
    args = [yg, xm, wt, g2, fg]
    aliases = {}
    if out_prev is not None:
        in_specs.append(pl.BlockSpec(memory_space=pl.ANY))
        args.append(out_prev)
        aliases = {5: 0}
    return pl.pallas_call(
        _final_kernel,
        grid=(bsz, nt),
        in_specs=in_specs,
        out_specs=pl.BlockSpec((1, tm, d), lambda b, i: (b0 + b, i, 0)),
        out_shape=jax.ShapeDtypeStruct((total, l, d), F32),
        input_output_aliases=aliases,
        compiler_params=_cparams(("arbitrary", "arbitrary")),
        name="final",
    )(*args)


def _layer(x, ctx, mod, mod_c, norm1_g, w_in, lb_logits, hgrn_norm_g, dw_kernel, dw_bias, conv_ln_g,
           conv_ln_b, w_out, norm2_g, wr, br, w_gate, w_up, w_down, final_norm_g):
    bsz, l, d = x.shape
    w = lb_logits.shape[2]
    heads = w // HEAD_DIM
    row = lambda v: v.reshape(1, -1)
    per_b = lambda v: v.reshape(bsz, 1, d)
    sh1, sc1, g1, sh2, sc2, g2 = [per_b(mod[:, j]) for j in range(N_MOD)]
    csh1 = jnp.broadcast_to(mod_c[0].reshape(1, 1, d), (bsz, 1, d))
    csc1 = jnp.broadcast_to(mod_c[1].reshape(1, 1, d), (bsz, 1, d))

    w_in_b = w_in.astype(BF16)
    zf, zr = _inproj(x, row(norm1_g), sc1, sh1, w_in_b, 2 * w, True)
    zcf, zcr = _inproj(ctx, row(norm1_g), csc1, csh1, w_in_b[:, :3 * w], 2 * w, False)
    o_f, o_b = _hgrn(lb_logits, zcf, zcr, zf, zr)
    w_out_b = w_out.astype(BF16)

    groups = MOE_GROUPS if bsz % MOE_GROUPS == 0 else 1
    out = None
    for gi in range(groups):
        b0, gb = gi * (bsz // groups), bsz // groups
        rep8 = lambda v: jnp.broadcast_to(v[..., None, :], v.shape[:-1] + (8, v.shape[-1]))
        mixed = _mix(o_f, o_b, zr, x, g1, sc2, sh2, row(jnp.tile(hgrn_norm_g, heads)), rep8(dw_kernel),
                     rep8(dw_bias), row(conv_ln_g), row(conv_ln_b), w_out_b, row(norm2_g), wr, br, b0, gb)
        out = _moe_group(mixed, g2, w_gate, w_up, w_down, row(final_norm_g), out, b0, bsz)
    return out


def _moe_group(mixed, g2, w_gate, w_up, w_down, final_g, out_prev, b0, total):
    xm, h2, eid, wts, rank, cnt = mixed
    gb, l, d = xm.shape
    n = gb * l

    tb = FFN_TB
    n_blk = -(-(n * TOP_K) // tb) + N_EXPERTS
    counts = cnt[:, 0].astype(I32)
    pcounts = (counts + tb - 1) // tb * tb
    pends = jnp.cumsum(pcounts)
    pstarts = pends - pcounts
    pos = _positions(pstarts.astype(I32), eid, rank)
    blk_start = jnp.arange(n_blk, dtype=I32) * tb
    blk_e = jnp.minimum(jnp.sum((pends[None, :] <= blk_start[:, None]).astype(I32), axis=1), N_EXPERTS - 1)
    nblk_used = (pends[-1:] // tb).astype(I32)

    workers = SC_CORES * SC_SUBCORES
    pos4 = pos.reshape(TOP_K, workers, n // (workers * SC_W), SC_W)
    lane = jnp.arange(tb, dtype=I32)[None, :]
    pad_rows = jnp.where(lane < (pcounts - counts)[:, None], (pstarts + counts)[:, None] + lane,
                         n_blk * tb + lane)
    padidx = pad_rows.astype(I32).reshape(workers, N_EXPERTS // workers * tb // SC_W, SC_W)
    xbuf = _sc_dispatch(h2.reshape(n, d // 2), pos4, padidx, jnp.zeros((SC_W, d // 2), F32), (n_blk + 1) * tb)
    ybuf = _ffn(blk_e, nblk_used, xbuf, w_gate, w_up, w_down, tb)
    yg = _sc_combine_gather(ybuf, pos4)
    return _final(yg, xm, wts.T, g2, final_g, out_prev, b0, total)


def kernel(x, c, ctx, c_ctx, w_ada, b_ada, norm1_g, w_in, lb_logits, hgrn_norm_g, dw_kernel, dw_bias,
           conv_ln_g, conv_ln_b, w_out, norm2_g, router_group_w, router_group_b, router_expert_w,
           router_expert_b, w_expert_gate, w_expert_up, w_expert_down, final_norm_g):
    depth = w_ada.shape[0]
    assert depth == 1, "context tokens are only updated between layers; a single layer is implemented"
    bsz, l, d = x.shape
    l0 = 0
    rows = -(-(bsz + 1) // 8) * 8
    cc = jnp.zeros((rows, d), F32).at[:bsz].set(c).at[bsz].set(c_ctx)
    mod_all = _ada(cc, w_ada[l0], b_ada[l0].reshape(1, -1))
    mod = mod_all[:bsz].reshape(bsz, N_MOD, d)
    mod_c = mod_all[bsz].reshape(N_MOD, d)
    pad8 = jnp.zeros((d, 8 - N_GROUPS), F32)
    wr = jnp.concatenate([router_group_w[l0], pad8, router_expert_w[l0], jnp.zeros((d, 8), F32)], axis=1).T
    br = jnp.concatenate([router_group_b[l0], jnp.zeros((8 - N_GROUPS,), F32), router_expert_b[l0],
                          jnp.zeros((8,), F32)]).reshape(ROUTER_ROWS, 1)
    return _layer(x, ctx, mod, mod_c, norm1_g[l0], w_in[l0], lb_logits, hgrn_norm_g[l0], dw_kernel[l0], dw_bias[l0], conv_ln_g[l0], conv_ln_b[l0], w_out[l0],
                  norm2_g[l0], wr, br, w_expert_gate[l0], w_expert_up[l0], w_expert_down[l0], final_norm_g)
```

```python
import functools

import numpy as np

import jax
import jax.numpy as jnp
from jax import lax
from jax.experimental import pallas as pl
from jax.experimental.pallas import tpu as pltpu
from jax.experimental.pallas import tpu_sc as plsc

F32 = jnp.float32
BF16 = jnp.bfloat16
I32 = jnp.int32
U32 = jnp.uint32

EPS = 1e-6
HEAD_DIM = 128
CHUNK = 64
GRID_W = 64
CONV_K = 31
CONV_PAD = CONV_K // 2
CONV_LANES = 256
CONV_LEAD = 16
CONV_SPAN = GRID_W + 24
N_GROUPS = 4
EXPERTS_PER_GROUP = 8
N_EXPERTS = N_GROUPS * EXPERTS_PER_GROUP
TOP_K = 2
N_MOD = 6
ROUTER_ROWS = 48

ADA_TN = 1024
INPROJ_TM = 1024
HGRN_TT = 512
MIX_TM = 256
MOE_TM = 1024
FFN_TB = 1024
VMEM_LIMIT = 48 * 1024 * 1024
SC_CORES = 2
SC_SUBCORES = 16
SC_W = 128


def _cparams(sem):
    return pltpu.CompilerParams(dimension_semantics=sem, vmem_limit_bytes=VMEM_LIMIT)


def _silu(v):
    return v * jax.nn.sigmoid(v)


def _dot(a, b):
    return jnp.dot(a, b, preferred_element_type=F32)


def _dot_nt(a, b):
    return lax.dot_general(a, b, (((1,), (1,)), ((), ())), preferred_element_type=F32)


def _dot_tn(a, b):
    return lax.dot_general(a, b, (((0,), (0,)), ((), ())), preferred_element_type=F32)


def _pack_halves(vr):
    m = vr.shape[1] // 2
    bits = lax.bitcast_convert_type(vr, U32)
    return lax.bitcast_convert_type((bits[:, :m] >> 16) | bits[:, m:], F32)


def _unpack_halves(p):
    u = lax.bitcast_convert_type(p, U32)
    lo = lax.bitcast_convert_type(u << 16, F32)
    hi = lax.bitcast_convert_type(u & jnp.uint32(0xFFFF0000), F32)
    return lo.astype(BF16), hi.astype(BF16)


def _split2(a):
    hi = a.astype(BF16)
    lo = (a - hi.astype(F32)).astype(BF16)
    return hi, lo


def _ada_kernel(c_ref, w_ref, b_ref, o_ref):
    s = _silu(c_ref[...]).astype(BF16)
    o_ref[...] = _dot(s, w_ref[...].astype(BF16)) + b_ref[...]


def _ada(cc, w, b):
    r, d = cc.shape
    nc = w.shape[1]
    return pl.pallas_call(
        _ada_kernel,
        grid=(nc // ADA_TN,),
        in_specs=[pl.BlockSpec((r, d), lambda j: (0, 0)),
                  pl.BlockSpec((d, ADA_TN), lambda j: (0, j)),
                  pl.BlockSpec((1, ADA_TN), lambda j: (0, j))],
        out_specs=pl.BlockSpec((r, ADA_TN), lambda j: (0, j)),
        out_shape=jax.ShapeDtypeStruct((r, nc), F32),
        compiler_params=_cparams(("arbitrary",)),
        name="ada",
    )(cc, w, b)


def _inproj_kernel(x_ref, g_ref, sc_ref, sh_ref, w_ref, zf_ref, zr_ref, *, activate):
    x = x_ref[0]
    ms = jnp.mean(x * x, axis=-1, keepdims=True)
    gs = g_ref[...] * (1.0 + sc_ref[0])
    h = x * lax.rsqrt(ms + EPS) * gs + sh_ref[0]
    z = _dot(h.astype(BF16), w_ref[...])
    nf = zf_ref.shape[2]
    zf_ref[0] = z[:, :nf]
    if activate:
        wd = nf // 2
        col = lambda j: z[:, nf + j * wd:nf + (j + 1) * wd]
        zr_ref[0, :, 0:wd] = col(0).astype(BF16)
        zr_ref[0, :, wd:2 * wd] = _silu(col(1)).astype(BF16)
        zr_ref[0, :, 2 * wd:3 * wd] = _silu(col(2)).astype(BF16)
        zr_ref[0, :, 3 * wd:] = (col(3) * jax.nn.sigmoid(col(4))).astype(BF16)
    else:
        zr_ref[0] = z[:, nf:].astype(BF16)


def _inproj(x, g, sc, sh, w, nf, activate):
    bsz, l, d = x.shape
    nc = w.shape[1]
    tm = min(INPROJ_TM, l)
    nr = nc - nf - (nf // 2 if activate else 0)
    return pl.pallas_call(
        functools.partial(_inproj_kernel, activate=activate),
        grid=(bsz, l // tm),
        in_specs=[pl.BlockSpec((1, tm, d), lambda b, i: (b, i, 0)),
                  pl.BlockSpec((1, d), lambda b, i: (0, 0)),
                  pl.BlockSpec((1, 1, d), lambda b, i: (b, 0, 0)),
                  pl.BlockSpec((1, 1, d), lambda b, i: (b, 0, 0)),
                  pl.BlockSpec((d, nc), lambda b, i: (0, 0))],
        out_specs=[pl.BlockSpec((1, tm, nf), lambda b, i: (b, i, 0)),
                   pl.BlockSpec((1, tm, nr), lambda b, i: (b, i, 0))],
        out_shape=[jax.ShapeDtypeStruct((bsz, l, nf), F32),
                   jax.ShapeDtypeStruct((bsz, l, nr), BF16)],
        compiler_params=_cparams(("arbitrary", "arbitrary")),
        name="inproj",
    )(x, g, sc, sh, w)


class _HgrnDirection:
    def __init__(self, lbl, st_ref, reverse):
        self.st_ref, self.reverse = st_ref, reverse
        e = jnp.exp(lbl - jnp.max(lbl, axis=0, keepdims=True))
        self.lb = e[0:1] / jnp.sum(e, axis=0, keepdims=True)
        row = lax.broadcasted_iota(I32, (CHUNK, CHUNK), 0)
        col = lax.broadcasted_iota(I32, (CHUNK, CHUNK), 1)
        self.tri = (col >= row) if reverse else (col <= row)
        tm = jnp.where(self.tri, 1.0, 0.0).astype(BF16)
        self.tm2 = jnp.concatenate([tm, tm], axis=1)

    def tile(self, f_ref, v_ref, q_ref=None, o_ref=None):
        heads = self.st_ref.shape[0]
        n_chunks = f_ref.shape[1] // CHUNK
        crow = lambda c: slice(c * CHUNK, (c + 1) * CHUNK)
        hsl = lambda h: slice(h * HEAD_DIM, (h + 1) * HEAD_DIM)
        f = self.lb + (1.0 - self.lb) * jax.nn.sigmoid(f_ref[0])
        k = 1.0 - f
        hi, lo = _split2(jnp.log2(f))
        cums = [_dot(self.tm2, jnp.concatenate([hi[crow(c)], lo[crow(c)]], axis=0))
                for c in range(n_chunks)]
        lasts = [cm[0:1] if self.reverse else cm[CHUNK - 1:CHUNK] for cm in cums]
        cum = jnp.concatenate(cums, axis=0)
        dec = [jnp.exp2(la) for la in lasts]
        k_intra = k * jnp.exp2(-cum)
        ks = (k_intra * jnp.concatenate([jnp.broadcast_to(dc, (CHUNK, dc.shape[1])) for dc in dec], axis=0)
              ).astype(BF16)
        vb = v_ref[0].astype(BF16)
        kv = [[_dot_tn(vb[crow(c), hsl(h)], ks[crow(c), hsl(h)]) for h in range(heads)] for c in range(n_chunks)]
        emit = o_ref is not None
        if emit:
            qd = (q_ref[0].astype(F32) * jnp.exp2(cum)).astype(BF16)
            ki = k_intra.astype(BF16)
            sc = [[jnp.where(self.tri, _dot_nt(qd[crow(c), hsl(h)], ki[crow(c), hsl(h)]), 0.0).astype(BF16)
                   for h in range(heads)] for c in range(n_chunks)]
            lhs = [[jnp.concatenate([qd[crow(c), hsl(h)], sc[c][h]], axis=1) for h in range(heads)]
                   for c in range(n_chunks)]
            vt = [[vb[crow(c), hsl(h)].T for h in range(heads)] for c in range(n_chunks)]
        state = [self.st_ref[h] for h in range(heads)]
        for c in (range(n_chunks - 1, -1, -1) if self.reverse else range(n_chunks)):
            for h in range(heads):
                if emit:
                    rhs = jnp.concatenate([state[h].astype(BF16), vt[c][h]], axis=1)
                    o_ref[0, crow(c), hsl(h)] = _dot_nt(lhs[c][h], rhs).astype(o_ref.dtype)
                state[h] = state[h] * dec[c][:, hsl(h)] + kv[c][h]
        for h in range(heads):
            self.st_ref[h] = state[h]


def _hgrn_kernel(lbl_ref, cff_ref, cfb_ref, cv_ref, ff_ref, vf_ref, qf_ref, fb_ref, vb_ref, qb_ref,
                 of_ref, ob_ref, stf_ref, stb_ref):
    fwd = _HgrnDirection(lbl_ref[0], stf_ref, reverse=False)
    bwd = _HgrnDirection(lbl_ref[1], stb_ref, reverse=True)

    @pl.when(pl.program_id(1) == 0)
    def _():
        stf_ref[...] = jnp.zeros_like(stf_ref)
        stb_ref[...] = jnp.zeros_like(stb_ref)
        fwd.tile(cff_ref, cv_ref)
        bwd.tile(cfb_ref, cv_ref)

    fwd.tile(ff_ref, vf_ref, qf_ref, of_ref)
    bwd.tile(fb_ref, vb_ref, qb_ref, ob_ref)


def _hgrn(lbl, zcf, zcr, zf, zr):
    bsz, l, _ = zf.shape
    n_ctx_tok = zcf.shape[1]
    w = lbl.shape[2]
    heads = w // HEAD_DIM
    tt = min(HGRN_TT, l)
    nt = l // tt
    ctx_col = lambda c: pl.BlockSpec((1, n_ctx_tok, w), lambda b, i: (b, 0, c))
    fwd_col = lambda c: pl.BlockSpec((1, tt, w), lambda b, i: (b, i, c))
    bwd_col = lambda c: pl.BlockSpec((1, tt, w), lambda b, i: (b, nt - 1 - i, c))
    state = pltpu.VMEM((heads, HEAD_DIM, HEAD_DIM), F32)
    return pl.pallas_call(
        _hgrn_kernel,
        grid=(bsz, nt),
        in_specs=[pl.BlockSpec(lbl.shape, lambda b, i: (0, 0, 0)),
                  ctx_col(0), ctx_col(1), ctx_col(0),
                  fwd_col(0), fwd_col(0), fwd_col(1),
                  bwd_col(1), bwd_col(0), bwd_col(1)],
        out_specs=[fwd_col(0), bwd_col(0)],
        out_shape=[jax.ShapeDtypeStruct((bsz, l, w), BF16), jax.ShapeDtypeStruct((bsz, l, w), BF16)],
        scratch_shapes=[state, state],
        compiler_params=_cparams(("arbitrary", "arbitrary")),
        name="hgrn",
    )(lbl, zcf, zcf, zcr, zf, zr, zr, zf, zr, zr)


def _mix_kernel(of_ref, ob_ref, zg_ref, zu_ref, x_ref, g1_ref, sc2_ref, sh2_ref,
                hg_ref, dwk_ref, dwb_ref, lng_ref, lnb_ref, wo_ref, n2g_ref, wr_ref, br_ref, shm_ref,
                xm_ref, h2_ref, eid_ref, wts_ref, rank_ref, cnt_ref,
                shf_ref, cv_ref, carry_ref, hh_ref, hl_ref, *, tm):
    w = of_ref.shape[2]
    heads = w // HEAD_DIM
    step = pl.program_id(0)

    @pl.when(step == 0)
    def _():
        carry_ref[...] = jnp.zeros_like(carry_ref)
        hh_ref[...] = jnp.zeros_like(hh_ref)
        hl_ref[...] = jnp.zeros_like(hl_ref)

    router = _route(hh_ref[...], hl_ref[...], wr_ref, br_ref, carry_ref, jnp.where(step > 0, 1.0, 0.0),
                    eid_ref, wts_ref, rank_ref, cnt_ref)
    next(router)

    o = of_ref[0].astype(F32) + ob_ref[0].astype(F32)
    parts = []
    for h in range(heads):
        oh = o[:, h * HEAD_DIM:(h + 1) * HEAD_DIM]
        parts.append(oh * lax.rsqrt(jnp.mean(oh * oh, axis=-1, keepdims=True) + EPS))
    o_mix = jnp.concatenate(parts, axis=-1) * hg_ref[...] * zg_ref[0].astype(F32)

    u = zu_ref[0].astype(F32)
    span = shf_ref.shape[2]
    u_hi, u_lo = _split2(u)
    for s in range(tm // GRID_W):
        rows = slice(s * GRID_W, (s + 1) * GRID_W)
        shifted = _dot(shm_ref[...], jnp.concatenate([u_hi[rows], u_lo[rows]], axis=0))
        for b in range(8):
            shf_ref[s, b] = shifted[b * span:(b + 1) * span]
    next(router)
    lanes = CONV_LANES
    for s in range(tm // GRID_W):
        for c0 in range(0, u.shape[1], lanes):
            cs = slice(c0, c0 + lanes)
            acc = jnp.zeros((GRID_W // 8, 8, lanes), F32) + dwb_ref[:, cs][None]
            for k in range(CONV_K):
                a, b = divmod(CONV_LEAD - CONV_PAD + k, 8)
                tap = shf_ref[s, b, 8 * a:8 * a + GRID_W, cs].reshape(GRID_W // 8, 8, lanes)
                acc = acc + tap * dwk_ref[k, :, cs][None]
            cv_ref[s * GRID_W:(s + 1) * GRID_W, cs] = acc.reshape(GRID_W, lanes)
        if s == 0:
            next(router)
    cv = cv_ref[...]
    mu = jnp.mean(cv, axis=-1, keepdims=True)
    cen = cv - mu
    var = jnp.mean(cen * cen, axis=-1, keepdims=True)
    c_mix = _silu(cen * lax.rsqrt(var + EPS) * lng_ref[...] + lnb_ref[...])
    next(router, None)

    mix = _dot(jnp.concatenate([o_mix.astype(BF16), c_mix.astype(BF16)], axis=1), wo_ref[...])
    xm = x_ref[0] + g1_ref[0] * mix
    xm_ref[0] = xm

    ms = jnp.mean(xm * xm, axis=-1, keepdims=True)
    h2 = xm * lax.rsqrt(ms + EPS) * (n2g_ref[...] * (1.0 + sc2_ref[0])) + sh2_ref[0]
    hh = h2.astype(BF16)
    h2_rounded = hh.astype(F32)
    h2_ref[0] = _pack_halves(h2_rounded)
    hh_ref[...] = hh
    hl_ref[...] = (h2 - h2_rounded).astype(BF16)


def _route(hh, hl, wr_ref, br_ref, carry_ref, live, eid_ref, wts_ref, rank_ref, cnt_ref):
    tm = hh.shape[0]
    wr = wr_ref[...]
    wh = wr.astype(BF16)
    wl = (wr - wh.astype(F32)).astype(BF16)
    lg = _dot_nt(wh, hh) + _dot_nt(wh, hl) + _dot_nt(wl, hh) + br_ref[...]
    yield

    neg = -jnp.inf
    r8 = lax.broadcasted_iota(I32, (EXPERTS_PER_GROUP, tm), 0).astype(F32)
    gl = jnp.where(r8 < N_GROUPS, lg[0:8], neg)
    gmax = jnp.max(gl, axis=0, keepdims=True)
    grp = jnp.min(jnp.where(gl == gmax, r8, 8.0), axis=0, keepdims=True)
    p_grp = 1.0 / jnp.sum(jnp.exp(gl - gmax), axis=0, keepdims=True)
    es = lg[8:16]
    for g in range(1, N_GROUPS):
        es = jnp.where(grp == float(g), lg[8 + 8 * g:16 + 8 * g], es)
    m1 = jnp.max(es, axis=0, keepdims=True)
    i1 = jnp.min(jnp.where(es == m1, r8, 8.0), axis=0, keepdims=True)
    es2 = jnp.where(r8 == i1, neg, es)
    m2 = jnp.max(es2, axis=0, keepdims=True)
    i2 = jnp.min(jnp.where(es2 == m2, r8, 8.0), axis=0, keepdims=True)
    e2 = jnp.exp(m2 - m1)
    w1 = p_grp / (1.0 + e2)
    w2 = p_grp * e2 / (1.0 + e2)
    eid0 = grp * float(EXPERTS_PER_GROUP) + i1
    eid1 = grp * float(EXPERTS_PER_GROUP) + i2
    eid_ref[0:1, :] = eid0.astype(I32)
    eid_ref[1:2, :] = eid1.astype(I32)
    wts_ref[0:1, :] = w1
    wts_ref[1:2, :] = w2
    yield

    r32 = lax.broadcasted_iota(I32, (N_EXPERTS, tm), 0).astype(F32)
    oh0 = jnp.where(r32 == eid0, live, 0.0)
    oh1 = jnp.where(r32 == eid1, live, 0.0)
    src = lax.broadcasted_iota(I32, (tm, tm), 0)
    dst = lax.broadcasted_iota(I32, (tm, tm), 1)
    before = jnp.where(src < dst, 1.0, 0.0).astype(BF16)
    ex = _dot(jnp.concatenate([oh0, oh1], axis=0).astype(BF16), before)
    yield
    base = ex[0:N_EXPERTS] + ex[N_EXPERTS:] + carry_ref[...]
    rank_ref[0:1, :] = jnp.sum(oh0 * base, axis=0, keepdims=True).astype(I32)
    rank_ref[1:2, :] = jnp.sum(oh1 * base, axis=0, keepdims=True).astype(I32)
    carry = carry_ref[...] + jnp.sum(oh0 + oh1, axis=1, keepdims=True)
    carry_ref[...] = carry
    cnt_ref[...] = jnp.broadcast_to(carry, cnt_ref.shape)


def _conv_shift_matrix():
    r = np.arange(8 * CONV_SPAN)
    src = r // CONV_SPAN + r % CONV_SPAN - CONV_LEAD
    sel = (src[:, None] == np.arange(GRID_W)[None, :]).astype(np.float32)
    return jnp.asarray(np.concatenate([sel, sel], axis=1), dtype=BF16)


def _mix(o_f, o_b, z, x, g1, sc2, sh2, hg, dwk, dwb, lng, lnb, wo, n2g, wr, br):
    bsz, l, d = x.shape
    shm = _conv_shift_matrix()
    w = o_f.shape[2]
    cw = dwk.shape[-1]
    tm = min(MIX_TM, l)
    nt = l // tm
    n = bsz * l
    tiles = bsz * nt
    mixed = lambda s: jnp.minimum(s, tiles - 1)
    src = lambda c: (lambda s: (mixed(s) // nt, mixed(s) % nt, c))
    tok = src(0)
    vec = lambda s: (0, 0)
    per_b = lambda s: (mixed(s) // nt, 0, 0)
    flat = lambda s: (0, jnp.maximum(s - 1, 0))
    kern = functools.partial(_mix_kernel, tm=tm)
    return pl.pallas_call(
        kern,
        grid=(tiles + 1,),
        in_specs=[pl.BlockSpec((1, tm, w), src(0)),
                  pl.BlockSpec((1, tm, w), src(0)),
                  pl.BlockSpec((1, tm, w), src(2)),
                  pl.BlockSpec((1, tm, cw), src(3)),
                  pl.BlockSpec((1, tm, d), src(0)),
                  pl.BlockSpec((1, 1, d), per_b),
                  pl.BlockSpec((1, 1, d), per_b),
                  pl.BlockSpec((1, 1, d), per_b),
                  pl.BlockSpec((1, w), vec),
                  pl.BlockSpec((CONV_K, 8, cw), lambda s: (0, 0, 0)),
                  pl.BlockSpec((8, cw), vec),
                  pl.BlockSpec((1, cw), vec),
                  pl.BlockSpec((1, cw), vec),
                  pl.BlockSpec((w + cw, d), vec),
                  pl.BlockSpec((1, d), vec),
                  pl.BlockSpec((ROUTER_ROWS, d), vec),
                  pl.BlockSpec((ROUTER_ROWS, 1), vec),
                  pl.BlockSpec(shm.shape, vec)],
        out_specs=[pl.BlockSpec((1, tm, d), tok),
                   pl.BlockSpec((1, tm, d // 2), tok),
                   pl.BlockSpec((TOP_K, tm), flat),
                   pl.BlockSpec((TOP_K, tm), flat),
                   pl.BlockSpec((TOP_K, tm), flat),
                   pl.BlockSpec((N_EXPERTS, 128), vec)],
        out_shape=[jax.ShapeDtypeStruct((bsz, l, d), F32),
                   jax.ShapeDtypeStruct((bsz, l, d // 2), F32),
                   jax.ShapeDtypeStruct((TOP_K, n), I32),
                   jax.ShapeDtypeStruct((TOP_K, n), F32),
                   jax.ShapeDtypeStruct((TOP_K, n), I32),
                   jax.ShapeDtypeStruct((N_EXPERTS, 128), F32)],
        scratch_shapes=[pltpu.VMEM((tm // GRID_W, 8, CONV_SPAN, cw), F32),
                        pltpu.VMEM((tm, cw), F32),
                        pltpu.VMEM((N_EXPERTS, 1), F32),
                        pltpu.VMEM((tm, d), BF16),
                        pltpu.VMEM((tm, d), BF16)],
        compiler_params=_cparams(("arbitrary",)),
        name="mix",
    )(o_f, o_b, z, z, x, g1, sc2, sh2, hg, dwk, dwb, lng, lnb, wo, n2g, wr, br, shm)


def _pos_kernel(ps_ref, eid_ref, rank_ref, pos_ref):
    eid = eid_ref[...]
    acc = rank_ref[...]
    for e in range(N_EXPERTS):
        acc = acc + jnp.where(eid == e, ps_ref[e], 0)
    pos_ref[...] = acc


def _positions(pstarts, eid, rank):
    shape = eid.shape
    eid2 = eid.reshape(-1, 128)
    full = pl.BlockSpec(eid2.shape, lambda i: (0, 0))
    pos = pl.pallas_call(
        _pos_kernel,
        grid=(1,),
        in_specs=[pl.BlockSpec(memory_space=pltpu.SMEM), full, full],
        out_specs=full,
        out_shape=jax.ShapeDtypeStruct(eid2.shape, I32),
        name="positions",
    )(pstarts, eid2, rank.reshape(-1, 128))
    return pos.reshape(shape)


def _sc_mesh():
    return plsc.VectorSubcoreMesh(core_axis_name="c", subcore_axis_name="s",
                                  num_cores=SC_CORES, num_subcores=SC_SUBCORES)


def _sc_worker():
    return lax.axis_index("s") * SC_CORES + lax.axis_index("c")


def _sc_dispatch(h2p, pos4, padidx, zero_rows, n_rows):
    n, dw = h2p.shape
    _, workers, chunks, _ = pos4.shape
    pad_chunks = padidx.shape[1]
    per_worker = n // workers

    def body(h_hbm, pos_hbm, pad_hbm, zero_hbm, out_hbm, idx_v, pad_v, rows_v):
        wid = _sc_worker()
        base = wid * per_worker
        for k in range(TOP_K):
            pltpu.sync_copy(pos_hbm.at[k, wid], idx_v.at[k])
        pltpu.sync_copy(pad_hbm.at[wid], pad_v)
        pltpu.sync_copy(zero_hbm, rows_v)
        for j in range(pad_chunks):
            pltpu.sync_copy(rows_v, out_hbm.at[pad_v.at[j]])
        for j in range(chunks):
            pltpu.sync_copy(h_hbm.at[pl.ds(base + j * SC_W, SC_W)], rows_v)
            for k in range(TOP_K):
                pltpu.sync_copy(rows_v, out_hbm.at[idx_v.at[k, j]])

    return pl.kernel(
        body,
        out_type=jax.ShapeDtypeStruct((n_rows, dw), F32),
        mesh=_sc_mesh(),
        scratch_types=[pltpu.VMEM((TOP_K, chunks, SC_W), I32),
                       pltpu.VMEM((pad_chunks, SC_W), I32),
                       pltpu.VMEM((SC_W, dw), F32)],
        name="sc_dispatch",
    )(h2p, pos4, padidx, zero_rows)


def _sc_combine_gather(ybuf, pos4):
    _, dw = ybuf.shape
    _, workers, chunks, _ = pos4.shape
    per_worker = chunks * SC_W

    def body(y_hbm, pos_hbm, out_hbm, idx_v, rows_v):
        wid = _sc_worker()
        base = wid * per_worker
        for k in range(TOP_K):
            pltpu.sync_copy(pos_hbm.at[k, wid], idx_v.at[k])
        for k in range(TOP_K):
            for j in range(chunks):
                pltpu.sync_copy(y_hbm.at[idx_v.at[k, j]], rows_v)
                pltpu.sync_copy(rows_v, out_hbm.at[k, pl.ds(base + j * SC_W, SC_W)])

    return pl.kernel(
        body,
        out_type=jax.ShapeDtypeStruct((TOP_K, workers * per_worker, dw), F32),
        mesh=_sc_mesh(),
        scratch_types=[pltpu.VMEM((TOP_K, chunks, SC_W), I32),
                       pltpu.VMEM((SC_W, dw), F32)],
        name="sc_combine",
    )(ybuf, pos4)


def _ffn_kernel(blk_e_ref, nblk_ref, x_ref, wg_ref, wu_ref, wd_ref, y_ref):
    del blk_e_ref
    used = pl.program_id(0) < nblk_ref[0]
    half = x_ref.shape[1]

    @pl.when(used)
    def _():
        x_lo, x_hi = _unpack_halves(x_ref[...])
        bf = lambda wv: wv.astype(BF16)
        g = _dot(x_lo, bf(wg_ref[0, 0:half, :])) + _dot(x_hi, bf(wg_ref[0, half:, :]))
        u = _dot(x_lo, bf(wu_ref[0, 0:half, :])) + _dot(x_hi, bf(wu_ref[0, half:, :]))
        y = _dot((_silu(g) * u).astype(BF16), bf(wd_ref[0]))
        y_ref[...] = _pack_halves(y.astype(BF16).astype(F32))

    @pl.when(jnp.logical_not(used))
    def _():
        y_ref[...] = jnp.zeros_like(y_ref)


def _ffn(blk_e, nblk, xbuf, wg, wu, wd, tb):
    _, dw = xbuf.shape
    _, d, de = wg.shape
    n_blk = blk_e.shape[0]
    x_blk = lambda i, be, nb: (jnp.minimum(i, nb[0] - 1), 0)
    grid_spec = pltpu.PrefetchScalarGridSpec(
        num_scalar_prefetch=2,
        grid=(n_blk,),
        in_specs=[pl.BlockSpec((tb, dw), x_blk),
                  pl.BlockSpec((1, d, de), lambda i, be, nb: (be[i], 0, 0)),
                  pl.BlockSpec((1, d, de), lambda i, be, nb: (be[i], 0, 0)),
                  pl.BlockSpec((1, de, d), lambda i, be, nb: (be[i], 0, 0))],
        out_specs=pl.BlockSpec((tb, dw), lambda i, be, nb: (i, 0)),
    )
    return pl.pallas_call(
        _ffn_kernel,
        grid_spec=grid_spec,
        out_shape=jax.ShapeDtypeStruct((n_blk * tb, dw), F32),
        compiler_params=_cparams(("arbitrary",)),
        name="ffn",
    )(blk_e, nblk, xbuf, wg, wu, wd)


def _final_kernel(yg_ref, xm_ref, wt_ref, g2_ref, fg_ref, o_ref):
    wt = wt_ref[...]
    y = None
    for k in range(TOP_K):
        lo, hi = _unpack_halves(yg_ref[k])
        yk = jnp.concatenate([lo, hi], axis=-1).astype(F32) * wt[:, k:k + 1]
        y = yk if y is None else y + yk
    x = xm_ref[0] + g2_ref[0] * y
    ms = jnp.mean(x * x, axis=-1, keepdims=True)
    o_ref[0] = x * lax.rsqrt(ms + EPS) * fg_ref[...]


def _final(yg, xm, wt, g2, fg):
    bsz, l, d = xm.shape
    dw = yg.shape[2]
    tm = min(MOE_TM, l)
    nt = l // tm
    return pl.pallas_call(
        _final_kernel,
        grid=(bsz, nt),
        in_specs=[pl.BlockSpec((TOP_K, tm, dw), lambda b, i: (0, b * nt + i, 0)),
                  pl.BlockSpec((1, tm, d), lambda b, i: (b, i, 0)),
                  pl.BlockSpec((tm, TOP_K), lambda b, i: (b * nt + i, 0)),
                  pl.BlockSpec((1, 1, d), lambda b, i: (b, 0, 0)),
                  pl.BlockSpec((1, d), lambda b, i: (0, 0))],
        out_specs=pl.BlockSpec((1, tm, d), lambda b, i: (b, i, 0)),
        out_shape=jax.ShapeDtypeStruct((bsz, l, d), F32),
        compiler_params=_cparams(("arbitrary", "arbitrary")),
        name="final",
    )(yg, xm, wt, g2, fg)


def _layer(x, ctx, mod, mod_c, norm1_g, w_in, lb_logits, hgrn_norm_g, dw_kernel, dw_bias, conv_ln_g,
           conv_ln_b, w_out, norm2_g, wr, br, w_gate, w_up, w_down, final_norm_g):
    bsz, l, d = x.shape
    w = lb_logits.shape[2]
    heads = w // HEAD_DIM
    row = lambda v: v.reshape(1, -1)
    per_b = lambda v: v.reshape(bsz, 1, d)
    sh1, sc1, g1, sh2, sc2, g2 = [per_b(mod[:, j]) for j in range(N_MOD)]
    csh1 = jnp.broadcast_to(mod_c[0].reshape(1, 1, d), (bsz, 1, d))
    csc1 = jnp.broadcast_to(mod_c[1].reshape(1, 1, d), (bsz, 1, d))

    w_in_b = w_in.astype(BF16)
    zf, zr = _inproj(x, row(norm1_g), sc1, sh1, w_in_b, 2 * w, True)
    zcf, zcr = _inproj(ctx, row(norm1_g), csc1, csh1, w_in_b[:, :3 * w], 2 * w, False)
    o_f, o_b = _hgrn(lb_logits, zcf, zcr, zf, zr)
    rep8 = lambda v: jnp.broadcast_to(v[..., None, :], v.shape[:-1] + (8, v.shape[-1]))
    xm, h2, eid, wts, rank, cnt = _mix(
        o_f, o_b, zr, x, g1, sc2, sh2, row(jnp.tile(hgrn_norm_g, heads)), rep8(dw_kernel), rep8(dw_bias),
        row(conv_ln_g), row(conv_ln_b), w_out.astype(BF16), row(norm2_g), wr, br)
    n = bsz * l

    tb = FFN_TB
    n_blk = -(-(n * TOP_K) // tb) + N_EXPERTS
    counts = cnt[:, 0].astype(I32)
    pcounts = (counts + tb - 1) // tb * tb
    pends = jnp.cumsum(pcounts)
    pstarts = pends - pcounts
    pos = _positions(pstarts.astype(I32), eid, rank)
    blk_start = jnp.arange(n_blk, dtype=I32) * tb
    blk_e = jnp.minimum(jnp.sum((pends[None, :] <= blk_start[:, None]).astype(I32), axis=1), N_EXPERTS - 1)
    nblk_used = (pends[-1:] // tb).astype(I32)

    workers = SC_CORES * SC_SUBCORES
    pos4 = pos.reshape(TOP_K, workers, n // (workers * SC_W), SC_W)
    lane = jnp.arange(tb, dtype=I32)[None, :]
    pad_rows = jnp.where(lane < (pcounts - counts)[:, None], (pstarts + counts)[:, None] + lane,
                         n_blk * tb + lane)
    padidx = pad_rows.astype(I32).reshape(workers, N_EXPERTS // workers * tb // SC_W, SC_W)
    xbuf = _sc_dispatch(h2.reshape(n, d // 2), pos4, padidx, jnp.zeros((SC_W, d // 2), F32), (n_blk + 1) * tb)
    ybuf = _ffn(blk_e, nblk_used, xbuf, w_gate, w_up, w_down, tb)
    yg = _sc_combine_gather(ybuf, pos4)
    return _final(yg, xm, wts.T, g2, row(final_norm_g))


def kernel(x, c, ctx, c_ctx, w_ada, b_ada, norm1_g, w_in, lb_logits, hgrn_norm_g, dw_kernel, dw_bias,
           conv_ln_g, conv_ln_b, w_out, norm2_g, router_group_w, router_group_b, router_expert_w,
           router_expert_b, w_expert_gate, w_expert_up, w_expert_down, final_norm_g):
    depth = w_ada.shape[0]
    assert depth == 1, "context tokens are only updated between layers; a single layer is implemented"
    bsz, l, d = x.shape
    l0 = 0
    rows = -(-(bsz + 1) // 8) * 8
    cc = jnp.zeros((rows, d), F32).at[:bsz].set(c).at[bsz].set(c_ctx)
    mod_all = _ada(cc, w_ada[l0], b_ada[l0].reshape(1, -1))
    mod = mod_all[:bsz].reshape(bsz, N_MOD, d)
    mod_c = mod_all[bsz].reshape(N_MOD, d)
    pad8 = jnp.zeros((d, 8 - N_GROUPS), F32)
    wr = jnp.concatenate([router_group_w[l0], pad8, router_expert_w[l0], jnp.zeros((d, 8), F32)], axis=1).T
    br = jnp.concatenate([router_group_b[l0], jnp.zeros((8 - N_GROUPS,), F32), router_expert_b[l0],
                          jnp.zeros((8,), F32)]).reshape(ROUTER_ROWS, 1)
    return _layer(x, ctx, mod, mod_c, norm1_g[l0], w_in[l0], lb_logits, hgrn_norm_g[l0], dw_kernel[l0],
                  dw_bias[l0], conv_ln_g[l0], conv_ln_b[l0], w_out[l0], norm2_g[l0], wr, br,
                  w_expert_gate[l0], w_expert_up[l0], w_expert_down[l0], final_norm_g)
```

```python
import functools

import numpy as np

import jax
import jax.numpy as jnp
from jax import lax
from jax.experimental import pallas as pl
from jax.experimental.pallas import tpu as pltpu
from jax.experimental.pallas import tpu_sc as plsc

F32 = jnp.float32
BF16 = jnp.bfloat16
I32 = jnp.int32
U32 = jnp.uint32

EPS = 1e-6
HEAD_DIM = 128
CHUNK = 64
GRID_W = 64
CONV_K = 31
CONV_PAD = CONV_K // 2
CONV_LANES = 256
CONV_LEAD = 16
CONV_SPAN = GRID_W + 24
N_GROUPS = 4
EXPERTS_PER_GROUP = 8
N_EXPERTS = N_GROUPS * EXPERTS_PER_GROUP
TOP_K = 2
N_MOD = 6
ROUTER_ROWS = 48

ADA_TN = 1024
INPROJ_TM = 1024
HGRN_TT = 512
MIX_TM = 256
MOE_TM = 1024
FFN_TB = 1024
VMEM_LIMIT = 48 * 1024 * 1024
SC_CORES = 2
SC_SUBCORES = 16
SC_W = 128


def _cparams(sem):
    return pltpu.CompilerParams(dimension_semantics=sem, vmem_limit_bytes=VMEM_LIMIT)


def _silu(v):
    return v * jax.nn.sigmoid(v)


def _dot(a, b):
    return jnp.dot(a, b, preferred_element_type=F32)


def _dot_nt(a, b):
    return lax.dot_general(a, b, (((1,), (1,)), ((), ())), preferred_element_type=F32)


def _dot_tn(a, b):
    return lax.dot_general(a, b, (((0,), (0,)), ((), ())), preferred_element_type=F32)


def _pack_halves(vr):
    m = vr.shape[1] // 2
    bits = lax.bitcast_convert_type(vr, U32)
    return lax.bitcast_convert_type((bits[:, :m] >> 16) | bits[:, m:], F32)


def _unpack_halves(p):
    u = lax.bitcast_convert_type(p, U32)
    lo = lax.bitcast_convert_type(u << 16, F32)
    hi = lax.bitcast_convert_type(u & jnp.uint32(0xFFFF0000), F32)
    return lo.astype(BF16), hi.astype(BF16)


def _split2(a):
    hi = a.astype(BF16)
    lo = (a - hi.astype(F32)).astype(BF16)
    return hi, lo


def _ada_kernel(c_ref, w_ref, b_ref, o_ref):
    s = _silu(c_ref[...]).astype(BF16)
    o_ref[...] = _dot(s, w_ref[...].astype(BF16)) + b_ref[...]


def _ada(cc, w, b):
    r, d = cc.shape
    nc = w.shape[1]
    return pl.pallas_call(
        _ada_kernel,
        grid=(nc // ADA_TN,),
        in_specs=[pl.BlockSpec((r, d), lambda j: (0, 0)),
                  pl.BlockSpec((d, ADA_TN), lambda j: (0, j)),
                  pl.BlockSpec((1, ADA_TN), lambda j: (0, j))],
        out_specs=pl.BlockSpec((r, ADA_TN), lambda j: (0, j)),
        out_shape=jax.ShapeDtypeStruct((r, nc), F32),
        compiler_params=_cparams(("arbitrary",)),
        name="ada",
    )(cc, w, b)


def _inproj_kernel(x_ref, g_ref, sc_ref, sh_ref, w_ref, zf_ref, zr_ref, *, activate):
    x = x_ref[0]
    ms = jnp.mean(x * x, axis=-1, keepdims=True)
    gs = g_ref[...] * (1.0 + sc_ref[0])
    h = x * lax.rsqrt(ms + EPS) * gs + sh_ref[0]
    z = _dot(h.astype(BF16), w_ref[...])
    nf = zf_ref.shape[2]
    zf_ref[0] = z[:, :nf]
    if activate:
        wd = nf // 2
        col = lambda j: z[:, nf + j * wd:nf + (j + 1) * wd]
        zr_ref[0, :, 0:wd] = col(0).astype(BF16)
        zr_ref[0, :, wd:2 * wd] = _silu(col(1)).astype(BF16)
        zr_ref[0, :, 2 * wd:3 * wd] = _silu(col(2)).astype(BF16)
        zr_ref[0, :, 3 * wd:] = (col(3) * jax.nn.sigmoid(col(4))).astype(BF16)
    else:
        zr_ref[0] = z[:, nf:].astype(BF16)


def _inproj(x, g, sc, sh, w, nf, activate):
    bsz, l, d = x.shape
    nc = w.shape[1]
    tm = min(INPROJ_TM, l)
    nr = nc - nf - (nf // 2 if activate else 0)
    return pl.pallas_call(
        functools.partial(_inproj_kernel, activate=activate),
        grid=(bsz, l // tm),
        in_specs=[pl.BlockSpec((1, tm, d), lambda b, i: (b, i, 0)),
                  pl.BlockSpec((1, d), lambda b, i: (0, 0)),
                  pl.BlockSpec((1, 1, d), lambda b, i: (b, 0, 0)),
                  pl.BlockSpec((1, 1, d), lambda b, i: (b, 0, 0)),
                  pl.BlockSpec((d, nc), lambda b, i: (0, 0))],
        out_specs=[pl.BlockSpec((1, tm, nf), lambda b, i: (b, i, 0)),
                   pl.BlockSpec((1, tm, nr), lambda b, i: (b, i, 0))],
        out_shape=[jax.ShapeDtypeStruct((bsz, l, nf), F32),
                   jax.ShapeDtypeStruct((bsz, l, nr), BF16)],
        compiler_params=_cparams(("arbitrary", "arbitrary")),
        name="inproj",
    )(x, g, sc, sh, w)


class _HgrnDirection:
    def __init__(self, lbl, st_ref, reverse):
        self.st_ref, self.reverse = st_ref, reverse
        e = jnp.exp(lbl - jnp.max(lbl, axis=0, keepdims=True))
        self.lb = e[0:1] / jnp.sum(e, axis=0, keepdims=True)
        row = lax.broadcasted_iota(I32, (CHUNK, CHUNK), 0)
        col = lax.broadcasted_iota(I32, (CHUNK, CHUNK), 1)
        self.tri = (col >= row) if reverse else (col <= row)
        tm = jnp.where(self.tri, 1.0, 0.0).astype(BF16)
        self.tm2 = jnp.concatenate([tm, tm], axis=1)

    def tile(self, f_ref, v_ref, q_ref=None, o_ref=None):
        heads = self.st_ref.shape[0]
        n_chunks = f_ref.shape[1] // CHUNK
        crow = lambda c: slice(c * CHUNK, (c + 1) * CHUNK)
        hsl = lambda h: slice(h * HEAD_DIM, (h + 1) * HEAD_DIM)
        f = self.lb + (1.0 - self.lb) * jax.nn.sigmoid(f_ref[0])
        k = 1.0 - f
        hi, lo = _split2(jnp.log2(f))
        cums = [_dot(self.tm2, jnp.concatenate([hi[crow(c)], lo[crow(c)]], axis=0))
                for c in range(n_chunks)]
        lasts = [cm[0:1] if self.reverse else cm[CHUNK - 1:CHUNK] for cm in cums]
        cum = jnp.concatenate(cums, axis=0)
        dec = [jnp.exp2(la) for la in lasts]
        k_intra = k * jnp.exp2(-cum)
        ks = (k_intra * jnp.concatenate([jnp.broadcast_to(dc, (CHUNK, dc.shape[1])) for dc in dec], axis=0)
              ).astype(BF16)
        vb = v_ref[0].astype(BF16)
        kv = [[_dot_tn(vb[crow(c), hsl(h)], ks[crow(c), hsl(h)]) for h in range(heads)] for c in range(n_chunks)]
        emit = o_ref is not None
        if emit:
            qd = (q_ref[0].astype(F32) * jnp.exp2(cum)).astype(BF16)
            ki = k_intra.astype(BF16)
            sc = [[jnp.where(self.tri, _dot_nt(qd[crow(c), hsl(h)], ki[crow(c), hsl(h)]), 0.0).astype(BF16)
                   for h in range(heads)] for c in range(n_chunks)]
            lhs = [[jnp.concatenate([qd[crow(c), hsl(h)], sc[c][h]], axis=1) for h in range(heads)]
                   for c in range(n_chunks)]
            vt = [[vb[crow(c), hsl(h)].T for h in range(heads)] for c in range(n_chunks)]
        state = [self.st_ref[h] for h in range(heads)]
        for c in (range(n_chunks - 1, -1, -1) if self.reverse else range(n_chunks)):
            for h in range(heads):
                if emit:
                    rhs = jnp.concatenate([state[h].astype(BF16), vt[c][h]], axis=1)
                    o_ref[0, crow(c), hsl(h)] = _dot_nt(lhs[c][h], rhs).astype(o_ref.dtype)
                state[h] = state[h] * dec[c][:, hsl(h)] + kv[c][h]
        for h in range(heads):
            self.st_ref[h] = state[h]


def _hgrn_kernel(lbl_ref, cff_ref, cfb_ref, cv_ref, ff_ref, vf_ref, qf_ref, fb_ref, vb_ref, qb_ref,
                 of_ref, ob_ref, stf_ref, stb_ref):
    fwd = _HgrnDirection(lbl_ref[0], stf_ref, reverse=False)
    bwd = _HgrnDirection(lbl_ref[1], stb_ref, reverse=True)

    @pl.when(pl.program_id(1) == 0)
    def _():
        stf_ref[...] = jnp.zeros_like(stf_ref)
        stb_ref[...] = jnp.zeros_like(stb_ref)
        fwd.tile(cff_ref, cv_ref)
        bwd.tile(cfb_ref, cv_ref)

    fwd.tile(ff_ref, vf_ref, qf_ref, of_ref)
    bwd.tile(fb_ref, vb_ref, qb_ref, ob_ref)


def _hgrn(lbl, zcf, zcr, zf, zr):
    bsz, l, _ = zf.shape
    n_ctx_tok = zcf.shape[1]
    w = lbl.shape[2]
    heads = w // HEAD_DIM
    tt = min(HGRN_TT, l)
    nt = l // tt
    ctx_col = lambda c: pl.BlockSpec((1, n_ctx_tok, w), lambda b, i: (b, 0, c))
    fwd_col = lambda c: pl.BlockSpec((1, tt, w), lambda b, i: (b, i, c))
    bwd_col = lambda c: pl.BlockSpec((1, tt, w), lambda b, i: (b, nt - 1 - i, c))
    state = pltpu.VMEM((heads, HEAD_DIM, HEAD_DIM), F32)
    return pl.pallas_call(
        _hgrn_kernel,
        grid=(bsz, nt),
        in_specs=[pl.BlockSpec(lbl.shape, lambda b, i: (0, 0, 0)),
                  ctx_col(0), ctx_col(1), ctx_col(0),
                  fwd_col(0), fwd_col(0), fwd_col(1),
                  bwd_col(1), bwd_col(0), bwd_col(1)],
        out_specs=[fwd_col(0), bwd_col(0)],
        out_shape=[jax.ShapeDtypeStruct((bsz, l, w), BF16), jax.ShapeDtypeStruct((bsz, l, w), BF16)],
        scratch_shapes=[state, state],
        compiler_params=_cparams(("arbitrary", "arbitrary")),
        name="hgrn",
    )(lbl, zcf, zcf, zcr, zf, zr, zr, zf, zr, zr)


def _mix_kernel(of_ref, ob_ref, zg_ref, zu_ref, x_ref, g1_ref, sc2_ref, sh2_ref,
                hg_ref, dwk_ref, dwb_ref, lng_ref, lnb_ref, wo_ref, n2g_ref, wr_ref, br_ref, shm_ref,
                xm_ref, h2_ref, eid_ref, wts_ref, rank_ref, cnt_ref,
                shf_ref, cv_ref, carry_ref, hh_ref, hl_ref, *, tm):
    w = of_ref.shape[2]
    heads = w // HEAD_DIM
    step = pl.program_id(0)

    @pl.when(step == 0)
    def _():
        carry_ref[...] = jnp.zeros_like(carry_ref)
        hh_ref[...] = jnp.zeros_like(hh_ref)
        hl_ref[...] = jnp.zeros_like(hl_ref)

    router = _route(hh_ref[...], hl_ref[...], wr_ref, br_ref, carry_ref, jnp.where(step > 0, 1.0, 0.0),
                    eid_ref, wts_ref, rank_ref, cnt_ref)
    next(router)

    o = of_ref[0].astype(F32) + ob_ref[0].astype(F32)
    parts = []
    for h in range(heads):
        oh = o[:, h * HEAD_DIM:(h + 1) * HEAD_DIM]
        parts.append(oh * lax.rsqrt(jnp.mean(oh * oh, axis=-1, keepdims=True) + EPS))
    o_mix = jnp.concatenate(parts, axis=-1) * hg_ref[...] * zg_ref[0].astype(F32)

    u = zu_ref[0].astype(F32)
    span = shf_ref.shape[2]
    u_hi, u_lo = _split2(u)
    for s in range(tm // GRID_W):
        rows = slice(s * GRID_W, (s + 1) * GRID_W)
        shifted = _dot(shm_ref[...], jnp.concatenate([u_hi[rows], u_lo[rows]], axis=0))
        for b in range(8):
            shf_ref[s, b] = shifted[b * span:(b + 1) * span]
    next(router)
    lanes = CONV_LANES
    for s in range(tm // GRID_W):
        for c0 in range(0, u.shape[1], lanes):
            cs = slice(c0, c0 + lanes)
            acc = jnp.zeros((GRID_W // 8, 8, lanes), F32) + dwb_ref[:, cs][None]
            for b in range(8):
                copy = shf_ref[s, b, :, cs].reshape(span // 8, 8, lanes)
                for a in range(span // 8 - GRID_W // 8 + 1):
                    k = 8 * a + b - (CONV_LEAD - CONV_PAD)
                    if 0 <= k < CONV_K:
                        acc = acc + copy[a:a + GRID_W // 8] * dwk_ref[k, :, cs][None]
            cv_ref[s * GRID_W:(s + 1) * GRID_W, cs] = acc.reshape(GRID_W, lanes)
        if s == 0:
            next(router)
    cv = cv_ref[...]
    mu = jnp.mean(cv, axis=-1, keepdims=True)
    cen = cv - mu
    var = jnp.mean(cen * cen, axis=-1, keepdims=True)
    c_mix = _silu(cen * lax.rsqrt(var + EPS) * lng_ref[...] + lnb_ref[...])
    next(router, None)

    mix = _dot(jnp.concatenate([o_mix.astype(BF16), c_mix.astype(BF16)], axis=1), wo_ref[...])
    xm = x_ref[0] + g1_ref[0] * mix
    xm_ref[0] = xm

    ms = jnp.mean(xm * xm, axis=-1, keepdims=True)
    h2 = xm * lax.rsqrt(ms + EPS) * (n2g_ref[...] * (1.0 + sc2_ref[0])) + sh2_ref[0]
    hh = h2.astype(BF16)
    h2_rounded = hh.astype(F32)
    h2_ref[0] = _pack_halves(h2_rounded)
    hh_ref[...] = hh
    hl_ref[...] = (h2 - h2_rounded).astype(BF16)


def _route(hh, hl, wr_ref, br_ref, carry_ref, live, eid_ref, wts_ref, rank_ref, cnt_ref):
    tm = hh.shape[0]
    wr = wr_ref[...]
    wh = wr.astype(BF16)
    wl = (wr - wh.astype(F32)).astype(BF16)
    lg = _dot_nt(wh, hh) + _dot_nt(wh, hl) + _dot_nt(wl, hh) + br_ref[...]
    yield

    neg = -jnp.inf
    r8 = lax.broadcasted_iota(I32, (EXPERTS_PER_GROUP, tm), 0).astype(F32)
    gl = jnp.where(r8 < N_GROUPS, lg[0:8], neg)
    gmax = jnp.max(gl, axis=0, keepdims=True)
    grp = jnp.min(jnp.where(gl == gmax, r8, 8.0), axis=0, keepdims=True)
    p_grp = 1.0 / jnp.sum(jnp.exp(gl - gmax), axis=0, keepdims=True)
    es = lg[8:16]
    for g in range(1, N_GROUPS):
        es = jnp.where(grp == float(g), lg[8 + 8 * g:16 + 8 * g], es)
    m1 = jnp.max(es, axis=0, keepdims=True)
    i1 = jnp.min(jnp.where(es == m1, r8, 8.0), axis=0, keepdims=True)
    es2 = jnp.where(r8 == i1, neg, es)
    m2 = jnp.max(es2, axis=0, keepdims=True)
    i2 = jnp.min(jnp.where(es2 == m2, r8, 8.0), axis=0, keepdims=True)
    e2 = jnp.exp(m2 - m1)
    w1 = p_grp / (1.0 + e2)
    w2 = p_grp * e2 / (1.0 + e2)
    eid0 = grp * float(EXPERTS_PER_GROUP) + i1
    eid1 = grp * float(EXPERTS_PER_GROUP) + i2
    eid_ref[0:1, :] = eid0.astype(I32)
    eid_ref[1:2, :] = eid1.astype(I32)
    wts_ref[0:1, :] = w1
    wts_ref[1:2, :] = w2
    yield

    r32 = lax.broadcasted_iota(I32, (N_EXPERTS, tm), 0).astype(F32)
    oh0 = jnp.where(r32 == eid0, live, 0.0)
    oh1 = jnp.where(r32 == eid1, live, 0.0)
    src = lax.broadcasted_iota(I32, (tm, tm), 0)
    dst = lax.broadcasted_iota(I32, (tm, tm), 1)
    before = jnp.where(src < dst, 1.0, 0.0).astype(BF16)
    ex = _dot(jnp.concatenate([oh0, oh1], axis=0).astype(BF16), before)
    yield
    base = ex[0:N_EXPERTS] + ex[N_EXPERTS:] + carry_ref[...]
    rank_ref[0:1, :] = jnp.sum(oh0 * base, axis=0, keepdims=True).astype(I32)
    rank_ref[1:2, :] = jnp.sum(oh1 * base, axis=0, keepdims=True).astype(I32)
    carry = carry_ref[...] + jnp.sum(oh0 + oh1, axis=1, keepdims=True)
    carry_ref[...] = carry
    cnt_ref[...] = jnp.broadcast_to(carry, cnt_ref.shape)


def _conv_shift_matrix():
    r = np.arange(8 * CONV_SPAN)
    src = r // CONV_SPAN + r % CONV_SPAN - CONV_LEAD
    sel = (src[:, None] == np.arange(GRID_W)[None, :]).astype(np.float32)
    return jnp.asarray(np.concatenate([sel, sel], axis=1), dtype=BF16)


def _mix(o_f, o_b, z, x, g1, sc2, sh2, hg, dwk, dwb, lng, lnb, wo, n2g, wr, br):
    bsz, l, d = x.shape
    shm = _conv_shift_matrix()
    w = o_f.shape[2]
    cw = dwk.shape[-1]
    tm = min(MIX_TM, l)
    nt = l // tm
    n = bsz * l
    tiles = bsz * nt
    mixed = lambda s: jnp.minimum(s, tiles - 1)
    src = lambda c: (lambda s: (mixed(s) // nt, mixed(s) % nt, c))
    tok = src(0)
    vec = lambda s: (0, 0)
    per_b = lambda s: (mixed(s) // nt, 0, 0)
    flat = lambda s: (0, jnp.maximum(s - 1, 0))
    kern = functools.partial(_mix_kernel, tm=tm)
    return pl.pallas_call(
        kern,
        grid=(tiles + 1,),
        in_specs=[pl.BlockSpec((1, tm, w), src(0)),
                  pl.BlockSpec((1, tm, w), src(0)),
                  pl.BlockSpec((1, tm, w), src(2)),
                  pl.BlockSpec((1, tm, cw), src(3)),
                  pl.BlockSpec((1, tm, d), src(0)),
                  pl.BlockSpec((1, 1, d), per_b),
                  pl.BlockSpec((1, 1, d), per_b),
                  pl.BlockSpec((1, 1, d), per_b),
                  pl.BlockSpec((1, w), vec),
                  pl.BlockSpec((CONV_K, 8, cw), lambda s: (0, 0, 0)),
                  pl.BlockSpec((8, cw), vec),
                  pl.BlockSpec((1, cw), vec),
                  pl.BlockSpec((1, cw), vec),
                  pl.BlockSpec((w + cw, d), vec),
                  pl.BlockSpec((1, d), vec),
                  pl.BlockSpec((ROUTER_ROWS, d), vec),
                  pl.BlockSpec((ROUTER_ROWS, 1), vec),
                  pl.BlockSpec(shm.shape, vec)],
        out_specs=[pl.BlockSpec((1, tm, d), tok),
                   pl.BlockSpec((1, tm, d // 2), tok),
                   pl.BlockSpec((TOP_K, tm), flat),
                   pl.BlockSpec((TOP_K, tm), flat),
                   pl.BlockSpec((TOP_K, tm), flat),
                   pl.BlockSpec((N_EXPERTS, 128), vec)],
        out_shape=[jax.ShapeDtypeStruct((bsz, l, d), F32),
                   jax.ShapeDtypeStruct((bsz, l, d // 2), F32),
                   jax.ShapeDtypeStruct((TOP_K, n), I32),
                   jax.ShapeDtypeStruct((TOP_K, n), F32),
                   jax.ShapeDtypeStruct((TOP_K, n), I32),
                   jax.ShapeDtypeStruct((N_EXPERTS, 128), F32)],
        scratch_shapes=[pltpu.VMEM((tm // GRID_W, 8, CONV_SPAN, cw), F32),
                        pltpu.VMEM((tm, cw), F32),
                        pltpu.VMEM((N_EXPERTS, 1), F32),
                        pltpu.VMEM((tm, d), BF16),
                        pltpu.VMEM((tm, d), BF16)],
        compiler_params=_cparams(("arbitrary",)),
        name="mix",
    )(o_f, o_b, z, z, x, g1, sc2, sh2, hg, dwk, dwb, lng, lnb, wo, n2g, wr, br, shm)


def _pos_kernel(ps_ref, eid_ref, rank_ref, pos_ref):
    eid = eid_ref[...]
    acc = rank_ref[...]
    for e in range(N_EXPERTS):
        acc = acc + jnp.where(eid == e, ps_ref[e], 0)
    pos_ref[...] = acc


def _positions(pstarts, eid, rank):
    shape = eid.shape
    eid2 = eid.reshape(-1, 128)
    full = pl.BlockSpec(eid2.shape, lambda i: (0, 0))
    pos = pl.pallas_call(
        _pos_kernel,
        grid=(1,),
        in_specs=[pl.BlockSpec(memory_space=pltpu.SMEM), full, full],
        out_specs=full,
        out_shape=jax.ShapeDtypeStruct(eid2.shape, I32),
        name="positions",
    )(pstarts, eid2, rank.reshape(-1, 128))
    return pos.reshape(shape)


def _sc_mesh():
    return plsc.VectorSubcoreMesh(core_axis_name="c", subcore_axis_name="s",
                                  num_cores=SC_CORES, num_subcores=SC_SUBCORES)


def _sc_worker():
    return lax.axis_index("s") * SC_CORES + lax.axis_index("c")


def _sc_dispatch(h2p, pos4, padidx, zero_rows, n_rows):
    n, dw = h2p.shape
    _, workers, chunks, _ = pos4.shape
    pad_chunks = padidx.shape[1]
    per_worker = n // workers

    def body(h_hbm, pos_hbm, pad_hbm, zero_hbm, out_hbm, idx_v, pad_v, rows_v):
        wid = _sc_worker()
        base = wid * per_worker
        for k in range(TOP_K):
            pltpu.sync_copy(pos_hbm.at[k, wid], idx_v.at[k])
        pltpu.sync_copy(pad_hbm.at[wid], pad_v)
        pltpu.sync_copy(zero_hbm, rows_v)
        for j in range(pad_chunks):
            pltpu.sync_copy(rows_v, out_hbm.at[pad_v.at[j]])
        for j in range(chunks):
            pltpu.sync_copy(h_hbm.at[pl.ds(base + j * SC_W, SC_W)], rows_v)
            for k in range(TOP_K):
                pltpu.sync_copy(rows_v, out_hbm.at[idx_v.at[k, j]])

    return pl.kernel(
        body,
        out_type=jax.ShapeDtypeStruct((n_rows, dw), F32),
        mesh=_sc_mesh(),
        scratch_types=[pltpu.VMEM((TOP_K, chunks, SC_W), I32),
                       pltpu.VMEM((pad_chunks, SC_W), I32),
                       pltpu.VMEM((SC_W, dw), F32)],
        name="sc_dispatch",
    )(h2p, pos4, padidx, zero_rows)


def _sc_combine_gather(ybuf, pos4):
    _, dw = ybuf.shape
    _, workers, chunks, _ = pos4.shape
    per_worker = chunks * SC_W

    def body(y_hbm, pos_hbm, out_hbm, idx_v, rows_v):
        wid = _sc_worker()
        base = wid * per_worker
        for k in range(TOP_K):
            pltpu.sync_copy(pos_hbm.at[k, wid], idx_v.at[k])
        for k in range(TOP_K):
            for j in range(chunks):
                pltpu.sync_copy(y_hbm.at[idx_v.at[k, j]], rows_v)
                pltpu.sync_copy(rows_v, out_hbm.at[k, pl.ds(base + j * SC_W, SC_W)])

    return pl.kernel(
        body,
        out_type=jax.ShapeDtypeStruct((TOP_K, workers * per_worker, dw), F32),
        mesh=_sc_mesh(),
        scratch_types=[pltpu.VMEM((TOP_K, chunks, SC_W), I32),
                       pltpu.VMEM((SC_W, dw), F32)],
        name="sc_combine",
    )(ybuf, pos4)


def _ffn_kernel(blk_e_ref, nblk_ref, x_ref, wg_ref, wu_ref, wd_ref, y_ref):
    del blk_e_ref
    used = pl.program_id(0) < nblk_ref[0]
    half = x_ref.shape[1]

    @pl.when(used)
    def _():
        x_lo, x_hi = _unpack_halves(x_ref[...])
        bf = lambda wv: wv.astype(BF16)
        g = _dot(x_lo, bf(wg_ref[0, 0:half, :])) + _dot(x_hi, bf(wg_ref[0, half:, :]))
        u = _dot(x_lo, bf(wu_ref[0, 0:half, :])) + _dot(x_hi, bf(wu_ref[0, half:, :]))
        y = _dot((_silu(g) * u).astype(BF16), bf(wd_ref[0]))
        y_ref[...] = _pack_halves(y.astype(BF16).astype(F32))

    @pl.when(jnp.logical_not(used))
    def _():
        y_ref[...] = jnp.zeros_like(y_ref)


def _ffn(blk_e, nblk, xbuf, wg, wu, wd, tb):
    _, dw = xbuf.shape
    _, d, de = wg.shape
    n_blk = blk_e.shape[0]
    x_blk = lambda i, be, nb: (jnp.minimum(i, nb[0] - 1), 0)
    grid_spec = pltpu.PrefetchScalarGridSpec(
        num_scalar_prefetch=2,
        grid=(n_blk,),
        in_specs=[pl.BlockSpec((tb, dw), x_blk),
                  pl.BlockSpec((1, d, de), lambda i, be, nb: (be[i], 0, 0)),
                  pl.BlockSpec((1, d, de), lambda i, be, nb: (be[i], 0, 0)),
                  pl.BlockSpec((1, de, d), lambda i, be, nb: (be[i], 0, 0))],
        out_specs=pl.BlockSpec((tb, dw), lambda i, be, nb: (i, 0)),
    )
    return pl.pallas_call(
        _ffn_kernel,
        grid_spec=grid_spec,
        out_shape=jax.ShapeDtypeStruct((n_blk * tb, dw), F32),
        compiler_params=_cparams(("arbitrary",)),
        name="ffn",
    )(blk_e, nblk, xbuf, wg, wu, wd)


def _final_kernel(yg_ref, xm_ref, wt_ref, g2_ref, fg_ref, o_ref):
    wt = wt_ref[...]
    y = None
    for k in range(TOP_K):
        lo, hi = _unpack_halves(yg_ref[k])
        yk = jnp.concatenate([lo, hi], axis=-1).astype(F32) * wt[:, k:k + 1]
        y = yk if y is None else y + yk
    x = xm_ref[0] + g2_ref[0] * y
    ms = jnp.mean(x * x, axis=-1, keepdims=True)
    o_ref[0] = x * lax.rsqrt(ms + EPS) * fg_ref[...]


def _final(yg, xm, wt, g2, fg):
    bsz, l, d = xm.shape
    dw = yg.shape[2]
    tm = min(MOE_TM, l)
    nt = l // tm
    return pl.pallas_call(
        _final_kernel,
        grid=(bsz, nt),
        in_specs=[pl.BlockSpec((TOP_K, tm, dw), lambda b, i: (0, b * nt + i, 0)),
                  pl.BlockSpec((1, tm, d), lambda b, i: (b, i, 0)),
                  pl.BlockSpec((tm, TOP_K), lambda b, i: (b * nt + i, 0)),
                  pl.BlockSpec((1, 1, d), lambda b, i: (b, 0, 0)),
                  pl.BlockSpec((1, d), lambda b, i: (0, 0))],
        out_specs=pl.BlockSpec((1, tm, d), lambda b, i: (b, i, 0)),
        out_shape=jax.ShapeDtypeStruct((bsz, l, d), F32),
        compiler_params=_cparams(("arbitrary", "arbitrary")),
        name="final",
    )(yg, xm, wt, g2, fg)


def _layer(x, ctx, mod, mod_c, norm1_g, w_in, lb_logits, hgrn_norm_g, dw_kernel, dw_bias, conv_ln_g,
           conv_ln_b, w_out, norm2_g, wr, br, w_gate, w_up, w_down, final_norm_g):
    bsz, l, d = x.shape
    w = lb_logits.shape[2]
    heads = w // HEAD_DIM
    row = lambda v: v.reshape(1, -1)
    per_b = lambda v: v.reshape(bsz, 1, d)
    sh1, sc1, g1, sh2, sc2, g2 = [per_b(mod[:, j]) for j in range(N_MOD)]
    csh1 = jnp.broadcast_to(mod_c[0].reshape(1, 1, d), (bsz, 1, d))
    csc1 = jnp.broadcast_to(mod_c[1].reshape(1, 1, d), (bsz, 1, d))

    w_in_b = w_in.astype(BF16)
    zf, zr = _inproj(x, row(norm1_g), sc1, sh1, w_in_b, 2 * w, True)
    zcf, zcr = _inproj(ctx, row(norm1_g), csc1, csh1, w_in_b[:, :3 * w], 2 * w, False)
    o_f, o_b = _hgrn(lb_logits, zcf, zcr, zf, zr)
    rep8 = lambda v: jnp.broadcast_to(v[..., None, :], v.shape[:-1] + (8, v.shape[-1]))
    xm, h2, eid, wts, rank, cnt = _mix(
        o_f, o_b, zr, x, g1, sc2, sh2, row(jnp.tile(hgrn_norm_g, heads)), rep8(dw_kernel), rep8(dw_bias),
        row(conv_ln_g), row(conv_ln_b), w_out.astype(BF16), row(norm2_g), wr, br)
    n = bsz * l

    tb = FFN_TB
    n_blk = -(-(n * TOP_K) // tb) + N_EXPERTS
    counts = cnt[:, 0].astype(I32)
    pcounts = (counts + tb - 1) // tb * tb
    pends = jnp.cumsum(pcounts)
    pstarts = pends - pcounts
    pos = _positions(pstarts.astype(I32), eid, rank)
    blk_start = jnp.arange(n_blk, dtype=I32) * tb
    blk_e = jnp.minimum(jnp.sum((pends[None, :] <= blk_start[:, None]).astype(I32), axis=1), N_EXPERTS - 1)
    nblk_used = (pends[-1:] // tb).astype(I32)

    workers = SC_CORES * SC_SUBCORES
    pos4 = pos.reshape(TOP_K, workers, n // (workers * SC_W), SC_W)
    lane = jnp.arange(tb, dtype=I32)[None, :]
    pad_rows = jnp.where(lane < (pcounts - counts)[:, None], (pstarts + counts)[:, None] + lane,
                         n_blk * tb + lane)
    padidx = pad_rows.astype(I32).reshape(workers, N_EXPERTS // workers * tb // SC_W, SC_W)
    xbuf = _sc_dispatch(h2.reshape(n, d // 2), pos4, padidx, jnp.zeros((SC_W, d // 2), F32), (n_blk + 1) * tb)
    ybuf = _ffn(blk_e, nblk_used, xbuf, w_gate, w_up, w_down, tb)
    yg = _sc_combine_gather(ybuf, pos4)
    return _final(yg, xm, wts.T, g2, row(final_norm_g))


def kernel(x, c, ctx, c_ctx, w_ada, b_ada, norm1_g, w_in, lb_logits, hgrn_norm_g, dw_kernel, dw_bias,
           conv_ln_g, conv_ln_b, w_out, norm2_g, router_group_w, router_group_b, router_expert_w,
           router_expert_b, w_expert_gate, w_expert_up, w_expert_down, final_norm_g):
    depth = w_ada.shape[0]
    assert depth == 1, "context tokens are only updated between layers; a single layer is implemented"
    bsz, l, d = x.shape
    l0 = 0
    rows = -(-(bsz + 1) // 8) * 8
    cc = jnp.zeros((rows, d), F32).at[:bsz].set(c).at[bsz].set(c_ctx)
    mod_all = _ada(cc, w_ada[l0], b_ada[l0].reshape(1, -1))
    mod = mod_all[:bsz].reshape(bsz, N_MOD, d)
    mod_c = mod_all[bsz].reshape(N_MOD, d)
    pad8 = jnp.zeros((d, 8 - N_GROUPS), F32)
    wr = jnp.concatenate([router_group_w[l0], pad8, router_expert_w[l0], jnp.zeros((d, 8), F32)], axis=1).T
    br = jnp.concatenate([router_group_b[l0], jnp.zeros((8 - N_GROUPS,), F32), router_expert_b[l0],
                          jnp.zeros((8,), F32)]).reshape(ROUTER_ROWS, 1)
    return _layer(x, ctx, mod, mod_c, norm1_g[l0], w_in[l0], lb_logits, hgrn_norm_g[l0], dw_kernel[l0],
                  dw_bias[l0], conv_ln_g[l0], conv_ln_b[l0], w_out[l0], norm2_g[l0], wr, br,
                  w_expert_gate[l0], w_expert_up[l0], w_expert_down[l0], final_norm_g)
```

```python
import functools

import numpy as np

import jax
import jax.numpy as jnp
from jax import lax
from jax.experimental import pallas as pl
from jax.experimental.pallas import tpu as pltpu
from jax.experimental.pallas import tpu_sc as plsc

F32 = jnp.float32
BF16 = jnp.bfloat16
I32 = jnp.int32
U32 = jnp.uint32

EPS = 1e-6
HEAD_DIM = 128
CHUNK = 64
GRID_W = 64
CONV_K = 31
CONV_PAD = CONV_K // 2
CONV_LANES = 256
CONV_LEAD = 16
CONV_SPAN = GRID_W + 24
N_GROUPS = 4
EXPERTS_PER_GROUP = 8
N_EXPERTS = N_GROUPS * EXPERTS_PER_GROUP
TOP_K = 2
N_MOD = 6
ROUTER_ROWS = 48

ADA_TN = 1024
INPROJ_TM = 1024
HGRN_TT = 512
MIX_TM = 256
MOE_TM = 1024
FFN_TB = 1024
VMEM_LIMIT = 48 * 1024 * 1024
SC_CORES = 2
SC_SUBCORES = 16
SC_W = 128


def _cparams(sem):
    return pltpu.CompilerParams(dimension_semantics=sem, vmem_limit_bytes=VMEM_LIMIT)


def _silu(v):
    return v * jax.nn.sigmoid(v)


def _dot(a, b):
    return jnp.dot(a, b, preferred_element_type=F32)


def _dot_nt(a, b):
    return lax.dot_general(a, b, (((1,), (1,)), ((), ())), preferred_element_type=F32)


def _dot_tn(a, b):
    return lax.dot_general(a, b, (((0,), (0,)), ((), ())), preferred_element_type=F32)


def _pack_halves(vr):
    m = vr.shape[1] // 2
    bits = lax.bitcast_convert_type(vr, U32)
    return lax.bitcast_convert_type((bits[:, :m] >> 16) | bits[:, m:], F32)


def _unpack_halves(p):
    u = lax.bitcast_convert_type(p, U32)
    lo = lax.bitcast_convert_type(u << 16, F32)
    hi = lax.bitcast_convert_type(u & jnp.uint32(0xFFFF0000), F32)
    return lo.astype(BF16), hi.astype(BF16)


def _split2(a):
    hi = a.astype(BF16)
    lo = (a - hi.astype(F32)).astype(BF16)
    return hi, lo


def _ada_kernel(c_ref, w_ref, b_ref, o_ref):
    s = _silu(c_ref[...]).astype(BF16)
    o_ref[...] = _dot(s, w_ref[...].astype(BF16)) + b_ref[...]


def _ada(cc, w, b):
    r, d = cc.shape
    nc = w.shape[1]
    return pl.pallas_call(
        _ada_kernel,
        grid=(nc // ADA_TN,),
        in_specs=[pl.BlockSpec((r, d), lambda j: (0, 0)),
                  pl.BlockSpec((d, ADA_TN), lambda j: (0, j)),
                  pl.BlockSpec((1, ADA_TN), lambda j: (0, j))],
        out_specs=pl.BlockSpec((r, ADA_TN), lambda j: (0, j)),
        out_shape=jax.ShapeDtypeStruct((r, nc), F32),
        compiler_params=_cparams(("arbitrary",)),
        name="ada",
    )(cc, w, b)


def _inproj_kernel(x_ref, g_ref, sc_ref, sh_ref, w_ref, zf_ref, zr_ref, *, activate):
    x = x_ref[0]
    ms = jnp.mean(x * x, axis=-1, keepdims=True)
    gs = g_ref[...] * (1.0 + sc_ref[0])
    h = x * lax.rsqrt(ms + EPS) * gs + sh_ref[0]
    z = _dot(h.astype(BF16), w_ref[...])
    nf = zf_ref.shape[2]
    zf_ref[0] = z[:, :nf]
    if activate:
        wd = nf // 2
        col = lambda j: z[:, nf + j * wd:nf + (j + 1) * wd]
        zr_ref[0, :, 0:wd] = col(0).astype(BF16)
        zr_ref[0, :, wd:2 * wd] = _silu(col(1)).astype(BF16)
        zr_ref[0, :, 2 * wd:3 * wd] = _silu(col(2)).astype(BF16)
        zr_ref[0, :, 3 * wd:] = (col(3) * jax.nn.sigmoid(col(4))).astype(BF16)
    else:
        zr_ref[0] = z[:, nf:].astype(BF16)


def _inproj(x, g, sc, sh, w, nf, activate):
    bsz, l, d = x.shape
    nc = w.shape[1]
    tm = min(INPROJ_TM, l)
    nr = nc - nf - (nf // 2 if activate else 0)
    return pl.pallas_call(
        functools.partial(_inproj_kernel, activate=activate),
        grid=(bsz, l // tm),
        in_specs=[pl.BlockSpec((1, tm, d), lambda b, i: (b, i, 0)),
                  pl.BlockSpec((1, d), lambda b, i: (0, 0)),
                  pl.BlockSpec((1, 1, d), lambda b, i: (b, 0, 0)),
                  pl.BlockSpec((1, 1, d), lambda b, i: (b, 0, 0)),
                  pl.BlockSpec((d, nc), lambda b, i: (0, 0))],
        out_specs=[pl.BlockSpec((1, tm, nf), lambda b, i: (b, i, 0)),
                   pl.BlockSpec((1, tm, nr), lambda b, i: (b, i, 0))],
        out_shape=[jax.ShapeDtypeStruct((bsz, l, nf), F32),
                   jax.ShapeDtypeStruct((bsz, l, nr), BF16)],
        compiler_params=_cparams(("arbitrary", "arbitrary")),
        name="inproj",
    )(x, g, sc, sh, w)


class _HgrnDirection:
    def __init__(self, lbl, st_ref, reverse):
        self.st_ref, self.reverse = st_ref, reverse
        e = jnp.exp(lbl - jnp.max(lbl, axis=0, keepdims=True))
        self.lb = e[0:1] / jnp.sum(e, axis=0, keepdims=True)
        row = lax.broadcasted_iota(I32, (CHUNK, CHUNK), 0)
        col = lax.broadcasted_iota(I32, (CHUNK, CHUNK), 1)
        self.tri = (col >= row) if reverse else (col <= row)
        tm = jnp.where(self.tri, 1.0, 0.0).astype(BF16)
        self.tm2 = jnp.concatenate([tm, tm], axis=1)

    def tile(self, f_ref, v_ref, q_ref=None, o_ref=None):
        heads = self.st_ref.shape[0]
        n_chunks = f_ref.shape[1] // CHUNK
        crow = lambda c: slice(c * CHUNK, (c + 1) * CHUNK)
        hsl = lambda h: slice(h * HEAD_DIM, (h + 1) * HEAD_DIM)
        f = self.lb + (1.0 - self.lb) * jax.nn.sigmoid(f_ref[0])
        k = 1.0 - f
        hi, lo = _split2(jnp.log2(f))
        cums = [_dot(self.tm2, jnp.concatenate([hi[crow(c)], lo[crow(c)]], axis=0))
                for c in range(n_chunks)]
        lasts = [cm[0:1] if self.reverse else cm[CHUNK - 1:CHUNK] for cm in cums]
        cum = jnp.concatenate(cums, axis=0)
        dec = [jnp.exp2(la) for la in lasts]
        k_intra = k * jnp.exp2(-cum)
        ks = (k_intra * jnp.concatenate([jnp.broadcast_to(dc, (CHUNK, dc.shape[1])) for dc in dec], axis=0)
              ).astype(BF16)
        vb = v_ref[0].astype(BF16)
        kv_of = lambda c: [_dot_tn(vb[crow(c), hsl(h)], ks[crow(c), hsl(h)]) for h in range(heads)]
        emit = o_ref is not None
        if emit:
            qd = (q_ref[0].astype(F32) * jnp.exp2(cum)).astype(BF16)
            ki = k_intra.astype(BF16)
            sc = [[jnp.where(self.tri, _dot_nt(qd[crow(c), hsl(h)], ki[crow(c), hsl(h)]), 0.0).astype(BF16)
                   for h in range(heads)] for c in range(n_chunks)]
            lhs = [[jnp.concatenate([qd[crow(c), hsl(h)], sc[c][h]], axis=1) for h in range(heads)]
                   for c in range(n_chunks)]
            vt = [[vb[crow(c), hsl(h)].T for h in range(heads)] for c in range(n_chunks)]
        state = [self.st_ref[h] for h in range(heads)]
        order = list(range(n_chunks - 1, -1, -1) if self.reverse else range(n_chunks))
        kv_next = kv_of(order[0])
        for pos, c in enumerate(order):
            kv = kv_next
            if pos + 1 < n_chunks:
                kv_next = kv_of(order[pos + 1])
            for h in range(heads):
                if emit:
                    rhs = jnp.concatenate([state[h].astype(BF16), vt[c][h]], axis=1)
                    o_ref[0, crow(c), hsl(h)] = _dot_nt(lhs[c][h], rhs).astype(o_ref.dtype)
                state[h] = state[h] * dec[c][:, hsl(h)] + kv[h]
        for h in range(heads):
            self.st_ref[h] = state[h]


def _hgrn_kernel(lbl_ref, cff_ref, cfb_ref, cv_ref, ff_ref, vf_ref, qf_ref, fb_ref, vb_ref, qb_ref,
                 of_ref, ob_ref, stf_ref, stb_ref):
    fwd = _HgrnDirection(lbl_ref[0], stf_ref, reverse=False)
    bwd = _HgrnDirection(lbl_ref[1], stb_ref, reverse=True)

    @pl.when(pl.program_id(1) == 0)
    def _():
        stf_ref[...] = jnp.zeros_like(stf_ref)
        stb_ref[...] = jnp.zeros_like(stb_ref)
        fwd.tile(cff_ref, cv_ref)
        bwd.tile(cfb_ref, cv_ref)

    fwd.tile(ff_ref, vf_ref, qf_ref, of_ref)
    bwd.tile(fb_ref, vb_ref, qb_ref, ob_ref)


def _hgrn(lbl, zcf, zcr, zf, zr):
    bsz, l, _ = zf.shape
    n_ctx_tok = zcf.shape[1]
    w = lbl.shape[2]
    heads = w // HEAD_DIM
    tt = min(HGRN_TT, l)
    nt = l // tt
    ctx_col = lambda c: pl.BlockSpec((1, n_ctx_tok, w), lambda b, i: (b, 0, c))
    fwd_col = lambda c: pl.BlockSpec((1, tt, w), lambda b, i: (b, i, c))
    bwd_col = lambda c: pl.BlockSpec((1, tt, w), lambda b, i: (b, nt - 1 - i, c))
    state = pltpu.VMEM((heads, HEAD_DIM, HEAD_DIM), F32)
    return pl.pallas_call(
        _hgrn_kernel,
        grid=(bsz, nt),
        in_specs=[pl.BlockSpec(lbl.shape, lambda b, i: (0, 0, 0)),
                  ctx_col(0), ctx_col(1), ctx_col(0),
                  fwd_col(0), fwd_col(0), fwd_col(1),
                  bwd_col(1), bwd_col(0), bwd_col(1)],
        out_specs=[fwd_col(0), bwd_col(0)],
        out_shape=[jax.ShapeDtypeStruct((bsz, l, w), BF16), jax.ShapeDtypeStruct((bsz, l, w), BF16)],
        scratch_shapes=[state, state],
        compiler_params=_cparams(("arbitrary", "arbitrary")),
        name="hgrn",
    )(lbl, zcf, zcf, zcr, zf, zr, zr, zf, zr, zr)


def _mix_kernel(of_ref, ob_ref, zg_ref, zu_ref, x_ref, g1_ref, sc2_ref, sh2_ref,
                hg_ref, dwk_ref, dwb_ref, lng_ref, lnb_ref, wo_ref, n2g_ref, wr_ref, br_ref, shm_ref,
                xm_ref, h2_ref, eid_ref, wts_ref, rank_ref, cnt_ref,
                shf_ref, cv_ref, carry_ref, hh_ref, hl_ref, *, tm):
    w = of_ref.shape[2]
    heads = w // HEAD_DIM
    step = pl.program_id(0)

    @pl.when(step == 0)
    def _():
        carry_ref[...] = jnp.zeros_like(carry_ref)
        hh_ref[...] = jnp.zeros_like(hh_ref)
        hl_ref[...] = jnp.zeros_like(hl_ref)

    router = _route(hh_ref[...], hl_ref[...], wr_ref, br_ref, carry_ref, jnp.where(step > 0, 1.0, 0.0),
                    eid_ref, wts_ref, rank_ref, cnt_ref)
    next(router)

    o = of_ref[0].astype(F32) + ob_ref[0].astype(F32)
    parts = []
    for h in range(heads):
        oh = o[:, h * HEAD_DIM:(h + 1) * HEAD_DIM]
        parts.append(oh * lax.rsqrt(jnp.mean(oh * oh, axis=-1, keepdims=True) + EPS))
    o_mix = jnp.concatenate(parts, axis=-1) * hg_ref[...] * zg_ref[0].astype(F32)

    u = zu_ref[0].astype(F32)
    span = shf_ref.shape[2]
    u_hi, u_lo = _split2(u)
    for s in range(tm // GRID_W):
        rows = slice(s * GRID_W, (s + 1) * GRID_W)
        shifted = _dot(shm_ref[...], jnp.concatenate([u_hi[rows], u_lo[rows]], axis=0))
        for b in range(8):
            shf_ref[s, b] = shifted[b * span:(b + 1) * span]
    next(router)
    lanes = CONV_LANES
    for s in range(tm // GRID_W):
        for c0 in range(0, u.shape[1], lanes):
            cs = slice(c0, c0 + lanes)
            acc = jnp.zeros((GRID_W // 8, 8, lanes), F32) + dwb_ref[:, cs][None]
            for b in range(8):
                copy = shf_ref[s, b, :, cs].reshape(span // 8, 8, lanes)
                for a in range(span // 8 - GRID_W // 8 + 1):
                    k = 8 * a + b - (CONV_LEAD - CONV_PAD)
                    if 0 <= k < CONV_K:
                        acc = acc + copy[a:a + GRID_W // 8] * dwk_ref[k, :, cs][None]
            cv_ref[s * GRID_W:(s + 1) * GRID_W, cs] = acc.reshape(GRID_W, lanes)
        if s == 0:
            next(router)
    cv = cv_ref[...]
    mu = jnp.mean(cv, axis=-1, keepdims=True)
    cen = cv - mu
    var = jnp.mean(cen * cen, axis=-1, keepdims=True)
    c_mix = _silu(cen * lax.rsqrt(var + EPS) * lng_ref[...] + lnb_ref[...])
    next(router, None)

    mix = _dot(jnp.concatenate([o_mix.astype(BF16), c_mix.astype(BF16)], axis=1), wo_ref[...])
    xm = x_ref[0] + g1_ref[0] * mix
    xm_ref[0] = xm

    ms = jnp.mean(xm * xm, axis=-1, keepdims=True)
    h2 = xm * lax.rsqrt(ms + EPS) * (n2g_ref[...] * (1.0 + sc2_ref[0])) + sh2_ref[0]
    hh = h2.astype(BF16)
    h2_rounded = hh.astype(F32)
    h2_ref[0] = _pack_halves(h2_rounded)
    hh_ref[...] = hh
    hl_ref[...] = (h2 - h2_rounded).astype(BF16)


def _route(hh, hl, wr_ref, br_ref, carry_ref, live, eid_ref, wts_ref, rank_ref, cnt_ref):
    tm = hh.shape[0]
    wr = wr_ref[...]
    wh = wr.astype(BF16)
    wl = (wr - wh.astype(F32)).astype(BF16)
    lg = _dot_nt(wh, hh) + _dot_nt(wh, hl) + _dot_nt(wl, hh) + br_ref[...]
    yield

    neg = -jnp.inf
    r8 = lax.broadcasted_iota(I32, (EXPERTS_PER_GROUP, tm), 0).astype(F32)
    gl = jnp.where(r8 < N_GROUPS, lg[0:8], neg)
    gmax = jnp.max(gl, axis=0, keepdims=True)
    grp = jnp.min(jnp.where(gl == gmax, r8, 8.0), axis=0, keepdims=True)
    p_grp = 1.0 / jnp.sum(jnp.exp(gl - gmax), axis=0, keepdims=True)
    es = lg[8:16]
    for g in range(1, N_GROUPS):
        es = jnp.where(grp == float(g), lg[8 + 8 * g:16 + 8 * g], es)
    m1 = jnp.max(es, axis=0, keepdims=True)
    i1 = jnp.min(jnp.where(es == m1, r8, 8.0), axis=0, keepdims=True)
    es2 = jnp.where(r8 == i1, neg, es)
    m2 = jnp.max(es2, axis=0, keepdims=True)
    i2 = jnp.min(jnp.where(es2 == m2, r8, 8.0), axis=0, keepdims=True)
    e2 = jnp.exp(m2 - m1)
    w1 = p_grp / (1.0 + e2)
    w2 = p_grp * e2 / (1.0 + e2)
    eid0 = grp * float(EXPERTS_PER_GROUP) + i1
    eid1 = grp * float(EXPERTS_PER_GROUP) + i2
    eid_ref[0:1, :] = eid0.astype(I32)
    eid_ref[1:2, :] = eid1.astype(I32)
    wts_ref[0:1, :] = w1
    wts_ref[1:2, :] = w2
    yield

    r32 = lax.broadcasted_iota(I32, (N_EXPERTS, tm), 0).astype(F32)
    oh0 = jnp.where(r32 == eid0, live, 0.0)
    oh1 = jnp.where(r32 == eid1, live, 0.0)
    src = lax.broadcasted_iota(I32, (tm, tm), 0)
    dst = lax.broadcasted_iota(I32, (tm, tm), 1)
    before = jnp.where(src < dst, 1.0, 0.0).astype(BF16)
    ex = _dot(jnp.concatenate([oh0, oh1], axis=0).astype(BF16), before)
    yield
    base = ex[0:N_EXPERTS] + ex[N_EXPERTS:] + carry_ref[...]
    rank_ref[0:1, :] = jnp.sum(oh0 * base, axis=0, keepdims=True).astype(I32)
    rank_ref[1:2, :] = jnp.sum(oh1 * base, axis=0, keepdims=True).astype(I32)
    carry = carry_ref[...] + jnp.sum(oh0 + oh1, axis=1, keepdims=True)
    carry_ref[...] = carry
    cnt_ref[...] = jnp.broadcast_to(carry, cnt_ref.shape)


def _conv_shift_matrix():
    r = np.arange(8 * CONV_SPAN)
    src = r // CONV_SPAN + r % CONV_SPAN - CONV_LEAD
    sel = (src[:, None] == np.arange(GRID_W)[None, :]).astype(np.float32)
    return jnp.asarray(np.concatenate([sel, sel], axis=1), dtype=BF16)


def _mix(o_f, o_b, z, x, g1, sc2, sh2, hg, dwk, dwb, lng, lnb, wo, n2g, wr, br):
    bsz, l, d = x.shape
    shm = _conv_shift_matrix()
    w = o_f.shape[2]
    cw = dwk.shape[-1]
    tm = min(MIX_TM, l)
    nt = l // tm
    n = bsz * l
    tiles = bsz * nt
    mixed = lambda s: jnp.minimum(s, tiles - 1)
    src = lambda c: (lambda s: (mixed(s) // nt, mixed(s) % nt, c))
    tok = src(0)
    vec = lambda s: (0, 0)
    per_b = lambda s: (mixed(s) // nt, 0, 0)
    flat = lambda s: (0, jnp.maximum(s - 1, 0))
    kern = functools.partial(_mix_kernel, tm=tm)
    return pl.pallas_call(
        kern,
        grid=(tiles + 1,),
        in_specs=[pl.BlockSpec((1, tm, w), src(0)),
                  pl.BlockSpec((1, tm, w), src(0)),
                  pl.BlockSpec((1, tm, w), src(2)),
                  pl.BlockSpec((1, tm, cw), src(3)),
                  pl.BlockSpec((1, tm, d), src(0)),
                  pl.BlockSpec((1, 1, d), per_b),
                  pl.BlockSpec((1, 1, d), per_b),
                  pl.BlockSpec((1, 1, d), per_b),
                  pl.BlockSpec((1, w), vec),
                  pl.BlockSpec((CONV_K, 8, cw), lambda s: (0, 0, 0)),
                  pl.BlockSpec((8, cw), vec),
                  pl.BlockSpec((1, cw), vec),
                  pl.BlockSpec((1, cw), vec),
                  pl.BlockSpec((w + cw, d), vec),
                  pl.BlockSpec((1, d), vec),
                  pl.BlockSpec((ROUTER_ROWS, d), vec),
                  pl.BlockSpec((ROUTER_ROWS, 1), vec),
                  pl.BlockSpec(shm.shape, vec)],
        out_specs=[pl.BlockSpec((1, tm, d), tok),
                   pl.BlockSpec((1, tm, d // 2), tok),
                   pl.BlockSpec((TOP_K, tm), flat),
                   pl.BlockSpec((TOP_K, tm), flat),
                   pl.BlockSpec((TOP_K, tm), flat),
                   pl.BlockSpec((N_EXPERTS, 128), vec)],
        out_shape=[jax.ShapeDtypeStruct((bsz, l, d), F32),
                   jax.ShapeDtypeStruct((bsz, l, d // 2), F32),
                   jax.ShapeDtypeStruct((TOP_K, n), I32),
                   jax.ShapeDtypeStruct((TOP_K, n), F32),
                   jax.ShapeDtypeStruct((TOP_K, n), I32),
                   jax.ShapeDtypeStruct((N_EXPERTS, 128), F32)],
        scratch_shapes=[pltpu.VMEM((tm // GRID_W, 8, CONV_SPAN, cw), F32),
                        pltpu.VMEM((tm, cw), F32),
                        pltpu.VMEM((N_EXPERTS, 1), F32),
                        pltpu.VMEM((tm, d), BF16),
                        pltpu.VMEM((tm, d), BF16)],
        compiler_params=_cparams(("arbitrary",)),
        name="mix",
    )(o_f, o_b, z, z, x, g1, sc2, sh2, hg, dwk, dwb, lng, lnb, wo, n2g, wr, br, shm)


def _pos_kernel(ps_ref, eid_ref, rank_ref, pos_ref):
    eid = eid_ref[...]
    acc = rank_ref[...]
    for e in range(N_EXPERTS):
        acc = acc + jnp.where(eid == e, ps_ref[e], 0)
    pos_ref[...] = acc


def _positions(pstarts, eid, rank):
    shape = eid.shape
    eid2 = eid.reshape(-1, 128)
    full = pl.BlockSpec(eid2.shape, lambda i: (0, 0))
    pos = pl.pallas_call(
        _pos_kernel,
        grid=(1,),
        in_specs=[pl.BlockSpec(memory_space=pltpu.SMEM), full, full],
        out_specs=full,
        out_shape=jax.ShapeDtypeStruct(eid2.shape, I32),
        name="positions",
    )(pstarts, eid2, rank.reshape(-1, 128))
    return pos.reshape(shape)


def _sc_mesh():
    return plsc.VectorSubcoreMesh(core_axis_name="c", subcore_axis_name="s",
                                  num_cores=SC_CORES, num_subcores=SC_SUBCORES)


def _sc_worker():
    return lax.axis_index("s") * SC_CORES + lax.axis_index("c")


def _sc_dispatch(h2p, pos4, padidx, zero_rows, n_rows):
    n, dw = h2p.shape
    _, workers, chunks, _ = pos4.shape
    pad_chunks = padidx.shape[1]
    per_worker = n // workers

    def body(h_hbm, pos_hbm, pad_hbm, zero_hbm, out_hbm, idx_v, pad_v, rows_v):
        wid = _sc_worker()
        base = wid * per_worker
        for k in range(TOP_K):
            pltpu.sync_copy(pos_hbm.at[k, wid], idx_v.at[k])
        pltpu.sync_copy(pad_hbm.at[wid], pad_v)
        pltpu.sync_copy(zero_hbm, rows_v)
        for j in range(pad_chunks):
            pltpu.sync_copy(rows_v, out_hbm.at[pad_v.at[j]])
        for j in range(chunks):
            pltpu.sync_copy(h_hbm.at[pl.ds(base + j * SC_W, SC_W)], rows_v)
            for k in range(TOP_K):
                pltpu.sync_copy(rows_v, out_hbm.at[idx_v.at[k, j]])

    return pl.kernel(
        body,
        out_type=jax.ShapeDtypeStruct((n_rows, dw), F32),
        mesh=_sc_mesh(),
        scratch_types=[pltpu.VMEM((TOP_K, chunks, SC_W), I32),
                       pltpu.VMEM((pad_chunks, SC_W), I32),
                       pltpu.VMEM((SC_W, dw), F32)],
        name="sc_dispatch",
    )(h2p, pos4, padidx, zero_rows)


def _sc_combine_gather(ybuf, pos4):
    _, dw = ybuf.shape
    _, workers, chunks, _ = pos4.shape
    per_worker = chunks * SC_W

    def body(y_hbm, pos_hbm, out_hbm, idx_v, rows_v):
        wid = _sc_worker()
        base = wid * per_worker
        for k in range(TOP_K):
            pltpu.sync_copy(pos_hbm.at[k, wid], idx_v.at[k])
        for k in range(TOP_K):
            for j in range(chunks):
                pltpu.sync_copy(y_hbm.at[idx_v.at[k, j]], rows_v)
                pltpu.sync_copy(rows_v, out_hbm.at[k, pl.ds(base + j * SC_W, SC_W)])

    return pl.kernel(
        body,
        out_type=jax.ShapeDtypeStruct((TOP_K, workers * per_worker, dw), F32),
        mesh=_sc_mesh(),
        scratch_types=[pltpu.VMEM((TOP_K, chunks, SC_W), I32),
                       pltpu.VMEM((SC_W, dw), F32)],
        name="sc_combine",
    )(ybuf, pos4)


def _ffn_kernel(blk_e_ref, nblk_ref, x_ref, wg_ref, wu_ref, wd_ref, y_ref):
    del blk_e_ref
    used = pl.program_id(0) < nblk_ref[0]
    half = x_ref.shape[1]

    @pl.when(used)
    def _():
        x_lo, x_hi = _unpack_halves(x_ref[...])
        bf = lambda wv: wv.astype(BF16)
        g = _dot(x_lo, bf(wg_ref[0, 0:half, :])) + _dot(x_hi, bf(wg_ref[0, half:, :]))
        u = _dot(x_lo, bf(wu_ref[0, 0:half, :])) + _dot(x_hi, bf(wu_ref[0, half:, :]))
        y = _dot((_silu(g) * u).astype(BF16), bf(wd_ref[0]))
        y_ref[...] = _pack_halves(y.astype(BF16).astype(F32))

    @pl.when(jnp.logical_not(used))
    def _():
        y_ref[...] = jnp.zeros_like(y_ref)


def _ffn(blk_e, nblk, xbuf, wg, wu, wd, tb):
    _, dw = xbuf.shape
    _, d, de = wg.shape
    n_blk = blk_e.shape[0]
    x_blk = lambda i, be, nb: (jnp.minimum(i, nb[0] - 1), 0)
    grid_spec = pltpu.PrefetchScalarGridSpec(
        num_scalar_prefetch=2,
        grid=(n_blk,),
        in_specs=[pl.BlockSpec((tb, dw), x_blk),
                  pl.BlockSpec((1, d, de), lambda i, be, nb: (be[i], 0, 0)),
                  pl.BlockSpec((1, d, de), lambda i, be, nb: (be[i], 0, 0)),
                  pl.BlockSpec((1, de, d), lambda i, be, nb: (be[i], 0, 0))],
        out_specs=pl.BlockSpec((tb, dw), lambda i, be, nb: (i, 0)),
    )
    return pl.pallas_call(
        _ffn_kernel,
        grid_spec=grid_spec,
        out_shape=jax.ShapeDtypeStruct((n_blk * tb, dw), F32),
        compiler_params=_cparams(("arbitrary",)),
        name="ffn",
    )(blk_e, nblk, xbuf, wg, wu, wd)


def _final_kernel(yg_ref, xm_ref, wt_ref, g2_ref, fg_ref, o_ref):
    wt = wt_ref[...]
    y = None
    for k in range(TOP_K):
        lo, hi = _unpack_halves(yg_ref[k])
        yk = jnp.concatenate([lo, hi], axis=-1).astype(F32) * wt[:, k:k + 1]
        y = yk if y is None else y + yk
    x = xm_ref[0] + g2_ref[0] * y
    ms = jnp.mean(x * x, axis=-1, keepdims=True)
    o_ref[0] = x * lax.rsqrt(ms + EPS) * fg_ref[...]


def _final(yg, xm, wt, g2, fg):
    bsz, l, d = xm.shape
    dw = yg.shape[2]
    tm = min(MOE_TM, l)
    nt = l // tm
    return pl.pallas_call(
        _final_kernel,
        grid=(bsz, nt),
        in_specs=[pl.BlockSpec((TOP_K, tm, dw), lambda b, i: (0, b * nt + i, 0)),
                  pl.BlockSpec((1, tm, d), lambda b, i: (b, i, 0)),
                  pl.BlockSpec((tm, TOP_K), lambda b, i: (b * nt + i, 0)),
                  pl.BlockSpec((1, 1, d), lambda b, i: (b, 0, 0)),
                  pl.BlockSpec((1, d), lambda b, i: (0, 0))],
        out_specs=pl.BlockSpec((1, tm, d), lambda b, i: (b, i, 0)),
        out_shape=jax.ShapeDtypeStruct((bsz, l, d), F32),
        compiler_params=_cparams(("arbitrary", "arbitrary")),
        name="final",
    )(yg, xm, wt, g2, fg)


def _layer(x, ctx, mod, mod_c, norm1_g, w_in, lb_logits, hgrn_norm_g, dw_kernel, dw_bias, conv_ln_g,
           conv_ln_b, w_out, norm2_g, wr, br, w_gate, w_up, w_down, final_norm_g):
    bsz, l, d = x.shape
    w = lb_logits.shape[2]
    heads = w // HEAD_DIM
    row = lambda v: v.reshape(1, -1)
    per_b = lambda v: v.reshape(bsz, 1, d)
    sh1, sc1, g1, sh2, sc2, g2 = [per_b(mod[:, j]) for j in range(N_MOD)]
    csh1 = jnp.broadcast_to(mod_c[0].reshape(1, 1, d), (bsz, 1, d))
    csc1 = jnp.broadcast_to(mod_c[1].reshape(1, 1, d), (bsz, 1, d))

    w_in_b = w_in.astype(BF16)
    zf, zr = _inproj(x, row(norm1_g), sc1, sh1, w_in_b, 2 * w, True)
    zcf, zcr = _inproj(ctx, row(norm1_g), csc1, csh1, w_in_b[:, :3 * w], 2 * w, False)
    o_f, o_b = _hgrn(lb_logits, zcf, zcr, zf, zr)
    rep8 = lambda v: jnp.broadcast_to(v[..., None, :], v.shape[:-1] + (8, v.shape[-1]))
    xm, h2, eid, wts, rank, cnt = _mix(
        o_f, o_b, zr, x, g1, sc2, sh2, row(jnp.tile(hgrn_norm_g, heads)), rep8(dw_kernel), rep8(dw_bias),
        row(conv_ln_g), row(conv_ln_b), w_out.astype(BF16), row(norm2_g), wr, br)
    n = bsz * l

    tb = FFN_TB
    n_blk = -(-(n * TOP_K) // tb) + N_EXPERTS
    counts = cnt[:, 0].astype(I32)
    pcounts = (counts + tb - 1) // tb * tb
    pends = jnp.cumsum(pcounts)
    pstarts = pends - pcounts
    pos = _positions(pstarts.astype(I32), eid, rank)
    blk_start = jnp.arange(n_blk, dtype=I32) * tb
    blk_e = jnp.minimum(jnp.sum((pends[None, :] <= blk_start[:, None]).astype(I32), axis=1), N_EXPERTS - 1)
    nblk_used = (pends[-1:] // tb).astype(I32)

    workers = SC_CORES * SC_SUBCORES
    pos4 = pos.reshape(TOP_K, workers, n // (workers * SC_W), SC_W)
    lane = jnp.arange(tb, dtype=I32)[None, :]
    pad_rows = jnp.where(lane < (pcounts - counts)[:, None], (pstarts + counts)[:, None] + lane,
                         n_blk * tb + lane)
    padidx = pad_rows.astype(I32).reshape(workers, N_EXPERTS // workers * tb // SC_W, SC_W)
    xbuf = _sc_dispatch(h2.reshape(n, d // 2), pos4, padidx, jnp.zeros((SC_W, d // 2), F32), (n_blk + 1) * tb)
    ybuf = _ffn(blk_e, nblk_used, xbuf, w_gate, w_up, w_down, tb)
    yg = _sc_combine_gather(ybuf, pos4)
    return _final(yg, xm, wts.T, g2, row(final_norm_g))


def kernel(x, c, ctx, c_ctx, w_ada, b_ada, norm1_g, w_in, lb_logits, hgrn_norm_g, dw_kernel, dw_bias,
           conv_ln_g, conv_ln_b, w_out, norm2_g, router_group_w, router_group_b, router_expert_w,
           router_expert_b, w_expert_gate, w_expert_up, w_expert_down, final_norm_g):
    depth = w_ada.shape[0]
    assert depth == 1, "context tokens are only updated between layers; a single layer is implemented"
    bsz, l, d = x.shape
    l0 = 0
    rows = -(-(bsz + 1) // 8) * 8
    cc = jnp.zeros((rows, d), F32).at[:bsz].set(c).at[bsz].set(c_ctx)
    mod_all = _ada(cc, w_ada[l0], b_ada[l0].reshape(1, -1))
    mod = mod_all[:bsz].reshape(bsz, N_MOD, d)
    mod_c = mod_all[bsz].reshape(N_MOD, d)
    pad8 = jnp.zeros((d, 8 - N_GROUPS), F32)
    wr = jnp.concatenate([router_group_w[l0], pad8, router_expert_w[l0], jnp.zeros((d, 8), F32)], axis=1).T
    br = jnp.concatenate([router_group_b[l0], jnp.zeros((8 - N_GROUPS,), F32), router_expert_b[l0],
                          jnp.zeros((8,), F32)]).reshape(ROUTER_ROWS, 1)
    return _layer(x, ctx, mod, mod_c, norm1_g[l0], w_in[l0], lb_logits, hgrn_norm_g[l0], dw_kernel[l0],
                  dw_bias[l0], conv_ln_g[l0], conv_ln_b[l0], w_out[l0], norm2_g[l0], wr, br,
                  w_expert_gate[l0], w_expert_up[l0], w_expert_down[l0], final_norm_g)
```

```python
import functools

import numpy as np

import jax
import jax.numpy as jnp
from jax import lax
from jax.experimental import pallas as pl
from jax.experimental.pallas import tpu as pltpu
from jax.experimental.pallas import tpu_sc as plsc

F32 = jnp.float32
BF16 = jnp.bfloat16
I32 = jnp.int32
U32 = jnp.uint32

EPS = 1e-6
HEAD_DIM = 128
CHUNK = 64
GRID_W = 64
CONV_K = 31
CONV_PAD = CONV_K // 2
CONV_LANES = 256
CONV_LEAD = 16
CONV_SPAN = GRID_W + 24
N_GROUPS = 4
EXPERTS_PER_GROUP = 8
N_EXPERTS = N_GROUPS * EXPERTS_PER_GROUP
TOP_K = 2
N_MOD = 6
ROUTER_ROWS = 48

ADA_TN = 1024
INPROJ_TM = 1024
HGRN_TT = 512
MIX_TM = 256
MOE_TM = 1024
FFN_TB = 1024
VMEM_LIMIT = 48 * 1024 * 1024
SC_CORES = 2
SC_SUBCORES = 16
SC_W = 128


def _cparams(sem):
    return pltpu.CompilerParams(dimension_semantics=sem, vmem_limit_bytes=VMEM_LIMIT)


def _silu(v):
    return v * jax.nn.sigmoid(v)


def _dot(a, b):
    return jnp.dot(a, b, preferred_element_type=F32)


def _dot_nt(a, b):
    return lax.dot_general(a, b, (((1,), (1,)), ((), ())), preferred_element_type=F32)


def _dot_tn(a, b):
    return lax.dot_general(a, b, (((0,), (0,)), ((), ())), preferred_element_type=F32)


def _pack_halves(vr):
    m = vr.shape[1] // 2
    bits = lax.bitcast_convert_type(vr, U32)
    return lax.bitcast_convert_type((bits[:, :m] >> 16) | bits[:, m:], F32)


def _unpack_halves(p):
    u = lax.bitcast_convert_type(p, U32)
    lo = lax.bitcast_convert_type(u << 16, F32)
    hi = lax.bitcast_convert_type(u & jnp.uint32(0xFFFF0000), F32)
    return lo.astype(BF16), hi.astype(BF16)


def _split2(a):
    hi = a.astype(BF16)
    lo = (a - hi.astype(F32)).astype(BF16)
    return hi, lo


def _ada_kernel(c_ref, w_ref, b_ref, o_ref):
    s = _silu(c_ref[...]).astype(BF16)
    o_ref[...] = _dot(s, w_ref[...].astype(BF16)) + b_ref[...]


def _ada(cc, w, b):
    r, d = cc.shape
    nc = w.shape[1]
    return pl.pallas_call(
        _ada_kernel,
        grid=(nc // ADA_TN,),
        in_specs=[pl.BlockSpec((r, d), lambda j: (0, 0)),
                  pl.BlockSpec((d, ADA_TN), lambda j: (0, j)),
                  pl.BlockSpec((1, ADA_TN), lambda j: (0, j))],
        out_specs=pl.BlockSpec((r, ADA_TN), lambda j: (0, j)),
        out_shape=jax.ShapeDtypeStruct((r, nc), F32),
        compiler_params=_cparams(("arbitrary",)),
        name="ada",
    )(cc, w, b)


def _inproj_kernel(x_ref, g_ref, sc_ref, sh_ref, w_ref, zf_ref, zr_ref, *, activate):
    x = x_ref[0]
    ms = jnp.mean(x * x, axis=-1, keepdims=True)
    gs = g_ref[...] * (1.0 + sc_ref[0])
    h = x * lax.rsqrt(ms + EPS) * gs + sh_ref[0]
    z = _dot(h.astype(BF16), w_ref[...])
    nf = zf_ref.shape[2]
    zf_ref[0] = z[:, :nf]
    if activate:
        wd = nf // 2
        col = lambda j: z[:, nf + j * wd:nf + (j + 1) * wd]
        zr_ref[0, :, 0:wd] = col(0).astype(BF16)
        zr_ref[0, :, wd:2 * wd] = _silu(col(1)).astype(BF16)
        zr_ref[0, :, 2 * wd:3 * wd] = _silu(col(2)).astype(BF16)
        zr_ref[0, :, 3 * wd:] = (col(3) * jax.nn.sigmoid(col(4))).astype(BF16)
    else:
        zr_ref[0] = z[:, nf:].astype(BF16)


def _inproj(x, g, sc, sh, w, nf, activate):
    bsz, l, d = x.shape
    nc = w.shape[1]
    tm = min(INPROJ_TM, l)
    nr = nc - nf - (nf // 2 if activate else 0)
    return pl.pallas_call(
        functools.partial(_inproj_kernel, activate=activate),
        grid=(bsz, l // tm),
        in_specs=[pl.BlockSpec((1, tm, d), lambda b, i: (b, i, 0)),
                  pl.BlockSpec((1, d), lambda b, i: (0, 0)),
                  pl.BlockSpec((1, 1, d), lambda b, i: (b, 0, 0)),
                  pl.BlockSpec((1, 1, d), lambda b, i: (b, 0, 0)),
                  pl.BlockSpec((d, nc), lambda b, i: (0, 0))],
        out_specs=[pl.BlockSpec((1, tm, nf), lambda b, i: (b, i, 0)),
                   pl.BlockSpec((1, tm, nr), lambda b, i: (b, i, 0))],
        out_shape=[jax.ShapeDtypeStruct((bsz, l, nf), F32),
                   jax.ShapeDtypeStruct((bsz, l, nr), BF16)],
        compiler_params=_cparams(("arbitrary", "arbitrary")),
        name="inproj",
    )(x, g, sc, sh, w)


class _HgrnDirection:
    def __init__(self, lbl, st_ref, reverse):
        self.st_ref, self.reverse = st_ref, reverse
        e = jnp.exp(lbl - jnp.max(lbl, axis=0, keepdims=True))
        self.lb = e[0:1] / jnp.sum(e, axis=0, keepdims=True)
        row = lax.broadcasted_iota(I32, (CHUNK, CHUNK), 0)
        col = lax.broadcasted_iota(I32, (CHUNK, CHUNK), 1)
        self.tri = (col >= row) if reverse else (col <= row)
        tm = jnp.where(self.tri, 1.0, 0.0).astype(BF16)
        self.tm2 = jnp.concatenate([tm, tm], axis=1)

    def tile(self, f_ref, v_ref, q_ref=None, o_ref=None):
        heads = self.st_ref.shape[0]
        n_chunks = f_ref.shape[1] // CHUNK
        crow = lambda c: slice(c * CHUNK, (c + 1) * CHUNK)
        hsl = lambda h: slice(h * HEAD_DIM, (h + 1) * HEAD_DIM)
        f = self.lb + (1.0 - self.lb) * jax.nn.sigmoid(f_ref[0])
        k = 1.0 - f
        hi, lo = _split2(jnp.log2(f))
        cums = [_dot(self.tm2, jnp.concatenate([hi[crow(c)], lo[crow(c)]], axis=0))
                for c in range(n_chunks)]
        lasts = [cm[0:1] if self.reverse else cm[CHUNK - 1:CHUNK] for cm in cums]
        cum = jnp.concatenate(cums, axis=0)
        dec = [jnp.exp2(la) for la in lasts]
        k_intra = k * jnp.exp2(-cum)
        ks = (k_intra * jnp.concatenate([jnp.broadcast_to(dc, (CHUNK, dc.shape[1])) for dc in dec], axis=0)
              ).astype(BF16)
        vb = v_ref[0].astype(BF16)
        kv_of = lambda c: [_dot_tn(vb[crow(c), hsl(h)], ks[crow(c), hsl(h)]) for h in range(heads)]
        emit = o_ref is not None
        if emit:
            qd = (q_ref[0].astype(F32) * jnp.exp2(cum)).astype(BF16)
            ki = k_intra.astype(BF16)
            sc = [[jnp.where(self.tri, _dot_nt(qd[crow(c), hsl(h)], ki[crow(c), hsl(h)]), 0.0).astype(BF16)
                   for h in range(heads)] for c in range(n_chunks)]
            lhs = [[jnp.concatenate([qd[crow(c), hsl(h)], sc[c][h]], axis=1) for h in range(heads)]
                   for c in range(n_chunks)]
            vt = [[vb[crow(c), hsl(h)].T for h in range(heads)] for c in range(n_chunks)]
        state = [self.st_ref[h] for h in range(heads)]
        order = list(range(n_chunks - 1, -1, -1) if self.reverse else range(n_chunks))
        kv_next = kv_of(order[0])
        for pos, c in enumerate(order):
            kv = kv_next
            if pos + 1 < n_chunks:
                kv_next = kv_of(order[pos + 1])
            for h in range(heads):
                if emit:
                    rhs = jnp.concatenate([state[h].astype(BF16), vt[c][h]], axis=1)
                    o_ref[0, crow(c), hsl(h)] = _dot_nt(lhs[c][h], rhs).astype(o_ref.dtype)
                state[h] = state[h] * dec[c][:, hsl(h)] + kv[h]
        for h in range(heads):
            self.st_ref[h] = state[h]


def _hgrn_kernel(lbl_ref, cff_ref, cfb_ref, cv_ref, ff_ref, vf_ref, qf_ref, fb_ref, vb_ref, qb_ref,
                 of_ref, ob_ref, stf_ref, stb_ref):
    fwd = _HgrnDirection(lbl_ref[0], stf_ref, reverse=False)
    bwd = _HgrnDirection(lbl_ref[1], stb_ref, reverse=True)

    @pl.when(pl.program_id(1) == 0)
    def _():
        stf_ref[...] = jnp.zeros_like(stf_ref)
        stb_ref[...] = jnp.zeros_like(stb_ref)
        fwd.tile(cff_ref, cv_ref)
        bwd.tile(cfb_ref, cv_ref)

    fwd.tile(ff_ref, vf_ref, qf_ref, of_ref)
    bwd.tile(fb_ref, vb_ref, qb_ref, ob_ref)


def _hgrn(lbl, zcf, zcr, zf, zr):
    bsz, l, _ = zf.shape
    n_ctx_tok = zcf.shape[1]
    w = lbl.shape[2]
    heads = w // HEAD_DIM
    tt = min(HGRN_TT, l)
    nt = l // tt
    ctx_col = lambda c: pl.BlockSpec((1, n_ctx_tok, w), lambda b, i: (b, 0, c))
    fwd_col = lambda c: pl.BlockSpec((1, tt, w), lambda b, i: (b, i, c))
    bwd_col = lambda c: pl.BlockSpec((1, tt, w), lambda b, i: (b, nt - 1 - i, c))
    state = pltpu.VMEM((heads, HEAD_DIM, HEAD_DIM), F32)
    return pl.pallas_call(
        _hgrn_kernel,
        grid=(bsz, nt),
        in_specs=[pl.BlockSpec(lbl.shape, lambda b, i: (0, 0, 0)),
                  ctx_col(0), ctx_col(1), ctx_col(0),
                  fwd_col(0), fwd_col(0), fwd_col(1),
                  bwd_col(1), bwd_col(0), bwd_col(1)],
        out_specs=[fwd_col(0), bwd_col(0)],
        out_shape=[jax.ShapeDtypeStruct((bsz, l, w), BF16), jax.ShapeDtypeStruct((bsz, l, w), BF16)],
        scratch_shapes=[state, state],
        compiler_params=_cparams(("arbitrary", "arbitrary")),
        name="hgrn",
    )(lbl, zcf, zcf, zcr, zf, zr, zr, zf, zr, zr)


def _mix_kernel(of_ref, ob_ref, zg_ref, zu_ref, x_ref, g1_ref, sc2_ref, sh2_ref,
                hg_ref, dwk_ref, dwb_ref, lng_ref, lnb_ref, wo_ref, n2g_ref, wr_ref, br_ref, shm_ref,
                xm_ref, h2_ref, eid_ref, wts_ref, rank_ref, cnt_ref,
                shf_ref, cv_ref, carry_ref, hh_ref, hl_ref, *, tm):
    w = of_ref.shape[2]
    heads = w // HEAD_DIM
    step = pl.program_id(0)

    @pl.when(step == 0)
    def _():
        carry_ref[...] = jnp.zeros_like(carry_ref)
        hh_ref[...] = jnp.zeros_like(hh_ref)
        hl_ref[...] = jnp.zeros_like(hl_ref)

    router = _route(hh_ref[...], hl_ref[...], wr_ref, br_ref, carry_ref, jnp.where(step > 0, 1.0, 0.0),
                    eid_ref, wts_ref, rank_ref, cnt_ref)
    next(router)

    o = of_ref[0].astype(F32) + ob_ref[0].astype(F32)
    parts = []
    for h in range(heads):
        oh = o[:, h * HEAD_DIM:(h + 1) * HEAD_DIM]
        parts.append(oh * lax.rsqrt(jnp.mean(oh * oh, axis=-1, keepdims=True) + EPS))
    o_mix = jnp.concatenate(parts, axis=-1) * hg_ref[...] * zg_ref[0].astype(F32)

    u = zu_ref[0].astype(F32)
    span = shf_ref.shape[2]
    u_hi, u_lo = _split2(u)
    for s in range(tm // GRID_W):
        rows = slice(s * GRID_W, (s + 1) * GRID_W)
        shifted = _dot(shm_ref[...], jnp.concatenate([u_hi[rows], u_lo[rows]], axis=0))
        for b in range(8):
            shf_ref[s, b] = shifted[b * span:(b + 1) * span]
    next(router)
    lanes = CONV_LANES
    for s in range(tm // GRID_W):
        for c0 in range(0, u.shape[1], lanes):
            cs = slice(c0, c0 + lanes)
            acc = jnp.zeros((GRID_W // 8, 8, lanes), F32) + dwb_ref[:, cs][None]
            for b in range(8):
                copy = shf_ref[s, b, :, cs].reshape(span // 8, 8, lanes)
                for a in range(span // 8 - GRID_W // 8 + 1):
                    k = 8 * a + b - (CONV_LEAD - CONV_PAD)
                    if 0 <= k < CONV_K:
                        acc = acc + copy[a:a + GRID_W // 8] * dwk_ref[k, :, cs][None]
            cv_ref[s * GRID_W:(s + 1) * GRID_W, cs] = acc.reshape(GRID_W, lanes)
        if s == 0:
            next(router)
    cv = cv_ref[...]
    mu = jnp.mean(cv, axis=-1, keepdims=True)
    cen = cv - mu
    var = jnp.mean(cen * cen, axis=-1, keepdims=True)
    c_mix = _silu(cen * lax.rsqrt(var + EPS) * lng_ref[...] + lnb_ref[...])
    next(router, None)

    mix = _dot(jnp.concatenate([o_mix.astype(BF16), c_mix.astype(BF16)], axis=1), wo_ref[...])
    xm = x_ref[0] + g1_ref[0] * mix
    xm_ref[0] = xm

    ms = jnp.mean(xm * xm, axis=-1, keepdims=True)
    h2 = xm * lax.rsqrt(ms + EPS) * (n2g_ref[...] * (1.0 + sc2_ref[0])) + sh2_ref[0]
    hh = h2.astype(BF16)
    h2_rounded = hh.astype(F32)
    h2_ref[0] = _pack_halves(h2_rounded)
    hh_ref[...] = hh
    hl_ref[...] = (h2 - h2_rounded).astype(BF16)


def _route(hh, hl, wr_ref, br_ref, carry_ref, live, eid_ref, wts_ref, rank_ref, cnt_ref):
    tm = hh.shape[0]
    wr = wr_ref[...]
    wh = wr.astype(BF16)
    wl = (wr - wh.astype(F32)).astype(BF16)
    lg = _dot_nt(wh, hh) + _dot_nt(wh, hl) + _dot_nt(wl, hh) + br_ref[...]
    yield

    neg = -jnp.inf
    r8 = lax.broadcasted_iota(I32, (EXPERTS_PER_GROUP, tm), 0).astype(F32)
    gl = jnp.where(r8 < N_GROUPS, lg[0:8], neg)
    gmax = jnp.max(gl, axis=0, keepdims=True)
    grp = jnp.min(jnp.where(gl == gmax, r8, 8.0), axis=0, keepdims=True)
    p_grp = 1.0 / jnp.sum(jnp.exp(gl - gmax), axis=0, keepdims=True)
    es = lg[8:16]
    for g in range(1, N_GROUPS):
        es = jnp.where(grp == float(g), lg[8 + 8 * g:16 + 8 * g], es)
    m1 = jnp.max(es, axis=0, keepdims=True)
    i1 = jnp.min(jnp.where(es == m1, r8, 8.0), axis=0, keepdims=True)
    es2 = jnp.where(r8 == i1, neg, es)
    m2 = jnp.max(es2, axis=0, keepdims=True)
    i2 = jnp.min(jnp.where(es2 == m2, r8, 8.0), axis=0, keepdims=True)
    e2 = jnp.exp(m2 - m1)
    w1 = p_grp / (1.0 + e2)
    w2 = p_grp * e2 / (1.0 + e2)
    eid0 = grp * float(EXPERTS_PER_GROUP) + i1
    eid1 = grp * float(EXPERTS_PER_GROUP) + i2
    eid_ref[0:1, :] = eid0.astype(I32)
    eid_ref[1:2, :] = eid1.astype(I32)
    wts_ref[0:1, :] = w1
    wts_ref[1:2, :] = w2
    yield

    r32 = lax.broadcasted_iota(I32, (N_EXPERTS, tm), 0).astype(F32)
    oh0 = jnp.where(r32 == eid0, live, 0.0)
    oh1 = jnp.where(r32 == eid1, live, 0.0)
    src = lax.broadcasted_iota(I32, (tm, tm), 0)
    dst = lax.broadcasted_iota(I32, (tm, tm), 1)
    before = jnp.where(src < dst, 1.0, 0.0).astype(BF16)
    ex = _dot(jnp.concatenate([oh0, oh1], axis=0).astype(BF16), before)
    yield
    base = ex[0:N_EXPERTS] + ex[N_EXPERTS:] + carry_ref[...]
    rank_ref[0:1, :] = jnp.sum(oh0 * base, axis=0, keepdims=True).astype(I32)
    rank_ref[1:2, :] = jnp.sum(oh1 * base, axis=0, keepdims=True).astype(I32)
    carry = carry_ref[...] + jnp.sum(oh0 + oh1, axis=1, keepdims=True)
    carry_ref[...] = carry
    cnt_ref[...] = jnp.broadcast_to(carry, cnt_ref.shape)


def _conv_shift_matrix():
    r = np.arange(8 * CONV_SPAN)
    src = r // CONV_SPAN + r % CONV_SPAN - CONV_LEAD
    sel = (src[:, None] == np.arange(GRID_W)[None, :]).astype(np.float32)
    return jnp.asarray(np.concatenate([sel, sel], axis=1), dtype=BF16)


def _mix(o_f, o_b, z, x, g1, sc2, sh2, hg, dwk, dwb, lng, lnb, wo, n2g, wr, br):
    bsz, l, d = x.shape
    shm = _conv_shift_matrix()
    w = o_f.shape[2]
    cw = dwk.shape[-1]
    tm = min(MIX_TM, l)
    nt = l // tm
    n = bsz * l
    tiles = bsz * nt
    mixed = lambda s: jnp.minimum(s, tiles - 1)
    src = lambda c: (lambda s: (mixed(s) // nt, mixed(s) % nt, c))
    tok = src(0)
    vec = lambda s: (0, 0)
    per_b = lambda s: (mixed(s) // nt, 0, 0)
    flat = lambda s: (0, jnp.maximum(s - 1, 0))
    kern = functools.partial(_mix_kernel, tm=tm)
    return pl.pallas_call(
        kern,
        grid=(tiles + 1,),
        in_specs=[pl.BlockSpec((1, tm, w), src(0)),
                  pl.BlockSpec((1, tm, w), src(0)),
                  pl.BlockSpec((1, tm, w), src(2)),
                  pl.BlockSpec((1, tm, cw), src(3)),
                  pl.BlockSpec((1, tm, d), src(0)),
                  pl.BlockSpec((1, 1, d), per_b),
                  pl.BlockSpec((1, 1, d), per_b),
                  pl.BlockSpec((1, 1, d), per_b),
                  pl.BlockSpec((1, w), vec),
                  pl.BlockSpec((CONV_K, 8, cw), lambda s: (0, 0, 0)),
                  pl.BlockSpec((8, cw), vec),
                  pl.BlockSpec((1, cw), vec),
                  pl.BlockSpec((1, cw), vec),
                  pl.BlockSpec((w + cw, d), vec),
                  pl.BlockSpec((1, d), vec),
                  pl.BlockSpec((ROUTER_ROWS, d), vec),
                  pl.BlockSpec((ROUTER_ROWS, 1), vec),
                  pl.BlockSpec(shm.shape, vec)],
        out_specs=[pl.BlockSpec((1, tm, d), tok),
                   pl.BlockSpec((1, tm, d // 2), tok),
                   pl.BlockSpec((TOP_K, tm), flat),
                   pl.BlockSpec((TOP_K, tm), flat),
                   pl.BlockSpec((TOP_K, tm), flat),
                   pl.BlockSpec((N_EXPERTS, 128), vec)],
        out_shape=[jax.ShapeDtypeStruct((bsz, l, d), F32),
                   jax.ShapeDtypeStruct((bsz, l, d // 2), F32),
                   jax.ShapeDtypeStruct((TOP_K, n), I32),
                   jax.ShapeDtypeStruct((TOP_K, n), F32),
                   jax.ShapeDtypeStruct((TOP_K, n), I32),
                   jax.ShapeDtypeStruct((N_EXPERTS, 128), F32)],
        scratch_shapes=[pltpu.VMEM((tm // GRID_W, 8, CONV_SPAN, cw), F32),
                        pltpu.VMEM((tm, cw), F32),
                        pltpu.VMEM((N_EXPERTS, 1), F32),
                        pltpu.VMEM((tm, d), BF16),
                        pltpu.VMEM((tm, d), BF16)],
        compiler_params=_cparams(("arbitrary",)),
        name="mix",
    )(o_f, o_b, z, z, x, g1, sc2, sh2, hg, dwk, dwb, lng, lnb, wo, n2g, wr, br, shm)


def _pos_kernel(ps_ref, eid_ref, rank_ref, pos_ref):
    eid = eid_ref[...]
    acc = rank_ref[...]
    for e in range(N_EXPERTS):
        acc = acc + jnp.where(eid == e, ps_ref[e], 0)
    pos_ref[...] = acc


def _positions(pstarts, eid, rank):
    shape = eid.shape
    eid2 = eid.reshape(-1, 128)
    full = pl.BlockSpec(eid2.shape, lambda i: (0, 0))
    pos = pl.pallas_call(
        _pos_kernel,
        grid=(1,),
        in_specs=[pl.BlockSpec(memory_space=pltpu.SMEM), full, full],
        out_specs=full,
        out_shape=jax.ShapeDtypeStruct(eid2.shape, I32),
        name="positions",
    )(pstarts, eid2, rank.reshape(-1, 128))
    return pos.reshape(shape)


def _sc_mesh():
    return plsc.VectorSubcoreMesh(core_axis_name="c", subcore_axis_name="s",
                                  num_cores=SC_CORES, num_subcores=SC_SUBCORES)


def _sc_worker():
    return lax.axis_index("s") * SC_CORES + lax.axis_index("c")


def _sc_dispatch(h2p, pos4, n_rows):
    n, dw = h2p.shape
    _, workers, chunks, _ = pos4.shape
    per_worker = n // workers

    def body(h_hbm, pos_hbm, out_hbm, idx_v, rows_v):
        wid = _sc_worker()
        base = wid * per_worker
        for k in range(TOP_K):
            pltpu.sync_copy(pos_hbm.at[k, wid], idx_v.at[k])
        for j in range(chunks):
            pltpu.sync_copy(h_hbm.at[pl.ds(base + j * SC_W, SC_W)], rows_v)
            for k in range(TOP_K):
                pltpu.sync_copy(rows_v, out_hbm.at[idx_v.at[k, j]])

    return pl.kernel(
        body,
        out_type=jax.ShapeDtypeStruct((n_rows, dw), F32),
        mesh=_sc_mesh(),
        scratch_types=[pltpu.VMEM((TOP_K, chunks, SC_W), I32),
                       pltpu.VMEM((SC_W, dw), F32)],
        name="sc_dispatch",
    )(h2p, pos4)


def _sc_combine_gather(ybuf, pos4):
    _, dw = ybuf.shape
    _, workers, chunks, _ = pos4.shape
    per_worker = chunks * SC_W

    def body(y_hbm, pos_hbm, out_hbm, idx_v, rows_v):
        wid = _sc_worker()
        base = wid * per_worker
        for k in range(TOP_K):
            pltpu.sync_copy(pos_hbm.at[k, wid], idx_v.at[k])
        for k in range(TOP_K):
            for j in range(chunks):
                pltpu.sync_copy(y_hbm.at[idx_v.at[k, j]], rows_v)
                pltpu.sync_copy(rows_v, out_hbm.at[k, pl.ds(base + j * SC_W, SC_W)])

    return pl.kernel(
        body,
        out_type=jax.ShapeDtypeStruct((TOP_K, workers * per_worker, dw), F32),
        mesh=_sc_mesh(),
        scratch_types=[pltpu.VMEM((TOP_K, chunks, SC_W), I32),
                       pltpu.VMEM((SC_W, dw), F32)],
        name="sc_combine",
    )(ybuf, pos4)


def _ffn_kernel(blk_e_ref, nblk_ref, x_ref, wg_ref, wu_ref, wd_ref, y_ref):
    del blk_e_ref
    i = pl.program_id(0)
    used = i < nblk_ref[0]
    half = x_ref.shape[1]

    @pl.when(used)
    def _():
        row = lax.broadcasted_iota(I32, x_ref.shape, 0)
        x = jnp.where(row < nblk_ref[1 + i], x_ref[...], 0.0)
        x_lo, x_hi = _unpack_halves(x)
        bf = lambda wv: wv.astype(BF16)
        g = _dot(x_lo, bf(wg_ref[0, 0:half, :])) + _dot(x_hi, bf(wg_ref[0, half:, :]))
        u = _dot(x_lo, bf(wu_ref[0, 0:half, :])) + _dot(x_hi, bf(wu_ref[0, half:, :]))
        y = _dot((_silu(g) * u).astype(BF16), bf(wd_ref[0]))
        y_ref[...] = _pack_halves(y.astype(BF16).astype(F32))

    @pl.when(jnp.logical_not(used))
    def _():
        y_ref[...] = jnp.zeros_like(y_ref)


def _ffn(blk_e, nblk, xbuf, wg, wu, wd, tb):
    _, dw = xbuf.shape
    _, d, de = wg.shape
    n_blk = blk_e.shape[0]
    x_blk = lambda i, be, nb: (jnp.minimum(i, nb[0] - 1), 0)
    grid_spec = pltpu.PrefetchScalarGridSpec(
        num_scalar_prefetch=2,
        grid=(n_blk,),
        in_specs=[pl.BlockSpec((tb, dw), x_blk),
                  pl.BlockSpec((1, d, de), lambda i, be, nb: (be[i], 0, 0)),
                  pl.BlockSpec((1, d, de), lambda i, be, nb: (be[i], 0, 0)),
                  pl.BlockSpec((1, de, d), lambda i, be, nb: (be[i], 0, 0))],
        out_specs=pl.BlockSpec((tb, dw), lambda i, be, nb: (i, 0)),
    )
    return pl.pallas_call(
        _ffn_kernel,
        grid_spec=grid_spec,
        out_shape=jax.ShapeDtypeStruct((n_blk * tb, dw), F32),
        compiler_params=_cparams(("arbitrary",)),
        name="ffn",
    )(blk_e, nblk, xbuf, wg, wu, wd)


def _final_kernel(yg_ref, xm_ref, wt_ref, g2_ref, fg_ref, o_ref):
    wt = wt_ref[...]
    y = None
    for k in range(TOP_K):
        lo, hi = _unpack_halves(yg_ref[k])
        yk = jnp.concatenate([lo, hi], axis=-1).astype(F32) * wt[:, k:k + 1]
        y = yk if y is None else y + yk
    x = xm_ref[0] + g2_ref[0] * y
    ms = jnp.mean(x * x, axis=-1, keepdims=True)
    o_ref[0] = x * lax.rsqrt(ms + EPS) * fg_ref[...]


def _final(yg, xm, wt, g2, fg):
    bsz, l, d = xm.shape
    dw = yg.shape[2]
    tm = min(MOE_TM, l)
    nt = l // tm
    return pl.pallas_call(
        _final_kernel,
        grid=(bsz, nt),
        in_specs=[pl.BlockSpec((TOP_K, tm, dw), lambda b, i: (0, b * nt + i, 0)),
                  pl.BlockSpec((1, tm, d), lambda b, i: (b, i, 0)),
                  pl.BlockSpec((tm, TOP_K), lambda b, i: (b * nt + i, 0)),
                  pl.BlockSpec((1, 1, d), lambda b, i: (b, 0, 0)),
                  pl.BlockSpec((1, d), lambda b, i: (0, 0))],
        out_specs=pl.BlockSpec((1, tm, d), lambda b, i: (b, i, 0)),
        out_shape=jax.ShapeDtypeStruct((bsz, l, d), F32),
        compiler_params=_cparams(("arbitrary", "arbitrary")),
        name="final",
    )(yg, xm, wt, g2, fg)


def _layer(x, ctx, mod, mod_c, norm1_g, w_in, lb_logits, hgrn_norm_g, dw_kernel, dw_bias, conv_ln_g,
           conv_ln_b, w_out, norm2_g, wr, br, w_gate, w_up, w_down, final_norm_g):
    bsz, l, d = x.shape
    w = lb_logits.shape[2]
    heads = w // HEAD_DIM
    row = lambda v: v.reshape(1, -1)
    per_b = lambda v: v.reshape(bsz, 1, d)
    sh1, sc1, g1, sh2, sc2, g2 = [per_b(mod[:, j]) for j in range(N_MOD)]
    csh1 = jnp.broadcast_to(mod_c[0].reshape(1, 1, d), (bsz, 1, d))
    csc1 = jnp.broadcast_to(mod_c[1].reshape(1, 1, d), (bsz, 1, d))

    w_in_b = w_in.astype(BF16)
    zf, zr = _inproj(x, row(norm1_g), sc1, sh1, w_in_b, 2 * w, True)
    zcf, zcr = _inproj(ctx, row(norm1_g), csc1, csh1, w_in_b[:, :3 * w], 2 * w, False)
    o_f, o_b = _hgrn(lb_logits, zcf, zcr, zf, zr)
    rep8 = lambda v: jnp.broadcast_to(v[..., None, :], v.shape[:-1] + (8, v.shape[-1]))
    xm, h2, eid, wts, rank, cnt = _mix(
        o_f, o_b, zr, x, g1, sc2, sh2, row(jnp.tile(hgrn_norm_g, heads)), rep8(dw_kernel), rep8(dw_bias),
        row(conv_ln_g), row(conv_ln_b), w_out.astype(BF16), row(norm2_g), wr, br)
    n = bsz * l

    tb = FFN_TB
    n_blk = -(-(n * TOP_K) // tb) + N_EXPERTS
    counts = cnt[:, 0].astype(I32)
    pcounts = (counts + tb - 1) // tb * tb
    pends = jnp.cumsum(pcounts)
    pstarts = pends - pcounts
    pos = _positions(pstarts.astype(I32), eid, rank)
    blk_start = jnp.arange(n_blk, dtype=I32) * tb
    blk_e = jnp.minimum(jnp.sum((pends[None, :] <= blk_start[:, None]).astype(I32), axis=1), N_EXPERTS - 1)
    blk_valid = jnp.clip((pstarts + counts)[blk_e] - blk_start, 0, tb)
    blk_meta = jnp.concatenate([pends[-1:] // tb, blk_valid]).astype(I32)

    workers = SC_CORES * SC_SUBCORES
    pos4 = pos.reshape(TOP_K, workers, n // (workers * SC_W), SC_W)
    xbuf = _sc_dispatch(h2.reshape(n, d // 2), pos4, n_blk * tb)
    ybuf = _ffn(blk_e, blk_meta, xbuf, w_gate, w_up, w_down, tb)
    yg = _sc_combine_gather(ybuf, pos4)
    return _final(yg, xm, wts.T, g2, row(final_norm_g))


def kernel(x, c, ctx, c_ctx, w_ada, b_ada, norm1_g, w_in, lb_logits, hgrn_norm_g, dw_kernel, dw_bias,
           conv_ln_g, conv_ln_b, w_out, norm2_g, router_group_w, router_group_b, router_expert_w,
           router_expert_b, w_expert_gate, w_expert_up, w_expert_down, final_norm_g):
    depth = w_ada.shape[0]
    assert depth == 1, "context tokens are only updated between layers; a single layer is implemented"
    bsz, l, d = x.shape
    l0 = 0
    rows = -(-(bsz + 1) // 8) * 8
    cc = jnp.zeros((rows, d), F32).at[:bsz].set(c).at[bsz].set(c_ctx)
    mod_all = _ada(cc, w_ada[l0], b_ada[l0].reshape(1, -1))
    mod = mod_all[:bsz].reshape(bsz, N_MOD, d)
    mod_c = mod_all[bsz].reshape(N_MOD, d)
    pad8 = jnp.zeros((d, 8 - N_GROUPS), F32)
    wr = jnp.concatenate([router_group_w[l0], pad8, router_expert_w[l0], jnp.zeros((d, 8), F32)], axis=1).T
    br = jnp.concatenate([router_group_b[l0], jnp.zeros((8 - N_GROUPS,), F32), router_expert_b[l0],
                          jnp.zeros((8,), F32)]).reshape(ROUTER_ROWS, 1)
    return _layer(x, ctx, mod, mod_c, norm1_g[l0], w_in[l0], lb_logits, hgrn_norm_g[l0], dw_kernel[l0],
                  dw_bias[l0], conv_ln_g[l0], conv_ln_b[l0], w_out[l0], norm2_g[l0], wr, br,
                  w_expert_gate[l0], w_expert_up[l0], w_expert_down[l0], final_norm_g)
```
